```python
import jax
import jax.numpy as jnp
from jax import lax
import numpy as np

D_MODEL = 2048
BATCH = 2
SEQ = 8192
DEPTH = 2
DEC_BATCH = 16
DEC_SEQ = 32
PAST_LEN = 1024

CHUNK = 64
EPS = 1e-6
NEG_INF = -1e30
H_A = 8
DK_A = 64
DV_A = 128
GLA_RANK = 16
GLA_TAU = 16.0
H_B = 8
DH_B = 128
B_PREV = 8
REL_CLIP = 128
H_C = 32
KV_C = 4
G_C = H_C // KV_C
DH_C = 64
WINDOW = 128
C_PREV = WINDOW // CHUNK
ROT_DIM = DH_C // 4
ROPE_THETA = 500000.0
N_GROUPS = 4
EXP_PER_GROUP = 8
TOP_K_INNER = 2
D_FF_EXPERT = 512
N_AB = (DEPTH + 1) // 2
N_C = DEPTH // 2
AB_SIZES = (H_A * DK_A, H_A * DK_A, H_A * DV_A, H_A * DV_A, GLA_RANK, H_B * DH_B, H_B * DH_B, H_B * DH_B)
D_IN_AB = sum(AB_SIZES)
D_MIX_AB = H_A * DV_A + H_B * DH_B
C_SIZES = (H_C * DH_C, KV_C * DH_C, KV_C * DH_C)
D_IN_C = sum(C_SIZES)

kernel_name = 'hybrid_streaming_gla_band_swa_hmoe_step'


def _split_points(sizes):
    return [int(s) for s in np.cumsum(sizes)[:-1]]


def rmsnorm(x, g):
    xf = x.astype(jnp.float32)
    r = xf * lax.rsqrt(jnp.mean(xf * xf, axis=-1, keepdims=True) + EPS)
    return (r * g.astype(jnp.float32)).astype(x.dtype)


def rope_partial(x, pos):
    half = ROT_DIM // 2
    inv = jnp.power(ROPE_THETA, -jnp.arange(half, dtype=jnp.float32) * 2.0 / ROT_DIM)
    ang = pos.astype(jnp.float32)[:, None] * inv[None, :]
    shape = (pos.shape[0],) + (1,) * (x.ndim - 3) + (half,)
    cos = jnp.cos(ang).reshape(shape)
    sin = jnp.sin(ang).reshape(shape)
    xf = x.astype(jnp.float32)
    x1 = xf[..., :half]
    x2 = xf[..., half:ROT_DIM]
    out = jnp.concatenate([x1 * cos - x2 * sin, x2 * cos + x1 * sin, xf[..., ROT_DIM:]], axis=-1)
    return out.astype(x.dtype)


def softmax_sink(s, sink):
    if sink is None:
        return jax.nn.softmax(s, axis=-1)
    m = jnp.maximum(jnp.max(s, axis=-1, keepdims=True), sink)
    e = jnp.exp(s - m)
    return e / (jnp.sum(e, axis=-1, keepdims=True) + jnp.exp(sink - m))


def rel_bias(table, nq, nk, offset):
    rel = jnp.arange(nq)[:, None] - jnp.arange(nk)[None, :] + offset
    idx = jnp.clip(rel, -REL_CLIP, REL_CLIP) + REL_CLIP
    return table.astype(jnp.float32)[:, idx][:, None]


def band_attention(q, k, v, n_prev, bias, sink):
    B, S, KH, G, D = q.shape
    n = S // CHUNK
    W = (n_prev + 1) * CHUNK
    qc = q.reshape(B, n, CHUNK, KH, G, D)
    pad = ((0, 0), (n_prev, 0), (0, 0), (0, 0), (0, 0))
    kp = jnp.pad(k.reshape(B, n, CHUNK, KH, D), pad)
    vp = jnp.pad(v.reshape(B, n, CHUNK, KH, D), pad)
    idx = jnp.arange(n)[:, None] + jnp.arange(n_prev + 1)[None, :]
    kb = kp[:, idx].reshape(B, n, W, KH, D)
    vb = vp[:, idx].reshape(B, n, W, KH, D)
    s = jnp.einsum('bnqhgd,bnkhd->bnhgqk', qc, kb).astype(jnp.float32)
    if bias is not None:
        s = s + bias
    key_chunk = jnp.arange(n)[:, None] - n_prev + jnp.arange(W)[None, :] // CHUNK
    s = jnp.where((key_chunk >= 0)[None, :, None, None, None, :], s, NEG_INF)
    p = softmax_sink(s, None if sink is None else sink[None, None, :, :, None, None])
    o = jnp.einsum('bnhgqk,bnkhd->bnqhgd', p.astype(v.dtype), vb)
    return o.reshape(B, S, KH, G, D)


def cached_attention(q, k, v, bias, sink):
    s = jnp.einsum('bqhgd,bkhd->bhgqk', q, k).astype(jnp.float32)
    if bias is not None:
        s = s + bias
    p = softmax_sink(s, None if sink is None else sink[None, :, :, None, None])
    return jnp.einsum('bhgqk,bkhd->bqhgd', p.astype(v.dtype), v)


def gla_chunked(q, k, v, log_a, s0):
    B, T, H, K = q.shape
    V = v.shape[-1]
    C = min(CHUNK, T)
    n = T // C
    f32 = jnp.float32
    qc = q.astype(f32).reshape(B, n, C, H, K)
    kc = k.astype(f32).reshape(B, n, C, H, K)
    vc = v.astype(f32).reshape(B, n, C, H, V)
    b = jnp.cumsum(log_a.astype(f32).reshape(B, n, C, H, K), axis=2)
    b_last = b[:, :, -1:]
    q_dec = qc * jnp.exp(b)
    k_inv = kc * jnp.exp(-b)
    k_end = kc * jnp.exp(b_last - b)
    att = jnp.einsum('bnihk,bnjhk->bnhij', q_dec, k_inv)
    att = jnp.where(jnp.tril(jnp.ones((C, C), dtype=bool)), att, 0.0)
    o_intra = jnp.einsum('bnhij,bnjhv->bnihv', att, vc)
    d_state = jnp.einsum('bnjhk,bnjhv->nbhkv', k_end, vc)
    decay = jnp.moveaxis(jnp.exp(b_last[:, :, 0]), 1, 0)

    def step(state, inp):
        dec, ds = inp
        return dec[..., None] * state + ds, state

    s_fin, s_before = lax.scan(step, s0.astype(f32), (decay, d_state))
    o_inter = jnp.einsum('bnihk,nbhkv->bnihv', q_dec, s_before)
    o = (o_intra + o_inter).reshape(B, T, H, V)
    return o.astype(v.dtype), s_fin.astype(s0.dtype)


def ab_mixer(h, w_in, w_lr, b_lr, g_norm, rel_tab, w_out, s0, k_cache, v_cache):
    B, T, _ = h.shape
    z = h @ w_in
    qa, ka, va, ga, la, qb, kb, vb = jnp.split(z, _split_points(AB_SIZES), axis=-1)
    qa = qa.reshape(B, T, H_A, DK_A) * (DK_A ** -0.5)
    ka = ka.reshape(B, T, H_A, DK_A)
    va = va.reshape(B, T, H_A, DV_A)
    log_a = jax.nn.log_sigmoid((la @ w_lr + b_lr).astype(jnp.float32)).reshape(B, T, H_A, DK_A) / GLA_TAU
    if s0 is None:
        s0 = jnp.zeros((B, H_A, DK_A, DV_A), h.dtype)
    oa, s_new = gla_chunked(qa, ka, va, log_a, s0)
    oa = rmsnorm(oa, g_norm) * jax.nn.silu(ga.reshape(B, T, H_A, DV_A))
    qb = qb.reshape(B, T, H_B, 1, DH_B) * (DH_B ** -0.5)
    kb = kb.reshape(B, T, H_B, DH_B)
    vb = vb.reshape(B, T, H_B, DH_B)
    if k_cache is None:
        bias = rel_bias(rel_tab, CHUNK, (B_PREV + 1) * CHUNK, B_PREV * CHUNK)
        ob = band_attention(qb, kb, vb, B_PREV, bias, None)
        keep = min(B_PREV * CHUNK, T)
        k_new = kb[:, T - keep:]
        v_new = vb[:, T - keep:]
    else:
        L = k_cache.shape[1]
        k_all = jnp.concatenate([k_cache, kb], axis=1)
        v_all = jnp.concatenate([v_cache, vb], axis=1)
        ob = cached_attention(qb, k_all, v_all, rel_bias(rel_tab, T, L + T, L), None)
        k_new = k_all[:, T:]
        v_new = v_all[:, T:]
    o = jnp.concatenate([oa.reshape(B, T, -1), ob.reshape(B, T, -1)], axis=-1)
    return o @ w_out, s_new, k_new, v_new


def swa_mixer(h, w_in, b_in, sinks, w_out, pos0, k_cache, v_cache):
    B, T, _ = h.shape
    z = h @ w_in + b_in
    q, k, v = jnp.split(z, _split_points(C_SIZES), axis=-1)
    pos = pos0 + jnp.arange(T)
    q = rope_partial(q.reshape(B, T, KV_C, G_C, DH_C), pos) * (DH_C ** -0.5)
    k = rope_partial(k.reshape(B, T, KV_C, DH_C), pos)
    v = v.reshape(B, T, KV_C, DH_C)
    sink = sinks.reshape(KV_C, G_C).astype(jnp.float32)
    if k_cache is None:
        o = band_attention(q, k, v, C_PREV, None, sink)
        keep = min(WINDOW, T)
        k_new = k[:, T - keep:]
        v_new = v[:, T - keep:]
    else:
        k_all = jnp.concatenate([k_cache, k], axis=1)
        v_all = jnp.concatenate([v_cache, v], axis=1)
        o = cached_attention(q, k_all, v_all, None, sink)
        k_new = k_all[:, T:]
        v_new = v_all[:, T:]
    return o.reshape(B, T, -1) @ w_out, k_new, v_new


def hier_moe(x, w_grp, w_exp, w_gate, w_up, w_down):
    f32 = jnp.float32
    grp_logits = (x @ w_grp).astype(f32)
    grp_prob = jax.nn.softmax(grp_logits, axis=-1)
    g_sel = jnp.argmax(grp_logits, axis=-1)
    g_oh = jax.nn.one_hot(g_sel, N_GROUPS, dtype=f32)
    p_grp = jnp.sum(grp_prob * g_oh, axis=-1)
    exp_logits = jnp.einsum('nd,dge->nge', x, w_exp).astype(f32)
    exp_logits = jnp.take_along_axis(exp_logits, g_sel[:, None, None], axis=1)[:, 0]
    top_v, top_i = lax.top_k(exp_logits, TOP_K_INNER)
    top_w = jax.nn.softmax(top_v, axis=-1) * p_grp[:, None]
    e_w = jnp.sum(jax.nn.one_hot(top_i, EXP_PER_GROUP, dtype=f32) * top_w[..., None], axis=1)
    gate = g_oh[:, :, None] * e_w[:, None, :]
    out = jnp.zeros(x.shape, f32)
    for g in range(N_GROUPS):
        hg = jax.nn.silu(jnp.einsum('nd,edf->nef', x, w_gate[g])) * jnp.einsum('nd,edf->nef', x, w_up[g])
        hg = hg * gate[:, g, :, None].astype(hg.dtype)
        out = out + jnp.einsum('nef,efd->nd', hg, w_down[g]).astype(f32)
    return out.astype(x.dtype)


def moe_block(y, norm_g, w_grp, w_exp, w_gate, w_up, w_down):
    B, T, D = y.shape
    h = rmsnorm(y, norm_g).reshape(B * T, D)
    return hier_moe(h, w_grp, w_exp, w_gate, w_up, w_down).reshape(B, T, D)


def setup_inputs(seed: int = 0) -> dict:
    key = jax.random.key(seed)
    ks = jax.random.split(key, 32)
    L_B = min(B_PREV * CHUNK, PAST_LEN)
    L_C = min(WINDOW, PAST_LEN)
    f32 = jnp.float32

    def nrm(i, shape, scale):
        return jax.random.normal(ks[i], shape, f32) * scale

    return {
        'x_prompt': nrm(0, (BATCH, SEQ, D_MODEL), 1.0),
        'x_sample': nrm(1, (DEC_BATCH, DEC_SEQ, D_MODEL), 1.0),
        'cache_gla_state': nrm(2, (N_AB, DEC_BATCH, H_A, DK_A, DV_A), 0.5),
        'cache_band_k': nrm(3, (N_AB, DEC_BATCH, L_B, H_B, DH_B), 1.0),
        'cache_band_v': nrm(4, (N_AB, DEC_BATCH, L_B, H_B, DH_B), 1.0),
        'cache_swa_k': nrm(5, (N_C, DEC_BATCH, L_C, KV_C, DH_C), 1.0),
        'cache_swa_v': nrm(6, (N_C, DEC_BATCH, L_C, KV_C, DH_C), 1.0),
        'norm_mix': 1.0 + nrm(7, (DEPTH, D_MODEL), 0.1),
        'norm_ffn': 1.0 + nrm(8, (DEPTH, D_MODEL), 0.1),
        'norm_final': 1.0 + nrm(9, (D_MODEL,), 0.1),
        'w_in_ab': nrm(10, (N_AB, D_MODEL, D_IN_AB), D_MODEL ** -0.5),
        'w_gla_lr': nrm(11, (N_AB, GLA_RANK, H_A * DK_A), GLA_RANK ** -0.5),
        'b_gla_lr': nrm(12, (N_AB, H_A * DK_A), 0.1),
        'gla_out_norm': 1.0 + nrm(13, (N_AB, DV_A), 0.1),
        'rel_bias_tab': nrm(14, (N_AB, H_B, 2 * REL_CLIP + 1), 0.3),
        'w_out_ab': nrm(15, (N_AB, D_MIX_AB, D_MODEL), D_MIX_AB ** -0.5),
        'w_in_c': nrm(16, (N_C, D_MODEL, D_IN_C), D_MODEL ** -0.5),
        'b_in_c': nrm(17, (N_C, D_IN_C), 0.02),
        'attn_sinks': nrm(18, (N_C, H_C), 0.5),
        'w_out_c': nrm(19, (N_C, H_C * DH_C, D_MODEL), (H_C * DH_C) ** -0.5),
        'w_router_grp': nrm(20, (DEPTH, D_MODEL, N_GROUPS), D_MODEL ** -0.5),
        'w_router_exp': nrm(21, (DEPTH, D_MODEL, N_GROUPS, EXP_PER_GROUP), D_MODEL ** -0.5),
        'w_exp_gate': nrm(22, (DEPTH, N_GROUPS, EXP_PER_GROUP, D_MODEL, D_FF_EXPERT), D_MODEL ** -0.5),
        'w_exp_up': nrm(23, (DEPTH, N_GROUPS, EXP_PER_GROUP, D_MODEL, D_FF_EXPERT), D_MODEL ** -0.5),
        'w_exp_down': nrm(24, (DEPTH, N_GROUPS, EXP_PER_GROUP, D_FF_EXPERT, D_MODEL), D_FF_EXPERT ** -0.5),
    }


def reference(x_prompt, x_sample, cache_gla_state, cache_band_k, cache_band_v, cache_swa_k, cache_swa_v,
              norm_mix, norm_ffn, norm_final, w_in_ab, w_gla_lr, b_gla_lr, gla_out_norm, rel_bias_tab,
              w_out_ab, w_in_c, b_in_c, attn_sinks, w_out_c, w_router_grp, w_router_exp, w_exp_gate,
              w_exp_up, w_exp_down):
    yp = x_prompt
    ys = x_sample
    gla_p, gla_s, bk_p, bv_p, bk_s, bv_s = [], [], [], [], [], []
    sk_p, sv_p, sk_s, sv_s = [], [], [], []
    for l in range(DEPTH):
        hp = rmsnorm(yp, norm_mix[l])
        hs = rmsnorm(ys, norm_mix[l])
        i = l // 2
        if l % 2 == 0:
            mp, s_p, k_p, v_p = ab_mixer(hp, w_in_ab[i], w_gla_lr[i], b_gla_lr[i], gla_out_norm[i],
                                         rel_bias_tab[i], w_out_ab[i], None, None, None)
            ms, s_s, k_s, v_s = ab_mixer(hs, w_in_ab[i], w_gla_lr[i], b_gla_lr[i], gla_out_norm[i],
                                         rel_bias_tab[i], w_out_ab[i], cache_gla_state[i],
                                         cache_band_k[i], cache_band_v[i])
            gla_p.append(s_p)
            gla_s.append(s_s)
            bk_p.append(k_p)
            bv_p.append(v_p)
            bk_s.append(k_s)
            bv_s.append(v_s)
        else:
            mp, k_p, v_p = swa_mixer(hp, w_in_c[i], b_in_c[i], attn_sinks[i], w_out_c[i], 0, None, None)
            ms, k_s, v_s = swa_mixer(hs, w_in_c[i], b_in_c[i], attn_sinks[i], w_out_c[i], PAST_LEN,
                                     cache_swa_k[i], cache_swa_v[i])
            sk_p.append(k_p)
            sv_p.append(v_p)
            sk_s.append(k_s)
            sv_s.append(v_s)
        yp = yp + mp
        ys = ys + ms
        yp = yp + moe_block(yp, norm_ffn[l], w_router_grp[l], w_router_exp[l], w_exp_gate[l], w_exp_up[l], w_exp_down[l])
        ys = ys + moe_block(ys, norm_ffn[l], w_router_grp[l], w_router_exp[l], w_exp_gate[l], w_exp_up[l], w_exp_down[l])
    y_prompt = rmsnorm(yp, norm_final)
    y_sample = rmsnorm(ys, norm_final)
    return (y_prompt, y_sample, jnp.stack(gla_p), jnp.stack(gla_s), jnp.stack(bk_p), jnp.stack(bv_p),
            jnp.stack(bk_s), jnp.stack(bv_s), jnp.stack(sk_p), jnp.stack(sv_p), jnp.stack(sk_s), jnp.stack(sv_s))
```

```python
import functools

import numpy as np
import jax
import jax.numpy as jnp
from jax import lax
from jax.experimental import pallas as pl
from jax.experimental.pallas import tpu as pltpu

f32 = jnp.float32
bf16 = jnp.bfloat16

CHUNK = 64
EPS = 1e-6
NEG_INF = -1e30
PAST_LEN = 1024
H_A = 8
DK_A = 64
DV_A = 128
GLA_RANK = 16
GLA_TAU = 16.0
H_B = 8
DH_B = 128
B_PREV = 8
REL_CLIP = 128
H_C = 32
KV_C = 4
G_C = H_C // KV_C
DH_C = 64
WINDOW = 128
ROT_DIM = DH_C // 4
ROPE_THETA = 500000.0
N_GROUPS = 4
EXP_PER_GROUP = 8
N_EXPERTS = N_GROUPS * EXP_PER_GROUP

LANES = 128
VMEM_LIMIT = 52 * 1024 * 1024

_QA0, _KA0, _VA0, _GA0 = 0, 512, 1024, 2048
_QB0, _KB0, _VB0, _LA0 = 3072, 4096, 5120, 6144
_NC_AB = 6272
_QC0, _KC0, _VC0 = 0, 2048, 2560
_NC_C = 3072


def _cp(sem):
    return pltpu.CompilerParams(dimension_semantics=sem, vmem_limit_bytes=VMEM_LIMIT)


def _dot(a, b):
    return jnp.dot(a, b, preferred_element_type=f32)


def _dot_nt(a, b):
    return lax.dot_general(a, b, (((1,), (1,)), ((), ())), preferred_element_type=f32)


def _dot_tn(a, b):
    return lax.dot_general(a, b, (((0,), (0,)), ((), ())), preferred_element_type=f32)


def _rms(x, g):
    ms = jnp.mean(x * x, axis=-1, keepdims=True)
    return (x * lax.rsqrt(ms + EPS)) * g


def _silu(x):
    return x / (1.0 + jnp.exp(-x))


def _proj_kernel(*refs, n_first, has_add, has_bias, n_rope):
    it = iter(refs)
    ya_ref = next(it)
    yb_ref = next(it) if n_first is not None else None
    add_ref = next(it) if has_add else None
    g_ref = next(it)
    w_ref = next(it)
    b_ref = next(it) if has_bias else None
    if n_rope:
        c_ref, s1_ref, s2_ref = next(it), next(it), next(it)
    ysum_ref = next(it) if has_add else None
    z_ref = next(it)
    h_ref = next(it)
    i = pl.program_id(0)
    j = pl.program_id(1)

    @pl.when(j == 0)
    def _():
        x = ya_ref[...]
        if n_first is not None:
            x = jnp.where(i < n_first, x, yb_ref[...])
        if has_add:
            x = x + add_ref[...]
            ysum_ref[...] = x
        h_ref[...] = _rms(x, g_ref[...]).astype(bf16)

    acc = _dot(h_ref[...], w_ref[...])
    if has_bias:
        acc = acc + b_ref[...]
    if n_rope:
        tn = acc.shape[1]

        @pl.when(j < n_rope)
        def _():
            rep = tn // LANES
            c = jnp.tile(c_ref[...], (1, rep))
            s1 = jnp.tile(s1_ref[...], (1, rep))
            s2 = jnp.tile(s2_ref[...], (1, rep))
            half = ROT_DIM // 2
            lo = pltpu.roll(acc, half, axis=1)
            hi = pltpu.roll(acc, tn - half, axis=1)
            z_ref[...] = (acc * c + lo * s1 + hi * s2).astype(z_ref.dtype)

        @pl.when(j >= n_rope)
        def _():
            z_ref[...] = acc.astype(z_ref.dtype)
    else:
        z_ref[...] = acc.astype(z_ref.dtype)


def _proj(ya, g, w, *, tm, tn, yb=None, add=None, bias=None, rope=None, n_rope=0):
    d = ya.shape[1]
    na = ya.shape[0] // tm
    nb = 0 if yb is None else yb.shape[0] // tm
    n = (na + nb) * tm
    nc = w.shape[1]
    grid = (na + nb, nc // tn)
    in_specs = []
    args = []
    if yb is None:
        in_specs.append(pl.BlockSpec((tm, d), lambda i, j: (i, 0)))
        args.append(ya)
    else:
        in_specs.append(pl.BlockSpec((tm, d), lambda i, j: (jnp.minimum(i, na - 1), 0)))
        in_specs.append(pl.BlockSpec((tm, d), lambda i, j: (jnp.maximum(i - na, 0), 0)))
        args += [ya, yb]
    if add is not None:
        in_specs.append(pl.BlockSpec((tm, d), lambda i, j: (i, 0)))
        args.append(add)
    in_specs.append(pl.BlockSpec((1, d), lambda i, j: (0, 0)))
    args.append(g.reshape(1, d))
    in_specs.append(pl.BlockSpec((d, tn), lambda i, j: (0, j)))
    args.append(w)
    if bias is not None:
        in_specs.append(pl.BlockSpec((1, tn), lambda i, j: (0, j)))
        args.append(bias.reshape(1, nc))
    if n_rope:
        for t in rope:
            in_specs.append(pl.BlockSpec((tm, LANES), lambda i, j: (i, 0)))
            args.append(t)
    out_shape = []
    out_specs = []
    if add is not None:
        out_shape.append(jax.ShapeDtypeStruct((n, d), f32))
        out_specs.append(pl.BlockSpec((tm, d), lambda i, j: (i, 0)))
    out_shape.append(jax.ShapeDtypeStruct((n, nc), bf16))
    out_specs.append(pl.BlockSpec((tm, tn), lambda i, j: (i, j)))
    kern = functools.partial(_proj_kernel, n_first=(na if yb is not None else None),
                             has_add=add is not None, has_bias=bias is not None, n_rope=n_rope)
    out = pl.pallas_call(
        kern, grid=grid, in_specs=in_specs, out_specs=out_specs, out_shape=out_shape,
        scratch_shapes=[pltpu.VMEM((tm, d), bf16)],
        compiler_params=_cp(("arbitrary", "arbitrary")), name="proj")(*args)
    return out if add is not None else out[0]


def _gla_kernel(q_ref, k_ref, v_ref, g_ref, la_ref, wlr_ref, blr_ref, gn_ref, s0_ref,
                o_ref, sfin_ref, st_ref, *, C, CB):
    s = pl.program_id(1)

    @pl.when(s == 0)
    def _():
        st_ref[...] = s0_ref[0]

    row = lax.broadcasted_iota(jnp.int32, (C, C), 0)
    col = lax.broadcasted_iota(jnp.int32, (C, C), 1)
    tril = row >= col
    tri = jnp.where(tril, 1.0, 0.0).astype(bf16)
    lane = lax.broadcasted_iota(jnp.int32, (C, LANES), 1)
    even = lane < DK_A
    srow = lax.broadcasted_iota(jnp.int32, (2 * DV_A, LANES), 0)
    slane = lax.broadcasted_iota(jnp.int32, (2 * DV_A, LANES), 1)
    blockdiag = (srow < DV_A) == (slane < DK_A)
    wlr = wlr_ref[...]
    blr = blr_ref[...]
    gn = gn_ref[...]

    def chunk(c, carry):
        rows = pl.ds(pl.multiple_of(c * C, C), C)
        pre = _dot(la_ref[rows, :], wlr) + blr
        loga = -(jnp.maximum(-pre, 0.0) + jnp.log1p(jnp.exp(-jnp.abs(pre)))) * (1.0 / GLA_TAU)
        hi = loga.astype(bf16)
        lo = (loga - hi.astype(f32)).astype(bf16)
        b = _dot(tri, hi) + _dot(tri, lo)
        b_last = b[C - 1:C, :]
        qf = q_ref[rows, :].astype(f32) * (DK_A ** -0.5)
        kf = k_ref[rows, :].astype(f32)
        q_dec = (qf * jnp.exp(b)).astype(bf16)
        k_inv = (kf * jnp.exp(-b)).astype(bf16)
        k_end = (kf * jnp.exp(b_last - b)).astype(bf16)
        dec = jnp.exp(b_last)
        for p in range(H_A // 2):
            sl = slice(p * LANES, (p + 1) * LANES)
            qp = q_dec[:, sl]
            kp = k_inv[:, sl]
            vp = v_ref[rows, p * 2 * DV_A:(p + 1) * 2 * DV_A]
            zero = jnp.zeros_like(qp)
            att_e = jnp.where(tril, _dot_nt(jnp.where(even, qp, zero), kp), 0.0).astype(bf16)
            att_o = jnp.where(tril, _dot_nt(jnp.where(even, zero, qp), kp), 0.0).astype(bf16)
            o_e = _dot(att_e, vp)[:, :DV_A]
            o_o = _dot(att_o, vp)[:, DV_A:]
            st = st_ref[p]
            o_inter = _dot_nt(qp, st.astype(bf16))
            ds = _dot_tn(vp, k_end[:, sl])
            st_ref[p] = dec[:, sl] * st + jnp.where(blockdiag, ds, 0.0)
            for half, o_intra in ((0, o_e), (1, o_o)):
                h = 2 * p + half
                hs = slice(h * DV_A, (h + 1) * DV_A)
                o = o_intra + o_inter[:, half * DV_A:(half + 1) * DV_A]
                gate = g_ref[rows, hs].astype(f32)
                o_ref[rows, hs] = (_rms(o, gn) * _silu(gate)).astype(o_ref.dtype)
        return carry

    lax.fori_loop(0, CB, chunk, 0)

    @pl.when(s == pl.num_programs(1) - 1)
    def _():
        sfin_ref[0] = st_ref[...]


def _state_to_kernel_layout(s0):
    bsz = s0.shape[0]
    st = jnp.swapaxes(s0, -1, -2).reshape(bsz, H_A // 2, 2, DV_A, DK_A)
    z = jnp.zeros_like(st[:, :, 0])
    top = jnp.concatenate([st[:, :, 0], z], axis=-1)
    bot = jnp.concatenate([z, st[:, :, 1]], axis=-1)
    return jnp.concatenate([top, bot], axis=-2)


def _state_from_kernel_layout(st):
    bsz = st.shape[0]
    e = st[:, :, :DV_A, :DK_A]
    o = st[:, :, DV_A:, DK_A:]
    s = jnp.stack([e, o], axis=2).reshape(bsz, H_A, DV_A, DK_A)
    return jnp.swapaxes(s, -1, -2)


def _gla(z, w_lr, b_lr, g_norm, s0, *, bsz, t, row0, C, CB):
    rb = C * CB
    ns = t // rb
    r0 = row0 // rb

    def zspec(width, cb):
        return pl.BlockSpec((rb, width), lambda b, s: (r0 + b * ns + s, cb))

    hk = H_A * DK_A
    hv = H_A * DV_A
    st0 = _state_to_kernel_layout(s0.astype(f32))
    wl = jnp.zeros((LANES, hk), f32).at[:GLA_RANK].set(w_lr).astype(bf16)
    st_shape = (H_A // 2, 2 * DV_A, 2 * DK_A)
    o, sfin = pl.pallas_call(
        functools.partial(_gla_kernel, C=C, CB=CB),
        grid=(bsz, ns),
        in_specs=[zspec(hk, _QA0 // hk), zspec(hk, _KA0 // hk), zspec(hv, _VA0 // hv),
                  zspec(hv, _GA0 // hv), zspec(LANES, _LA0 // LANES),
                  pl.BlockSpec((LANES, hk), lambda b, s: (0, 0)),
                  pl.BlockSpec((1, hk), lambda b, s: (0, 0)),
                  pl.BlockSpec((1, DV_A), lambda b, s: (0, 0)),
                  pl.BlockSpec((1,) + st_shape, lambda b, s: (b, 0, 0, 0))],
        out_specs=[pl.BlockSpec((rb, hv), lambda b, s: (b * ns + s, 0)),
                   pl.BlockSpec((1,) + st_shape, lambda b, s: (b, 0, 0, 0))],
        out_shape=[jax.ShapeDtypeStruct((bsz * t, hv), bf16),
                   jax.ShapeDtypeStruct((bsz,) + st_shape, f32)],
        scratch_shapes=[pltpu.VMEM(st_shape, f32)],
        compiler_params=_cp(("arbitrary", "arbitrary")), name="gla")(
            z, z, z, z, z, wl, b_lr.reshape(1, hk).astype(f32), g_norm.reshape(1, DV_A).astype(f32), st0)
    return o, _state_from_kernel_layout(sfin)


def _band_kernel(q_ref, ka_ref, kb_ref, kc_ref, va_ref, vb_ref, vc_ref, ba_ref, bb_ref, bc_ref,
                 o_ref, *, mask_first):
    qi = pl.program_id(1)
    for h in range(H_B):
        sl = slice(h * DH_B, (h + 1) * DH_B)
        qh = (q_ref[:, sl].astype(f32) * (DH_B ** -0.5)).astype(bf16)
        sa = _dot_nt(qh, ka_ref[:, sl].astype(bf16)) + ba_ref[h]
        sb = _dot_nt(qh, kb_ref[:, sl].astype(bf16)) + bb_ref[h]
        sc = _dot_nt(qh, kc_ref[:, sl].astype(bf16)) + bc_ref[h]
        if mask_first:
            sa = jnp.where(qi >= 2, sa, NEG_INF)
            sb = jnp.where(qi >= 1, sb, NEG_INF)
        m = jnp.maximum(jnp.maximum(jnp.max(sa, axis=-1, keepdims=True),
                                    jnp.max(sb, axis=-1, keepdims=True)),
                        jnp.max(sc, axis=-1, keepdims=True))
        ea = jnp.exp(sa - m)
        eb = jnp.exp(sb - m)
        ec = jnp.exp(sc - m)
        inv = 1.0 / (jnp.sum(ea, axis=-1, keepdims=True) + jnp.sum(eb, axis=-1, keepdims=True)
                     + jnp.sum(ec, axis=-1, keepdims=True))
        o = (_dot((ea * inv).astype(bf16), va_ref[:, sl].astype(bf16))
             + _dot((eb * inv).astype(bf16), vb_ref[:, sl].astype(bf16))
             + _dot((ec * inv).astype(bf16), vc_ref[:, sl].astype(bf16)))
        o_ref[:, sl] = o.astype(o_ref.dtype)


def _band_call(qkv_specs, args, biases, *, bsz, nq, qr, mask_first):
    hd = H_B * DH_B
    bias_specs = [pl.BlockSpec(b.shape, lambda b_, q_: (0, 0, 0)) for b in biases]
    return pl.pallas_call(
        functools.partial(_band_kernel, mask_first=mask_first),
        grid=(bsz, nq),
        in_specs=list(qkv_specs) + bias_specs,
        out_specs=pl.BlockSpec((qr, hd), lambda b, q: (b * nq + q, 0)),
        out_shape=jax.ShapeDtypeStruct((bsz * nq * qr, hd), bf16),
        compiler_params=_cp(("arbitrary", "arbitrary")), name="band")(*args, *biases)


def _band_bias_prompt(tab, qr, kb):
    nqc = qr // CHUNK
    tot = 2 * kb + qr
    qrow = np.arange(qr)[:, None]
    kcol = np.arange(tot)[None, :]
    qc = qrow // CHUNK + (2 * kb) // CHUNK
    kc = kcol // CHUNK
    valid = (kc <= qc) & (kc >= qc - B_PREV)
    w = (kc - (qc - B_PREV)) * CHUNK + kcol % CHUNK
    rel = qrow % CHUNK - w + B_PREV * CHUNK
    idx = np.clip(rel, -REL_CLIP, REL_CLIP) + REL_CLIP
    bias = jnp.where(jnp.asarray(valid)[None], tab.astype(f32)[:, jnp.asarray(idx)], NEG_INF)
    del nqc
    return bias[:, :, :kb], bias[:, :, kb:2 * kb], bias[:, :, 2 * kb:]


def _band_bias_sample(tab, t, lc):
    rel = np.arange(t)[:, None] - np.arange(lc + t)[None, :] + lc
    idx = np.clip(rel, -REL_CLIP, REL_CLIP) + REL_CLIP
    bias = tab.astype(f32)[:, jnp.asarray(idx)]
    return bias[:, :, :lc // 2], bias[:, :, lc // 2:lc], bias[:, :, lc:]


def _swa_kernel(sink_ref, q_ref, kp_ref, kc_ref, vp_ref, vc_ref, o_ref, *, chunk_mask):
    qi = pl.program_id(1)
    qr = q_ref.shape[0]
    pr = kp_ref.shape[0]
    lane = lax.broadcasted_iota(jnp.int32, (qr, LANES), 1)
    even = lane < DH_C
    if chunk_mask:
        qc = lax.broadcasted_iota(jnp.int32, (qr, pr), 0) // CHUNK
        kc = lax.broadcasted_iota(jnp.int32, (qr, pr), 1) // CHUNK
        ok_prev = (kc >= qc) & (qi >= 1)
        qc = lax.broadcasted_iota(jnp.int32, (qr, qr), 0) // CHUNK
        kc = lax.broadcasted_iota(jnp.int32, (qr, qr), 1) // CHUNK
        ok_cur = kc <= qc
    for c in range(KV_C):
        ks = slice(c * LANES, (c + 1) * LANES)
        k_prev = kp_ref[:, ks].astype(bf16)
        k_cur = kc_ref[:, ks].astype(bf16)
        v_prev = vp_ref[:, ks].astype(bf16)
        v_cur = vc_ref[:, ks].astype(bf16)
        for p in range(G_C // 2):
            qs = slice(c * G_C * DH_C + p * LANES, c * G_C * DH_C + (p + 1) * LANES)
            qp = (q_ref[:, qs].astype(f32) * (DH_C ** -0.5)).astype(bf16)
            zero = jnp.zeros_like(qp)
            outs = []
            for par in range(2):
                sink = sink_ref[c * G_C + 2 * p + par]
                qm = jnp.where(even, qp, zero) if par == 0 else jnp.where(even, zero, qp)
                s_prev = _dot_nt(qm, k_prev)
                s_cur = _dot_nt(qm, k_cur)
                if chunk_mask:
                    s_prev = jnp.where(ok_prev, s_prev, NEG_INF)
                    s_cur = jnp.where(ok_cur, s_cur, NEG_INF)
                m = jnp.maximum(jnp.maximum(jnp.max(s_prev, axis=-1, keepdims=True),
                                            jnp.max(s_cur, axis=-1, keepdims=True)), sink)
                e_prev = jnp.exp(s_prev - m)
                e_cur = jnp.exp(s_cur - m)
                inv = 1.0 / (jnp.sum(e_prev, axis=-1, keepdims=True)
                             + jnp.sum(e_cur, axis=-1, keepdims=True) + jnp.exp(sink - m))
                outs.append(_dot((e_prev * inv).astype(bf16), v_prev)
                            + _dot((e_cur * inv).astype(bf16), v_cur))
            o_ref[:, qs] = jnp.where(even, outs[0], outs[1]).astype(o_ref.dtype)


def _swa_call(specs, args, sinks, *, bsz, nq, qr, chunk_mask):
    hd = H_C * DH_C
    return pl.pallas_call(
        functools.partial(_swa_kernel, chunk_mask=chunk_mask),
        grid=(bsz, nq),
        in_specs=[pl.BlockSpec(memory_space=pltpu.SMEM)] + list(specs),
        out_specs=pl.BlockSpec((qr, hd), lambda b, q: (b * nq + q, 0)),
        out_shape=jax.ShapeDtypeStruct((bsz * nq * qr, hd), bf16),
        compiler_params=_cp(("arbitrary", "arbitrary")), name="swa")(sinks, *args)


def _out_kernel(*refs, n_parts, n_first):
    y_ref = refs[0]
    o_refs = refs[1:1 + 2 * n_parts]
    w_refs = refs[1 + 2 * n_parts:1 + 3 * n_parts]
    g_ref, rh_ref, rl_ref = refs[1 + 3 * n_parts:4 + 3 * n_parts]
    y1_ref, h2_ref, route_ref = refs[4 + 3 * n_parts:]
    i = pl.program_id(0)
    acc = y_ref[...]
    for k in range(n_parts):
        lhs = jnp.where(i < n_first, o_refs[2 * k][...], o_refs[2 * k + 1][...])
        acc = acc + _dot(lhs, w_refs[k][...])
    y1_ref[...] = acc
    h2 = _rms(acc, g_ref[...])
    hi = h2.astype(bf16)
    h2_ref[...] = hi
    lo = (h2 - hi.astype(f32)).astype(bf16)
    lg = _dot(hi, rh_ref[...]) + _dot(hi, rl_ref[...]) + _dot(lo, rh_ref[...])

    lane_i = lax.broadcasted_iota(jnp.int32, lg.shape, 1)
    lane = lane_i.astype(f32)
    big = float(LANES)
    ninf = -jnp.inf
    is_g = lane_i < N_GROUPS
    gmax = jnp.max(jnp.where(is_g, lg, ninf), axis=-1, keepdims=True)
    gsel = jnp.min(jnp.where(is_g & (lg == gmax), lane, big), axis=-1, keepdims=True)
    pg = 1.0 / jnp.sum(jnp.where(is_g, jnp.exp(lg - gmax), 0.0), axis=-1, keepdims=True)
    e0 = N_GROUPS + EXP_PER_GROUP * gsel
    is_e = (lane >= e0) & (lane < e0 + EXP_PER_GROUP)
    v1 = jnp.max(jnp.where(is_e, lg, ninf), axis=-1, keepdims=True)
    i1 = jnp.min(jnp.where(is_e & (lg == v1), lane, big), axis=-1, keepdims=True)
    is_e2 = is_e & (lane != i1)
    v2 = jnp.max(jnp.where(is_e2, lg, ninf), axis=-1, keepdims=True)
    i2 = jnp.min(jnp.where(is_e2 & (lg == v2), lane, big), axis=-1, keepdims=True)
    t = jnp.exp(v2 - v1)
    w1 = pg / (1.0 + t)
    w2 = pg * t / (1.0 + t)
    route_ref[...] = jnp.where(lane_i == 0, i1 - N_GROUPS,
                               jnp.where(lane_i == 1, i2 - N_GROUPS,
                                         jnp.where(lane_i == 2, w1,
                                                   jnp.where(lane_i == 3, w2, 0.0))))


def _out_proj(y, parts, g, w_router, *, tm):
    n, d = y.shape
    n_first = parts[0][0].shape[0] // tm
    n_tiles = n // tm
    in_specs = [pl.BlockSpec((tm, d), lambda i: (i, 0))]
    args = [y]
    for op, os_, _ in parts:
        dk = op.shape[1]
        in_specs.append(pl.BlockSpec((tm, dk), lambda i: (jnp.minimum(i, n_first - 1), 0)))
        in_specs.append(pl.BlockSpec((tm, dk), lambda i: (jnp.maximum(i - n_first, 0), 0)))
        args += [op, os_]
    for _, _, w in parts:
        in_specs.append(pl.BlockSpec(w.shape, lambda i: (0, 0)))
        args.append(w)
    rh = w_router.astype(bf16)
    rl = (w_router - rh.astype(f32)).astype(bf16)
    in_specs += [pl.BlockSpec((1, d), lambda i: (0, 0)),
                 pl.BlockSpec((d, LANES), lambda i: (0, 0)),
                 pl.BlockSpec((d, LANES), lambda i: (0, 0))]
    args += [g.reshape(1, d), rh, rl]
    return pl.pallas_call(
        functools.partial(_out_kernel, n_parts=len(parts), n_first=n_first),
        grid=(n_tiles,), in_specs=in_specs,
        out_specs=[pl.BlockSpec((tm, d), lambda i: (i, 0)),
                   pl.BlockSpec((tm, d), lambda i: (i, 0)),
                   pl.BlockSpec((tm, LANES), lambda i: (i, 0))],
        out_shape=[jax.ShapeDtypeStruct((n, d), f32),
                   jax.ShapeDtypeStruct((n, d), bf16),
                   jax.ShapeDtypeStruct((n, LANES), f32)],
        compiler_params=_cp(("arbitrary",)), name="out_proj")(*args)


def _moe_kernel(te_ref, tv_ref, x_ref, rw_ref, wg_ref, wu_ref, wd_ref, y_ref):
    t = pl.program_id(0)

    @pl.when(tv_ref[t] != 0)
    def _():
        x = x_ref[...]
        g = _dot(x, wg_ref[0])
        u = _dot(x, wu_ref[0])
        h = ((_silu(g) * u) * rw_ref[...]).astype(bf16)
        y_ref[...] = _dot(h, wd_ref[0])

    @pl.when(tv_ref[t] == 0)
    def _():
        y_ref[...] = jnp.zeros_like(y_ref)


def _moe_experts(xs, row_w, tile_e, tile_v, wg, wu, wd, *, tm):
    r, d = xs.shape
    ff = wg.shape[2]
    grid_spec = pltpu.PrefetchScalarGridSpec(
        num_scalar_prefetch=2, grid=(r // tm,),
        in_specs=[pl.BlockSpec((tm, d), lambda t, te, tv: (t, 0)),
                  pl.BlockSpec((tm, 1), lambda t, te, tv: (t, 0)),
                  pl.BlockSpec((1, d, ff), lambda t, te, tv: (te[t], 0, 0)),
                  pl.BlockSpec((1, d, ff), lambda t, te, tv: (te[t], 0, 0)),
                  pl.BlockSpec((1, ff, d), lambda t, te, tv: (te[t], 0, 0))],
        out_specs=pl.BlockSpec((tm, d), lambda t, te, tv: (t, 0)))
    return pl.pallas_call(
        _moe_kernel, grid_spec=grid_spec,
        out_shape=jax.ShapeDtypeStruct((r, d), f32),
        compiler_params=_cp(("arbitrary",)), name="moe")(tile_e, tile_v, xs, row_w, wg, wu, wd)


def _route_meta(route, tm, n_tiles):
    n = route.shape[0]
    e = route[:, 0:2].astype(jnp.int32).reshape(-1)
    w = route[:, 2:4].reshape(-1)
    oh = (e[:, None] == jnp.arange(N_EXPERTS, dtype=jnp.int32)[None, :]).astype(jnp.int32)
    cs = jnp.cumsum(oh, axis=0)
    rank = jnp.sum(cs * oh, axis=1) - 1
    counts = cs[-1]
    padded = ((counts + tm - 1) // tm) * tm
    ends = jnp.cumsum(padded)
    starts = ends - padded
    pos = starts[e] + rank
    tok = jnp.arange(2 * n, dtype=jnp.int32) // 2
    row_tok = jnp.zeros((n_tiles * tm,), jnp.int32).at[pos].set(tok)
    row_w = jnp.zeros((n_tiles * tm,), f32).at[pos].set(w)
    tile_start = jnp.arange(n_tiles, dtype=jnp.int32) * tm
    tile_e = jnp.sum((tile_start[:, None] >= ends[None, :]).astype(jnp.int32), axis=1)
    tile_v = (tile_start < ends[-1]).astype(jnp.int32)
    tile_e = jnp.minimum(tile_e, N_EXPERTS - 1)
    return row_tok, row_w, tile_e, tile_v, pos.reshape(n, 2)


def _moe(h2, route, wg, wu, wd, *, tm):
    n = h2.shape[0]
    n_tiles = (2 * n) // tm + N_EXPERTS
    row_tok, row_w, tile_e, tile_v, pos = _route_meta(route, tm, n_tiles)
    xs = jnp.take(h2, row_tok, axis=0)
    ys = _moe_experts(xs, row_w.reshape(-1, 1), tile_e, tile_v, wg, wu, wd, tm=tm)
    return jnp.take(ys, pos[:, 0], axis=0) + jnp.take(ys, pos[:, 1], axis=0)


def _final_kernel(y_ref, a_ref, g_ref, o_ref):
    o_ref[...] = _rms(y_ref[...] + a_ref[...], g_ref[...])


def _final(y, add, g, *, tm, row0, rows):
    d = y.shape[1]
    r0 = row0 // tm
    return pl.pallas_call(
        _final_kernel, grid=(rows // tm,),
        in_specs=[pl.BlockSpec((tm, d), lambda i: (r0 + i, 0)),
                  pl.BlockSpec((tm, d), lambda i: (r0 + i, 0)),
                  pl.BlockSpec((1, d), lambda i: (0, 0))],
        out_specs=pl.BlockSpec((tm, d), lambda i: (i, 0)),
        out_shape=jax.ShapeDtypeStruct((rows, d), f32),
        compiler_params=_cp(("arbitrary",)), name="final_norm")(y, add, g.reshape(1, d))


def _rope_tables(pos):
    half = ROT_DIM // 2
    inv = jnp.power(ROPE_THETA, -jnp.arange(half, dtype=f32) * 2.0 / ROT_DIM)
    ang = pos.astype(f32)[:, None] * inv[None, :]
    cos = jnp.cos(ang)
    sin = jnp.sin(ang)
    n = pos.shape[0]
    one = jnp.ones((n, DH_C - ROT_DIM), f32)
    zero = jnp.zeros((n, DH_C - ROT_DIM), f32)
    zh = jnp.zeros((n, half), f32)
    c = jnp.concatenate([cos, cos, one], axis=1)
    s1 = jnp.concatenate([zh, sin, zero], axis=1)
    s2 = jnp.concatenate([-sin, zh, zero], axis=1)
    rep = LANES // DH_C
    return tuple(jnp.tile(t, (1, rep)) for t in (c, s1, s2))


def _dup_heads(a):
    sh = a.shape[:-1]
    a4 = a.reshape(sh + (KV_C, 1, DH_C))
    return jnp.broadcast_to(a4, sh + (KV_C, 2, DH_C)).reshape(sh + (KV_C * 2 * DH_C,))


def kernel(x_prompt, x_sample, cache_gla_state, cache_band_k, cache_band_v, cache_swa_k, cache_swa_v,
           norm_mix, norm_ffn, norm_final, w_in_ab, w_gla_lr, b_gla_lr, gla_out_norm, rel_bias_tab,
           w_out_ab, w_in_c, b_in_c, attn_sinks, w_out_c, w_router_grp, w_router_exp, w_exp_gate,
           w_exp_up, w_exp_down):
    bp, sp, d = x_prompt.shape
    bs, ts, _ = x_sample.shape
    n_p = bp * sp
    n_s = bs * ts
    n = n_p + n_s
    tm = 512 if n_s % 512 == 0 else n_s
    tmo = min(tm, 256)
    tme = 256 if (2 * n) % 256 == 0 else 64
    ff = w_exp_gate.shape[-1]

    xp = x_prompt.reshape(n_p, d)
    xs_ = x_sample.reshape(n_s, d)

    def router_w(l):
        wr = jnp.concatenate([w_router_grp[l], w_router_exp[l].reshape(d, N_EXPERTS)], axis=1)
        return jnp.zeros((d, LANES), f32).at[:, :N_GROUPS + N_EXPERTS].set(wr)

    def expert_w(l):
        return (w_exp_gate[l].reshape(N_EXPERTS, d, ff).astype(bf16),
                w_exp_up[l].reshape(N_EXPERTS, d, ff).astype(bf16),
                w_exp_down[l].reshape(N_EXPERTS, ff, d).astype(bf16))

    w0 = w_in_ab[0]
    la0 = 2 * H_A * DK_A + 2 * H_A * DV_A
    w_cat = jnp.concatenate(
        [w0[:, :la0], w0[:, la0 + GLA_RANK:], w0[:, la0:la0 + GLA_RANK],
         jnp.zeros((d, LANES - GLA_RANK), f32)], axis=1).astype(bf16)
    x_all = jnp.concatenate([xp, xs_], axis=0)
    z = _proj(x_all, norm_mix[0], w_cat, tm=tm, tn=896)

    c_s = min(CHUNK, ts)
    oa_p, gla_p = _gla(z, w_gla_lr[0], b_gla_lr[0], gla_out_norm[0],
                       jnp.zeros((bp, H_A, DK_A, DV_A), f32), bsz=bp, t=sp, row0=0, C=CHUNK, CB=4)
    oa_s, gla_s = _gla(z, w_gla_lr[0], b_gla_lr[0], gla_out_norm[0], cache_gla_state[0],
                       bsz=bs, t=ts, row0=n_p, C=c_s, CB=ts // c_s)

    hd = H_B * DH_B
    qr = 256
    nq = sp // qr
    qcb, kcb, vcb = _QB0 // hd, _KB0 // hd, _VB0 // hd
    cur = lambda cb: pl.BlockSpec((qr, hd), lambda b, q: (b * nq + q, cb))
    prev = lambda cb, back: pl.BlockSpec((qr, hd), lambda b, q: (b * nq + jnp.maximum(q - back, 0), cb))
    ob_p = _band_call(
        [cur(qcb), prev(kcb, 2), prev(kcb, 1), cur(kcb), prev(vcb, 2), prev(vcb, 1), cur(vcb)],
        [z] * 7, _band_bias_prompt(rel_bias_tab[0], qr, qr), bsz=bp, nq=nq, qr=qr, mask_first=True)

    lb = cache_band_k.shape[2]
    ck = cache_band_k[0].reshape(bs * lb, hd)
    cv = cache_band_v[0].reshape(bs * lb, hd)
    rs0 = n_p // ts
    new = lambda cb: pl.BlockSpec((ts, hd), lambda b, q: (rs0 + b, cb))
    cpiece = lambda k: pl.BlockSpec((lb // 2, hd), lambda b, q: (2 * b + k, 0))
    ob_s = _band_call(
        [new(qcb), cpiece(0), cpiece(1), new(kcb), cpiece(0), cpiece(1), new(vcb)],
        [z, ck, ck, z, cv, cv, z], _band_bias_sample(rel_bias_tab[0], ts, lb),
        bsz=bs, nq=1, qr=ts, mask_first=False)

    dmix_a = H_A * DV_A
    wo = w_out_ab[0].astype(bf16)
    y1, h2, route = _out_proj(
        x_all, [(oa_p, oa_s, wo[:dmix_a]), (ob_p, ob_s, wo[dmix_a:])], norm_ffn[0], router_w(0), tm=tmo)
    moe0 = _moe(h2, route, *expert_w(0), tm=tme)

    keep = min(B_PREV * CHUNK, sp)
    zp3 = z[:n_p].reshape(bp, sp, _NC_AB)[:, sp - keep:]
    band_k_p = zp3[:, :, _KB0:_KB0 + hd].astype(f32).reshape(bp, keep, H_B, DH_B)[None]
    band_v_p = zp3[:, :, _VB0:_VB0 + hd].astype(f32).reshape(bp, keep, H_B, DH_B)[None]
    kb_s = z[n_p:, _KB0:_KB0 + hd].astype(f32).reshape(bs, ts, H_B, DH_B)
    vb_s = z[n_p:, _VB0:_VB0 + hd].astype(f32).reshape(bs, ts, H_B, DH_B)
    band_k_s = jnp.concatenate([cache_band_k[0], kb_s], axis=1)[:, ts:][None]
    band_v_s = jnp.concatenate([cache_band_v[0], vb_s], axis=1)[:, ts:][None]

    w1 = w_in_c[0]
    nq_c = H_C * DH_C
    nkv = KV_C * DH_C
    w1_cat = jnp.concatenate([w1[:, :nq_c], _dup_heads(w1[:, nq_c:nq_c + nkv]),
                              _dup_heads(w1[:, nq_c + nkv:])], axis=1).astype(bf16)
    b1 = b_in_c[0]
    b1_cat = jnp.concatenate([b1[:nq_c], _dup_heads(b1[nq_c:nq_c + nkv]), _dup_heads(b1[nq_c + nkv:])])
    pos = jnp.concatenate([jnp.tile(jnp.arange(sp), bp), jnp.tile(PAST_LEN + jnp.arange(ts), bs)])
    y2, z1 = _proj(y1, norm_mix[1], w1_cat, tm=tm, tn=512, add=moe0, bias=b1_cat,
                   rope=_rope_tables(pos), n_rope=(nq_c + 2 * nkv) // 512)

    qr_c = WINDOW
    nq1 = sp // qr_c
    kw = 2 * nkv
    kcb1, vcb1 = _KC0 // kw, _VC0 // kw
    qspec = pl.BlockSpec((qr_c, nq_c), lambda b, q: (b * nq1 + q, 0))
    cur1 = lambda cb: pl.BlockSpec((qr_c, kw), lambda b, q: (b * nq1 + q, cb))
    prev1 = lambda cb: pl.BlockSpec((qr_c, kw), lambda b, q: (b * nq1 + jnp.maximum(q - 1, 0), cb))
    sinks = attn_sinks[0].astype(f32)
    oc_p = _swa_call([qspec, prev1(kcb1), cur1(kcb1), prev1(vcb1), cur1(vcb1)], [z1] * 5, sinks,
                     bsz=bp, nq=nq1, qr=qr_c, chunk_mask=True)

    lc = cache_swa_k.shape[2]
    csk = _dup_heads(cache_swa_k[0].reshape(bs * lc, nkv))
    csv = _dup_heads(cache_swa_v[0].reshape(bs * lc, nkv))
    qspec_s = pl.BlockSpec((ts, nq_c), lambda b, q: (rs0 + b, 0))
    new1 = lambda cb: pl.BlockSpec((ts, kw), lambda b, q: (rs0 + b, cb))
    cspec = pl.BlockSpec((lc, kw), lambda b, q: (b, 0))
    oc_s = _swa_call([qspec_s, cspec, new1(kcb1), cspec, new1(vcb1)], [z1, csk, z1, csv, z1], sinks,
                     bsz=bs, nq=1, qr=ts, chunk_mask=False)

    y3, h4, route1 = _out_proj(y2, [(oc_p, oc_s, w_out_c[0].astype(bf16))], norm_ffn[1], router_w(1),
                               tm=tmo)
    moe1 = _moe(h4, route1, *expert_w(1), tm=tme)

    keep1 = min(WINDOW, sp)
    z1p = z1[:n_p].reshape(bp, sp, _NC_C)[:, sp - keep1:]

    def undup(a):
        return a.astype(f32).reshape(a.shape[:-1] + (KV_C, 2, DH_C))[..., 0, :]

    swa_k_p = undup(z1p[:, :, _KC0:_KC0 + kw])[None]
    swa_v_p = undup(z1p[:, :, _VC0:_VC0 + kw])[None]
    k1_s = undup(z1[n_p:, _KC0:_KC0 + kw]).reshape(bs, ts, KV_C, DH_C)
    v1_s = undup(z1[n_p:, _VC0:_VC0 + kw]).reshape(bs, ts, KV_C, DH_C)
    swa_k_s = jnp.concatenate([cache_swa_k[0], k1_s], axis=1)[:, ts:][None]
    swa_v_s = jnp.concatenate([cache_swa_v[0], v1_s], axis=1)[:, ts:][None]

    y_prompt = _final(y3, moe1, norm_final, tm=tm, row0=0, rows=n_p).reshape(bp, sp, d)
    y_sample = _final(y3, moe1, norm_final, tm=tm, row0=n_p, rows=n_s).reshape(bs, ts, d)

    return (y_prompt, y_sample, gla_p[None], gla_s[None].astype(cache_gla_state.dtype),
            band_k_p, band_v_p, band_k_s, band_v_s, swa_k_p, swa_v_p, swa_k_s, swa_v_s)
```

```python
import functools

import numpy as np
import jax
import jax.numpy as jnp
from jax import lax
from jax.experimental import pallas as pl
from jax.experimental.pallas import tpu as pltpu

f32 = jnp.float32
bf16 = jnp.bfloat16

CHUNK = 64
EPS = 1e-6
NEG_INF = -1e30
PAST_LEN = 1024
H_A = 8
DK_A = 64
DV_A = 128
GLA_RANK = 16
GLA_TAU = 16.0
H_B = 8
DH_B = 128
B_PREV = 8
REL_CLIP = 128
H_C = 32
KV_C = 4
G_C = H_C // KV_C
DH_C = 64
WINDOW = 128
ROT_DIM = DH_C // 4
ROPE_THETA = 500000.0
N_GROUPS = 4
EXP_PER_GROUP = 8
N_EXPERTS = N_GROUPS * EXP_PER_GROUP

LANES = 128
VMEM_LIMIT = 52 * 1024 * 1024

_QA0, _KA0, _VA0, _GA0 = 0, 512, 1024, 2048
_QB0, _KB0, _VB0, _LA0 = 3072, 4096, 5120, 6144
_NC_AB = 6272
_QC0, _KC0, _VC0 = 0, 2048, 2560
_NC_C = 3072


def _cp(sem):
    return pltpu.CompilerParams(dimension_semantics=sem, vmem_limit_bytes=VMEM_LIMIT)


def _dot(a, b):
    return jnp.dot(a, b, preferred_element_type=f32)


def _dot_nt(a, b):
    return lax.dot_general(a, b, (((1,), (1,)), ((), ())), preferred_element_type=f32)


def _dot_tn(a, b):
    return lax.dot_general(a, b, (((0,), (0,)), ((), ())), preferred_element_type=f32)


def _rms(x, g):
    ms = jnp.mean(x * x, axis=-1, keepdims=True)
    return (x * lax.rsqrt(ms + EPS)) * g


def _silu(x):
    return x / (1.0 + jnp.exp(-x))


def _proj_kernel(*refs, n_first, has_add, has_bias, n_rope):
    it = iter(refs)
    ya_ref = next(it)
    yb_ref = next(it) if n_first is not None else None
    add_ref = next(it) if has_add else None
    g_ref = next(it)
    w_ref = next(it)
    b_ref = next(it) if has_bias else None
    if n_rope:
        c_ref, s1_ref, s2_ref = next(it), next(it), next(it)
    ysum_ref = next(it) if has_add else None
    z_ref = next(it)
    h_ref = next(it)
    i = pl.program_id(0)
    j = pl.program_id(1)

    @pl.when(j == 0)
    def _():
        x = ya_ref[...]
        if n_first is not None:
            x = jnp.where(i < n_first, x, yb_ref[...])
        if has_add:
            x = x + add_ref[...]
            ysum_ref[...] = x
        h_ref[...] = _rms(x, g_ref[...]).astype(bf16)

    acc = _dot(h_ref[...], w_ref[...])
    if has_bias:
        acc = acc + b_ref[...]
    if n_rope:
        tn = acc.shape[1]

        @pl.when(j < n_rope)
        def _():
            rep = tn // LANES
            c = jnp.tile(c_ref[...], (1, rep))
            s1 = jnp.tile(s1_ref[...], (1, rep))
            s2 = jnp.tile(s2_ref[...], (1, rep))
            half = ROT_DIM // 2
            lo = pltpu.roll(acc, half, axis=1)
            hi = pltpu.roll(acc, tn - half, axis=1)
            z_ref[...] = (acc * c + lo * s1 + hi * s2).astype(z_ref.dtype)

        @pl.when(j >= n_rope)
        def _():
            z_ref[...] = acc.astype(z_ref.dtype)
    else:
        z_ref[...] = acc.astype(z_ref.dtype)


def _proj(ya, g, w, *, tm, tn, yb=None, add=None, bias=None, rope=None, n_rope=0):
    d = ya.shape[1]
    na = ya.shape[0] // tm
    nb = 0 if yb is None else yb.shape[0] // tm
    n = (na + nb) * tm
    nc = w.shape[1]
    grid = (na + nb, nc // tn)
    in_specs = []
    args = []
    if yb is None:
        in_specs.append(pl.BlockSpec((tm, d), lambda i, j: (i, 0)))
        args.append(ya)
    else:
        in_specs.append(pl.BlockSpec((tm, d), lambda i, j: (jnp.minimum(i, na - 1), 0)))
        in_specs.append(pl.BlockSpec((tm, d), lambda i, j: (jnp.maximum(i - na, 0), 0)))
        args += [ya, yb]
    if add is not None:
        in_specs.append(pl.BlockSpec((tm, d), lambda i, j: (i, 0)))
        args.append(add)
    in_specs.append(pl.BlockSpec((1, d), lambda i, j: (0, 0)))
    args.append(g.reshape(1, d))
    in_specs.append(pl.BlockSpec((d, tn), lambda i, j: (0, j)))
    args.append(w)
    if bias is not None:
        in_specs.append(pl.BlockSpec((1, tn), lambda i, j: (0, j)))
        args.append(bias.reshape(1, nc))
    if n_rope:
        for t in rope:
            in_specs.append(pl.BlockSpec((tm, LANES), lambda i, j: (i, 0)))
            args.append(t)
    out_shape = []
    out_specs = []
    if add is not None:
        out_shape.append(jax.ShapeDtypeStruct((n, d), f32))
        out_specs.append(pl.BlockSpec((tm, d), lambda i, j: (i, 0)))
    out_shape.append(jax.ShapeDtypeStruct((n, nc), bf16))
    out_specs.append(pl.BlockSpec((tm, tn), lambda i, j: (i, j)))
    kern = functools.partial(_proj_kernel, n_first=(na if yb is not None else None),
                             has_add=add is not None, has_bias=bias is not None, n_rope=n_rope)
    out = pl.pallas_call(
        kern, grid=grid, in_specs=in_specs, out_specs=out_specs, out_shape=out_shape,
        scratch_shapes=[pltpu.VMEM((tm, d), bf16)],
        compiler_params=_cp(("arbitrary", "arbitrary")), name="proj")(*args)
    return out if add is not None else out[0]


def _gla_kernel(q_ref, k_ref, v_ref, g_ref, la_ref, wlr_ref, blr_ref, gn_ref, s0_ref,
                o_ref, sfin_ref, st_ref, *, C, CB):
    s = pl.program_id(1)

    @pl.when(s == 0)
    def _():
        st_ref[...] = s0_ref[0]

    row = lax.broadcasted_iota(jnp.int32, (C, C), 0)
    col = lax.broadcasted_iota(jnp.int32, (C, C), 1)
    tril = row >= col
    tri = jnp.where(tril, 1.0, 0.0).astype(bf16)
    lane = lax.broadcasted_iota(jnp.int32, (C, LANES), 1)
    even = lane < DK_A
    srow = lax.broadcasted_iota(jnp.int32, (2 * DV_A, LANES), 0)
    slane = lax.broadcasted_iota(jnp.int32, (2 * DV_A, LANES), 1)
    blockdiag = (srow < DV_A) == (slane < DK_A)
    wlr = wlr_ref[...]
    blr = blr_ref[...]
    gn = gn_ref[...]

    def chunk(c, carry):
        rows = pl.ds(pl.multiple_of(c * C, C), C)
        pre = _dot(la_ref[rows, :], wlr) + blr
        loga = -(jnp.maximum(-pre, 0.0) + jnp.log1p(jnp.exp(-jnp.abs(pre)))) * (1.0 / GLA_TAU)
        hi = loga.astype(bf16)
        lo = (loga - hi.astype(f32)).astype(bf16)
        b = _dot(tri, hi) + _dot(tri, lo)
        b_last = b[C - 1:C, :]
        qf = q_ref[rows, :].astype(f32) * (DK_A ** -0.5)
        kf = k_ref[rows, :].astype(f32)
        q_dec = (qf * jnp.exp(b)).astype(bf16)
        k_inv = (kf * jnp.exp(-b)).astype(bf16)
        k_end = (kf * jnp.exp(b_last - b)).astype(bf16)
        dec = jnp.exp(b_last)
        for p in range(H_A // 2):
            sl = slice(p * LANES, (p + 1) * LANES)
            qp = q_dec[:, sl]
            kp = k_inv[:, sl]
            vp = v_ref[rows, p * 2 * DV_A:(p + 1) * 2 * DV_A]
            zero = jnp.zeros_like(qp)
            att_e = jnp.where(tril, _dot_nt(jnp.where(even, qp, zero), kp), 0.0).astype(bf16)
            att_o = jnp.where(tril, _dot_nt(jnp.where(even, zero, qp), kp), 0.0).astype(bf16)
            o_e = _dot(att_e, vp)[:, :DV_A]
            o_o = _dot(att_o, vp)[:, DV_A:]
            st = st_ref[p]
            o_inter = _dot_nt(qp, st.astype(bf16))
            ds = _dot_tn(vp, k_end[:, sl])
            st_ref[p] = dec[:, sl] * st + jnp.where(blockdiag, ds, 0.0)
            for half, o_intra in ((0, o_e), (1, o_o)):
                h = 2 * p + half
                hs = slice(h * DV_A, (h + 1) * DV_A)
                o = o_intra + o_inter[:, half * DV_A:(half + 1) * DV_A]
                gate = g_ref[rows, hs].astype(f32)
                o_ref[rows, hs] = (_rms(o, gn) * _silu(gate)).astype(o_ref.dtype)
        return carry

    lax.fori_loop(0, CB, chunk, 0)

    @pl.when(s == pl.num_programs(1) - 1)
    def _():
        sfin_ref[0] = st_ref[...]


def _state_to_kernel_layout(s0):
    bsz = s0.shape[0]
    st = jnp.swapaxes(s0, -1, -2).reshape(bsz, H_A // 2, 2, DV_A, DK_A)
    z = jnp.zeros_like(st[:, :, 0])
    top = jnp.concatenate([st[:, :, 0], z], axis=-1)
    bot = jnp.concatenate([z, st[:, :, 1]], axis=-1)
    return jnp.concatenate([top, bot], axis=-2)


def _state_from_kernel_layout(st):
    bsz = st.shape[0]
    e = st[:, :, :DV_A, :DK_A]
    o = st[:, :, DV_A:, DK_A:]
    s = jnp.stack([e, o], axis=2).reshape(bsz, H_A, DV_A, DK_A)
    return jnp.swapaxes(s, -1, -2)


def _gla(z, w_lr, b_lr, g_norm, s0, *, bsz, t, row0, C, CB):
    rb = C * CB
    ns = t // rb
    r0 = row0 // rb

    def zspec(width, cb):
        return pl.BlockSpec((rb, width), lambda b, s: (r0 + b * ns + s, cb))

    hk = H_A * DK_A
    hv = H_A * DV_A
    st0 = _state_to_kernel_layout(s0.astype(f32))
    wl = jnp.zeros((LANES, hk), f32).at[:GLA_RANK].set(w_lr).astype(bf16)
    st_shape = (H_A // 2, 2 * DV_A, 2 * DK_A)
    o, sfin = pl.pallas_call(
        functools.partial(_gla_kernel, C=C, CB=CB),
        grid=(bsz, ns),
        in_specs=[zspec(hk, _QA0 // hk), zspec(hk, _KA0 // hk), zspec(hv, _VA0 // hv),
                  zspec(hv, _GA0 // hv), zspec(LANES, _LA0 // LANES),
                  pl.BlockSpec((LANES, hk), lambda b, s: (0, 0)),
                  pl.BlockSpec((1, hk), lambda b, s: (0, 0)),
                  pl.BlockSpec((1, DV_A), lambda b, s: (0, 0)),
                  pl.BlockSpec((1,) + st_shape, lambda b, s: (b, 0, 0, 0))],
        out_specs=[pl.BlockSpec((rb, hv), lambda b, s: (b * ns + s, 0)),
                   pl.BlockSpec((1,) + st_shape, lambda b, s: (b, 0, 0, 0))],
        out_shape=[jax.ShapeDtypeStruct((bsz * t, hv), bf16),
                   jax.ShapeDtypeStruct((bsz,) + st_shape, f32)],
        scratch_shapes=[pltpu.VMEM(st_shape, f32)],
        compiler_params=_cp(("arbitrary", "arbitrary")), name="gla")(
            z, z, z, z, z, wl, b_lr.reshape(1, hk).astype(f32), g_norm.reshape(1, DV_A).astype(f32), st0)
    return o, _state_from_kernel_layout(sfin)


def _band_kernel(q_ref, ka_ref, kb_ref, kc_ref, va_ref, vb_ref, vc_ref, ba_ref, bb_ref, bc_ref,
                 o_ref, *, mask_first):
    qi = pl.program_id(1)
    for h in range(H_B):
        sl = slice(h * DH_B, (h + 1) * DH_B)
        qh = (q_ref[:, sl].astype(f32) * (DH_B ** -0.5)).astype(bf16)
        sa = _dot_nt(qh, ka_ref[:, sl].astype(bf16)) + ba_ref[h]
        sb = _dot_nt(qh, kb_ref[:, sl].astype(bf16)) + bb_ref[h]
        sc = _dot_nt(qh, kc_ref[:, sl].astype(bf16)) + bc_ref[h]
        if mask_first:
            sa = jnp.where(qi >= 2, sa, NEG_INF)
            sb = jnp.where(qi >= 1, sb, NEG_INF)
        m = jnp.maximum(jnp.maximum(jnp.max(sa, axis=-1, keepdims=True),
                                    jnp.max(sb, axis=-1, keepdims=True)),
                        jnp.max(sc, axis=-1, keepdims=True))
        ea = jnp.exp(sa - m)
        eb = jnp.exp(sb - m)
        ec = jnp.exp(sc - m)
        inv = 1.0 / (jnp.sum(ea, axis=-1, keepdims=True) + jnp.sum(eb, axis=-1, keepdims=True)
                     + jnp.sum(ec, axis=-1, keepdims=True))
        o = (_dot((ea * inv).astype(bf16), va_ref[:, sl].astype(bf16))
             + _dot((eb * inv).astype(bf16), vb_ref[:, sl].astype(bf16))
             + _dot((ec * inv).astype(bf16), vc_ref[:, sl].astype(bf16)))
        o_ref[:, sl] = o.astype(o_ref.dtype)


def _band_call(qkv_specs, args, biases, *, bsz, nq, qr, mask_first):
    hd = H_B * DH_B
    bias_specs = [pl.BlockSpec(b.shape, lambda b_, q_: (0, 0, 0)) for b in biases]
    return pl.pallas_call(
        functools.partial(_band_kernel, mask_first=mask_first),
        grid=(bsz, nq),
        in_specs=list(qkv_specs) + bias_specs,
        out_specs=pl.BlockSpec((qr, hd), lambda b, q: (b * nq + q, 0)),
        out_shape=jax.ShapeDtypeStruct((bsz * nq * qr, hd), bf16),
        compiler_params=_cp(("arbitrary", "arbitrary")), name="band")(*args, *biases)


def _band_bias_prompt(tab, qr, kb):
    nqc = qr // CHUNK
    tot = 2 * kb + qr
    qrow = np.arange(qr)[:, None]
    kcol = np.arange(tot)[None, :]
    qc = qrow // CHUNK + (2 * kb) // CHUNK
    kc = kcol // CHUNK
    valid = (kc <= qc) & (kc >= qc - B_PREV)
    w = (kc - (qc - B_PREV)) * CHUNK + kcol % CHUNK
    rel = qrow % CHUNK - w + B_PREV * CHUNK
    idx = np.clip(rel, -REL_CLIP, REL_CLIP) + REL_CLIP
    bias = jnp.where(jnp.asarray(valid)[None], tab.astype(f32)[:, jnp.asarray(idx)], NEG_INF)
    del nqc
    return bias[:, :, :kb], bias[:, :, kb:2 * kb], bias[:, :, 2 * kb:]


def _band_bias_sample(tab, t, lc):
    rel = np.arange(t)[:, None] - np.arange(lc + t)[None, :] + lc
    idx = np.clip(rel, -REL_CLIP, REL_CLIP) + REL_CLIP
    bias = tab.astype(f32)[:, jnp.asarray(idx)]
    return bias[:, :, :lc // 2], bias[:, :, lc // 2:lc], bias[:, :, lc:]


def _swa_kernel(sink_ref, q_ref, kp_ref, kc_ref, vp_ref, vc_ref, o_ref, qs_ref, *, chunk_mask):
    qi = pl.program_id(1)
    qr = q_ref.shape[0]
    pr = kp_ref.shape[0]
    nl = G_C * LANES
    even = lax.broadcasted_iota(jnp.int32, (qr, LANES), 1) < DH_C
    top = lax.broadcasted_iota(jnp.int32, (LANES, LANES), 0) < DH_C
    if chunk_mask:
        kch = lax.broadcasted_iota(jnp.int32, (pr, nl), 0) // CHUNK
        qch = (lax.broadcasted_iota(jnp.int32, (pr, nl), 1) % LANES) // CHUNK
        ok_prev = (kch >= qch) & (qi >= 1)
        kch = lax.broadcasted_iota(jnp.int32, (qr, nl), 0) // CHUNK
        qch = (lax.broadcasted_iota(jnp.int32, (qr, nl), 1) % LANES) // CHUNK
        ok_cur = kch <= qch
    if qr < LANES:
        qs_ref[...] = jnp.zeros_like(qs_ref)
    for c in range(KV_C):
        ks = slice(c * LANES, (c + 1) * LANES)
        for p in range(G_C // 2):
            cols = slice(c * G_C * DH_C + p * LANES, c * G_C * DH_C + (p + 1) * LANES)
            qp = (q_ref[:, cols].astype(f32) * (DH_C ** -0.5)).astype(bf16)
            zero = jnp.zeros_like(qp)
            qs_ref[2 * p * LANES:2 * p * LANES + qr, :] = jnp.where(even, qp, zero)
            qs_ref[(2 * p + 1) * LANES:(2 * p + 1) * LANES + qr, :] = jnp.where(even, zero, qp)
        q_all = qs_ref[...]
        sink = jnp.concatenate(
            [jnp.full((1, LANES), sink_ref[c * G_C + j], f32) for j in range(G_C)], axis=1)
        st_prev = _dot_nt(kp_ref[:, ks].astype(bf16), q_all)
        st_cur = _dot_nt(kc_ref[:, ks].astype(bf16), q_all)
        if chunk_mask:
            st_prev = jnp.where(ok_prev, st_prev, NEG_INF)
            st_cur = jnp.where(ok_cur, st_cur, NEG_INF)
        m = jnp.maximum(jnp.maximum(jnp.max(st_prev, axis=0, keepdims=True),
                                    jnp.max(st_cur, axis=0, keepdims=True)), sink)
        e_prev = jnp.exp(st_prev - m)
        e_cur = jnp.exp(st_cur - m)
        inv = 1.0 / (jnp.sum(e_prev, axis=0, keepdims=True) + jnp.sum(e_cur, axis=0, keepdims=True)
                     + jnp.exp(sink - m))
        ot = (_dot_tn(vp_ref[:, ks].astype(bf16), (e_prev * inv).astype(bf16))
              + _dot_tn(vc_ref[:, ks].astype(bf16), (e_cur * inv).astype(bf16)))
        for p in range(G_C // 2):
            cols = slice(c * G_C * DH_C + p * LANES, c * G_C * DH_C + (p + 1) * LANES)
            blk = jnp.where(top, ot[:, 2 * p * LANES:(2 * p + 1) * LANES],
                            ot[:, (2 * p + 1) * LANES:(2 * p + 2) * LANES])
            o_ref[:, cols] = blk.T[:qr].astype(o_ref.dtype)


def _swa_call(specs, args, sinks, *, bsz, nq, qr, chunk_mask):
    hd = H_C * DH_C
    return pl.pallas_call(
        functools.partial(_swa_kernel, chunk_mask=chunk_mask),
        grid=(bsz, nq),
        in_specs=[pl.BlockSpec(memory_space=pltpu.SMEM)] + list(specs),
        out_specs=pl.BlockSpec((qr, hd), lambda b, q: (b * nq + q, 0)),
        out_shape=jax.ShapeDtypeStruct((bsz * nq * qr, hd), bf16),
        scratch_shapes=[pltpu.VMEM((G_C * LANES, LANES), bf16)],
        compiler_params=_cp(("arbitrary", "arbitrary")), name="swa")(sinks, *args)


def _out_kernel(*refs, n_parts, n_first):
    y_ref = refs[0]
    o_refs = refs[1:1 + 2 * n_parts]
    w_refs = refs[1 + 2 * n_parts:1 + 3 * n_parts]
    g_ref, rh_ref, rl_ref = refs[1 + 3 * n_parts:4 + 3 * n_parts]
    y1_ref, h2_ref, route_ref = refs[4 + 3 * n_parts:]
    i = pl.program_id(0)
    acc = y_ref[...]
    for k in range(n_parts):
        lhs = jnp.where(i < n_first, o_refs[2 * k][...], o_refs[2 * k + 1][...])
        acc = acc + _dot(lhs, w_refs[k][...])
    y1_ref[...] = acc
    h2 = _rms(acc, g_ref[...])
    hi = h2.astype(bf16)
    h2_ref[...] = hi
    lo = (h2 - hi.astype(f32)).astype(bf16)
    lg = _dot(hi, rh_ref[...]) + _dot(hi, rl_ref[...]) + _dot(lo, rh_ref[...])

    lane_i = lax.broadcasted_iota(jnp.int32, lg.shape, 1)
    lane = lane_i.astype(f32)
    big = float(LANES)
    ninf = -jnp.inf
    is_g = lane_i < N_GROUPS
    gmax = jnp.max(jnp.where(is_g, lg, ninf), axis=-1, keepdims=True)
    gsel = jnp.min(jnp.where(is_g & (lg == gmax), lane, big), axis=-1, keepdims=True)
    pg = 1.0 / jnp.sum(jnp.where(is_g, jnp.exp(lg - gmax), 0.0), axis=-1, keepdims=True)
    e0 = N_GROUPS + EXP_PER_GROUP * gsel
    is_e = (lane >= e0) & (lane < e0 + EXP_PER_GROUP)
    v1 = jnp.max(jnp.where(is_e, lg, ninf), axis=-1, keepdims=True)
    i1 = jnp.min(jnp.where(is_e & (lg == v1), lane, big), axis=-1, keepdims=True)
    is_e2 = is_e & (lane != i1)
    v2 = jnp.max(jnp.where(is_e2, lg, ninf), axis=-1, keepdims=True)
    i2 = jnp.min(jnp.where(is_e2 & (lg == v2), lane, big), axis=-1, keepdims=True)
    t = jnp.exp(v2 - v1)
    w1 = pg / (1.0 + t)
    w2 = pg * t / (1.0 + t)
    route_ref[...] = jnp.where(lane_i == 0, i1 - N_GROUPS,
                               jnp.where(lane_i == 1, i2 - N_GROUPS,
                                         jnp.where(lane_i == 2, w1,
                                                   jnp.where(lane_i == 3, w2, 0.0))))


def _out_proj(y, parts, g, w_router, *, tm):
    n, d = y.shape
    n_first = parts[0][0].shape[0] // tm
    n_tiles = n // tm
    in_specs = [pl.BlockSpec((tm, d), lambda i: (i, 0))]
    args = [y]
    for op, os_, _ in parts:
        dk = op.shape[1]
        in_specs.append(pl.BlockSpec((tm, dk), lambda i: (jnp.minimum(i, n_first - 1), 0)))
        in_specs.append(pl.BlockSpec((tm, dk), lambda i: (jnp.maximum(i - n_first, 0), 0)))
        args += [op, os_]
    for _, _, w in parts:
        in_specs.append(pl.BlockSpec(w.shape, lambda i: (0, 0)))
        args.append(w)
    rh = w_router.astype(bf16)
    rl = (w_router - rh.astype(f32)).astype(bf16)
    in_specs += [pl.BlockSpec((1, d), lambda i: (0, 0)),
                 pl.BlockSpec((d, LANES), lambda i: (0, 0)),
                 pl.BlockSpec((d, LANES), lambda i: (0, 0))]
    args += [g.reshape(1, d), rh, rl]
    return pl.pallas_call(
        functools.partial(_out_kernel, n_parts=len(parts), n_first=n_first),
        grid=(n_tiles,), in_specs=in_specs,
        out_specs=[pl.BlockSpec((tm, d), lambda i: (i, 0)),
                   pl.BlockSpec((tm, d), lambda i: (i, 0)),
                   pl.BlockSpec((tm, LANES), lambda i: (i, 0))],
        out_shape=[jax.ShapeDtypeStruct((n, d), f32),
                   jax.ShapeDtypeStruct((n, d), bf16),
                   jax.ShapeDtypeStruct((n, LANES), f32)],
        compiler_params=_cp(("arbitrary",)), name="out_proj")(*args)


def _moe_kernel(te_ref, tv_ref, x_ref, wg_ref, wu_ref, wd_ref, y_ref, wg_s, wu_s, wd_s):
    t = pl.program_id(0)
    valid = tv_ref[t] != 0
    new_expert = (t == 0) | (te_ref[t] != te_ref[jnp.maximum(t - 1, 0)])

    @pl.when(valid & new_expert)
    def _():
        wg_s[...] = wg_ref[0].astype(bf16)
        wu_s[...] = wu_ref[0].astype(bf16)
        wd_s[...] = wd_ref[0].astype(bf16)

    @pl.when(valid)
    def _():
        x = x_ref[...]
        g = _dot(x, wg_s[...])
        u = _dot(x, wu_s[...])
        y_ref[...] = _dot((_silu(g) * u).astype(bf16), wd_s[...])

    @pl.when(jnp.logical_not(valid))
    def _():
        y_ref[...] = jnp.zeros_like(y_ref)


def _moe_experts(xs, tile_e, tile_v, wg, wu, wd, *, tm):
    r, d = xs.shape
    ff = wg.shape[2]
    grid_spec = pltpu.PrefetchScalarGridSpec(
        num_scalar_prefetch=2, grid=(r // tm,),
        in_specs=[pl.BlockSpec((tm, d), lambda t, te, tv: (t, 0)),
                  pl.BlockSpec((1, d, ff), lambda t, te, tv: (te[t], 0, 0)),
                  pl.BlockSpec((1, d, ff), lambda t, te, tv: (te[t], 0, 0)),
                  pl.BlockSpec((1, ff, d), lambda t, te, tv: (te[t], 0, 0))],
        out_specs=pl.BlockSpec((tm, d), lambda t, te, tv: (t, 0)),
        scratch_shapes=[pltpu.VMEM((d, ff), bf16), pltpu.VMEM((d, ff), bf16), pltpu.VMEM((ff, d), bf16)])
    return pl.pallas_call(
        _moe_kernel, grid_spec=grid_spec,
        out_shape=jax.ShapeDtypeStruct((r, d), f32),
        compiler_params=_cp(("arbitrary",)), name="moe")(tile_e, tile_v, xs, wg, wu, wd)


def _route_meta(route, tm, n_tiles):
    n = route.shape[0]
    e = route[:, 0:2].astype(jnp.int32).reshape(-1)
    oh = (e[:, None] == jnp.arange(N_EXPERTS, dtype=jnp.int32)[None, :]).astype(jnp.int32)
    cs = jnp.cumsum(oh, axis=0)
    rank = jnp.sum(cs * oh, axis=1) - 1
    counts = cs[-1]
    padded = ((counts + tm - 1) // tm) * tm
    ends = jnp.cumsum(padded)
    starts = ends - padded
    pos = starts[e] + rank
    tile_start = jnp.arange(n_tiles, dtype=jnp.int32) * tm
    tile_e = jnp.sum((tile_start[:, None] >= ends[None, :]).astype(jnp.int32), axis=1)
    tile_v = (tile_start < ends[-1]).astype(jnp.int32)
    tile_e = jnp.minimum(tile_e, N_EXPERTS - 1)
    order = jnp.argsort(e, stable=True).astype(jnp.int32)
    rows = jnp.arange(n_tiles * tm, dtype=jnp.int32)
    row_e = jnp.repeat(tile_e, tm)
    j = rows - starts[row_e]
    first = jnp.cumsum(counts) - counts
    src = jnp.clip(first[row_e] + j, 0, 2 * n - 1)
    row_tok = jnp.where(j < counts[row_e], order[src] // 2, rows % n)
    return row_tok, tile_e, tile_v, pos.reshape(n, 2)


def _moe(h2, route, wg, wu, wd, *, tm):
    n = h2.shape[0]
    n_tiles = (2 * n) // tm + N_EXPERTS
    row_tok, tile_e, tile_v, pos = _route_meta(route, tm, n_tiles)
    xs = jnp.take(h2, row_tok, axis=0)
    ys = _moe_experts(xs, tile_e, tile_v, wg, wu, wd, tm=tm)
    return (route[:, 2:3] * jnp.take(ys, pos[:, 0], axis=0)
            + route[:, 3:4] * jnp.take(ys, pos[:, 1], axis=0))


def _final_kernel(y_ref, a_ref, g_ref, o_ref):
    o_ref[...] = _rms(y_ref[...] + a_ref[...], g_ref[...])


def _final(y, add, g, *, tm, row0, rows):
    d = y.shape[1]
    r0 = row0 // tm
    return pl.pallas_call(
        _final_kernel, grid=(rows // tm,),
        in_specs=[pl.BlockSpec((tm, d), lambda i: (r0 + i, 0)),
                  pl.BlockSpec((tm, d), lambda i: (r0 + i, 0)),
                  pl.BlockSpec((1, d), lambda i: (0, 0))],
        out_specs=pl.BlockSpec((tm, d), lambda i: (i, 0)),
        out_shape=jax.ShapeDtypeStruct((rows, d), f32),
        compiler_params=_cp(("arbitrary",)), name="final_norm")(y, add, g.reshape(1, d))


def _rope_tables(pos):
    half = ROT_DIM // 2
    inv = jnp.power(ROPE_THETA, -jnp.arange(half, dtype=f32) * 2.0 / ROT_DIM)
    ang = pos.astype(f32)[:, None] * inv[None, :]
    cos = jnp.cos(ang)
    sin = jnp.sin(ang)
    n = pos.shape[0]
    one = jnp.ones((n, DH_C - ROT_DIM), f32)
    zero = jnp.zeros((n, DH_C - ROT_DIM), f32)
    zh = jnp.zeros((n, half), f32)
    c = jnp.concatenate([cos, cos, one], axis=1)
    s1 = jnp.concatenate([zh, sin, zero], axis=1)
    s2 = jnp.concatenate([-sin, zh, zero], axis=1)
    rep = LANES // DH_C
    return tuple(jnp.tile(t, (1, rep)) for t in (c, s1, s2))


def _dup_heads(a):
    sh = a.shape[:-1]
    a4 = a.reshape(sh + (KV_C, 1, DH_C))
    return jnp.broadcast_to(a4, sh + (KV_C, 2, DH_C)).reshape(sh + (KV_C * 2 * DH_C,))


def kernel(x_prompt, x_sample, cache_gla_state, cache_band_k, cache_band_v, cache_swa_k, cache_swa_v,
           norm_mix, norm_ffn, norm_final, w_in_ab, w_gla_lr, b_gla_lr, gla_out_norm, rel_bias_tab,
           w_out_ab, w_in_c, b_in_c, attn_sinks, w_out_c, w_router_grp, w_router_exp, w_exp_gate,
           w_exp_up, w_exp_down):
    bp, sp, d = x_prompt.shape
    bs, ts, _ = x_sample.shape
    n_p = bp * sp
    n_s = bs * ts
    n = n_p + n_s
    tm = 512 if n_s % 512 == 0 else n_s
    tmo = min(tm, 256)
    tme = 256 if (2 * n) % 256 == 0 else 64
    ff = w_exp_gate.shape[-1]

    xp = x_prompt.reshape(n_p, d)
    xs_ = x_sample.reshape(n_s, d)

    def router_w(l):
        wr = jnp.concatenate([w_router_grp[l], w_router_exp[l].reshape(d, N_EXPERTS)], axis=1)
        return jnp.zeros((d, LANES), f32).at[:, :N_GROUPS + N_EXPERTS].set(wr)

    def expert_w(l):
        return (w_exp_gate[l].reshape(N_EXPERTS, d, ff), w_exp_up[l].reshape(N_EXPERTS, d, ff),
                w_exp_down[l].reshape(N_EXPERTS, ff, d))

    w0 = w_in_ab[0]
    la0 = 2 * H_A * DK_A + 2 * H_A * DV_A
    w_cat = jnp.concatenate(
        [w0[:, :la0], w0[:, la0 + GLA_RANK:], w0[:, la0:la0 + GLA_RANK],
         jnp.zeros((d, LANES - GLA_RANK), f32)], axis=1).astype(bf16)
    x_all = jnp.concatenate([xp, xs_], axis=0)
    z = _proj(x_all, norm_mix[0], w_cat, tm=tm, tn=896)

    c_s = min(CHUNK, ts)
    oa_p, gla_p = _gla(z, w_gla_lr[0], b_gla_lr[0], gla_out_norm[0],
                       jnp.zeros((bp, H_A, DK_A, DV_A), f32), bsz=bp, t=sp, row0=0, C=CHUNK, CB=4)
    oa_s, gla_s = _gla(z, w_gla_lr[0], b_gla_lr[0], gla_out_norm[0], cache_gla_state[0],
                       bsz=bs, t=ts, row0=n_p, C=c_s, CB=ts // c_s)

    hd = H_B * DH_B
    qr = 256
    nq = sp // qr
    qcb, kcb, vcb = _QB0 // hd, _KB0 // hd, _VB0 // hd
    cur = lambda cb: pl.BlockSpec((qr, hd), lambda b, q: (b * nq + q, cb))
    prev = lambda cb, back: pl.BlockSpec((qr, hd), lambda b, q: (b * nq + jnp.maximum(q - back, 0), cb))
    ob_p = _band_call(
        [cur(qcb), prev(kcb, 2), prev(kcb, 1), cur(kcb), prev(vcb, 2), prev(vcb, 1), cur(vcb)],
        [z] * 7, _band_bias_prompt(rel_bias_tab[0], qr, qr), bsz=bp, nq=nq, qr=qr, mask_first=True)

    lb = cache_band_k.shape[2]
    ck = cache_band_k[0].reshape(bs * lb, hd)
    cv = cache_band_v[0].reshape(bs * lb, hd)
    rs0 = n_p // ts
    new = lambda cb: pl.BlockSpec((ts, hd), lambda b, q: (rs0 + b, cb))
    cpiece = lambda k: pl.BlockSpec((lb // 2, hd), lambda b, q: (2 * b + k, 0))
    ob_s = _band_call(
        [new(qcb), cpiece(0), cpiece(1), new(kcb), cpiece(0), cpiece(1), new(vcb)],
        [z, ck, ck, z, cv, cv, z], _band_bias_sample(rel_bias_tab[0], ts, lb),
        bsz=bs, nq=1, qr=ts, mask_first=False)

    dmix_a = H_A * DV_A
    wo = w_out_ab[0].astype(bf16)
    y1, h2, route = _out_proj(
        x_all, [(oa_p, oa_s, wo[:dmix_a]), (ob_p, ob_s, wo[dmix_a:])], norm_ffn[0], router_w(0), tm=tmo)
    moe0 = _moe(h2, route, *expert_w(0), tm=tme)

    keep = min(B_PREV * CHUNK, sp)

    def prompt_tail(a, rows, c0, width):
        return jnp.stack([a[(b + 1) * sp - rows:(b + 1) * sp, c0:c0 + width] for b in range(bp)])

    band_k_p = prompt_tail(z, keep, _KB0, hd).astype(f32).reshape(bp, keep, H_B, DH_B)[None]
    band_v_p = prompt_tail(z, keep, _VB0, hd).astype(f32).reshape(bp, keep, H_B, DH_B)[None]
    kb_s = z[n_p:, _KB0:_KB0 + hd].astype(f32).reshape(bs, ts, H_B, DH_B)
    vb_s = z[n_p:, _VB0:_VB0 + hd].astype(f32).reshape(bs, ts, H_B, DH_B)
    band_k_s = jnp.concatenate([cache_band_k[0], kb_s], axis=1)[:, ts:][None]
    band_v_s = jnp.concatenate([cache_band_v[0], vb_s], axis=1)[:, ts:][None]

    w1 = w_in_c[0]
    nq_c = H_C * DH_C
    nkv = KV_C * DH_C
    w1_cat = jnp.concatenate([w1[:, :nq_c], _dup_heads(w1[:, nq_c:nq_c + nkv]),
                              _dup_heads(w1[:, nq_c + nkv:])], axis=1).astype(bf16)
    b1 = b_in_c[0]
    b1_cat = jnp.concatenate([b1[:nq_c], _dup_heads(b1[nq_c:nq_c + nkv]), _dup_heads(b1[nq_c + nkv:])])
    pos = jnp.concatenate([jnp.tile(jnp.arange(sp), bp), jnp.tile(PAST_LEN + jnp.arange(ts), bs)])
    y2, z1 = _proj(y1, norm_mix[1], w1_cat, tm=tm, tn=512, add=moe0, bias=b1_cat,
                   rope=_rope_tables(pos), n_rope=(nq_c + 2 * nkv) // 512)

    qr_c = WINDOW
    nq1 = sp // qr_c
    kw = 2 * nkv
    kcb1, vcb1 = _KC0 // kw, _VC0 // kw
    qspec = pl.BlockSpec((qr_c, nq_c), lambda b, q: (b * nq1 + q, 0))
    cur1 = lambda cb: pl.BlockSpec((qr_c, kw), lambda b, q: (b * nq1 + q, cb))
    prev1 = lambda cb: pl.BlockSpec((qr_c, kw), lambda b, q: (b * nq1 + jnp.maximum(q - 1, 0), cb))
    sinks = attn_sinks[0].astype(f32)
    oc_p = _swa_call([qspec, prev1(kcb1), cur1(kcb1), prev1(vcb1), cur1(vcb1)], [z1] * 5, sinks,
                     bsz=bp, nq=nq1, qr=qr_c, chunk_mask=True)

    lc = cache_swa_k.shape[2]
    csk = _dup_heads(cache_swa_k[0].reshape(bs * lc, nkv))
    csv = _dup_heads(cache_swa_v[0].reshape(bs * lc, nkv))
    qspec_s = pl.BlockSpec((ts, nq_c), lambda b, q: (rs0 + b, 0))
    new1 = lambda cb: pl.BlockSpec((ts, kw), lambda b, q: (rs0 + b, cb))
    cspec = pl.BlockSpec((lc, kw), lambda b, q: (b, 0))
    oc_s = _swa_call([qspec_s, cspec, new1(kcb1), cspec, new1(vcb1)], [z1, csk, z1, csv, z1], sinks,
                     bsz=bs, nq=1, qr=ts, chunk_mask=False)

    y3, h4, route1 = _out_proj(y2, [(oc_p, oc_s, w_out_c[0].astype(bf16))], norm_ffn[1], router_w(1),
                               tm=tmo)
    moe1 = _moe(h4, route1, *expert_w(1), tm=tme)

    keep1 = min(WINDOW, sp)

    def undup(a):
        return a.astype(f32).reshape(a.shape[:-1] + (KV_C, 2, DH_C))[..., 0, :]

    swa_k_p = undup(prompt_tail(z1, keep1, _KC0, kw))[None]
    swa_v_p = undup(prompt_tail(z1, keep1, _VC0, kw))[None]
    k1_s = undup(z1[n_p:, _KC0:_KC0 + kw]).reshape(bs, ts, KV_C, DH_C)
    v1_s = undup(z1[n_p:, _VC0:_VC0 + kw]).reshape(bs, ts, KV_C, DH_C)
    swa_k_s = jnp.concatenate([cache_swa_k[0], k1_s], axis=1)[:, ts:][None]
    swa_v_s = jnp.concatenate([cache_swa_v[0], v1_s], axis=1)[:, ts:][None]

    y_prompt = _final(y3, moe1, norm_final, tm=tm, row0=0, rows=n_p).reshape(bp, sp, d)
    y_sample = _final(y3, moe1, norm_final, tm=tm, row0=n_p, rows=n_s).reshape(bs, ts, d)

    return (y_prompt, y_sample, gla_p[None], gla_s[None].astype(cache_gla_state.dtype),
            band_k_p, band_v_p, band_k_s, band_v_s, swa_k_p, swa_v_p, swa_k_s, swa_v_s)
```

```python
import functools

import numpy as np
import jax
import jax.numpy as jnp
from jax import lax
from jax.experimental import pallas as pl
from jax.experimental.pallas import tpu as pltpu

f32 = jnp.float32
bf16 = jnp.bfloat16

CHUNK = 64
EPS = 1e-6
NEG_INF = -1e30
PAST_LEN = 1024
H_A = 8
DK_A = 64
DV_A = 128
GLA_RANK = 16
GLA_TAU = 16.0
H_B = 8
DH_B = 128
B_PREV = 8
REL_CLIP = 128
H_C = 32
KV_C = 4
G_C = H_C // KV_C
DH_C = 64
WINDOW = 128
ROT_DIM = DH_C // 4
ROPE_THETA = 500000.0
N_GROUPS = 4
EXP_PER_GROUP = 8
N_EXPERTS = N_GROUPS * EXP_PER_GROUP

LANES = 128
VMEM_LIMIT = 52 * 1024 * 1024

_QA0, _KA0, _VA0, _GA0 = 0, 512, 1024, 2048
_QB0, _KB0, _VB0, _LA0 = 3072, 4096, 5120, 6144
_NC_AB = 6272
_QC0, _KC0, _VC0 = 0, 2048, 2560
_NC_C = 3072


def _cp(sem):
    return pltpu.CompilerParams(dimension_semantics=sem, vmem_limit_bytes=VMEM_LIMIT)


def _dot(a, b):
    return jnp.dot(a, b, preferred_element_type=f32)


def _dot_nt(a, b):
    return lax.dot_general(a, b, (((1,), (1,)), ((), ())), preferred_element_type=f32)


def _dot_tn(a, b):
    return lax.dot_general(a, b, (((0,), (0,)), ((), ())), preferred_element_type=f32)


def _rms(x, g):
    ms = jnp.mean(x * x, axis=-1, keepdims=True)
    return (x * lax.rsqrt(ms + EPS)) * g


def _silu(x):
    return x / (1.0 + jnp.exp(-x))


def _moe_combine(m0_ref, m1_ref, route_ref):
    rt = route_ref[...]
    return rt[:, 2:3] * m0_ref[...] + rt[:, 3:4] * m1_ref[...]


def _proj_kernel(*refs, n_first, has_add, has_bias, n_rope):
    it = iter(refs)
    ya_ref = next(it)
    yb_ref = next(it) if n_first is not None else None
    if has_add:
        m0_ref, m1_ref, rt_ref = next(it), next(it), next(it)
    g_ref = next(it)
    w_ref = next(it)
    b_ref = next(it) if has_bias else None
    if n_rope:
        c_ref, s1_ref, s2_ref = next(it), next(it), next(it)
    ysum_ref = next(it) if has_add else None
    z_ref = next(it)
    h_ref = next(it)
    i = pl.program_id(0)
    j = pl.program_id(1)

    @pl.when(j == 0)
    def _():
        x = ya_ref[...]
        if n_first is not None:
            x = jnp.where(i < n_first, x, yb_ref[...])
        if has_add:
            x = x + _moe_combine(m0_ref, m1_ref, rt_ref)
            ysum_ref[...] = x
        h_ref[...] = _rms(x, g_ref[...]).astype(bf16)

    acc = _dot(h_ref[...], w_ref[...])
    if has_bias:
        acc = acc + b_ref[...]
    if n_rope:
        tn = acc.shape[1]

        @pl.when(j < n_rope)
        def _():
            rep = tn // LANES
            c = jnp.tile(c_ref[...], (1, rep))
            s1 = jnp.tile(s1_ref[...], (1, rep))
            s2 = jnp.tile(s2_ref[...], (1, rep))
            half = ROT_DIM // 2
            lo = pltpu.roll(acc, half, axis=1)
            hi = pltpu.roll(acc, tn - half, axis=1)
            z_ref[...] = (acc * c + lo * s1 + hi * s2).astype(z_ref.dtype)

        @pl.when(j >= n_rope)
        def _():
            z_ref[...] = acc.astype(z_ref.dtype)
    else:
        z_ref[...] = acc.astype(z_ref.dtype)


def _proj(ya, g, w, *, tm, tn, yb=None, add=None, bias=None, rope=None, n_rope=0):
    d = ya.shape[1]
    na = ya.shape[0] // tm
    nb = 0 if yb is None else yb.shape[0] // tm
    n = (na + nb) * tm
    nc = w.shape[1]
    grid = (na + nb, nc // tn)
    in_specs = []
    args = []
    if yb is None:
        in_specs.append(pl.BlockSpec((tm, d), lambda i, j: (i, 0)))
        args.append(ya)
    else:
        in_specs.append(pl.BlockSpec((tm, d), lambda i, j: (jnp.minimum(i, na - 1), 0)))
        in_specs.append(pl.BlockSpec((tm, d), lambda i, j: (jnp.maximum(i - na, 0), 0)))
        args += [ya, yb]
    if add is not None:
        moe_out, route = add
        in_specs.append(pl.BlockSpec((tm, d), lambda i, j: (i, 0)))
        in_specs.append(pl.BlockSpec((tm, d), lambda i, j: (i + na, 0)))
        in_specs.append(pl.BlockSpec((tm, LANES), lambda i, j: (i, 0)))
        args += [moe_out, moe_out, route]
    in_specs.append(pl.BlockSpec((1, d), lambda i, j: (0, 0)))
    args.append(g.reshape(1, d))
    in_specs.append(pl.BlockSpec((d, tn), lambda i, j: (0, j)))
    args.append(w)
    if bias is not None:
        in_specs.append(pl.BlockSpec((1, tn), lambda i, j: (0, j)))
        args.append(bias.reshape(1, nc))
    if n_rope:
        for t in rope:
            in_specs.append(pl.BlockSpec((tm, LANES), lambda i, j: (i, 0)))
            args.append(t)
    out_shape = []
    out_specs = []
    if add is not None:
        out_shape.append(jax.ShapeDtypeStruct((n, d), f32))
        out_specs.append(pl.BlockSpec((tm, d), lambda i, j: (i, 0)))
    out_shape.append(jax.ShapeDtypeStruct((n, nc), bf16))
    out_specs.append(pl.BlockSpec((tm, tn), lambda i, j: (i, j)))
    kern = functools.partial(_proj_kernel, n_first=(na if yb is not None else None),
                             has_add=add is not None, has_bias=bias is not None, n_rope=n_rope)
    out = pl.pallas_call(
        kern, grid=grid, in_specs=in_specs, out_specs=out_specs, out_shape=out_shape,
        scratch_shapes=[pltpu.VMEM((tm, d), bf16)],
        compiler_params=_cp(("arbitrary", "arbitrary")), name="proj")(*args)
    return out if add is not None else out[0]


def _gla_kernel(q_ref, k_ref, v_ref, g_ref, la_ref, wlr_ref, blr_ref, gn_ref, s0_ref,
                o_ref, sfin_ref, st_ref, *, C, CB):
    s = pl.program_id(1)

    @pl.when(s == 0)
    def _():
        st_ref[...] = s0_ref[0]

    row = lax.broadcasted_iota(jnp.int32, (C, C), 0)
    col = lax.broadcasted_iota(jnp.int32, (C, C), 1)
    tril = row >= col
    tri = jnp.where(tril, 1.0, 0.0).astype(bf16)
    lane = lax.broadcasted_iota(jnp.int32, (C, LANES), 1)
    even = lane < DK_A
    srow = lax.broadcasted_iota(jnp.int32, (2 * DV_A, LANES), 0)
    slane = lax.broadcasted_iota(jnp.int32, (2 * DV_A, LANES), 1)
    blockdiag = (srow < DV_A) == (slane < DK_A)
    wlr = wlr_ref[...]
    blr = blr_ref[...]
    gn = gn_ref[...]

    def chunk(c, carry):
        rows = pl.ds(pl.multiple_of(c * C, C), C)
        pre = _dot(la_ref[rows, :], wlr) + blr
        loga = -(jnp.maximum(-pre, 0.0) + jnp.log1p(jnp.exp(-jnp.abs(pre)))) * (1.0 / GLA_TAU)
        hi = loga.astype(bf16)
        lo = (loga - hi.astype(f32)).astype(bf16)
        b = _dot(tri, hi) + _dot(tri, lo)
        b_last = b[C - 1:C, :]
        qf = q_ref[rows, :].astype(f32) * (DK_A ** -0.5)
        kf = k_ref[rows, :].astype(f32)
        q_dec = (qf * jnp.exp(b)).astype(bf16)
        k_inv = (kf * jnp.exp(-b)).astype(bf16)
        k_end = (kf * jnp.exp(b_last - b)).astype(bf16)
        dec = jnp.exp(b_last)
        for p in range(H_A // 2):
            sl = slice(p * LANES, (p + 1) * LANES)
            qp = q_dec[:, sl]
            kp = k_inv[:, sl]
            vp = v_ref[rows, p * 2 * DV_A:(p + 1) * 2 * DV_A]
            zero = jnp.zeros_like(qp)
            att_e = jnp.where(tril, _dot_nt(jnp.where(even, qp, zero), kp), 0.0).astype(bf16)
            att_o = jnp.where(tril, _dot_nt(jnp.where(even, zero, qp), kp), 0.0).astype(bf16)
            o_e = _dot(att_e, vp)[:, :DV_A]
            o_o = _dot(att_o, vp)[:, DV_A:]
            st = st_ref[p]
            o_inter = _dot_nt(qp, st.astype(bf16))
            ds = _dot_tn(vp, k_end[:, sl])
            st_ref[p] = dec[:, sl] * st + jnp.where(blockdiag, ds, 0.0)
            for half, o_intra in ((0, o_e), (1, o_o)):
                h = 2 * p + half
                hs = slice(h * DV_A, (h + 1) * DV_A)
                o = o_intra + o_inter[:, half * DV_A:(half + 1) * DV_A]
                gate = g_ref[rows, hs].astype(f32)
                o_ref[rows, hs] = (_rms(o, gn) * _silu(gate)).astype(o_ref.dtype)
        return carry

    lax.fori_loop(0, CB, chunk, 0)

    @pl.when(s == pl.num_programs(1) - 1)
    def _():
        sfin_ref[0] = st_ref[...]


def _state_to_kernel_layout(s0):
    bsz = s0.shape[0]
    st = jnp.swapaxes(s0, -1, -2).reshape(bsz, H_A // 2, 2, DV_A, DK_A)
    z = jnp.zeros_like(st[:, :, 0])
    top = jnp.concatenate([st[:, :, 0], z], axis=-1)
    bot = jnp.concatenate([z, st[:, :, 1]], axis=-1)
    return jnp.concatenate([top, bot], axis=-2)


def _state_from_kernel_layout(st):
    bsz = st.shape[0]
    e = st[:, :, :DV_A, :DK_A]
    o = st[:, :, DV_A:, DK_A:]
    s = jnp.stack([e, o], axis=2).reshape(bsz, H_A, DV_A, DK_A)
    return jnp.swapaxes(s, -1, -2)


def _gla(z, w_lr, b_lr, g_norm, s0, *, bsz, t, row0, C, CB):
    rb = C * CB
    ns = t // rb
    r0 = row0 // rb

    def zspec(width, cb):
        return pl.BlockSpec((rb, width), lambda b, s: (r0 + b * ns + s, cb))

    hk = H_A * DK_A
    hv = H_A * DV_A
    st0 = _state_to_kernel_layout(s0.astype(f32))
    wl = jnp.zeros((LANES, hk), f32).at[:GLA_RANK].set(w_lr).astype(bf16)
    st_shape = (H_A // 2, 2 * DV_A, 2 * DK_A)
    o, sfin = pl.pallas_call(
        functools.partial(_gla_kernel, C=C, CB=CB),
        grid=(bsz, ns),
        in_specs=[zspec(hk, _QA0 // hk), zspec(hk, _KA0 // hk), zspec(hv, _VA0 // hv),
                  zspec(hv, _GA0 // hv), zspec(LANES, _LA0 // LANES),
                  pl.BlockSpec((LANES, hk), lambda b, s: (0, 0)),
                  pl.BlockSpec((1, hk), lambda b, s: (0, 0)),
                  pl.BlockSpec((1, DV_A), lambda b, s: (0, 0)),
                  pl.BlockSpec((1,) + st_shape, lambda b, s: (b, 0, 0, 0))],
        out_specs=[pl.BlockSpec((rb, hv), lambda b, s: (b * ns + s, 0)),
                   pl.BlockSpec((1,) + st_shape, lambda b, s: (b, 0, 0, 0))],
        out_shape=[jax.ShapeDtypeStruct((bsz * t, hv), bf16),
                   jax.ShapeDtypeStruct((bsz,) + st_shape, f32)],
        scratch_shapes=[pltpu.VMEM(st_shape, f32)],
        compiler_params=_cp(("arbitrary", "arbitrary")), name="gla")(
            z, z, z, z, z, wl, b_lr.reshape(1, hk).astype(f32), g_norm.reshape(1, DV_A).astype(f32), st0)
    return o, _state_from_kernel_layout(sfin)


def _band_kernel(q_ref, ka_ref, kb_ref, kc_ref, va_ref, vb_ref, vc_ref, ba_ref, bb_ref, bc_ref,
                 o_ref, *, mask_first):
    qi = pl.program_id(1)
    for h in range(H_B):
        sl = slice(h * DH_B, (h + 1) * DH_B)
        qh = (q_ref[:, sl].astype(f32) * (DH_B ** -0.5)).astype(bf16)
        sa = _dot_nt(qh, ka_ref[:, sl].astype(bf16)) + ba_ref[h]
        sb = _dot_nt(qh, kb_ref[:, sl].astype(bf16)) + bb_ref[h]
        sc = _dot_nt(qh, kc_ref[:, sl].astype(bf16)) + bc_ref[h]
        if mask_first:
            sa = jnp.where(qi >= 2, sa, NEG_INF)
            sb = jnp.where(qi >= 1, sb, NEG_INF)
        m = jnp.maximum(jnp.maximum(jnp.max(sa, axis=-1, keepdims=True),
                                    jnp.max(sb, axis=-1, keepdims=True)),
                        jnp.max(sc, axis=-1, keepdims=True))
        ea = jnp.exp(sa - m)
        eb = jnp.exp(sb - m)
        ec = jnp.exp(sc - m)
        inv = 1.0 / (jnp.sum(ea, axis=-1, keepdims=True) + jnp.sum(eb, axis=-1, keepdims=True)
                     + jnp.sum(ec, axis=-1, keepdims=True))
        o = (_dot((ea * inv).astype(bf16), va_ref[:, sl].astype(bf16))
             + _dot((eb * inv).astype(bf16), vb_ref[:, sl].astype(bf16))
             + _dot((ec * inv).astype(bf16), vc_ref[:, sl].astype(bf16)))
        o_ref[:, sl] = o.astype(o_ref.dtype)


def _band_call(qkv_specs, args, biases, *, bsz, nq, qr, mask_first):
    hd = H_B * DH_B
    bias_specs = [pl.BlockSpec(b.shape, lambda b_, q_: (0, 0, 0)) for b in biases]
    return pl.pallas_call(
        functools.partial(_band_kernel, mask_first=mask_first),
        grid=(bsz, nq),
        in_specs=list(qkv_specs) + bias_specs,
        out_specs=pl.BlockSpec((qr, hd), lambda b, q: (b * nq + q, 0)),
        out_shape=jax.ShapeDtypeStruct((bsz * nq * qr, hd), bf16),
        compiler_params=_cp(("arbitrary", "arbitrary")), name="band")(*args, *biases)


def _band_bias_prompt(tab, qr, kb):
    nqc = qr // CHUNK
    tot = 2 * kb + qr
    qrow = np.arange(qr)[:, None]
    kcol = np.arange(tot)[None, :]
    qc = qrow // CHUNK + (2 * kb) // CHUNK
    kc = kcol // CHUNK
    valid = (kc <= qc) & (kc >= qc - B_PREV)
    w = (kc - (qc - B_PREV)) * CHUNK + kcol % CHUNK
    rel = qrow % CHUNK - w + B_PREV * CHUNK
    idx = np.clip(rel, -REL_CLIP, REL_CLIP) + REL_CLIP
    bias = jnp.where(jnp.asarray(valid)[None], tab.astype(f32)[:, jnp.asarray(idx)], NEG_INF)
    del nqc
    return bias[:, :, :kb], bias[:, :, kb:2 * kb], bias[:, :, 2 * kb:]


def _band_bias_sample(tab, t, lc):
    rel = np.arange(t)[:, None] - np.arange(lc + t)[None, :] + lc
    idx = np.clip(rel, -REL_CLIP, REL_CLIP) + REL_CLIP
    bias = tab.astype(f32)[:, jnp.asarray(idx)]
    return bias[:, :, :lc // 2], bias[:, :, lc // 2:lc], bias[:, :, lc:]


def _swa_kernel(sink_ref, q_ref, kp_ref, kc_ref, vp_ref, vc_ref, o_ref, qs_ref, *, chunk_mask):
    qi = pl.program_id(1)
    qr = q_ref.shape[0]
    pr = kp_ref.shape[0]
    nl = G_C * LANES
    even = lax.broadcasted_iota(jnp.int32, (qr, LANES), 1) < DH_C
    top = lax.broadcasted_iota(jnp.int32, (LANES, LANES), 0) < DH_C
    if chunk_mask:
        kch = lax.broadcasted_iota(jnp.int32, (pr, nl), 0) // CHUNK
        qch = (lax.broadcasted_iota(jnp.int32, (pr, nl), 1) % LANES) // CHUNK
        ok_prev = (kch >= qch) & (qi >= 1)
        kch = lax.broadcasted_iota(jnp.int32, (qr, nl), 0) // CHUNK
        qch = (lax.broadcasted_iota(jnp.int32, (qr, nl), 1) % LANES) // CHUNK
        ok_cur = kch <= qch
    if qr < LANES:
        qs_ref[...] = jnp.zeros_like(qs_ref)
    for c in range(KV_C):
        ks = slice(c * LANES, (c + 1) * LANES)
        for p in range(G_C // 2):
            cols = slice(c * G_C * DH_C + p * LANES, c * G_C * DH_C + (p + 1) * LANES)
            qp = (q_ref[:, cols].astype(f32) * (DH_C ** -0.5)).astype(bf16)
            zero = jnp.zeros_like(qp)
            qs_ref[2 * p * LANES:2 * p * LANES + qr, :] = jnp.where(even, qp, zero)
            qs_ref[(2 * p + 1) * LANES:(2 * p + 1) * LANES + qr, :] = jnp.where(even, zero, qp)
        q_all = qs_ref[...]
        sink = jnp.concatenate(
            [jnp.full((1, LANES), sink_ref[c * G_C + j], f32) for j in range(G_C)], axis=1)
        st_prev = _dot_nt(kp_ref[:, ks].astype(bf16), q_all)
        st_cur = _dot_nt(kc_ref[:, ks].astype(bf16), q_all)
        if chunk_mask:
            st_prev = jnp.where(ok_prev, st_prev, NEG_INF)
            st_cur = jnp.where(ok_cur, st_cur, NEG_INF)
        m = jnp.maximum(jnp.maximum(jnp.max(st_prev, axis=0, keepdims=True),
                                    jnp.max(st_cur, axis=0, keepdims=True)), sink)
        e_prev = jnp.exp(st_prev - m)
        e_cur = jnp.exp(st_cur - m)
        inv = 1.0 / (jnp.sum(e_prev, axis=0, keepdims=True) + jnp.sum(e_cur, axis=0, keepdims=True)
                     + jnp.exp(sink - m))
        ot = (_dot_tn(vp_ref[:, ks].astype(bf16), (e_prev * inv).astype(bf16))
              + _dot_tn(vc_ref[:, ks].astype(bf16), (e_cur * inv).astype(bf16)))
        for p in range(G_C // 2):
            cols = slice(c * G_C * DH_C + p * LANES, c * G_C * DH_C + (p + 1) * LANES)
            blk = jnp.where(top, ot[:, 2 * p * LANES:(2 * p + 1) * LANES],
                            ot[:, (2 * p + 1) * LANES:(2 * p + 2) * LANES])
            o_ref[:, cols] = blk.T[:qr].astype(o_ref.dtype)


def _swa_call(specs, args, sinks, *, bsz, nq, qr, chunk_mask):
    hd = H_C * DH_C
    return pl.pallas_call(
        functools.partial(_swa_kernel, chunk_mask=chunk_mask),
        grid=(bsz, nq),
        in_specs=[pl.BlockSpec(memory_space=pltpu.SMEM)] + list(specs),
        out_specs=pl.BlockSpec((qr, hd), lambda b, q: (b * nq + q, 0)),
        out_shape=jax.ShapeDtypeStruct((bsz * nq * qr, hd), bf16),
        scratch_shapes=[pltpu.VMEM((G_C * LANES, LANES), bf16)],
        compiler_params=_cp(("arbitrary", "arbitrary")), name="swa")(sinks, *args)


def _out_kernel(*refs, n_parts, n_first):
    y_ref = refs[0]
    o_refs = refs[1:1 + 2 * n_parts]
    w_refs = refs[1 + 2 * n_parts:1 + 3 * n_parts]
    g_ref, rh_ref, rl_ref = refs[1 + 3 * n_parts:4 + 3 * n_parts]
    y1_ref, h2_ref, route_ref, cnt_ref, run_ref = refs[4 + 3 * n_parts:]
    i = pl.program_id(0)

    @pl.when(i == 0)
    def _():
        run_ref[...] = jnp.zeros_like(run_ref)

    acc = y_ref[...]
    for k in range(n_parts):
        lhs = jnp.where(i < n_first, o_refs[2 * k][...], o_refs[2 * k + 1][...])
        acc = acc + _dot(lhs, w_refs[k][...])
    y1_ref[...] = acc
    h2 = _rms(acc, g_ref[...])
    hi = h2.astype(bf16)
    h2_ref[...] = hi.astype(f32)
    lo = (h2 - hi.astype(f32)).astype(bf16)
    lg = _dot(hi, rh_ref[...]) + _dot(hi, rl_ref[...]) + _dot(lo, rh_ref[...])

    lane_i = lax.broadcasted_iota(jnp.int32, lg.shape, 1)
    lane = lane_i.astype(f32)
    big = float(LANES)
    ninf = -jnp.inf
    is_g = lane_i < N_GROUPS
    gmax = jnp.max(jnp.where(is_g, lg, ninf), axis=-1, keepdims=True)
    gsel = jnp.min(jnp.where(is_g & (lg == gmax), lane, big), axis=-1, keepdims=True)
    pg = 1.0 / jnp.sum(jnp.where(is_g, jnp.exp(lg - gmax), 0.0), axis=-1, keepdims=True)
    e0 = N_GROUPS + EXP_PER_GROUP * gsel
    is_e = (lane >= e0) & (lane < e0 + EXP_PER_GROUP)
    v1 = jnp.max(jnp.where(is_e, lg, ninf), axis=-1, keepdims=True)
    i1 = jnp.min(jnp.where(is_e & (lg == v1), lane, big), axis=-1, keepdims=True)
    is_e2 = is_e & (lane != i1)
    v2 = jnp.max(jnp.where(is_e2, lg, ninf), axis=-1, keepdims=True)
    i2 = jnp.min(jnp.where(is_e2 & (lg == v2), lane, big), axis=-1, keepdims=True)
    t = jnp.exp(v2 - v1)
    w1 = pg / (1.0 + t)
    w2 = pg * t / (1.0 + t)

    tmr = lg.shape[0]
    oh1 = jnp.where(lane == i1, 1.0, 0.0)
    oh2 = jnp.where(lane == i2, 1.0, 0.0)
    before = jnp.where(lax.broadcasted_iota(jnp.int32, (tmr, tmr), 0)
                       > lax.broadcasted_iota(jnp.int32, (tmr, tmr), 1), 1.0, 0.0).astype(bf16)
    tot1 = jnp.sum(oh1, axis=0, keepdims=True)
    tot2 = jnp.sum(oh2, axis=0, keepdims=True)
    base = run_ref[...]
    r1 = jnp.sum(oh1 * (base + _dot(before, oh1.astype(bf16))), axis=-1, keepdims=True)
    r2 = jnp.sum(oh2 * (base + tot1 + _dot(before, oh2.astype(bf16))), axis=-1, keepdims=True)
    run_ref[...] = base + tot1 + tot2
    cnt_ref[...] = base + tot1 + tot2
    vals = (i1 - N_GROUPS, i2 - N_GROUPS, w1, w2, r1, r2)
    out = jnp.zeros_like(lg)
    for k, v in enumerate(vals):
        out = jnp.where(lane_i == k, v, out)
    route_ref[...] = out


def _out_proj(y, parts, g, w_router, *, tm):
    n, d = y.shape
    n_first = parts[0][0].shape[0] // tm
    n_tiles = n // tm
    in_specs = [pl.BlockSpec((tm, d), lambda i: (i, 0))]
    args = [y]
    for op, os_, _ in parts:
        dk = op.shape[1]
        in_specs.append(pl.BlockSpec((tm, dk), lambda i: (jnp.minimum(i, n_first - 1), 0)))
        in_specs.append(pl.BlockSpec((tm, dk), lambda i: (jnp.maximum(i - n_first, 0), 0)))
        args += [op, os_]
    for _, _, w in parts:
        in_specs.append(pl.BlockSpec(w.shape, lambda i: (0, 0)))
        args.append(w)
    rh = w_router.astype(bf16)
    rl = (w_router - rh.astype(f32)).astype(bf16)
    in_specs += [pl.BlockSpec((1, d), lambda i: (0, 0)),
                 pl.BlockSpec((d, LANES), lambda i: (0, 0)),
                 pl.BlockSpec((d, LANES), lambda i: (0, 0))]
    args += [g.reshape(1, d), rh, rl]
    return pl.pallas_call(
        functools.partial(_out_kernel, n_parts=len(parts), n_first=n_first),
        grid=(n_tiles,), in_specs=in_specs,
        out_specs=[pl.BlockSpec((tm, d), lambda i: (i, 0)),
                   pl.BlockSpec((tm, d), lambda i: (i, 0)),
                   pl.BlockSpec((tm, LANES), lambda i: (i, 0)),
                   pl.BlockSpec((1, LANES), lambda i: (0, 0))],
        out_shape=[jax.ShapeDtypeStruct((n, d), f32),
                   jax.ShapeDtypeStruct((n, d), f32),
                   jax.ShapeDtypeStruct((n, LANES), f32),
                   jax.ShapeDtypeStruct((1, LANES), f32)],
        scratch_shapes=[pltpu.VMEM((1, LANES), f32)],
        compiler_params=_cp(("arbitrary",)), name="out_proj")(*args)


def _moe_kernel(te_ref, tv_ref, pos_ref, plo_ref, phi_ref, h_ref, wg_ref, wu_ref, wd_ref, out_ref,
                slot_ref, xbuf, ybuf, gsem, ssem, wg_s, wu_s, wd_s, *, tm, n):
    t = pl.program_id(0)
    nt = pl.num_programs(0)
    b = t % 2
    n2 = 2 * n

    def gather_start(tile, buf):
        def body(r, carry):
            s = slot_ref[tile * tm + r]
            tok = jnp.where(s < n, s, jnp.where(s < n2, s - n, s - n2))
            pltpu.make_async_copy(h_ref.at[pl.ds(tok, 1)], xbuf.at[buf, pl.ds(r, 1)], gsem.at[buf]).start()
            return carry
        lax.fori_loop(0, tm, body, 0, unroll=8)

    def gather_wait(buf):
        pltpu.make_async_copy(h_ref.at[pl.ds(0, tm)], xbuf.at[buf], gsem.at[buf]).wait()

    def scatter_wait(buf):
        pltpu.make_async_copy(ybuf.at[buf], out_ref.at[pl.ds(0, tm)], ssem.at[buf]).wait()

    @pl.when(t == 0)
    def _():
        def per_expert(e, nxt):
            def body(r, v):
                slot_ref[r] = v
                return v + 1
            return lax.fori_loop(plo_ref[e], phi_ref[e], body, nxt)
        lax.fori_loop(0, N_EXPERTS, per_expert, n2)

        def body(s, carry):
            slot_ref[pos_ref[s]] = s
            return carry
        lax.fori_loop(0, n2, body, 0, unroll=8)
        gather_start(0, 0)

    @pl.when(t >= 2)
    def _():
        scatter_wait(b)

    valid = tv_ref[t] != 0

    @pl.when(valid)
    def _():
        gather_wait(b)

        @pl.when(tv_ref[jnp.minimum(t + 1, nt - 1)] * (t + 1 < nt).astype(jnp.int32) != 0)
        def _():
            gather_start(t + 1, 1 - b)

        @pl.when((t == 0) | (te_ref[t] != te_ref[jnp.maximum(t - 1, 0)]))
        def _():
            wg_s[...] = wg_ref[0].astype(bf16)
            wu_s[...] = wu_ref[0].astype(bf16)
            wd_s[...] = wd_ref[0].astype(bf16)

        x = xbuf[b].astype(bf16)
        g = _dot(x, wg_s[...])
        u = _dot(x, wu_s[...])
        ybuf[b] = _dot((_silu(g) * u).astype(bf16), wd_s[...])

        def body(r, carry):
            s = slot_ref[t * tm + r]
            pltpu.make_async_copy(ybuf.at[b, pl.ds(r, 1)], out_ref.at[pl.ds(s, 1)], ssem.at[b]).start()
            return carry
        lax.fori_loop(0, tm, body, 0, unroll=8)

    @pl.when(jnp.logical_not(valid))
    def _():
        ybuf[b] = jnp.zeros((tm, ybuf.shape[2]), f32)
        pltpu.make_async_copy(ybuf.at[b], out_ref.at[pl.ds(pl.multiple_of(t * tm, tm), tm)], ssem.at[b]).start()

    @pl.when(t == nt - 1)
    def _():
        scatter_wait(b)
        scatter_wait(1 - b)


def _moe_experts(h2, pos, tile_e, tile_v, pad_lo, pad_hi, wg, wu, wd, *, tm, n_tiles):
    n, d = h2.shape
    ff = wg.shape[2]
    grid_spec = pltpu.PrefetchScalarGridSpec(
        num_scalar_prefetch=5, grid=(n_tiles,),
        in_specs=[pl.BlockSpec(memory_space=pl.ANY),
                  pl.BlockSpec((1, d, ff), lambda t, te, *_: (te[t], 0, 0)),
                  pl.BlockSpec((1, d, ff), lambda t, te, *_: (te[t], 0, 0)),
                  pl.BlockSpec((1, ff, d), lambda t, te, *_: (te[t], 0, 0))],
        out_specs=pl.BlockSpec(memory_space=pl.ANY),
        scratch_shapes=[pltpu.SMEM((n_tiles * tm,), jnp.int32),
                        pltpu.VMEM((2, tm, d), f32), pltpu.VMEM((2, tm, d), f32),
                        pltpu.SemaphoreType.DMA((2,)), pltpu.SemaphoreType.DMA((2,)),
                        pltpu.VMEM((d, ff), bf16), pltpu.VMEM((d, ff), bf16), pltpu.VMEM((ff, d), bf16)])
    return pl.pallas_call(
        functools.partial(_moe_kernel, tm=tm, n=n), grid_spec=grid_spec,
        out_shape=jax.ShapeDtypeStruct((n_tiles * tm, d), f32),
        compiler_params=_cp(("arbitrary",)), name="moe")(
            tile_e, tile_v, pos, pad_lo, pad_hi, h2, wg, wu, wd)


def _moe(h2, route, counts, wg, wu, wd, *, tm):
    n = h2.shape[0]
    n_tiles = (2 * n) // tm + N_EXPERTS
    counts = counts[0, N_GROUPS:N_GROUPS + N_EXPERTS].astype(jnp.int32)
    padded = ((counts + tm - 1) // tm) * tm
    ends = jnp.cumsum(padded)
    starts = ends - padded
    e = route[:, 0:2].astype(jnp.int32)
    rank = route[:, 4:6].astype(jnp.int32)
    pos = (starts[e] + rank).T.reshape(-1)
    tile_start = jnp.arange(n_tiles, dtype=jnp.int32) * tm
    tile_e = jnp.sum((tile_start[:, None] >= ends[None, :]).astype(jnp.int32), axis=1)
    tile_v = (tile_start < ends[-1]).astype(jnp.int32)
    tile_e = jnp.minimum(tile_e, N_EXPERTS - 1)
    return _moe_experts(h2, pos, tile_e, tile_v, starts + counts, ends, wg, wu, wd, tm=tm, n_tiles=n_tiles)


def _final_kernel(y_ref, m0_ref, m1_ref, rt_ref, g_ref, o_ref):
    o_ref[...] = _rms(y_ref[...] + _moe_combine(m0_ref, m1_ref, rt_ref), g_ref[...])


def _final(y, moe_out, route, g, *, tm, row0, rows):
    n, d = y.shape
    r0 = row0 // tm
    r1 = (n + row0) // tm
    return pl.pallas_call(
        _final_kernel, grid=(rows // tm,),
        in_specs=[pl.BlockSpec((tm, d), lambda i: (r0 + i, 0)),
                  pl.BlockSpec((tm, d), lambda i: (r0 + i, 0)),
                  pl.BlockSpec((tm, d), lambda i: (r1 + i, 0)),
                  pl.BlockSpec((tm, LANES), lambda i: (r0 + i, 0)),
                  pl.BlockSpec((1, d), lambda i: (0, 0))],
        out_specs=pl.BlockSpec((tm, d), lambda i: (i, 0)),
        out_shape=jax.ShapeDtypeStruct((rows, d), f32),
        compiler_params=_cp(("arbitrary",)), name="final_norm")(y, moe_out, moe_out, route, g.reshape(1, d))


def _rope_tables(pos):
    half = ROT_DIM // 2
    inv = jnp.power(ROPE_THETA, -jnp.arange(half, dtype=f32) * 2.0 / ROT_DIM)
    ang = pos.astype(f32)[:, None] * inv[None, :]
    cos = jnp.cos(ang)
    sin = jnp.sin(ang)
    n = pos.shape[0]
    one = jnp.ones((n, DH_C - ROT_DIM), f32)
    zero = jnp.zeros((n, DH_C - ROT_DIM), f32)
    zh = jnp.zeros((n, half), f32)
    c = jnp.concatenate([cos, cos, one], axis=1)
    s1 = jnp.concatenate([zh, sin, zero], axis=1)
    s2 = jnp.concatenate([-sin, zh, zero], axis=1)
    rep = LANES // DH_C
    return tuple(jnp.tile(t, (1, rep)) for t in (c, s1, s2))


def _dup_heads(a):
    sh = a.shape[:-1]
    a4 = a.reshape(sh + (KV_C, 1, DH_C))
    return jnp.broadcast_to(a4, sh + (KV_C, 2, DH_C)).reshape(sh + (KV_C * 2 * DH_C,))


def kernel(x_prompt, x_sample, cache_gla_state, cache_band_k, cache_band_v, cache_swa_k, cache_swa_v,
           norm_mix, norm_ffn, norm_final, w_in_ab, w_gla_lr, b_gla_lr, gla_out_norm, rel_bias_tab,
           w_out_ab, w_in_c, b_in_c, attn_sinks, w_out_c, w_router_grp, w_router_exp, w_exp_gate,
           w_exp_up, w_exp_down):
    bp, sp, d = x_prompt.shape
    bs, ts, _ = x_sample.shape
    n_p = bp * sp
    n_s = bs * ts
    n = n_p + n_s
    tm = 512 if n_s % 512 == 0 else n_s
    tmo = min(tm, 256)
    tme = 256 if (2 * n) % 256 == 0 else 64
    ff = w_exp_gate.shape[-1]

    xp = x_prompt.reshape(n_p, d)
    xs_ = x_sample.reshape(n_s, d)

    def router_w(l):
        wr = jnp.concatenate([w_router_grp[l], w_router_exp[l].reshape(d, N_EXPERTS)], axis=1)
        return jnp.zeros((d, LANES), f32).at[:, :N_GROUPS + N_EXPERTS].set(wr)

    def expert_w(l):
        return (w_exp_gate[l].reshape(N_EXPERTS, d, ff), w_exp_up[l].reshape(N_EXPERTS, d, ff),
                w_exp_down[l].reshape(N_EXPERTS, ff, d))

    w0 = w_in_ab[0]
    la0 = 2 * H_A * DK_A + 2 * H_A * DV_A
    w_cat = jnp.concatenate(
        [w0[:, :la0], w0[:, la0 + GLA_RANK:], w0[:, la0:la0 + GLA_RANK],
         jnp.zeros((d, LANES - GLA_RANK), f32)], axis=1).astype(bf16)
    x_all = jnp.concatenate([xp, xs_], axis=0)
    z = _proj(x_all, norm_mix[0], w_cat, tm=tm, tn=896)

    c_s = min(CHUNK, ts)
    oa_p, gla_p = _gla(z, w_gla_lr[0], b_gla_lr[0], gla_out_norm[0],
                       jnp.zeros((bp, H_A, DK_A, DV_A), f32), bsz=bp, t=sp, row0=0, C=CHUNK, CB=4)
    oa_s, gla_s = _gla(z, w_gla_lr[0], b_gla_lr[0], gla_out_norm[0], cache_gla_state[0],
                       bsz=bs, t=ts, row0=n_p, C=c_s, CB=ts // c_s)

    hd = H_B * DH_B
    qr = 256
    nq = sp // qr
    qcb, kcb, vcb = _QB0 // hd, _KB0 // hd, _VB0 // hd
    cur = lambda cb: pl.BlockSpec((qr, hd), lambda b, q: (b * nq + q, cb))
    prev = lambda cb, back: pl.BlockSpec((qr, hd), lambda b, q: (b * nq + jnp.maximum(q - back, 0), cb))
    ob_p = _band_call(
        [cur(qcb), prev(kcb, 2), prev(kcb, 1), cur(kcb), prev(vcb, 2), prev(vcb, 1), cur(vcb)],
        [z] * 7, _band_bias_prompt(rel_bias_tab[0], qr, qr), bsz=bp, nq=nq, qr=qr, mask_first=True)

    lb = cache_band_k.shape[2]
    ck = cache_band_k[0].reshape(bs * lb, hd)
    cv = cache_band_v[0].reshape(bs * lb, hd)
    rs0 = n_p // ts
    new = lambda cb: pl.BlockSpec((ts, hd), lambda b, q: (rs0 + b, cb))
    cpiece = lambda k: pl.BlockSpec((lb // 2, hd), lambda b, q: (2 * b + k, 0))
    ob_s = _band_call(
        [new(qcb), cpiece(0), cpiece(1), new(kcb), cpiece(0), cpiece(1), new(vcb)],
        [z, ck, ck, z, cv, cv, z], _band_bias_sample(rel_bias_tab[0], ts, lb),
        bsz=bs, nq=1, qr=ts, mask_first=False)

    dmix_a = H_A * DV_A
    wo = w_out_ab[0].astype(bf16)
    y1, h2, route, cnt = _out_proj(
        x_all, [(oa_p, oa_s, wo[:dmix_a]), (ob_p, ob_s, wo[dmix_a:])], norm_ffn[0], router_w(0), tm=tmo)
    moe0 = _moe(h2, route, cnt, *expert_w(0), tm=tme)

    keep = min(B_PREV * CHUNK, sp)

    def prompt_tail(a, rows, c0, width):
        return jnp.stack([a[(b + 1) * sp - rows:(b + 1) * sp, c0:c0 + width] for b in range(bp)])

    band_k_p = prompt_tail(z, keep, _KB0, hd).astype(f32).reshape(bp, keep, H_B, DH_B)[None]
    band_v_p = prompt_tail(z, keep, _VB0, hd).astype(f32).reshape(bp, keep, H_B, DH_B)[None]
    kb_s = z[n_p:, _KB0:_KB0 + hd].astype(f32).reshape(bs, ts, H_B, DH_B)
    vb_s = z[n_p:, _VB0:_VB0 + hd].astype(f32).reshape(bs, ts, H_B, DH_B)
    band_k_s = jnp.concatenate([cache_band_k[0], kb_s], axis=1)[:, ts:][None]
    band_v_s = jnp.concatenate([cache_band_v[0], vb_s], axis=1)[:, ts:][None]

    w1 = w_in_c[0]
    nq_c = H_C * DH_C
    nkv = KV_C * DH_C
    w1_cat = jnp.concatenate([w1[:, :nq_c], _dup_heads(w1[:, nq_c:nq_c + nkv]),
                              _dup_heads(w1[:, nq_c + nkv:])], axis=1).astype(bf16)
    b1 = b_in_c[0]
    b1_cat = jnp.concatenate([b1[:nq_c], _dup_heads(b1[nq_c:nq_c + nkv]), _dup_heads(b1[nq_c + nkv:])])
    pos = jnp.concatenate([jnp.tile(jnp.arange(sp), bp), jnp.tile(PAST_LEN + jnp.arange(ts), bs)])
    y2, z1 = _proj(y1, norm_mix[1], w1_cat, tm=tm, tn=512, add=(moe0, route), bias=b1_cat,
                   rope=_rope_tables(pos), n_rope=(nq_c + 2 * nkv) // 512)

    qr_c = WINDOW
    nq1 = sp // qr_c
    kw = 2 * nkv
    kcb1, vcb1 = _KC0 // kw, _VC0 // kw
    qspec = pl.BlockSpec((qr_c, nq_c), lambda b, q: (b * nq1 + q, 0))
    cur1 = lambda cb: pl.BlockSpec((qr_c, kw), lambda b, q: (b * nq1 + q, cb))
    prev1 = lambda cb: pl.BlockSpec((qr_c, kw), lambda b, q: (b * nq1 + jnp.maximum(q - 1, 0), cb))
    sinks = attn_sinks[0].astype(f32)
    oc_p = _swa_call([qspec, prev1(kcb1), cur1(kcb1), prev1(vcb1), cur1(vcb1)], [z1] * 5, sinks,
                     bsz=bp, nq=nq1, qr=qr_c, chunk_mask=True)

    lc = cache_swa_k.shape[2]
    csk = _dup_heads(cache_swa_k[0].reshape(bs * lc, nkv))
    csv = _dup_heads(cache_swa_v[0].reshape(bs * lc, nkv))
    qspec_s = pl.BlockSpec((ts, nq_c), lambda b, q: (rs0 + b, 0))
    new1 = lambda cb: pl.BlockSpec((ts, kw), lambda b, q: (rs0 + b, cb))
    cspec = pl.BlockSpec((lc, kw), lambda b, q: (b, 0))
    oc_s = _swa_call([qspec_s, cspec, new1(kcb1), cspec, new1(vcb1)], [z1, csk, z1, csv, z1], sinks,
                     bsz=bs, nq=1, qr=ts, chunk_mask=False)

    y3, h4, route1, cnt1 = _out_proj(y2, [(oc_p, oc_s, w_out_c[0].astype(bf16))], norm_ffn[1],
                                     router_w(1), tm=tmo)
    moe1 = _moe(h4, route1, cnt1, *expert_w(1), tm=tme)

    keep1 = min(WINDOW, sp)

    def undup(a):
        return a.astype(f32).reshape(a.shape[:-1] + (KV_C, 2, DH_C))[..., 0, :]

    swa_k_p = undup(prompt_tail(z1, keep1, _KC0, kw))[None]
    swa_v_p = undup(prompt_tail(z1, keep1, _VC0, kw))[None]
    k1_s = undup(z1[n_p:, _KC0:_KC0 + kw]).reshape(bs, ts, KV_C, DH_C)
    v1_s = undup(z1[n_p:, _VC0:_VC0 + kw]).reshape(bs, ts, KV_C, DH_C)
    swa_k_s = jnp.concatenate([cache_swa_k[0], k1_s], axis=1)[:, ts:][None]
    swa_v_s = jnp.concatenate([cache_swa_v[0], v1_s], axis=1)[:, ts:][None]

    y_prompt = _final(y3, moe1, route1, norm_final, tm=tm, row0=0, rows=n_p).reshape(bp, sp, d)
    y_sample = _final(y3, moe1, route1, norm_final, tm=tm, row0=n_p, rows=n_s).reshape(bs, ts, d)

    return (y_prompt, y_sample, gla_p[None], gla_s[None].astype(cache_gla_state.dtype),
            band_k_p, band_v_p, band_k_s, band_v_s, swa_k_p, swa_v_p, swa_k_s, swa_v_s)
```

```python
import functools

import numpy as np
import jax
import jax.numpy as jnp
from jax import lax
from jax.experimental import pallas as pl
from jax.experimental.pallas import tpu as pltpu

f32 = jnp.float32
bf16 = jnp.bfloat16

CHUNK = 64
EPS = 1e-6
NEG_INF = -1e30
PAST_LEN = 1024
H_A = 8
DK_A = 64
DV_A = 128
GLA_RANK = 16
GLA_TAU = 16.0
H_B = 8
DH_B = 128
B_PREV = 8
REL_CLIP = 128
H_C = 32
KV_C = 4
G_C = H_C // KV_C
DH_C = 64
WINDOW = 128
ROT_DIM = DH_C // 4
ROPE_THETA = 500000.0
N_GROUPS = 4
EXP_PER_GROUP = 8
N_EXPERTS = N_GROUPS * EXP_PER_GROUP

LANES = 128
VMEM_LIMIT = 52 * 1024 * 1024

_QA0, _KA0, _VA0, _GA0 = 0, 512, 1024, 2048
_QB0, _KB0, _VB0, _LA0 = 3072, 4096, 5120, 6144
_NC_AB = 6272
_QC0, _KC0, _VC0 = 0, 2048, 2560
_NC_C = 3072


def _cp(sem):
    return pltpu.CompilerParams(dimension_semantics=sem, vmem_limit_bytes=VMEM_LIMIT)


def _dot(a, b):
    return jnp.dot(a, b, preferred_element_type=f32)


def _dot_nt(a, b):
    return lax.dot_general(a, b, (((1,), (1,)), ((), ())), preferred_element_type=f32)


def _dot_tn(a, b):
    return lax.dot_general(a, b, (((0,), (0,)), ((), ())), preferred_element_type=f32)


def _rms(x, g):
    ms = jnp.mean(x * x, axis=-1, keepdims=True)
    return (x * lax.rsqrt(ms + EPS)) * g


def _silu(x):
    return x / (1.0 + jnp.exp(-x))


def _moe_combine(m0_ref, m1_ref, route_ref):
    rt = route_ref[...]
    return rt[:, 2:3] * m0_ref[...] + rt[:, 3:4] * m1_ref[...]


def _proj_kernel(*refs, n_first, has_add, has_bias, n_rope):
    it = iter(refs)
    ya_ref = next(it)
    yb_ref = next(it) if n_first is not None else None
    if has_add:
        m0_ref, m1_ref, rt_ref = next(it), next(it), next(it)
    g_ref = next(it)
    w_ref = next(it)
    b_ref = next(it) if has_bias else None
    if n_rope:
        c_ref, s1_ref, s2_ref = next(it), next(it), next(it)
    ysum_ref = next(it) if has_add else None
    z_ref = next(it)
    h_ref = next(it)
    i = pl.program_id(0)
    j = pl.program_id(1)

    @pl.when(j == 0)
    def _():
        x = ya_ref[...]
        if n_first is not None:
            x = jnp.where(i < n_first, x, yb_ref[...])
        if has_add:
            x = x + _moe_combine(m0_ref, m1_ref, rt_ref)
            ysum_ref[...] = x
        h_ref[...] = _rms(x, g_ref[...]).astype(bf16)

    acc = _dot(h_ref[...], w_ref[...])
    if has_bias:
        acc = acc + b_ref[...]
    if n_rope:
        tn = acc.shape[1]

        @pl.when(j < n_rope)
        def _():
            rep = tn // LANES
            c = jnp.tile(c_ref[...], (1, rep))
            s1 = jnp.tile(s1_ref[...], (1, rep))
            s2 = jnp.tile(s2_ref[...], (1, rep))
            half = ROT_DIM // 2
            lo = pltpu.roll(acc, half, axis=1)
            hi = pltpu.roll(acc, tn - half, axis=1)
            z_ref[...] = (acc * c + lo * s1 + hi * s2).astype(z_ref.dtype)

        @pl.when(j >= n_rope)
        def _():
            z_ref[...] = acc.astype(z_ref.dtype)
    else:
        z_ref[...] = acc.astype(z_ref.dtype)


def _proj(ya, g, w, *, tm, tn, yb=None, add=None, bias=None, rope=None, n_rope=0):
    d = ya.shape[1]
    na = ya.shape[0] // tm
    nb = 0 if yb is None else yb.shape[0] // tm
    n = (na + nb) * tm
    nc = w.shape[1]
    grid = (na + nb, nc // tn)
    in_specs = []
    args = []
    if yb is None:
        in_specs.append(pl.BlockSpec((tm, d), lambda i, j: (i, 0)))
        args.append(ya)
    else:
        in_specs.append(pl.BlockSpec((tm, d), lambda i, j: (jnp.minimum(i, na - 1), 0)))
        in_specs.append(pl.BlockSpec((tm, d), lambda i, j: (jnp.maximum(i - na, 0), 0)))
        args += [ya, yb]
    if add is not None:
        moe_out, route = add
        in_specs.append(pl.BlockSpec((tm, d), lambda i, j: (i, 0)))
        in_specs.append(pl.BlockSpec((tm, d), lambda i, j: (i + na, 0)))
        in_specs.append(pl.BlockSpec((tm, LANES), lambda i, j: (i, 0)))
        args += [moe_out, moe_out, route]
    in_specs.append(pl.BlockSpec((1, d), lambda i, j: (0, 0)))
    args.append(g.reshape(1, d))
    in_specs.append(pl.BlockSpec((d, tn), lambda i, j: (0, j)))
    args.append(w)
    if bias is not None:
        in_specs.append(pl.BlockSpec((1, tn), lambda i, j: (0, j)))
        args.append(bias.reshape(1, nc))
    if n_rope:
        for t in rope:
            in_specs.append(pl.BlockSpec((tm, LANES), lambda i, j: (i, 0)))
            args.append(t)
    out_shape = []
    out_specs = []
    if add is not None:
        out_shape.append(jax.ShapeDtypeStruct((n, d), f32))
        out_specs.append(pl.BlockSpec((tm, d), lambda i, j: (i, 0)))
    out_shape.append(jax.ShapeDtypeStruct((n, nc), bf16))
    out_specs.append(pl.BlockSpec((tm, tn), lambda i, j: (i, j)))
    kern = functools.partial(_proj_kernel, n_first=(na if yb is not None else None),
                             has_add=add is not None, has_bias=bias is not None, n_rope=n_rope)
    out = pl.pallas_call(
        kern, grid=grid, in_specs=in_specs, out_specs=out_specs, out_shape=out_shape,
        scratch_shapes=[pltpu.VMEM((tm, d), bf16)],
        compiler_params=_cp(("arbitrary", "arbitrary")), name="proj")(*args)
    return out if add is not None else out[0]


def _gla_kernel(q_ref, k_ref, v_ref, g_ref, la_ref, wlr_ref, blr_ref, gn_ref, s0_ref,
                o_ref, sfin_ref, st_ref, *, C, CB):
    s = pl.program_id(1)

    @pl.when(s == 0)
    def _():
        st_ref[...] = s0_ref[0]

    row = lax.broadcasted_iota(jnp.int32, (C, C), 0)
    col = lax.broadcasted_iota(jnp.int32, (C, C), 1)
    tril = row >= col
    tri = jnp.where(tril, 1.0, 0.0).astype(bf16)
    lane = lax.broadcasted_iota(jnp.int32, (C, LANES), 1)
    even = lane < DK_A
    srow = lax.broadcasted_iota(jnp.int32, (2 * DV_A, LANES), 0)
    slane = lax.broadcasted_iota(jnp.int32, (2 * DV_A, LANES), 1)
    blockdiag = (srow < DV_A) == (slane < DK_A)
    wlr = wlr_ref[...]
    blr = blr_ref[...]
    gn = gn_ref[...]

    def chunk(c, carry):
        rows = pl.ds(pl.multiple_of(c * C, C), C)
        pre = _dot(la_ref[rows, :], wlr) + blr
        loga = -(jnp.maximum(-pre, 0.0) + jnp.log1p(jnp.exp(-jnp.abs(pre)))) * (1.0 / GLA_TAU)
        hi = loga.astype(bf16)
        lo = (loga - hi.astype(f32)).astype(bf16)
        b = _dot(tri, hi) + _dot(tri, lo)
        b_last = b[C - 1:C, :]
        qf = q_ref[rows, :].astype(f32) * (DK_A ** -0.5)
        kf = k_ref[rows, :].astype(f32)
        q_dec = (qf * jnp.exp(b)).astype(bf16)
        k_inv = (kf * jnp.exp(-b)).astype(bf16)
        k_end = (kf * jnp.exp(b_last - b)).astype(bf16)
        dec = jnp.exp(b_last)
        for p in range(H_A // 2):
            sl = slice(p * LANES, (p + 1) * LANES)
            qp = q_dec[:, sl]
            kp = k_inv[:, sl]
            vp = v_ref[rows, p * 2 * DV_A:(p + 1) * 2 * DV_A]
            zero = jnp.zeros_like(qp)
            att_e = jnp.where(tril, _dot_nt(jnp.where(even, qp, zero), kp), 0.0).astype(bf16)
            att_o = jnp.where(tril, _dot_nt(jnp.where(even, zero, qp), kp), 0.0).astype(bf16)
            o_e = _dot(att_e, vp)[:, :DV_A]
            o_o = _dot(att_o, vp)[:, DV_A:]
            st = st_ref[p]
            o_inter = _dot_nt(qp, st.astype(bf16))
            ds = _dot_tn(vp, k_end[:, sl])
            st_ref[p] = dec[:, sl] * st + jnp.where(blockdiag, ds, 0.0)
            for half, o_intra in ((0, o_e), (1, o_o)):
                h = 2 * p + half
                hs = slice(h * DV_A, (h + 1) * DV_A)
                o = o_intra + o_inter[:, half * DV_A:(half + 1) * DV_A]
                gate = g_ref[rows, hs].astype(f32)
                o_ref[rows, hs] = (_rms(o, gn) * _silu(gate)).astype(o_ref.dtype)
        return carry

    lax.fori_loop(0, CB, chunk, 0)

    @pl.when(s == pl.num_programs(1) - 1)
    def _():
        sfin_ref[0] = st_ref[...]


def _state_to_kernel_layout(s0):
    bsz = s0.shape[0]
    st = jnp.swapaxes(s0, -1, -2).reshape(bsz, H_A // 2, 2, DV_A, DK_A)
    z = jnp.zeros_like(st[:, :, 0])
    top = jnp.concatenate([st[:, :, 0], z], axis=-1)
    bot = jnp.concatenate([z, st[:, :, 1]], axis=-1)
    return jnp.concatenate([top, bot], axis=-2)


def _state_from_kernel_layout(st):
    bsz = st.shape[0]
    e = st[:, :, :DV_A, :DK_A]
    o = st[:, :, DV_A:, DK_A:]
    s = jnp.stack([e, o], axis=2).reshape(bsz, H_A, DV_A, DK_A)
    return jnp.swapaxes(s, -1, -2)


def _gla(z, w_lr, b_lr, g_norm, s0, *, bsz, t, row0, C, CB):
    rb = C * CB
    ns = t // rb
    r0 = row0 // rb

    def zspec(width, cb):
        return pl.BlockSpec((rb, width), lambda b, s: (r0 + b * ns + s, cb))

    hk = H_A * DK_A
    hv = H_A * DV_A
    st0 = _state_to_kernel_layout(s0.astype(f32))
    wl = jnp.zeros((LANES, hk), f32).at[:GLA_RANK].set(w_lr).astype(bf16)
    st_shape = (H_A // 2, 2 * DV_A, 2 * DK_A)
    o, sfin = pl.pallas_call(
        functools.partial(_gla_kernel, C=C, CB=CB),
        grid=(bsz, ns),
        in_specs=[zspec(hk, _QA0 // hk), zspec(hk, _KA0 // hk), zspec(hv, _VA0 // hv),
                  zspec(hv, _GA0 // hv), zspec(LANES, _LA0 // LANES),
                  pl.BlockSpec((LANES, hk), lambda b, s: (0, 0)),
                  pl.BlockSpec((1, hk), lambda b, s: (0, 0)),
                  pl.BlockSpec((1, DV_A), lambda b, s: (0, 0)),
                  pl.BlockSpec((1,) + st_shape, lambda b, s: (b, 0, 0, 0))],
        out_specs=[pl.BlockSpec((rb, hv), lambda b, s: (b * ns + s, 0)),
                   pl.BlockSpec((1,) + st_shape, lambda b, s: (b, 0, 0, 0))],
        out_shape=[jax.ShapeDtypeStruct((bsz * t, hv), bf16),
                   jax.ShapeDtypeStruct((bsz,) + st_shape, f32)],
        scratch_shapes=[pltpu.VMEM(st_shape, f32)],
        compiler_params=_cp(("arbitrary", "arbitrary")), name="gla")(
            z, z, z, z, z, wl, b_lr.reshape(1, hk).astype(f32), g_norm.reshape(1, DV_A).astype(f32), st0)
    return o, _state_from_kernel_layout(sfin)


def _band_kernel(q_ref, ka_ref, kb_ref, kc_ref, va_ref, vb_ref, vc_ref, ba_ref, bb_ref, bc_ref,
                 o_ref, *, mask_first):
    qi = pl.program_id(1)
    for h in range(H_B):
        sl = slice(h * DH_B, (h + 1) * DH_B)
        qh = (q_ref[:, sl].astype(f32) * (DH_B ** -0.5)).astype(bf16)
        sa = _dot_nt(qh, ka_ref[:, sl].astype(bf16)) + ba_ref[h]
        sb = _dot_nt(qh, kb_ref[:, sl].astype(bf16)) + bb_ref[h]
        sc = _dot_nt(qh, kc_ref[:, sl].astype(bf16)) + bc_ref[h]
        if mask_first:
            sa = jnp.where(qi >= 2, sa, NEG_INF)
            sb = jnp.where(qi >= 1, sb, NEG_INF)
        m = jnp.maximum(jnp.maximum(jnp.max(sa, axis=-1, keepdims=True),
                                    jnp.max(sb, axis=-1, keepdims=True)),
                        jnp.max(sc, axis=-1, keepdims=True))
        ea = jnp.exp(sa - m)
        eb = jnp.exp(sb - m)
        ec = jnp.exp(sc - m)
        inv = 1.0 / (jnp.sum(ea, axis=-1, keepdims=True) + jnp.sum(eb, axis=-1, keepdims=True)
                     + jnp.sum(ec, axis=-1, keepdims=True))
        o = (_dot((ea * inv).astype(bf16), va_ref[:, sl].astype(bf16))
             + _dot((eb * inv).astype(bf16), vb_ref[:, sl].astype(bf16))
             + _dot((ec * inv).astype(bf16), vc_ref[:, sl].astype(bf16)))
        o_ref[:, sl] = o.astype(o_ref.dtype)


def _band_call(qkv_specs, args, biases, *, bsz, nq, qr, mask_first):
    hd = H_B * DH_B
    bias_specs = [pl.BlockSpec(b.shape, lambda b_, q_: (0, 0, 0)) for b in biases]
    return pl.pallas_call(
        functools.partial(_band_kernel, mask_first=mask_first),
        grid=(bsz, nq),
        in_specs=list(qkv_specs) + bias_specs,
        out_specs=pl.BlockSpec((qr, hd), lambda b, q: (b * nq + q, 0)),
        out_shape=jax.ShapeDtypeStruct((bsz * nq * qr, hd), bf16),
        compiler_params=_cp(("arbitrary", "arbitrary")), name="band")(*args, *biases)


def _rel_bias_matrix(tab, nq, nk, off):
    m = nq + nk
    j = np.arange(m)
    diff = np.where(j < nk, -j, m - j)
    idx = np.clip(diff + off, -REL_CLIP, REL_CLIP) + REL_CLIP
    u = tab.astype(f32)[:, jnp.asarray(idx)]
    h = tab.shape[0]
    return jnp.tile(u, (1, nq))[:, :nq * (m - 1)].reshape(h, nq, m - 1)[:, :, :nk]


def _band_bias_prompt(tab, qr, kb):
    tot = 2 * kb + qr
    qc = (np.arange(qr)[:, None] + 2 * kb) // CHUNK
    kc = np.arange(tot)[None, :] // CHUNK
    valid = (kc <= qc) & (kc >= qc - B_PREV)
    bias = jnp.where(jnp.asarray(valid)[None], _rel_bias_matrix(tab, qr, tot, 2 * kb), NEG_INF)
    return bias[:, :, :kb], bias[:, :, kb:2 * kb], bias[:, :, 2 * kb:]


def _band_bias_sample(tab, t, lc):
    bias = _rel_bias_matrix(tab, t, lc + t, lc)
    return bias[:, :, :lc // 2], bias[:, :, lc // 2:lc], bias[:, :, lc:]


def _swa_kernel(sink_ref, q_ref, kp_ref, kc_ref, vp_ref, vc_ref, o_ref, qs_ref, *, chunk_mask):
    qi = pl.program_id(1)
    qr = q_ref.shape[0]
    pr = kp_ref.shape[0]
    nl = G_C * LANES
    even = lax.broadcasted_iota(jnp.int32, (qr, LANES), 1) < DH_C
    top = lax.broadcasted_iota(jnp.int32, (LANES, LANES), 0) < DH_C
    if chunk_mask:
        kch = lax.broadcasted_iota(jnp.int32, (pr, nl), 0) // CHUNK
        qch = (lax.broadcasted_iota(jnp.int32, (pr, nl), 1) % LANES) // CHUNK
        ok_prev = (kch >= qch) & (qi >= 1)
        kch = lax.broadcasted_iota(jnp.int32, (qr, nl), 0) // CHUNK
        qch = (lax.broadcasted_iota(jnp.int32, (qr, nl), 1) % LANES) // CHUNK
        ok_cur = kch <= qch
    if qr < LANES:
        qs_ref[...] = jnp.zeros_like(qs_ref)
    for c in range(KV_C):
        ks = slice(c * LANES, (c + 1) * LANES)
        for p in range(G_C // 2):
            cols = slice(c * G_C * DH_C + p * LANES, c * G_C * DH_C + (p + 1) * LANES)
            qp = (q_ref[:, cols].astype(f32) * (DH_C ** -0.5)).astype(bf16)
            zero = jnp.zeros_like(qp)
            qs_ref[2 * p * LANES:2 * p * LANES + qr, :] = jnp.where(even, qp, zero)
            qs_ref[(2 * p + 1) * LANES:(2 * p + 1) * LANES + qr, :] = jnp.where(even, zero, qp)
        q_all = qs_ref[...]
        sink = jnp.concatenate(
            [jnp.full((1, LANES), sink_ref[c * G_C + j], f32) for j in range(G_C)], axis=1)
        st_prev = _dot_nt(kp_ref[:, ks].astype(bf16), q_all)
        st_cur = _dot_nt(kc_ref[:, ks].astype(bf16), q_all)
        if chunk_mask:
            st_prev = jnp.where(ok_prev, st_prev, NEG_INF)
            st_cur = jnp.where(ok_cur, st_cur, NEG_INF)
        m = jnp.maximum(jnp.maximum(jnp.max(st_prev, axis=0, keepdims=True),
                                    jnp.max(st_cur, axis=0, keepdims=True)), sink)
        e_prev = jnp.exp(st_prev - m)
        e_cur = jnp.exp(st_cur - m)
        inv = 1.0 / (jnp.sum(e_prev, axis=0, keepdims=True) + jnp.sum(e_cur, axis=0, keepdims=True)
                     + jnp.exp(sink - m))
        ot = (_dot_tn(vp_ref[:, ks].astype(bf16), (e_prev * inv).astype(bf16))
              + _dot_tn(vc_ref[:, ks].astype(bf16), (e_cur * inv).astype(bf16)))
        for p in range(G_C // 2):
            cols = slice(c * G_C * DH_C + p * LANES, c * G_C * DH_C + (p + 1) * LANES)
            blk = jnp.where(top, ot[:, 2 * p * LANES:(2 * p + 1) * LANES],
                            ot[:, (2 * p + 1) * LANES:(2 * p + 2) * LANES])
            o_ref[:, cols] = blk.T[:qr].astype(o_ref.dtype)


def _swa_call(specs, args, sinks, *, bsz, nq, qr, chunk_mask):
    hd = H_C * DH_C
    return pl.pallas_call(
        functools.partial(_swa_kernel, chunk_mask=chunk_mask),
        grid=(bsz, nq),
        in_specs=[pl.BlockSpec(memory_space=pltpu.SMEM)] + list(specs),
        out_specs=pl.BlockSpec((qr, hd), lambda b, q: (b * nq + q, 0)),
        out_shape=jax.ShapeDtypeStruct((bsz * nq * qr, hd), bf16),
        scratch_shapes=[pltpu.VMEM((G_C * LANES, LANES), bf16)],
        compiler_params=_cp(("arbitrary", "arbitrary")), name="swa")(sinks, *args)


def _out_kernel(*refs, n_parts, n_first):
    y_ref = refs[0]
    o_refs = refs[1:1 + 2 * n_parts]
    w_refs = refs[1 + 2 * n_parts:1 + 3 * n_parts]
    g_ref, rh_ref, rl_ref = refs[1 + 3 * n_parts:4 + 3 * n_parts]
    y1_ref, h2_ref, route_ref, cnt_ref, run_ref = refs[4 + 3 * n_parts:]
    i = pl.program_id(0)

    @pl.when(i == 0)
    def _():
        run_ref[...] = jnp.zeros_like(run_ref)

    acc = y_ref[...]
    for k in range(n_parts):
        lhs = jnp.where(i < n_first, o_refs[2 * k][...], o_refs[2 * k + 1][...])
        acc = acc + _dot(lhs, w_refs[k][...])
    y1_ref[...] = acc
    h2 = _rms(acc, g_ref[...])
    hi = h2.astype(bf16)
    h2_ref[...] = hi.astype(f32)
    lo = (h2 - hi.astype(f32)).astype(bf16)
    lg = _dot(hi, rh_ref[...]) + _dot(hi, rl_ref[...]) + _dot(lo, rh_ref[...])

    lane_i = lax.broadcasted_iota(jnp.int32, lg.shape, 1)
    lane = lane_i.astype(f32)
    big = float(LANES)
    ninf = -jnp.inf
    is_g = lane_i < N_GROUPS
    gmax = jnp.max(jnp.where(is_g, lg, ninf), axis=-1, keepdims=True)
    gsel = jnp.min(jnp.where(is_g & (lg == gmax), lane, big), axis=-1, keepdims=True)
    pg = 1.0 / jnp.sum(jnp.where(is_g, jnp.exp(lg - gmax), 0.0), axis=-1, keepdims=True)
    e0 = N_GROUPS + EXP_PER_GROUP * gsel
    is_e = (lane >= e0) & (lane < e0 + EXP_PER_GROUP)
    v1 = jnp.max(jnp.where(is_e, lg, ninf), axis=-1, keepdims=True)
    i1 = jnp.min(jnp.where(is_e & (lg == v1), lane, big), axis=-1, keepdims=True)
    is_e2 = is_e & (lane != i1)
    v2 = jnp.max(jnp.where(is_e2, lg, ninf), axis=-1, keepdims=True)
    i2 = jnp.min(jnp.where(is_e2 & (lg == v2), lane, big), axis=-1, keepdims=True)
    t = jnp.exp(v2 - v1)
    w1 = pg / (1.0 + t)
    w2 = pg * t / (1.0 + t)

    tmr = lg.shape[0]
    oh1 = jnp.where(lane == i1, 1.0, 0.0)
    oh2 = jnp.where(lane == i2, 1.0, 0.0)
    before = jnp.where(lax.broadcasted_iota(jnp.int32, (tmr, tmr), 0)
                       > lax.broadcasted_iota(jnp.int32, (tmr, tmr), 1), 1.0, 0.0).astype(bf16)
    tot1 = jnp.sum(oh1, axis=0, keepdims=True)
    tot2 = jnp.sum(oh2, axis=0, keepdims=True)
    base = run_ref[...]
    r1 = jnp.sum(oh1 * (base + _dot(before, oh1.astype(bf16))), axis=-1, keepdims=True)
    r2 = jnp.sum(oh2 * (base + tot1 + _dot(before, oh2.astype(bf16))), axis=-1, keepdims=True)
    run_ref[...] = base + tot1 + tot2
    cnt_ref[...] = base + tot1 + tot2
    vals = (i1 - N_GROUPS, i2 - N_GROUPS, w1, w2, r1, r2)
    out = jnp.zeros_like(lg)
    for k, v in enumerate(vals):
        out = jnp.where(lane_i == k, v, out)
    route_ref[...] = out


def _out_proj(y, parts, g, w_router, *, tm):
    n, d = y.shape
    n_first = parts[0][0].shape[0] // tm
    n_tiles = n // tm
    in_specs = [pl.BlockSpec((tm, d), lambda i: (i, 0))]
    args = [y]
    for op, os_, _ in parts:
        dk = op.shape[1]
        in_specs.append(pl.BlockSpec((tm, dk), lambda i: (jnp.minimum(i, n_first - 1), 0)))
        in_specs.append(pl.BlockSpec((tm, dk), lambda i: (jnp.maximum(i - n_first, 0), 0)))
        args += [op, os_]
    for _, _, w in parts:
        in_specs.append(pl.BlockSpec(w.shape, lambda i: (0, 0)))
        args.append(w)
    rh = w_router.astype(bf16)
    rl = (w_router - rh.astype(f32)).astype(bf16)
    in_specs += [pl.BlockSpec((1, d), lambda i: (0, 0)),
                 pl.BlockSpec((d, LANES), lambda i: (0, 0)),
                 pl.BlockSpec((d, LANES), lambda i: (0, 0))]
    args += [g.reshape(1, d), rh, rl]
    return pl.pallas_call(
        functools.partial(_out_kernel, n_parts=len(parts), n_first=n_first),
        grid=(n_tiles,), in_specs=in_specs,
        out_specs=[pl.BlockSpec((tm, d), lambda i: (i, 0)),
                   pl.BlockSpec((tm, d), lambda i: (i, 0)),
                   pl.BlockSpec((tm, LANES), lambda i: (i, 0)),
                   pl.BlockSpec((1, LANES), lambda i: (0, 0))],
        out_shape=[jax.ShapeDtypeStruct((n, d), f32),
                   jax.ShapeDtypeStruct((n, d), f32),
                   jax.ShapeDtypeStruct((n, LANES), f32),
                   jax.ShapeDtypeStruct((1, LANES), f32)],
        scratch_shapes=[pltpu.VMEM((1, LANES), f32)],
        compiler_params=_cp(("arbitrary",)), name="out_proj")(*args)


def _moe_kernel(te_ref, tv_ref, pos_ref, plo_ref, phi_ref, h_ref, wg_ref, wu_ref, wd_ref, out_ref,
                slot_ref, x0, x1, y0, y1, gsem, ssem, wg_s, wu_s, wd_s, *, tm, n):
    t = pl.program_id(0)
    nt = pl.num_programs(0)
    n2 = 2 * n
    xs = (x0, x1)
    ys = (y0, y1)
    valid = tv_ref[t] != 0
    next_valid = (t + 1 < nt) & (tv_ref[jnp.minimum(t + 1, nt - 1)] != 0)
    prev_valid = (t >= 1) & (tv_ref[jnp.maximum(t - 1, 0)] != 0)

    def gather_row(tile, r, p):
        s = slot_ref[tile * tm + r]
        tok = jnp.where(s < n, s, jnp.where(s < n2, s - n, s - n2))
        pltpu.make_async_copy(h_ref.at[pl.ds(tok, 1)], xs[p].at[pl.ds(r, 1)], gsem.at[p]).start()

    def scatter_row(tile, r, p):
        s = slot_ref[tile * tm + r]
        pltpu.make_async_copy(ys[p].at[pl.ds(r, 1)], out_ref.at[pl.ds(s, 1)], ssem.at[p]).start()

    def rows_loop(row_fn, tile, p):
        def body(r, carry):
            row_fn(tile, r, p)
            return carry
        lax.fori_loop(0, tm, body, 0, unroll=8)

    def gather_wait(p):
        pltpu.make_async_copy(h_ref.at[pl.ds(0, tm)], xs[p], gsem.at[p]).wait()

    def scatter_wait(p):
        pltpu.make_async_copy(ys[p], out_ref.at[pl.ds(0, tm)], ssem.at[p]).wait()

    def compute(p):
        x = xs[p][...].astype(bf16)
        g = _dot(x, wg_s[...])
        u = _dot(x, wu_s[...])
        ys[p][...] = _dot((_silu(g) * u).astype(bf16), wd_s[...])

    @pl.when(t == 0)
    def _():
        def per_expert(e, nxt):
            def body(r, v):
                slot_ref[r] = v
                return v + 1
            return lax.fori_loop(plo_ref[e], phi_ref[e], body, nxt)
        lax.fori_loop(0, N_EXPERTS, per_expert, n2)

        def body(s, carry):
            slot_ref[pos_ref[s]] = s
            return carry
        lax.fori_loop(0, n2, body, 0, unroll=8)
        rows_loop(gather_row, 0, 0)

    def step(p):
        q = 1 - p

        @pl.when(t >= 2)
        def _():
            scatter_wait(p)

        @pl.when(valid)
        def _():
            gather_wait(p)

            @pl.when((t == 0) | (te_ref[t] != te_ref[jnp.maximum(t - 1, 0)]))
            def _():
                wg_s[...] = wg_ref[0].astype(bf16)
                wu_s[...] = wu_ref[0].astype(bf16)
                wd_s[...] = wd_ref[0].astype(bf16)

            steady = next_valid & (t >= 1)

            @pl.when(steady)
            def _():
                for r in range(tm):
                    gather_row(t + 1, r, q)
                    scatter_row(t - 1, r, q)
                compute(p)

            @pl.when(jnp.logical_not(steady))
            def _():
                @pl.when(next_valid)
                def _():
                    rows_loop(gather_row, t + 1, q)

                @pl.when(t >= 1)
                def _():
                    rows_loop(scatter_row, t - 1, q)
                compute(p)

        @pl.when(jnp.logical_not(valid))
        def _():
            @pl.when(prev_valid)
            def _():
                rows_loop(scatter_row, t - 1, q)
            ys[p][...] = jnp.zeros(ys[p].shape, f32)
            pltpu.make_async_copy(ys[p], out_ref.at[pl.ds(pl.multiple_of(t * tm, tm), tm)], ssem.at[p]).start()

        @pl.when(t == nt - 1)
        def _():
            @pl.when(valid)
            def _():
                rows_loop(scatter_row, t, p)
            scatter_wait(p)
            scatter_wait(q)

    for p in range(2):
        pl.when(t % 2 == p)(functools.partial(step, p))


def _moe_experts(h2, pos, tile_e, tile_v, pad_lo, pad_hi, wg, wu, wd, *, tm, n_tiles):
    n, d = h2.shape
    ff = wg.shape[2]
    assert n >= N_EXPERTS * tm, "padding rows read tokens 0 .. N_EXPERTS * (tm - 1)"
    grid_spec = pltpu.PrefetchScalarGridSpec(
        num_scalar_prefetch=5, grid=(n_tiles,),
        in_specs=[pl.BlockSpec(memory_space=pl.ANY),
                  pl.BlockSpec((1, d, ff), lambda t, te, *_: (te[t], 0, 0)),
                  pl.BlockSpec((1, d, ff), lambda t, te, *_: (te[t], 0, 0)),
                  pl.BlockSpec((1, ff, d), lambda t, te, *_: (te[t], 0, 0))],
        out_specs=pl.BlockSpec(memory_space=pl.ANY),
        scratch_shapes=[pltpu.SMEM((n_tiles * tm,), jnp.int32),
                        pltpu.VMEM((tm, d), f32), pltpu.VMEM((tm, d), f32),
                        pltpu.VMEM((tm, d), f32), pltpu.VMEM((tm, d), f32),
                        pltpu.SemaphoreType.DMA((2,)), pltpu.SemaphoreType.DMA((2,)),
                        pltpu.VMEM((d, ff), bf16), pltpu.VMEM((d, ff), bf16), pltpu.VMEM((ff, d), bf16)])
    return pl.pallas_call(
        functools.partial(_moe_kernel, tm=tm, n=n), grid_spec=grid_spec,
        out_shape=jax.ShapeDtypeStruct((n_tiles * tm, d), f32),
        compiler_params=_cp(("arbitrary",)), name="moe")(
            tile_e, tile_v, pos, pad_lo, pad_hi, h2, wg, wu, wd)


def _moe(h2, route, counts, wg, wu, wd, *, tm):
    n = h2.shape[0]
    n_tiles = (2 * n) // tm + N_EXPERTS
    counts = counts[0, N_GROUPS:N_GROUPS + N_EXPERTS].astype(jnp.int32)
    padded = ((counts + tm - 1) // tm) * tm
    ends = jnp.cumsum(padded)
    starts = ends - padded
    e = route[:, 0:2].astype(jnp.int32)
    rank = route[:, 4:6].astype(jnp.int32)
    pos = (starts[e] + rank).T.reshape(-1)
    tile_start = jnp.arange(n_tiles, dtype=jnp.int32) * tm
    tile_e = jnp.sum((tile_start[:, None] >= ends[None, :]).astype(jnp.int32), axis=1)
    tile_v = (tile_start < ends[-1]).astype(jnp.int32)
    tile_e = jnp.minimum(tile_e, N_EXPERTS - 1)
    return _moe_experts(h2, pos, tile_e, tile_v, starts + counts, ends, wg, wu, wd, tm=tm, n_tiles=n_tiles)


def _final_kernel(y_ref, m0_ref, m1_ref, rt_ref, g_ref, o_ref):
    o_ref[...] = _rms(y_ref[...] + _moe_combine(m0_ref, m1_ref, rt_ref), g_ref[...])


def _final(y, moe_out, route, g, *, tm, row0, rows):
    n, d = y.shape
    r0 = row0 // tm
    r1 = (n + row0) // tm
    return pl.pallas_call(
        _final_kernel, grid=(rows // tm,),
        in_specs=[pl.BlockSpec((tm, d), lambda i: (r0 + i, 0)),
                  pl.BlockSpec((tm, d), lambda i: (r0 + i, 0)),
                  pl.BlockSpec((tm, d), lambda i: (r1 + i, 0)),
                  pl.BlockSpec((tm, LANES), lambda i: (r0 + i, 0)),
                  pl.BlockSpec((1, d), lambda i: (0, 0))],
        out_specs=pl.BlockSpec((tm, d), lambda i: (i, 0)),
        out_shape=jax.ShapeDtypeStruct((rows, d), f32),
        compiler_params=_cp(("arbitrary",)), name="final_norm")(y, moe_out, moe_out, route, g.reshape(1, d))


def _rope_tables(pos):
    half = ROT_DIM // 2
    inv = jnp.power(ROPE_THETA, -jnp.arange(half, dtype=f32) * 2.0 / ROT_DIM)
    ang = pos.astype(f32)[:, None] * inv[None, :]
    cos, sin = lax.optimization_barrier((jnp.cos(ang), jnp.sin(ang)))
    n = pos.shape[0]
    one = jnp.ones((n, DH_C - ROT_DIM), f32)
    zero = jnp.zeros((n, DH_C - ROT_DIM), f32)
    zh = jnp.zeros((n, half), f32)
    c = jnp.concatenate([cos, cos, one], axis=1)
    s1 = jnp.concatenate([zh, sin, zero], axis=1)
    s2 = jnp.concatenate([-sin, zh, zero], axis=1)
    rep = LANES // DH_C
    return tuple(jnp.tile(t, (1, rep)) for t in (c, s1, s2))


def _dup_heads(a):
    sh = a.shape[:-1]
    a4 = a.reshape(sh + (KV_C, 1, DH_C))
    return jnp.broadcast_to(a4, sh + (KV_C, 2, DH_C)).reshape(sh + (KV_C * 2 * DH_C,))


def kernel(x_prompt, x_sample, cache_gla_state, cache_band_k, cache_band_v, cache_swa_k, cache_swa_v,
           norm_mix, norm_ffn, norm_final, w_in_ab, w_gla_lr, b_gla_lr, gla_out_norm, rel_bias_tab,
           w_out_ab, w_in_c, b_in_c, attn_sinks, w_out_c, w_router_grp, w_router_exp, w_exp_gate,
           w_exp_up, w_exp_down):
    bp, sp, d = x_prompt.shape
    bs, ts, _ = x_sample.shape
    n_p = bp * sp
    n_s = bs * ts
    n = n_p + n_s
    tm = 512 if n_s % 512 == 0 else n_s
    tmo = min(tm, 256)
    tme = 256 if (2 * n) % 256 == 0 else 64
    ff = w_exp_gate.shape[-1]

    xp = x_prompt.reshape(n_p, d)
    xs_ = x_sample.reshape(n_s, d)

    def router_w(l):
        wr = jnp.concatenate([w_router_grp[l], w_router_exp[l].reshape(d, N_EXPERTS)], axis=1)
        return jnp.zeros((d, LANES), f32).at[:, :N_GROUPS + N_EXPERTS].set(wr)

    def expert_w(l):
        return (w_exp_gate[l].reshape(N_EXPERTS, d, ff), w_exp_up[l].reshape(N_EXPERTS, d, ff),
                w_exp_down[l].reshape(N_EXPERTS, ff, d))

    w0 = w_in_ab[0]
    la0 = 2 * H_A * DK_A + 2 * H_A * DV_A
    w_cat = jnp.concatenate(
        [w0[:, :la0], w0[:, la0 + GLA_RANK:], w0[:, la0:la0 + GLA_RANK],
         jnp.zeros((d, LANES - GLA_RANK), f32)], axis=1).astype(bf16)
    x_all = jnp.concatenate([xp, xs_], axis=0)
    z = _proj(x_all, norm_mix[0], w_cat, tm=tm, tn=896)

    c_s = min(CHUNK, ts)
    oa_p, gla_p = _gla(z, w_gla_lr[0], b_gla_lr[0], gla_out_norm[0],
                       jnp.zeros((bp, H_A, DK_A, DV_A), f32), bsz=bp, t=sp, row0=0, C=CHUNK, CB=4)
    oa_s, gla_s = _gla(z, w_gla_lr[0], b_gla_lr[0], gla_out_norm[0], cache_gla_state[0],
                       bsz=bs, t=ts, row0=n_p, C=c_s, CB=ts // c_s)

    hd = H_B * DH_B
    qr = 256
    nq = sp // qr
    qcb, kcb, vcb = _QB0 // hd, _KB0 // hd, _VB0 // hd
    cur = lambda cb: pl.BlockSpec((qr, hd), lambda b, q: (b * nq + q, cb))
    prev = lambda cb, back: pl.BlockSpec((qr, hd), lambda b, q: (b * nq + jnp.maximum(q - back, 0), cb))
    ob_p = _band_call(
        [cur(qcb), prev(kcb, 2), prev(kcb, 1), cur(kcb), prev(vcb, 2), prev(vcb, 1), cur(vcb)],
        [z] * 7, _band_bias_prompt(rel_bias_tab[0], qr, qr), bsz=bp, nq=nq, qr=qr, mask_first=True)

    lb = cache_band_k.shape[2]
    ck = cache_band_k[0].reshape(bs * lb, hd)
    cv = cache_band_v[0].reshape(bs * lb, hd)
    rs0 = n_p // ts
    new = lambda cb: pl.BlockSpec((ts, hd), lambda b, q: (rs0 + b, cb))
    cpiece = lambda k: pl.BlockSpec((lb // 2, hd), lambda b, q: (2 * b + k, 0))
    ob_s = _band_call(
        [new(qcb), cpiece(0), cpiece(1), new(kcb), cpiece(0), cpiece(1), new(vcb)],
        [z, ck, ck, z, cv, cv, z], _band_bias_sample(rel_bias_tab[0], ts, lb),
        bsz=bs, nq=1, qr=ts, mask_first=False)

    dmix_a = H_A * DV_A
    wo = w_out_ab[0].astype(bf16)
    y1, h2, route, cnt = _out_proj(
        x_all, [(oa_p, oa_s, wo[:dmix_a]), (ob_p, ob_s, wo[dmix_a:])], norm_ffn[0], router_w(0), tm=tmo)
    moe0 = _moe(h2, route, cnt, *expert_w(0), tm=tme)

    keep = min(B_PREV * CHUNK, sp)

    def prompt_tail(a, rows, c0, width):
        return jnp.stack([a[(b + 1) * sp - rows:(b + 1) * sp, c0:c0 + width] for b in range(bp)])

    band_k_p = prompt_tail(z, keep, _KB0, hd).astype(f32).reshape(bp, keep, H_B, DH_B)[None]
    band_v_p = prompt_tail(z, keep, _VB0, hd).astype(f32).reshape(bp, keep, H_B, DH_B)[None]
    kb_s = z[n_p:, _KB0:_KB0 + hd].astype(f32).reshape(bs, ts, H_B, DH_B)
    vb_s = z[n_p:, _VB0:_VB0 + hd].astype(f32).reshape(bs, ts, H_B, DH_B)
    band_k_s = jnp.concatenate([cache_band_k[0], kb_s], axis=1)[:, ts:][None]
    band_v_s = jnp.concatenate([cache_band_v[0], vb_s], axis=1)[:, ts:][None]

    w1 = w_in_c[0]
    nq_c = H_C * DH_C
    nkv = KV_C * DH_C
    w1_cat = jnp.concatenate([w1[:, :nq_c], _dup_heads(w1[:, nq_c:nq_c + nkv]),
                              _dup_heads(w1[:, nq_c + nkv:])], axis=1).astype(bf16)
    b1 = b_in_c[0]
    b1_cat = jnp.concatenate([b1[:nq_c], _dup_heads(b1[nq_c:nq_c + nkv]), _dup_heads(b1[nq_c + nkv:])])
    uniq = _rope_tables(jnp.concatenate([jnp.arange(sp), PAST_LEN + jnp.arange(ts)]))
    rope = tuple(jnp.concatenate([jnp.tile(u[:sp], (bp, 1)), jnp.tile(u[sp:], (bs, 1))]) for u in uniq)
    y2, z1 = _proj(y1, norm_mix[1], w1_cat, tm=tm, tn=512, add=(moe0, route), bias=b1_cat,
                   rope=rope, n_rope=(nq_c + 2 * nkv) // 512)

    qr_c = WINDOW
    nq1 = sp // qr_c
    kw = 2 * nkv
    kcb1, vcb1 = _KC0 // kw, _VC0 // kw
    qspec = pl.BlockSpec((qr_c, nq_c), lambda b, q: (b * nq1 + q, 0))
    cur1 = lambda cb: pl.BlockSpec((qr_c, kw), lambda b, q: (b * nq1 + q, cb))
    prev1 = lambda cb: pl.BlockSpec((qr_c, kw), lambda b, q: (b * nq1 + jnp.maximum(q - 1, 0), cb))
    sinks = attn_sinks[0].astype(f32)
    oc_p = _swa_call([qspec, prev1(kcb1), cur1(kcb1), prev1(vcb1), cur1(vcb1)], [z1] * 5, sinks,
                     bsz=bp, nq=nq1, qr=qr_c, chunk_mask=True)

    lc = cache_swa_k.shape[2]
    csk = _dup_heads(cache_swa_k[0].reshape(bs * lc, nkv))
    csv = _dup_heads(cache_swa_v[0].reshape(bs * lc, nkv))
    qspec_s = pl.BlockSpec((ts, nq_c), lambda b, q: (rs0 + b, 0))
    new1 = lambda cb: pl.BlockSpec((ts, kw), lambda b, q: (rs0 + b, cb))
    cspec = pl.BlockSpec((lc, kw), lambda b, q: (b, 0))
    oc_s = _swa_call([qspec_s, cspec, new1(kcb1), cspec, new1(vcb1)], [z1, csk, z1, csv, z1], sinks,
                     bsz=bs, nq=1, qr=ts, chunk_mask=False)

    y3, h4, route1, cnt1 = _out_proj(y2, [(oc_p, oc_s, w_out_c[0].astype(bf16))], norm_ffn[1],
                                     router_w(1), tm=tmo)
    moe1 = _moe(h4, route1, cnt1, *expert_w(1), tm=tme)

    keep1 = min(WINDOW, sp)

    def undup(a):
        return a.astype(f32).reshape(a.shape[:-1] + (KV_C, 2, DH_C))[..., 0, :]

    swa_k_p = undup(prompt_tail(z1, keep1, _KC0, kw))[None]
    swa_v_p = undup(prompt_tail(z1, keep1, _VC0, kw))[None]
    k1_s = undup(z1[n_p:, _KC0:_KC0 + kw]).reshape(bs, ts, KV_C, DH_C)
    v1_s = undup(z1[n_p:, _VC0:_VC0 + kw]).reshape(bs, ts, KV_C, DH_C)
    swa_k_s = jnp.concatenate([cache_swa_k[0], k1_s], axis=1)[:, ts:][None]
    swa_v_s = jnp.concatenate([cache_swa_v[0], v1_s], axis=1)[:, ts:][None]

    y_prompt = _final(y3, moe1, route1, norm_final, tm=tm, row0=0, rows=n_p).reshape(bp, sp, d)
    y_sample = _final(y3, moe1, route1, norm_final, tm=tm, row0=n_p, rows=n_s).reshape(bs, ts, d)

    return (y_prompt, y_sample, gla_p[None], gla_s[None].astype(cache_gla_state.dtype),
            band_k_p, band_v_p, band_k_s, band_v_s, swa_k_p, swa_v_p, swa_k_s, swa_v_s)
```

```python
import functools

import numpy as np
import jax
import jax.numpy as jnp
from jax import lax
from jax.experimental import pallas as pl
from jax.experimental.pallas import tpu as pltpu

f32 = jnp.float32
bf16 = jnp.bfloat16

CHUNK = 64
EPS = 1e-6
NEG_INF = -1e30
PAST_LEN = 1024
H_A = 8
DK_A = 64
DV_A = 128
GLA_RANK = 16
GLA_TAU = 16.0
H_B = 8
DH_B = 128
B_PREV = 8
REL_CLIP = 128
H_C = 32
KV_C = 4
G_C = H_C // KV_C
DH_C = 64
WINDOW = 128
ROT_DIM = DH_C // 4
ROPE_THETA = 500000.0
N_GROUPS = 4
EXP_PER_GROUP = 8
N_EXPERTS = N_GROUPS * EXP_PER_GROUP

LANES = 128
VMEM_LIMIT = 52 * 1024 * 1024

_QA0, _KA0, _VA0, _GA0 = 0, 512, 1024, 2048
_QB0, _KB0, _VB0, _LA0 = 3072, 4096, 5120, 6144
_NC_AB = 6272
_QC0, _KC0, _VC0 = 0, 2048, 2560
_NC_C = 3072


def _cp(sem):
    return pltpu.CompilerParams(dimension_semantics=sem, vmem_limit_bytes=VMEM_LIMIT)


def _dot(a, b):
    return jnp.dot(a, b, preferred_element_type=f32)


def _dot_nt(a, b):
    return lax.dot_general(a, b, (((1,), (1,)), ((), ())), preferred_element_type=f32)


def _dot_tn(a, b):
    return lax.dot_general(a, b, (((0,), (0,)), ((), ())), preferred_element_type=f32)


def _rms(x, g):
    ms = jnp.mean(x * x, axis=-1, keepdims=True)
    return (x * lax.rsqrt(ms + EPS)) * g


def _silu(x):
    return x / (1.0 + jnp.exp(-x))


def _moe_combine(m0_ref, m1_ref, route_ref):
    rt = route_ref[...]
    return rt[:, 2:3] * m0_ref[...] + rt[:, 3:4] * m1_ref[...]


def _proj_kernel(*refs, n_first, has_add, has_bias, n_rope, tn):
    it = iter(refs)
    ya_ref = next(it)
    yb_ref = next(it) if n_first is not None else None
    if has_add:
        m0_ref, m1_ref, rt_ref = next(it), next(it), next(it)
    g_ref = next(it)
    w_ref = next(it)
    b_ref = next(it) if has_bias else None
    if n_rope:
        c_ref, s1_ref, s2_ref = next(it), next(it), next(it)
    ysum_ref = next(it) if has_add else None
    z_ref = next(it)
    h_ref = next(it)
    i = pl.program_id(0)

    x = ya_ref[...]
    if n_first is not None:
        x = jnp.where(i < n_first, x, yb_ref[...])
    if has_add:
        x = x + _moe_combine(m0_ref, m1_ref, rt_ref)
        ysum_ref[...] = x
    h_ref[...] = _rms(x, g_ref[...]).astype(bf16)

    def rope(acc):
        rep = tn // LANES
        c = jnp.tile(c_ref[...], (1, rep))
        s1 = jnp.tile(s1_ref[...], (1, rep))
        s2 = jnp.tile(s2_ref[...], (1, rep))
        half = ROT_DIM // 2
        lo = pltpu.roll(acc, half, axis=1)
        hi = pltpu.roll(acc, tn - half, axis=1)
        return acc * c + lo * s1 + hi * s2

    def col_tile(j, carry):
        cols = pl.ds(pl.multiple_of(j * tn, tn), tn)
        acc = _dot(h_ref[...], w_ref[:, cols])
        if has_bias:
            acc = acc + b_ref[:, cols]
        if n_rope:
            @pl.when(j < n_rope)
            def _():
                z_ref[:, cols] = rope(acc).astype(z_ref.dtype)

            @pl.when(j >= n_rope)
            def _():
                z_ref[:, cols] = acc.astype(z_ref.dtype)
        else:
            z_ref[:, cols] = acc.astype(z_ref.dtype)
        return carry

    lax.fori_loop(0, z_ref.shape[1] // tn, col_tile, 0)


def _proj(ya, g, w, *, tm, tn, yb=None, add=None, bias=None, rope=None, n_rope=0):
    d = ya.shape[1]
    na = ya.shape[0] // tm
    nb = 0 if yb is None else yb.shape[0] // tm
    n = (na + nb) * tm
    nc = w.shape[1]
    once = dict(pipeline_mode=pl.Buffered(1))
    in_specs = []
    args = []
    if yb is None:
        in_specs.append(pl.BlockSpec((tm, d), lambda i: (i, 0)))
        args.append(ya)
    else:
        in_specs.append(pl.BlockSpec((tm, d), lambda i: (jnp.minimum(i, na - 1), 0)))
        in_specs.append(pl.BlockSpec((tm, d), lambda i: (jnp.maximum(i - na, 0), 0)))
        args += [ya, yb]
    if add is not None:
        moe_out, route = add
        in_specs.append(pl.BlockSpec((tm, d), lambda i: (i, 0)))
        in_specs.append(pl.BlockSpec((tm, d), lambda i: (i + na, 0)))
        in_specs.append(pl.BlockSpec((tm, LANES), lambda i: (i, 0)))
        args += [moe_out, moe_out, route]
    in_specs.append(pl.BlockSpec((1, d), lambda i: (0, 0)))
    args.append(g.reshape(1, d))
    in_specs.append(pl.BlockSpec((d, nc), lambda i: (0, 0), **once))
    args.append(w)
    if bias is not None:
        in_specs.append(pl.BlockSpec((1, nc), lambda i: (0, 0)))
        args.append(bias.reshape(1, nc))
    if n_rope:
        for t in rope:
            in_specs.append(pl.BlockSpec((tm, LANES), lambda i: (i, 0)))
            args.append(t)
    out_shape = []
    out_specs = []
    if add is not None:
        out_shape.append(jax.ShapeDtypeStruct((n, d), f32))
        out_specs.append(pl.BlockSpec((tm, d), lambda i: (i, 0)))
    out_shape.append(jax.ShapeDtypeStruct((n, nc), bf16))
    out_specs.append(pl.BlockSpec((tm, nc), lambda i: (i, 0)))
    kern = functools.partial(_proj_kernel, n_first=(na if yb is not None else None),
                             has_add=add is not None, has_bias=bias is not None, n_rope=n_rope, tn=tn)
    out = pl.pallas_call(
        kern, grid=(na + nb,), in_specs=in_specs, out_specs=out_specs, out_shape=out_shape,
        scratch_shapes=[pltpu.VMEM((tm, d), bf16)],
        compiler_params=_cp(("arbitrary",)), name="proj")(*args)
    return out if add is not None else out[0]


def _gla_kernel(q_ref, k_ref, v_ref, g_ref, la_ref, wlr_ref, blr_ref, gn_ref, s0_ref,
                o_ref, sfin_ref, st_ref, *, C, CB):
    s = pl.program_id(1)

    @pl.when(s == 0)
    def _():
        st_ref[...] = s0_ref[0]

    row = lax.broadcasted_iota(jnp.int32, (C, C), 0)
    col = lax.broadcasted_iota(jnp.int32, (C, C), 1)
    tril = row >= col
    tri = jnp.where(tril, 1.0, 0.0).astype(bf16)
    lane = lax.broadcasted_iota(jnp.int32, (C, LANES), 1)
    even = lane < DK_A
    srow = lax.broadcasted_iota(jnp.int32, (2 * DV_A, LANES), 0)
    slane = lax.broadcasted_iota(jnp.int32, (2 * DV_A, LANES), 1)
    blockdiag = (srow < DV_A) == (slane < DK_A)
    wlr = wlr_ref[...]
    blr = blr_ref[...]
    gn = gn_ref[...]

    def chunk(c, carry):
        rows = pl.ds(pl.multiple_of(c * C, C), C)
        pre = _dot(la_ref[rows, :], wlr) + blr
        loga = -(jnp.maximum(-pre, 0.0) + jnp.log1p(jnp.exp(-jnp.abs(pre)))) * (1.0 / GLA_TAU)
        hi = loga.astype(bf16)
        lo = (loga - hi.astype(f32)).astype(bf16)
        b = _dot(tri, hi) + _dot(tri, lo)
        b_last = b[C - 1:C, :]
        qf = q_ref[rows, :].astype(f32) * (DK_A ** -0.5)
        kf = k_ref[rows, :].astype(f32)
        q_dec = (qf * jnp.exp(b)).astype(bf16)
        k_inv = (kf * jnp.exp(-b)).astype(bf16)
        k_end = (kf * jnp.exp(b_last - b)).astype(bf16)
        dec = jnp.exp(b_last)
        for p in range(H_A // 2):
            sl = slice(p * LANES, (p + 1) * LANES)
            qp = q_dec[:, sl]
            kp = k_inv[:, sl]
            vp = v_ref[rows, p * 2 * DV_A:(p + 1) * 2 * DV_A]
            zero = jnp.zeros_like(qp)
            att_e = jnp.where(tril, _dot_nt(jnp.where(even, qp, zero), kp), 0.0).astype(bf16)
            att_o = jnp.where(tril, _dot_nt(jnp.where(even, zero, qp), kp), 0.0).astype(bf16)
            o_e = _dot(att_e, vp)[:, :DV_A]
            o_o = _dot(att_o, vp)[:, DV_A:]
            st = st_ref[p]
            o_inter = _dot_nt(qp, st.astype(bf16))
            ds = _dot_tn(vp, k_end[:, sl])
            st_ref[p] = dec[:, sl] * st + jnp.where(blockdiag, ds, 0.0)
            for half, o_intra in ((0, o_e), (1, o_o)):
                h = 2 * p + half
                hs = slice(h * DV_A, (h + 1) * DV_A)
                o = o_intra + o_inter[:, half * DV_A:(half + 1) * DV_A]
                gate = g_ref[rows, hs].astype(f32)
                o_ref[rows, hs] = (_rms(o, gn) * _silu(gate)).astype(o_ref.dtype)
        return carry

    lax.fori_loop(0, CB, chunk, 0)

    @pl.when(s == pl.num_programs(1) - 1)
    def _():
        sfin_ref[0] = st_ref[...]


def _state_to_kernel_layout(s0):
    bsz = s0.shape[0]
    st = jnp.swapaxes(s0, -1, -2).reshape(bsz, H_A // 2, 2, DV_A, DK_A)
    z = jnp.zeros_like(st[:, :, 0])
    top = jnp.concatenate([st[:, :, 0], z], axis=-1)
    bot = jnp.concatenate([z, st[:, :, 1]], axis=-1)
    return jnp.concatenate([top, bot], axis=-2)


def _state_from_kernel_layout(st):
    bsz = st.shape[0]
    e = st[:, :, :DV_A, :DK_A]
    o = st[:, :, DV_A:, DK_A:]
    s = jnp.stack([e, o], axis=2).reshape(bsz, H_A, DV_A, DK_A)
    return jnp.swapaxes(s, -1, -2)


def _gla(z, w_lr, b_lr, g_norm, s0, *, bsz, t, row0, C, CB):
    rb = C * CB
    ns = t // rb
    r0 = row0 // rb

    def zspec(width, cb):
        return pl.BlockSpec((rb, width), lambda b, s: (r0 + b * ns + s, cb))

    hk = H_A * DK_A
    hv = H_A * DV_A
    st0 = _state_to_kernel_layout(s0.astype(f32))
    wl = jnp.zeros((LANES, hk), f32).at[:GLA_RANK].set(w_lr).astype(bf16)
    st_shape = (H_A // 2, 2 * DV_A, 2 * DK_A)
    o, sfin = pl.pallas_call(
        functools.partial(_gla_kernel, C=C, CB=CB),
        grid=(bsz, ns),
        in_specs=[zspec(hk, _QA0 // hk), zspec(hk, _KA0 // hk), zspec(hv, _VA0 // hv),
                  zspec(hv, _GA0 // hv), zspec(LANES, _LA0 // LANES),
                  pl.BlockSpec((LANES, hk), lambda b, s: (0, 0)),
                  pl.BlockSpec((1, hk), lambda b, s: (0, 0)),
                  pl.BlockSpec((1, DV_A), lambda b, s: (0, 0)),
                  pl.BlockSpec((1,) + st_shape, lambda b, s: (b, 0, 0, 0))],
        out_specs=[pl.BlockSpec((rb, hv), lambda b, s: (b * ns + s, 0)),
                   pl.BlockSpec((1,) + st_shape, lambda b, s: (b, 0, 0, 0))],
        out_shape=[jax.ShapeDtypeStruct((bsz * t, hv), bf16),
                   jax.ShapeDtypeStruct((bsz,) + st_shape, f32)],
        scratch_shapes=[pltpu.VMEM(st_shape, f32)],
        compiler_params=_cp(("arbitrary", "arbitrary")), name="gla")(
            z, z, z, z, z, wl, b_lr.reshape(1, hk).astype(f32), g_norm.reshape(1, DV_A).astype(f32), st0)
    return o, _state_from_kernel_layout(sfin)


def _band_kernel(q_ref, ka_ref, kb_ref, kc_ref, va_ref, vb_ref, vc_ref, ba_ref, bb_ref, bc_ref,
                 o_ref, *, mask_first):
    qi = pl.program_id(1)
    for h in range(H_B):
        sl = slice(h * DH_B, (h + 1) * DH_B)
        qh = (q_ref[:, sl].astype(f32) * (DH_B ** -0.5)).astype(bf16)
        sa = _dot_nt(qh, ka_ref[:, sl].astype(bf16)) + ba_ref[h]
        sb = _dot_nt(qh, kb_ref[:, sl].astype(bf16)) + bb_ref[h]
        sc = _dot_nt(qh, kc_ref[:, sl].astype(bf16)) + bc_ref[h]
        if mask_first:
            sa = jnp.where(qi >= 2, sa, NEG_INF)
            sb = jnp.where(qi >= 1, sb, NEG_INF)
        m = jnp.maximum(jnp.maximum(jnp.max(sa, axis=-1, keepdims=True),
                                    jnp.max(sb, axis=-1, keepdims=True)),
                        jnp.max(sc, axis=-1, keepdims=True))
        ea = jnp.exp(sa - m)
        eb = jnp.exp(sb - m)
        ec = jnp.exp(sc - m)
        inv = 1.0 / (jnp.sum(ea, axis=-1, keepdims=True) + jnp.sum(eb, axis=-1, keepdims=True)
                     + jnp.sum(ec, axis=-1, keepdims=True))
        o = (_dot((ea * inv).astype(bf16), va_ref[:, sl].astype(bf16))
             + _dot((eb * inv).astype(bf16), vb_ref[:, sl].astype(bf16))
             + _dot((ec * inv).astype(bf16), vc_ref[:, sl].astype(bf16)))
        o_ref[:, sl] = o.astype(o_ref.dtype)


def _band_call(qkv_specs, args, biases, *, bsz, nq, qr, mask_first):
    hd = H_B * DH_B
    bias_specs = [pl.BlockSpec(b.shape, lambda b_, q_: (0, 0, 0)) for b in biases]
    return pl.pallas_call(
        functools.partial(_band_kernel, mask_first=mask_first),
        grid=(bsz, nq),
        in_specs=list(qkv_specs) + bias_specs,
        out_specs=pl.BlockSpec((qr, hd), lambda b, q: (b * nq + q, 0)),
        out_shape=jax.ShapeDtypeStruct((bsz * nq * qr, hd), bf16),
        compiler_params=_cp(("arbitrary", "arbitrary")), name="band")(*args, *biases)


def _rel_bias_matrix(tab, nq, nk, off):
    m = nq + nk
    j = np.arange(m)
    diff = np.where(j < nk, -j, m - j)
    idx = np.clip(diff + off, -REL_CLIP, REL_CLIP) + REL_CLIP
    u = tab.astype(f32)[:, jnp.asarray(idx)]
    h = tab.shape[0]
    return jnp.tile(u, (1, nq))[:, :nq * (m - 1)].reshape(h, nq, m - 1)[:, :, :nk]


def _band_bias_prompt(tab, qr, kb):
    tot = 2 * kb + qr
    qc = (np.arange(qr)[:, None] + 2 * kb) // CHUNK
    kc = np.arange(tot)[None, :] // CHUNK
    valid = (kc <= qc) & (kc >= qc - B_PREV)
    bias = jnp.where(jnp.asarray(valid)[None], _rel_bias_matrix(tab, qr, tot, 2 * kb), NEG_INF)
    return bias[:, :, :kb], bias[:, :, kb:2 * kb], bias[:, :, 2 * kb:]


def _band_bias_sample(tab, t, lc):
    bias = _rel_bias_matrix(tab, t, lc + t, lc)
    return bias[:, :, :lc // 2], bias[:, :, lc // 2:lc], bias[:, :, lc:]


def _swa_kernel(sink_ref, q_ref, kp_ref, kc_ref, vp_ref, vc_ref, o_ref, qs_ref, *, chunk_mask):
    qi = pl.program_id(1)
    qr = q_ref.shape[0]
    pr = kp_ref.shape[0]
    nl = G_C * LANES
    even = lax.broadcasted_iota(jnp.int32, (qr, LANES), 1) < DH_C
    top = lax.broadcasted_iota(jnp.int32, (LANES, LANES), 0) < DH_C
    if chunk_mask:
        kch = lax.broadcasted_iota(jnp.int32, (pr, nl), 0) // CHUNK
        qch = (lax.broadcasted_iota(jnp.int32, (pr, nl), 1) % LANES) // CHUNK
        ok_prev = (kch >= qch) & (qi >= 1)
        kch = lax.broadcasted_iota(jnp.int32, (qr, nl), 0) // CHUNK
        qch = (lax.broadcasted_iota(jnp.int32, (qr, nl), 1) % LANES) // CHUNK
        ok_cur = kch <= qch
    if qr < LANES:
        qs_ref[...] = jnp.zeros_like(qs_ref)
    for c in range(KV_C):
        ks = slice(c * LANES, (c + 1) * LANES)
        for p in range(G_C // 2):
            cols = slice(c * G_C * DH_C + p * LANES, c * G_C * DH_C + (p + 1) * LANES)
            qp = (q_ref[:, cols].astype(f32) * (DH_C ** -0.5)).astype(bf16)
            zero = jnp.zeros_like(qp)
            qs_ref[2 * p * LANES:2 * p * LANES + qr, :] = jnp.where(even, qp, zero)
            qs_ref[(2 * p + 1) * LANES:(2 * p + 1) * LANES + qr, :] = jnp.where(even, zero, qp)
        q_all = qs_ref[...]
        sink = jnp.concatenate(
            [jnp.full((1, LANES), sink_ref[c * G_C + j], f32) for j in range(G_C)], axis=1)
        st_prev = _dot_nt(kp_ref[:, ks].astype(bf16), q_all)
        st_cur = _dot_nt(kc_ref[:, ks].astype(bf16), q_all)
        if chunk_mask:
            st_prev = jnp.where(ok_prev, st_prev, NEG_INF)
            st_cur = jnp.where(ok_cur, st_cur, NEG_INF)
        m = jnp.maximum(jnp.maximum(jnp.max(st_prev, axis=0, keepdims=True),
                                    jnp.max(st_cur, axis=0, keepdims=True)), sink)
        e_prev = jnp.exp(st_prev - m)
        e_cur = jnp.exp(st_cur - m)
        inv = 1.0 / (jnp.sum(e_prev, axis=0, keepdims=True) + jnp.sum(e_cur, axis=0, keepdims=True)
                     + jnp.exp(sink - m))
        ot = (_dot_tn(vp_ref[:, ks].astype(bf16), (e_prev * inv).astype(bf16))
              + _dot_tn(vc_ref[:, ks].astype(bf16), (e_cur * inv).astype(bf16)))
        for p in range(G_C // 2):
            cols = slice(c * G_C * DH_C + p * LANES, c * G_C * DH_C + (p + 1) * LANES)
            blk = jnp.where(top, ot[:, 2 * p * LANES:(2 * p + 1) * LANES],
                            ot[:, (2 * p + 1) * LANES:(2 * p + 2) * LANES])
            o_ref[:, cols] = blk.T[:qr].astype(o_ref.dtype)


def _swa_call(specs, args, sinks, *, bsz, nq, qr, chunk_mask):
    hd = H_C * DH_C
    return pl.pallas_call(
        functools.partial(_swa_kernel, chunk_mask=chunk_mask),
        grid=(bsz, nq),
        in_specs=[pl.BlockSpec(memory_space=pltpu.SMEM)] + list(specs),
        out_specs=pl.BlockSpec((qr, hd), lambda b, q: (b * nq + q, 0)),
        out_shape=jax.ShapeDtypeStruct((bsz * nq * qr, hd), bf16),
        scratch_shapes=[pltpu.VMEM((G_C * LANES, LANES), bf16)],
        compiler_params=_cp(("arbitrary", "arbitrary")), name="swa")(sinks, *args)


def _out_kernel(*refs, n_parts, n_first, split_y):
    ny = 2 if split_y else 1
    y_refs = refs[:ny]
    refs = refs[ny - 1:]
    o_refs = refs[1:1 + 2 * n_parts]
    w_refs = refs[1 + 2 * n_parts:1 + 3 * n_parts]
    g_ref, rh_ref, rl_ref = refs[1 + 3 * n_parts:4 + 3 * n_parts]
    y1_ref, h2_ref, route_ref, cnt_ref, run_ref = refs[4 + 3 * n_parts:]
    i = pl.program_id(0)

    @pl.when(i == 0)
    def _():
        run_ref[...] = jnp.zeros_like(run_ref)

    acc = y_refs[0][...]
    if split_y:
        acc = jnp.where(i < n_first, acc, y_refs[1][...])
    for k in range(n_parts):
        lhs = jnp.where(i < n_first, o_refs[2 * k][...], o_refs[2 * k + 1][...])
        acc = acc + _dot(lhs, w_refs[k][...])
    y1_ref[...] = acc
    h2 = _rms(acc, g_ref[...])
    hi = h2.astype(bf16)
    h2_ref[...] = hi.astype(f32)
    lo = (h2 - hi.astype(f32)).astype(bf16)
    lg = _dot(hi, rh_ref[...]) + _dot(hi, rl_ref[...]) + _dot(lo, rh_ref[...])

    lane_i = lax.broadcasted_iota(jnp.int32, lg.shape, 1)
    lane = lane_i.astype(f32)
    big = float(LANES)
    ninf = -jnp.inf
    is_g = lane_i < N_GROUPS
    gmax = jnp.max(jnp.where(is_g, lg, ninf), axis=-1, keepdims=True)
    gsel = jnp.min(jnp.where(is_g & (lg == gmax), lane, big), axis=-1, keepdims=True)
    pg = 1.0 / jnp.sum(jnp.where(is_g, jnp.exp(lg - gmax), 0.0), axis=-1, keepdims=True)
    e0 = N_GROUPS + EXP_PER_GROUP * gsel
    is_e = (lane >= e0) & (lane < e0 + EXP_PER_GROUP)
    v1 = jnp.max(jnp.where(is_e, lg, ninf), axis=-1, keepdims=True)
    i1 = jnp.min(jnp.where(is_e & (lg == v1), lane, big), axis=-1, keepdims=True)
    is_e2 = is_e & (lane != i1)
    v2 = jnp.max(jnp.where(is_e2, lg, ninf), axis=-1, keepdims=True)
    i2 = jnp.min(jnp.where(is_e2 & (lg == v2), lane, big), axis=-1, keepdims=True)
    t = jnp.exp(v2 - v1)
    w1 = pg / (1.0 + t)
    w2 = pg * t / (1.0 + t)

    tmr = lg.shape[0]
    oh1 = jnp.where(lane == i1, 1.0, 0.0)
    oh2 = jnp.where(lane == i2, 1.0, 0.0)
    before = jnp.where(lax.broadcasted_iota(jnp.int32, (tmr, tmr), 0)
                       > lax.broadcasted_iota(jnp.int32, (tmr, tmr), 1), 1.0, 0.0).astype(bf16)
    tot1 = jnp.sum(oh1, axis=0, keepdims=True)
    tot2 = jnp.sum(oh2, axis=0, keepdims=True)
    base = run_ref[...]
    r1 = jnp.sum(oh1 * (base + _dot(before, oh1.astype(bf16))), axis=-1, keepdims=True)
    r2 = jnp.sum(oh2 * (base + tot1 + _dot(before, oh2.astype(bf16))), axis=-1, keepdims=True)
    run_ref[...] = base + tot1 + tot2
    cnt_ref[...] = base + tot1 + tot2
    vals = (i1 - N_GROUPS, i2 - N_GROUPS, w1, w2, r1, r2)
    out = jnp.zeros_like(lg)
    for k, v in enumerate(vals):
        out = jnp.where(lane_i == k, v, out)
    route_ref[...] = out


def _out_proj(y, parts, g, w_router, *, tm):
    n_first = parts[0][0].shape[0] // tm
    split_y = isinstance(y, tuple)
    if split_y:
        d = y[0].shape[1]
        n = y[0].shape[0] + y[1].shape[0]
        in_specs = [pl.BlockSpec((tm, d), lambda i: (jnp.minimum(i, n_first - 1), 0)),
                    pl.BlockSpec((tm, d), lambda i: (jnp.maximum(i - n_first, 0), 0))]
        args = list(y)
    else:
        n, d = y.shape
        in_specs = [pl.BlockSpec((tm, d), lambda i: (i, 0))]
        args = [y]
    n_tiles = n // tm
    for op, os_, _ in parts:
        dk = op.shape[1]
        in_specs.append(pl.BlockSpec((tm, dk), lambda i: (jnp.minimum(i, n_first - 1), 0)))
        in_specs.append(pl.BlockSpec((tm, dk), lambda i: (jnp.maximum(i - n_first, 0), 0)))
        args += [op, os_]
    for _, _, w in parts:
        in_specs.append(pl.BlockSpec(w.shape, lambda i: (0, 0)))
        args.append(w)
    rh = w_router.astype(bf16)
    rl = (w_router - rh.astype(f32)).astype(bf16)
    in_specs += [pl.BlockSpec((1, d), lambda i: (0, 0)),
                 pl.BlockSpec((d, LANES), lambda i: (0, 0)),
                 pl.BlockSpec((d, LANES), lambda i: (0, 0))]
    args += [g.reshape(1, d), rh, rl]
    return pl.pallas_call(
        functools.partial(_out_kernel, n_parts=len(parts), n_first=n_first, split_y=split_y),
        grid=(n_tiles,), in_specs=in_specs,
        out_specs=[pl.BlockSpec((tm, d), lambda i: (i, 0)),
                   pl.BlockSpec((tm, d), lambda i: (i, 0)),
                   pl.BlockSpec((tm, LANES), lambda i: (i, 0)),
                   pl.BlockSpec((1, LANES), lambda i: (0, 0))],
        out_shape=[jax.ShapeDtypeStruct((n, d), f32),
                   jax.ShapeDtypeStruct((n, d), f32),
                   jax.ShapeDtypeStruct((n, LANES), f32),
                   jax.ShapeDtypeStruct((1, LANES), f32)],
        scratch_shapes=[pltpu.VMEM((1, LANES), f32)],
        compiler_params=_cp(("arbitrary",)), name="out_proj")(*args)


def _moe_kernel(te_ref, tv_ref, pos_ref, plo_ref, phi_ref, h_ref, wg_ref, wu_ref, wd_ref, out_ref,
                slot_ref, x0, x1, y0, y1, gsem, ssem, wg_s, wu_s, wd_s, *, tm, n):
    t = pl.program_id(0)
    nt = pl.num_programs(0)
    n2 = 2 * n
    xs = (x0, x1)
    ys = (y0, y1)
    valid = tv_ref[t] != 0
    next_valid = (t + 1 < nt) & (tv_ref[jnp.minimum(t + 1, nt - 1)] != 0)
    prev_valid = (t >= 1) & (tv_ref[jnp.maximum(t - 1, 0)] != 0)

    def gather_row(tile, r, p):
        s = slot_ref[tile * tm + r]
        tok = jnp.where(s < n, s, jnp.where(s < n2, s - n, s - n2))
        pltpu.make_async_copy(h_ref.at[pl.ds(tok, 1)], xs[p].at[pl.ds(r, 1)], gsem.at[p]).start()

    def scatter_row(tile, r, p):
        s = slot_ref[tile * tm + r]
        pltpu.make_async_copy(ys[p].at[pl.ds(r, 1)], out_ref.at[pl.ds(s, 1)], ssem.at[p]).start()

    def rows_loop(row_fn, tile, p):
        def body(r, carry):
            row_fn(tile, r, p)
            return carry
        lax.fori_loop(0, tm, body, 0, unroll=8)

    def gather_wait(p):
        pltpu.make_async_copy(h_ref.at[pl.ds(0, tm)], xs[p], gsem.at[p]).wait()

    def scatter_wait(p):
        pltpu.make_async_copy(ys[p], out_ref.at[pl.ds(0, tm)], ssem.at[p]).wait()

    def compute(p):
        x = xs[p][...].astype(bf16)
        g = _dot(x, wg_s[...])
        u = _dot(x, wu_s[...])
        ys[p][...] = _dot((_silu(g) * u).astype(bf16), wd_s[...])

    @pl.when(t == 0)
    def _():
        def per_expert(e, nxt):
            def body(r, v):
                slot_ref[r] = v
                return v + 1
            return lax.fori_loop(plo_ref[e], phi_ref[e], body, nxt)
        lax.fori_loop(0, N_EXPERTS, per_expert, n2)

        def body(s, carry):
            slot_ref[pos_ref[s]] = s
            return carry
        lax.fori_loop(0, n2, body, 0, unroll=8)
        rows_loop(gather_row, 0, 0)

    def step(p):
        q = 1 - p

        @pl.when(t >= 2)
        def _():
            scatter_wait(p)

        @pl.when(valid)
        def _():
            gather_wait(p)

            @pl.when((t == 0) | (te_ref[t] != te_ref[jnp.maximum(t - 1, 0)]))
            def _():
                wg_s[...] = wg_ref[0].astype(bf16)
                wu_s[...] = wu_ref[0].astype(bf16)
                wd_s[...] = wd_ref[0].astype(bf16)

            steady = next_valid & (t >= 1)

            @pl.when(steady)
            def _():
                for r in range(tm):
                    gather_row(t + 1, r, q)
                    scatter_row(t - 1, r, q)
                compute(p)

            @pl.when(jnp.logical_not(steady))
            def _():
                @pl.when(next_valid)
                def _():
                    rows_loop(gather_row, t + 1, q)

                @pl.when(t >= 1)
                def _():
                    rows_loop(scatter_row, t - 1, q)
                compute(p)

        @pl.when(jnp.logical_not(valid))
        def _():
            @pl.when(prev_valid)
            def _():
                rows_loop(scatter_row, t - 1, q)
            ys[p][...] = jnp.zeros(ys[p].shape, f32)
            pltpu.make_async_copy(ys[p], out_ref.at[pl.ds(pl.multiple_of(t * tm, tm), tm)], ssem.at[p]).start()

        @pl.when(t == nt - 1)
        def _():
            @pl.when(valid)
            def _():
                rows_loop(scatter_row, t, p)
            scatter_wait(p)
            scatter_wait(q)

    for p in range(2):
        pl.when(t % 2 == p)(functools.partial(step, p))


def _moe_experts(h2, pos, tile_e, tile_v, pad_lo, pad_hi, wg, wu, wd, *, layer, tm, n_tiles):
    n, d = h2.shape
    ff = wg.shape[2]
    assert n >= N_EXPERTS * tm, "padding rows read tokens 0 .. N_EXPERTS * (tm - 1)"
    e0 = layer * N_EXPERTS
    grid_spec = pltpu.PrefetchScalarGridSpec(
        num_scalar_prefetch=5, grid=(n_tiles,),
        in_specs=[pl.BlockSpec(memory_space=pl.ANY),
                  pl.BlockSpec((1, d, ff), lambda t, te, *_: (e0 + te[t], 0, 0)),
                  pl.BlockSpec((1, d, ff), lambda t, te, *_: (e0 + te[t], 0, 0)),
                  pl.BlockSpec((1, ff, d), lambda t, te, *_: (e0 + te[t], 0, 0))],
        out_specs=pl.BlockSpec(memory_space=pl.ANY),
        scratch_shapes=[pltpu.SMEM((n_tiles * tm,), jnp.int32),
                        pltpu.VMEM((tm, d), f32), pltpu.VMEM((tm, d), f32),
                        pltpu.VMEM((tm, d), f32), pltpu.VMEM((tm, d), f32),
                        pltpu.SemaphoreType.DMA((2,)), pltpu.SemaphoreType.DMA((2,)),
                        pltpu.VMEM((d, ff), bf16), pltpu.VMEM((d, ff), bf16), pltpu.VMEM((ff, d), bf16)])
    return pl.pallas_call(
        functools.partial(_moe_kernel, tm=tm, n=n), grid_spec=grid_spec,
        out_shape=jax.ShapeDtypeStruct((n_tiles * tm, d), f32),
        compiler_params=_cp(("arbitrary",)), name="moe")(
            tile_e, tile_v, pos, pad_lo, pad_hi, h2, wg, wu, wd)


def _moe(h2, route, counts, wg, wu, wd, *, layer, tm):
    n = h2.shape[0]
    n_tiles = (2 * n) // tm + N_EXPERTS
    counts = counts[0, N_GROUPS:N_GROUPS + N_EXPERTS].astype(jnp.int32)
    padded = ((counts + tm - 1) // tm) * tm
    ends = jnp.cumsum(padded)
    starts = ends - padded
    e = route[:, 0:2].astype(jnp.int32)
    rank = route[:, 4:6].astype(jnp.int32)
    pos = (starts[e] + rank).T.reshape(-1)
    tile_start = jnp.arange(n_tiles, dtype=jnp.int32) * tm
    tile_e = jnp.sum((tile_start[:, None] >= ends[None, :]).astype(jnp.int32), axis=1)
    tile_v = (tile_start < ends[-1]).astype(jnp.int32)
    tile_e = jnp.minimum(tile_e, N_EXPERTS - 1)
    return _moe_experts(h2, pos, tile_e, tile_v, starts + counts, ends, wg, wu, wd, layer=layer, tm=tm,
                        n_tiles=n_tiles)


def _final_kernel(y_ref, m0_ref, m1_ref, rt_ref, g_ref, o_ref):
    o_ref[...] = _rms(y_ref[...] + _moe_combine(m0_ref, m1_ref, rt_ref), g_ref[...])


def _final(y, moe_out, route, g, *, tm, row0, rows):
    n, d = y.shape
    r0 = row0 // tm
    r1 = (n + row0) // tm
    return pl.pallas_call(
        _final_kernel, grid=(rows // tm,),
        in_specs=[pl.BlockSpec((tm, d), lambda i: (r0 + i, 0)),
                  pl.BlockSpec((tm, d), lambda i: (r0 + i, 0)),
                  pl.BlockSpec((tm, d), lambda i: (r1 + i, 0)),
                  pl.BlockSpec((tm, LANES), lambda i: (r0 + i, 0)),
                  pl.BlockSpec((1, d), lambda i: (0, 0))],
        out_specs=pl.BlockSpec((tm, d), lambda i: (i, 0)),
        out_shape=jax.ShapeDtypeStruct((rows, d), f32),
        compiler_params=_cp(("arbitrary",)), name="final_norm")(y, moe_out, moe_out, route, g.reshape(1, d))


def _rope_tables(pos):
    half = ROT_DIM // 2
    inv = jnp.power(ROPE_THETA, -jnp.arange(half, dtype=f32) * 2.0 / ROT_DIM)
    ang = pos.astype(f32)[:, None] * inv[None, :]
    cos, sin = lax.optimization_barrier((jnp.cos(ang), jnp.sin(ang)))
    n = pos.shape[0]
    one = jnp.ones((n, DH_C - ROT_DIM), f32)
    zero = jnp.zeros((n, DH_C - ROT_DIM), f32)
    zh = jnp.zeros((n, half), f32)
    c = jnp.concatenate([cos, cos, one], axis=1)
    s1 = jnp.concatenate([zh, sin, zero], axis=1)
    s2 = jnp.concatenate([-sin, zh, zero], axis=1)
    rep = LANES // DH_C
    return tuple(jnp.tile(t, (1, rep)) for t in (c, s1, s2))


def _dup_heads(a):
    sh = a.shape[:-1]
    a4 = a.reshape(sh + (KV_C, 1, DH_C))
    return jnp.broadcast_to(a4, sh + (KV_C, 2, DH_C)).reshape(sh + (KV_C * 2 * DH_C,))


def kernel(x_prompt, x_sample, cache_gla_state, cache_band_k, cache_band_v, cache_swa_k, cache_swa_v,
           norm_mix, norm_ffn, norm_final, w_in_ab, w_gla_lr, b_gla_lr, gla_out_norm, rel_bias_tab,
           w_out_ab, w_in_c, b_in_c, attn_sinks, w_out_c, w_router_grp, w_router_exp, w_exp_gate,
           w_exp_up, w_exp_down):
    bp, sp, d = x_prompt.shape
    bs, ts, _ = x_sample.shape
    n_p = bp * sp
    n_s = bs * ts
    n = n_p + n_s
    tm = 512 if n_s % 512 == 0 else n_s
    tmo = min(tm, 256)
    tme = 256 if (2 * n) % 256 == 0 else 64
    ff = w_exp_gate.shape[-1]

    xp = x_prompt.reshape(n_p, d)
    xs_ = x_sample.reshape(n_s, d)

    def router_w(l):
        wr = jnp.concatenate([w_router_grp[l], w_router_exp[l].reshape(d, N_EXPERTS)], axis=1)
        return jnp.zeros((d, LANES), f32).at[:, :N_GROUPS + N_EXPERTS].set(wr)

    experts = (w_exp_gate.reshape(-1, d, ff), w_exp_up.reshape(-1, d, ff), w_exp_down.reshape(-1, ff, d))

    w0 = w_in_ab[0]
    la0 = 2 * H_A * DK_A + 2 * H_A * DV_A
    w_cat = jnp.concatenate(
        [w0[:, :la0], w0[:, la0 + GLA_RANK:], w0[:, la0:la0 + GLA_RANK],
         jnp.zeros((d, LANES - GLA_RANK), f32)], axis=1).astype(bf16)
    z = _proj(xp, norm_mix[0], w_cat, tm=tmo, tn=896, yb=xs_)

    c_s = min(CHUNK, ts)
    oa_p, gla_p = _gla(z, w_gla_lr[0], b_gla_lr[0], gla_out_norm[0],
                       jnp.zeros((bp, H_A, DK_A, DV_A), f32), bsz=bp, t=sp, row0=0, C=CHUNK, CB=4)
    oa_s, gla_s = _gla(z, w_gla_lr[0], b_gla_lr[0], gla_out_norm[0], cache_gla_state[0],
                       bsz=bs, t=ts, row0=n_p, C=c_s, CB=ts // c_s)

    hd = H_B * DH_B
    qr = 256
    nq = sp // qr
    qcb, kcb, vcb = _QB0 // hd, _KB0 // hd, _VB0 // hd
    cur = lambda cb: pl.BlockSpec((qr, hd), lambda b, q: (b * nq + q, cb))
    prev = lambda cb, back: pl.BlockSpec((qr, hd), lambda b, q: (b * nq + jnp.maximum(q - back, 0), cb))
    ob_p = _band_call(
        [cur(qcb), prev(kcb, 2), prev(kcb, 1), cur(kcb), prev(vcb, 2), prev(vcb, 1), cur(vcb)],
        [z] * 7, _band_bias_prompt(rel_bias_tab[0], qr, qr), bsz=bp, nq=nq, qr=qr, mask_first=True)

    lb = cache_band_k.shape[2]
    ck = cache_band_k[0].reshape(bs * lb, hd)
    cv = cache_band_v[0].reshape(bs * lb, hd)
    rs0 = n_p // ts
    new = lambda cb: pl.BlockSpec((ts, hd), lambda b, q: (rs0 + b, cb))
    cpiece = lambda k: pl.BlockSpec((lb // 2, hd), lambda b, q: (2 * b + k, 0))
    ob_s = _band_call(
        [new(qcb), cpiece(0), cpiece(1), new(kcb), cpiece(0), cpiece(1), new(vcb)],
        [z, ck, ck, z, cv, cv, z], _band_bias_sample(rel_bias_tab[0], ts, lb),
        bsz=bs, nq=1, qr=ts, mask_first=False)

    dmix_a = H_A * DV_A
    wo = w_out_ab[0].astype(bf16)
    y1, h2, route, cnt = _out_proj(
        (xp, xs_), [(oa_p, oa_s, wo[:dmix_a]), (ob_p, ob_s, wo[dmix_a:])], norm_ffn[0], router_w(0),
        tm=tmo)
    moe0 = _moe(h2, route, cnt, *experts, layer=0, tm=tme)

    keep = min(B_PREV * CHUNK, sp)

    def prompt_tail(a, rows, c0, width):
        return jnp.stack([a[(b + 1) * sp - rows:(b + 1) * sp, c0:c0 + width] for b in range(bp)])

    band_k_p = prompt_tail(z, keep, _KB0, hd).astype(f32).reshape(bp, keep, H_B, DH_B)[None]
    band_v_p = prompt_tail(z, keep, _VB0, hd).astype(f32).reshape(bp, keep, H_B, DH_B)[None]
    kb_s = z[n_p:, _KB0:_KB0 + hd].astype(f32).reshape(bs, ts, H_B, DH_B)
    vb_s = z[n_p:, _VB0:_VB0 + hd].astype(f32).reshape(bs, ts, H_B, DH_B)
    band_k_s = jnp.concatenate([cache_band_k[0], kb_s], axis=1)[:, ts:][None]
    band_v_s = jnp.concatenate([cache_band_v[0], vb_s], axis=1)[:, ts:][None]

    w1 = w_in_c[0]
    nq_c = H_C * DH_C
    nkv = KV_C * DH_C
    w1_cat = jnp.concatenate([w1[:, :nq_c], _dup_heads(w1[:, nq_c:nq_c + nkv]),
                              _dup_heads(w1[:, nq_c + nkv:])], axis=1).astype(bf16)
    b1 = b_in_c[0]
    b1_cat = jnp.concatenate([b1[:nq_c], _dup_heads(b1[nq_c:nq_c + nkv]), _dup_heads(b1[nq_c + nkv:])])
    uniq = _rope_tables(jnp.concatenate([jnp.arange(sp), PAST_LEN + jnp.arange(ts)]))
    rope = tuple(jnp.concatenate([jnp.tile(u[:sp], (bp, 1)), jnp.tile(u[sp:], (bs, 1))]) for u in uniq)
    y2, z1 = _proj(y1, norm_mix[1], w1_cat, tm=tmo, tn=512, add=(moe0, route), bias=b1_cat,
                   rope=rope, n_rope=(nq_c + 2 * nkv) // 512)

    qr_c = WINDOW
    nq1 = sp // qr_c
    kw = 2 * nkv
    kcb1, vcb1 = _KC0 // kw, _VC0 // kw
    qspec = pl.BlockSpec((qr_c, nq_c), lambda b, q: (b * nq1 + q, 0))
    cur1 = lambda cb: pl.BlockSpec((qr_c, kw), lambda b, q: (b * nq1 + q, cb))
    prev1 = lambda cb: pl.BlockSpec((qr_c, kw), lambda b, q: (b * nq1 + jnp.maximum(q - 1, 0), cb))
    sinks = attn_sinks[0].astype(f32)
    oc_p = _swa_call([qspec, prev1(kcb1), cur1(kcb1), prev1(vcb1), cur1(vcb1)], [z1] * 5, sinks,
                     bsz=bp, nq=nq1, qr=qr_c, chunk_mask=True)

    lc = cache_swa_k.shape[2]
    csk = _dup_heads(cache_swa_k[0].reshape(bs * lc, nkv))
    csv = _dup_heads(cache_swa_v[0].reshape(bs * lc, nkv))
    qspec_s = pl.BlockSpec((ts, nq_c), lambda b, q: (rs0 + b, 0))
    new1 = lambda cb: pl.BlockSpec((ts, kw), lambda b, q: (rs0 + b, cb))
    cspec = pl.BlockSpec((lc, kw), lambda b, q: (b, 0))
    oc_s = _swa_call([qspec_s, cspec, new1(kcb1), cspec, new1(vcb1)], [z1, csk, z1, csv, z1], sinks,
                     bsz=bs, nq=1, qr=ts, chunk_mask=False)

    y3, h4, route1, cnt1 = _out_proj(y2, [(oc_p, oc_s, w_out_c[0].astype(bf16))], norm_ffn[1],
                                     router_w(1), tm=tmo)
    moe1 = _moe(h4, route1, cnt1, *experts, layer=1, tm=tme)

    keep1 = min(WINDOW, sp)

    def undup(a):
        return a.astype(f32).reshape(a.shape[:-1] + (KV_C, 2, DH_C))[..., 0, :]

    swa_k_p = undup(prompt_tail(z1, keep1, _KC0, kw))[None]
    swa_v_p = undup(prompt_tail(z1, keep1, _VC0, kw))[None]
    k1_s = undup(z1[n_p:, _KC0:_KC0 + kw]).reshape(bs, ts, KV_C, DH_C)
    v1_s = undup(z1[n_p:, _VC0:_VC0 + kw]).reshape(bs, ts, KV_C, DH_C)
    swa_k_s = jnp.concatenate([cache_swa_k[0], k1_s], axis=1)[:, ts:][None]
    swa_v_s = jnp.concatenate([cache_swa_v[0], v1_s], axis=1)[:, ts:][None]

    y_prompt = _final(y3, moe1, route1, norm_final, tm=tm, row0=0, rows=n_p).reshape(bp, sp, d)
    y_sample = _final(y3, moe1, route1, norm_final, tm=tm, row0=n_p, rows=n_s).reshape(bs, ts, d)

    return (y_prompt, y_sample, gla_p[None], gla_s[None].astype(cache_gla_state.dtype),
            band_k_p, band_v_p, band_k_s, band_v_s, swa_k_p, swa_v_p, swa_k_s, swa_v_s)
```

```python
import functools

import numpy as np
import jax
import jax.numpy as jnp
from jax import lax
from jax.experimental import pallas as pl
from jax.experimental.pallas import tpu as pltpu

f32 = jnp.float32
bf16 = jnp.bfloat16

CHUNK = 64
EPS = 1e-6
NEG_INF = -1e30
PAST_LEN = 1024
H_A = 8
DK_A = 64
DV_A = 128
GLA_RANK = 16
GLA_TAU = 16.0
H_B = 8
DH_B = 128
B_PREV = 8
REL_CLIP = 128
H_C = 32
KV_C = 4
G_C = H_C // KV_C
DH_C = 64
WINDOW = 128
ROT_DIM = DH_C // 4
ROPE_THETA = 500000.0
N_GROUPS = 4
EXP_PER_GROUP = 8
N_EXPERTS = N_GROUPS * EXP_PER_GROUP

LANES = 128
VMEM_LIMIT = 52 * 1024 * 1024

_QA0, _KA0, _VA0, _GA0 = 0, 512, 1024, 2048
_QB0, _KB0, _VB0, _LA0 = 3072, 4096, 5120, 6144
_NC_AB = 6272
_QC0, _KC0, _VC0 = 0, 2048, 2560
_NC_C = 3072


def _cp(sem):
    return pltpu.CompilerParams(dimension_semantics=sem, vmem_limit_bytes=VMEM_LIMIT)


def _dot(a, b):
    return jnp.dot(a, b, preferred_element_type=f32)


def _dot_nt(a, b):
    return lax.dot_general(a, b, (((1,), (1,)), ((), ())), preferred_element_type=f32)


def _dot_tn(a, b):
    return lax.dot_general(a, b, (((0,), (0,)), ((), ())), preferred_element_type=f32)


def _rms(x, g):
    ms = jnp.mean(x * x, axis=-1, keepdims=True)
    return (x * lax.rsqrt(ms + EPS)) * g


def _silu(x):
    return x / (1.0 + jnp.exp(-x))


def _slab_load(ref, rows, nslab):
    return jnp.concatenate([ref[pl.ds(c, rows, stride=nslab), :] for c in range(nslab)], axis=1)


def _slab_store(ref, val, nslab):
    rows = val.shape[0]
    for c in range(nslab):
        ref[pl.ds(c, rows, stride=nslab), :] = val[:, c * LANES:(c + 1) * LANES]


def _moe_combine(m0_ref, m1_ref, route_ref):
    rt = route_ref[...]
    rows = rt.shape[0]
    nslab = m0_ref.shape[0] // rows
    return rt[:, 2:3] * _slab_load(m0_ref, rows, nslab) + rt[:, 3:4] * _slab_load(m1_ref, rows, nslab)


def _proj_kernel(*refs, n_first, has_add, has_bias, n_rope, tn):
    it = iter(refs)
    ya_ref = next(it)
    yb_ref = next(it) if n_first is not None else None
    if has_add:
        m0_ref, m1_ref, rt_ref = next(it), next(it), next(it)
    g_ref = next(it)
    w_ref = next(it)
    b_ref = next(it) if has_bias else None
    if n_rope:
        c_ref, s1_ref, s2_ref = next(it), next(it), next(it)
    ysum_ref = next(it) if has_add else None
    z_ref = next(it)
    h_ref = next(it)
    i = pl.program_id(0)

    x = ya_ref[...]
    if n_first is not None:
        x = jnp.where(i < n_first, x, yb_ref[...])
    if has_add:
        x = x + _moe_combine(m0_ref, m1_ref, rt_ref)
        ysum_ref[...] = x
    h_ref[...] = _rms(x, g_ref[...]).astype(bf16)

    def rope(acc):
        rep = tn // LANES
        c = jnp.tile(c_ref[...], (1, rep))
        s1 = jnp.tile(s1_ref[...], (1, rep))
        s2 = jnp.tile(s2_ref[...], (1, rep))
        half = ROT_DIM // 2
        lo = pltpu.roll(acc, half, axis=1)
        hi = pltpu.roll(acc, tn - half, axis=1)
        return acc * c + lo * s1 + hi * s2

    def col_tile(j, carry):
        cols = pl.ds(pl.multiple_of(j * tn, tn), tn)
        acc = _dot(h_ref[...], w_ref[:, cols])
        if has_bias:
            acc = acc + b_ref[:, cols]
        if n_rope:
            @pl.when(j < n_rope)
            def _():
                z_ref[:, cols] = rope(acc).astype(z_ref.dtype)

            @pl.when(j >= n_rope)
            def _():
                z_ref[:, cols] = acc.astype(z_ref.dtype)
        else:
            z_ref[:, cols] = acc.astype(z_ref.dtype)
        return carry

    lax.fori_loop(0, z_ref.shape[1] // tn, col_tile, 0)


def _proj(ya, g, w, *, tm, tn, yb=None, add=None, bias=None, rope=None, n_rope=0):
    d = ya.shape[1]
    na = ya.shape[0] // tm
    nb = 0 if yb is None else yb.shape[0] // tm
    n = (na + nb) * tm
    nc = w.shape[1]
    once = dict(pipeline_mode=pl.Buffered(1))
    in_specs = []
    args = []
    if yb is None:
        in_specs.append(pl.BlockSpec((tm, d), lambda i: (i, 0)))
        args.append(ya)
    else:
        in_specs.append(pl.BlockSpec((tm, d), lambda i: (jnp.minimum(i, na - 1), 0)))
        in_specs.append(pl.BlockSpec((tm, d), lambda i: (jnp.maximum(i - na, 0), 0)))
        args += [ya, yb]
    if add is not None:
        moe_out, route = add
        nslab = d // LANES
        in_specs.append(pl.BlockSpec((tm * nslab, LANES), lambda i: (i, 0)))
        in_specs.append(pl.BlockSpec((tm * nslab, LANES), lambda i: (i + na, 0)))
        in_specs.append(pl.BlockSpec((tm, LANES), lambda i: (i, 0)))
        args += [moe_out, moe_out, route]
    in_specs.append(pl.BlockSpec((1, d), lambda i: (0, 0)))
    args.append(g.reshape(1, d))
    in_specs.append(pl.BlockSpec((d, nc), lambda i: (0, 0), **once))
    args.append(w)
    if bias is not None:
        in_specs.append(pl.BlockSpec((1, nc), lambda i: (0, 0)))
        args.append(bias.reshape(1, nc))
    if n_rope:
        for t in rope:
            in_specs.append(pl.BlockSpec((tm, LANES), lambda i: (i, 0)))
            args.append(t)
    out_shape = []
    out_specs = []
    if add is not None:
        out_shape.append(jax.ShapeDtypeStruct((n, d), f32))
        out_specs.append(pl.BlockSpec((tm, d), lambda i: (i, 0)))
    out_shape.append(jax.ShapeDtypeStruct((n, nc), bf16))
    out_specs.append(pl.BlockSpec((tm, nc), lambda i: (i, 0)))
    kern = functools.partial(_proj_kernel, n_first=(na if yb is not None else None),
                             has_add=add is not None, has_bias=bias is not None, n_rope=n_rope, tn=tn)
    out = pl.pallas_call(
        kern, grid=(na + nb,), in_specs=in_specs, out_specs=out_specs, out_shape=out_shape,
        scratch_shapes=[pltpu.VMEM((tm, d), bf16)],
        compiler_params=_cp(("arbitrary",)), name="proj")(*args)
    return out if add is not None else out[0]


def _gla_kernel(q_ref, k_ref, v_ref, g_ref, la_ref, wlr_ref, blr_ref, gn_ref, s0_ref,
                o_ref, sfin_ref, st_ref, *, C, CB):
    s = pl.program_id(1)

    @pl.when(s == 0)
    def _():
        st_ref[...] = s0_ref[0]

    row = lax.broadcasted_iota(jnp.int32, (C, C), 0)
    col = lax.broadcasted_iota(jnp.int32, (C, C), 1)
    tril = row >= col
    tri = jnp.where(tril, 1.0, 0.0).astype(bf16)
    lane = lax.broadcasted_iota(jnp.int32, (C, LANES), 1)
    even = lane < DK_A
    srow = lax.broadcasted_iota(jnp.int32, (2 * DV_A, LANES), 0)
    slane = lax.broadcasted_iota(jnp.int32, (2 * DV_A, LANES), 1)
    blockdiag = (srow < DV_A) == (slane < DK_A)
    wlr = wlr_ref[...]
    blr = blr_ref[...]
    gn = gn_ref[...]

    def chunk(c, carry):
        rows = pl.ds(pl.multiple_of(c * C, C), C)
        pre = _dot(la_ref[rows, :], wlr) + blr
        loga = -(jnp.maximum(-pre, 0.0) + jnp.log1p(jnp.exp(-jnp.abs(pre)))) * (1.0 / GLA_TAU)
        hi = loga.astype(bf16)
        lo = (loga - hi.astype(f32)).astype(bf16)
        b = _dot(tri, hi) + _dot(tri, lo)
        b_last = b[C - 1:C, :]
        qf = q_ref[rows, :].astype(f32) * (DK_A ** -0.5)
        kf = k_ref[rows, :].astype(f32)
        q_dec = (qf * jnp.exp(b)).astype(bf16)
        k_inv = (kf * jnp.exp(-b)).astype(bf16)
        k_end = (kf * jnp.exp(b_last - b)).astype(bf16)
        dec = jnp.exp(b_last)
        for p in range(H_A // 2):
            sl = slice(p * LANES, (p + 1) * LANES)
            qp = q_dec[:, sl]
            kp = k_inv[:, sl]
            vp = v_ref[rows, p * 2 * DV_A:(p + 1) * 2 * DV_A]
            zero = jnp.zeros_like(qp)
            att_e = jnp.where(tril, _dot_nt(jnp.where(even, qp, zero), kp), 0.0).astype(bf16)
            att_o = jnp.where(tril, _dot_nt(jnp.where(even, zero, qp), kp), 0.0).astype(bf16)
            o_e = _dot(att_e, vp)[:, :DV_A]
            o_o = _dot(att_o, vp)[:, DV_A:]
            st = st_ref[p]
            o_inter = _dot_nt(qp, st.astype(bf16))
            ds = _dot_tn(vp, k_end[:, sl])
            st_ref[p] = dec[:, sl] * st + jnp.where(blockdiag, ds, 0.0)
            for half, o_intra in ((0, o_e), (1, o_o)):
                h = 2 * p + half
                hs = slice(h * DV_A, (h + 1) * DV_A)
                o = o_intra + o_inter[:, half * DV_A:(half + 1) * DV_A]
                gate = g_ref[rows, hs].astype(f32)
                o_ref[rows, hs] = (_rms(o, gn) * _silu(gate)).astype(o_ref.dtype)
        return carry

    lax.fori_loop(0, CB, chunk, 0)

    @pl.when(s == pl.num_programs(1) - 1)
    def _():
        sfin_ref[0] = st_ref[...]


def _state_to_kernel_layout(s0):
    bsz = s0.shape[0]
    st = jnp.swapaxes(s0, -1, -2).reshape(bsz, H_A // 2, 2, DV_A, DK_A)
    z = jnp.zeros_like(st[:, :, 0])
    top = jnp.concatenate([st[:, :, 0], z], axis=-1)
    bot = jnp.concatenate([z, st[:, :, 1]], axis=-1)
    return jnp.concatenate([top, bot], axis=-2)


def _state_from_kernel_layout(st):
    bsz = st.shape[0]
    e = st[:, :, :DV_A, :DK_A]
    o = st[:, :, DV_A:, DK_A:]
    s = jnp.stack([e, o], axis=2).reshape(bsz, H_A, DV_A, DK_A)
    return jnp.swapaxes(s, -1, -2)


def _gla(z, w_lr, b_lr, g_norm, s0, *, bsz, t, row0, C, CB):
    rb = C * CB
    ns = t // rb
    r0 = row0 // rb

    def zspec(width, cb):
        return pl.BlockSpec((rb, width), lambda b, s: (r0 + b * ns + s, cb))

    hk = H_A * DK_A
    hv = H_A * DV_A
    st0 = _state_to_kernel_layout(s0.astype(f32))
    wl = jnp.zeros((LANES, hk), f32).at[:GLA_RANK].set(w_lr).astype(bf16)
    st_shape = (H_A // 2, 2 * DV_A, 2 * DK_A)
    o, sfin = pl.pallas_call(
        functools.partial(_gla_kernel, C=C, CB=CB),
        grid=(bsz, ns),
        in_specs=[zspec(hk, _QA0 // hk), zspec(hk, _KA0 // hk), zspec(hv, _VA0 // hv),
                  zspec(hv, _GA0 // hv), zspec(LANES, _LA0 // LANES),
                  pl.BlockSpec((LANES, hk), lambda b, s: (0, 0)),
                  pl.BlockSpec((1, hk), lambda b, s: (0, 0)),
                  pl.BlockSpec((1, DV_A), lambda b, s: (0, 0)),
                  pl.BlockSpec((1,) + st_shape, lambda b, s: (b, 0, 0, 0))],
        out_specs=[pl.BlockSpec((rb, hv), lambda b, s: (b * ns + s, 0)),
                   pl.BlockSpec((1,) + st_shape, lambda b, s: (b, 0, 0, 0))],
        out_shape=[jax.ShapeDtypeStruct((bsz * t, hv), bf16),
                   jax.ShapeDtypeStruct((bsz,) + st_shape, f32)],
        scratch_shapes=[pltpu.VMEM(st_shape, f32)],
        compiler_params=_cp(("arbitrary", "arbitrary")), name="gla")(
            z, z, z, z, z, wl, b_lr.reshape(1, hk).astype(f32), g_norm.reshape(1, DV_A).astype(f32), st0)
    return o, _state_from_kernel_layout(sfin)


def _band_kernel(q_ref, ka_ref, kb_ref, kc_ref, va_ref, vb_ref, vc_ref, ba_ref, bb_ref, bc_ref,
                 o_ref, *, mask_first):
    qi = pl.program_id(1)
    for h in range(H_B):
        sl = slice(h * DH_B, (h + 1) * DH_B)
        qh = (q_ref[:, sl].astype(f32) * (DH_B ** -0.5)).astype(bf16)
        sa = _dot_nt(qh, ka_ref[:, sl].astype(bf16)) + ba_ref[h]
        sb = _dot_nt(qh, kb_ref[:, sl].astype(bf16)) + bb_ref[h]
        sc = _dot_nt(qh, kc_ref[:, sl].astype(bf16)) + bc_ref[h]
        if mask_first:
            sa = jnp.where(qi >= 2, sa, NEG_INF)
            sb = jnp.where(qi >= 1, sb, NEG_INF)
        m = jnp.maximum(jnp.maximum(jnp.max(sa, axis=-1, keepdims=True),
                                    jnp.max(sb, axis=-1, keepdims=True)),
                        jnp.max(sc, axis=-1, keepdims=True))
        ea = jnp.exp(sa - m)
        eb = jnp.exp(sb - m)
        ec = jnp.exp(sc - m)
        inv = 1.0 / (jnp.sum(ea, axis=-1, keepdims=True) + jnp.sum(eb, axis=-1, keepdims=True)
                     + jnp.sum(ec, axis=-1, keepdims=True))
        o = (_dot((ea * inv).astype(bf16), va_ref[:, sl].astype(bf16))
             + _dot((eb * inv).astype(bf16), vb_ref[:, sl].astype(bf16))
             + _dot((ec * inv).astype(bf16), vc_ref[:, sl].astype(bf16)))
        o_ref[:, sl] = o.astype(o_ref.dtype)


def _band_call(qkv_specs, args, biases, *, bsz, nq, qr, mask_first):
    hd = H_B * DH_B
    bias_specs = [pl.BlockSpec(b.shape, lambda b_, q_: (0, 0, 0)) for b in biases]
    return pl.pallas_call(
        functools.partial(_band_kernel, mask_first=mask_first),
        grid=(bsz, nq),
        in_specs=list(qkv_specs) + bias_specs,
        out_specs=pl.BlockSpec((qr, hd), lambda b, q: (b * nq + q, 0)),
        out_shape=jax.ShapeDtypeStruct((bsz * nq * qr, hd), bf16),
        compiler_params=_cp(("arbitrary", "arbitrary")), name="band")(*args, *biases)


def _rel_bias_matrix(tab, nq, nk, off):
    m = nq + nk
    j = np.arange(m)
    diff = np.where(j < nk, -j, m - j)
    idx = np.clip(diff + off, -REL_CLIP, REL_CLIP) + REL_CLIP
    u = tab.astype(f32)[:, jnp.asarray(idx)]
    h = tab.shape[0]
    return jnp.tile(u, (1, nq))[:, :nq * (m - 1)].reshape(h, nq, m - 1)[:, :, :nk]


def _band_bias_prompt(tab, qr, kb):
    tot = 2 * kb + qr
    qc = (np.arange(qr)[:, None] + 2 * kb) // CHUNK
    kc = np.arange(tot)[None, :] // CHUNK
    valid = (kc <= qc) & (kc >= qc - B_PREV)
    bias = jnp.where(jnp.asarray(valid)[None], _rel_bias_matrix(tab, qr, tot, 2 * kb), NEG_INF)
    return bias[:, :, :kb], bias[:, :, kb:2 * kb], bias[:, :, 2 * kb:]


def _band_bias_sample(tab, t, lc):
    bias = _rel_bias_matrix(tab, t, lc + t, lc)
    return bias[:, :, :lc // 2], bias[:, :, lc // 2:lc], bias[:, :, lc:]


def _swa_kernel(sink_ref, q_ref, kp_ref, kc_ref, vp_ref, vc_ref, o_ref, qs_ref, *, chunk_mask):
    qi = pl.program_id(1)
    qr = q_ref.shape[0]
    pr = kp_ref.shape[0]
    nl = G_C * LANES
    even = lax.broadcasted_iota(jnp.int32, (qr, LANES), 1) < DH_C
    top = lax.broadcasted_iota(jnp.int32, (LANES, LANES), 0) < DH_C
    if chunk_mask:
        kch = lax.broadcasted_iota(jnp.int32, (pr, nl), 0) // CHUNK
        qch = (lax.broadcasted_iota(jnp.int32, (pr, nl), 1) % LANES) // CHUNK
        ok_prev = (kch >= qch) & (qi >= 1)
        kch = lax.broadcasted_iota(jnp.int32, (qr, nl), 0) // CHUNK
        qch = (lax.broadcasted_iota(jnp.int32, (qr, nl), 1) % LANES) // CHUNK
        ok_cur = kch <= qch
    if qr < LANES:
        qs_ref[...] = jnp.zeros_like(qs_ref)
    for c in range(KV_C):
        ks = slice(c * LANES, (c + 1) * LANES)
        for p in range(G_C // 2):
            cols = slice(c * G_C * DH_C + p * LANES, c * G_C * DH_C + (p + 1) * LANES)
            qp = (q_ref[:, cols].astype(f32) * (DH_C ** -0.5)).astype(bf16)
            zero = jnp.zeros_like(qp)
            qs_ref[2 * p * LANES:2 * p * LANES + qr, :] = jnp.where(even, qp, zero)
            qs_ref[(2 * p + 1) * LANES:(2 * p + 1) * LANES + qr, :] = jnp.where(even, zero, qp)
        q_all = qs_ref[...]
        sink = jnp.concatenate(
            [jnp.full((1, LANES), sink_ref[c * G_C + j], f32) for j in range(G_C)], axis=1)
        st_prev = _dot_nt(kp_ref[:, ks].astype(bf16), q_all)
        st_cur = _dot_nt(kc_ref[:, ks].astype(bf16), q_all)
        if chunk_mask:
            st_prev = jnp.where(ok_prev, st_prev, NEG_INF)
            st_cur = jnp.where(ok_cur, st_cur, NEG_INF)
        m = jnp.maximum(jnp.maximum(jnp.max(st_prev, axis=0, keepdims=True),
                                    jnp.max(st_cur, axis=0, keepdims=True)), sink)
        e_prev = jnp.exp(st_prev - m)
        e_cur = jnp.exp(st_cur - m)
        inv = 1.0 / (jnp.sum(e_prev, axis=0, keepdims=True) + jnp.sum(e_cur, axis=0, keepdims=True)
                     + jnp.exp(sink - m))
        ot = (_dot_tn(vp_ref[:, ks].astype(bf16), (e_prev * inv).astype(bf16))
              + _dot_tn(vc_ref[:, ks].astype(bf16), (e_cur * inv).astype(bf16)))
        for p in range(G_C // 2):
            cols = slice(c * G_C * DH_C + p * LANES, c * G_C * DH_C + (p + 1) * LANES)
            blk = jnp.where(top, ot[:, 2 * p * LANES:(2 * p + 1) * LANES],
                            ot[:, (2 * p + 1) * LANES:(2 * p + 2) * LANES])
            o_ref[:, cols] = blk.T[:qr].astype(o_ref.dtype)


def _swa_call(specs, args, sinks, *, bsz, nq, qr, chunk_mask):
    hd = H_C * DH_C
    return pl.pallas_call(
        functools.partial(_swa_kernel, chunk_mask=chunk_mask),
        grid=(bsz, nq),
        in_specs=[pl.BlockSpec(memory_space=pltpu.SMEM)] + list(specs),
        out_specs=pl.BlockSpec((qr, hd), lambda b, q: (b * nq + q, 0)),
        out_shape=jax.ShapeDtypeStruct((bsz * nq * qr, hd), bf16),
        scratch_shapes=[pltpu.VMEM((G_C * LANES, LANES), bf16)],
        compiler_params=_cp(("arbitrary", "arbitrary")), name="swa")(sinks, *args)


def _out_kernel(*refs, n_parts, n_first, split_y):
    ny = 2 if split_y else 1
    y_refs = refs[:ny]
    refs = refs[ny - 1:]
    o_refs = refs[1:1 + 2 * n_parts]
    w_refs = refs[1 + 2 * n_parts:1 + 3 * n_parts]
    g_ref, rh_ref, rl_ref = refs[1 + 3 * n_parts:4 + 3 * n_parts]
    y1_ref, h2_ref, route_ref, cnt_ref, run_ref = refs[4 + 3 * n_parts:]
    i = pl.program_id(0)

    @pl.when(i == 0)
    def _():
        run_ref[...] = jnp.zeros_like(run_ref)

    acc = y_refs[0][...]
    if split_y:
        acc = jnp.where(i < n_first, acc, y_refs[1][...])
    for k in range(n_parts):
        lhs = jnp.where(i < n_first, o_refs[2 * k][...], o_refs[2 * k + 1][...])
        acc = acc + _dot(lhs, w_refs[k][...])
    y1_ref[...] = acc
    h2 = _rms(acc, g_ref[...])
    hi = h2.astype(bf16)
    _slab_store(h2_ref, hi.astype(f32), hi.shape[1] // LANES)
    lo = (h2 - hi.astype(f32)).astype(bf16)
    lg = _dot(hi, rh_ref[...]) + _dot(hi, rl_ref[...]) + _dot(lo, rh_ref[...])

    lane_i = lax.broadcasted_iota(jnp.int32, lg.shape, 1)
    lane = lane_i.astype(f32)
    big = float(LANES)
    ninf = -jnp.inf
    is_g = lane_i < N_GROUPS
    gmax = jnp.max(jnp.where(is_g, lg, ninf), axis=-1, keepdims=True)
    gsel = jnp.min(jnp.where(is_g & (lg == gmax), lane, big), axis=-1, keepdims=True)
    pg = 1.0 / jnp.sum(jnp.where(is_g, jnp.exp(lg - gmax), 0.0), axis=-1, keepdims=True)
    e0 = N_GROUPS + EXP_PER_GROUP * gsel
    is_e = (lane >= e0) & (lane < e0 + EXP_PER_GROUP)
    v1 = jnp.max(jnp.where(is_e, lg, ninf), axis=-1, keepdims=True)
    i1 = jnp.min(jnp.where(is_e & (lg == v1), lane, big), axis=-1, keepdims=True)
    is_e2 = is_e & (lane != i1)
    v2 = jnp.max(jnp.where(is_e2, lg, ninf), axis=-1, keepdims=True)
    i2 = jnp.min(jnp.where(is_e2 & (lg == v2), lane, big), axis=-1, keepdims=True)
    t = jnp.exp(v2 - v1)
    w1 = pg / (1.0 + t)
    w2 = pg * t / (1.0 + t)

    tmr = lg.shape[0]
    oh1 = jnp.where(lane == i1, 1.0, 0.0)
    oh2 = jnp.where(lane == i2, 1.0, 0.0)
    before = jnp.where(lax.broadcasted_iota(jnp.int32, (tmr, tmr), 0)
                       > lax.broadcasted_iota(jnp.int32, (tmr, tmr), 1), 1.0, 0.0).astype(bf16)
    tot1 = jnp.sum(oh1, axis=0, keepdims=True)
    tot2 = jnp.sum(oh2, axis=0, keepdims=True)
    base = run_ref[...]
    r1 = jnp.sum(oh1 * (base + _dot(before, oh1.astype(bf16))), axis=-1, keepdims=True)
    r2 = jnp.sum(oh2 * (base + tot1 + _dot(before, oh2.astype(bf16))), axis=-1, keepdims=True)
    run_ref[...] = base + tot1 + tot2
    cnt_ref[...] = base + tot1 + tot2
    vals = (i1 - N_GROUPS, i2 - N_GROUPS, w1, w2, r1, r2)
    out = jnp.zeros_like(lg)
    for k, v in enumerate(vals):
        out = jnp.where(lane_i == k, v, out)
    route_ref[...] = out


def _out_proj(y, parts, g, w_router, *, tm):
    n_first = parts[0][0].shape[0] // tm
    split_y = isinstance(y, tuple)
    if split_y:
        d = y[0].shape[1]
        n = y[0].shape[0] + y[1].shape[0]
        in_specs = [pl.BlockSpec((tm, d), lambda i: (jnp.minimum(i, n_first - 1), 0)),
                    pl.BlockSpec((tm, d), lambda i: (jnp.maximum(i - n_first, 0), 0))]
        args = list(y)
    else:
        n, d = y.shape
        in_specs = [pl.BlockSpec((tm, d), lambda i: (i, 0))]
        args = [y]
    n_tiles = n // tm
    for op, os_, _ in parts:
        dk = op.shape[1]
        in_specs.append(pl.BlockSpec((tm, dk), lambda i: (jnp.minimum(i, n_first - 1), 0)))
        in_specs.append(pl.BlockSpec((tm, dk), lambda i: (jnp.maximum(i - n_first, 0), 0)))
        args += [op, os_]
    for _, _, w in parts:
        in_specs.append(pl.BlockSpec(w.shape, lambda i: (0, 0)))
        args.append(w)
    rh = w_router.astype(bf16)
    rl = (w_router - rh.astype(f32)).astype(bf16)
    in_specs += [pl.BlockSpec((1, d), lambda i: (0, 0)),
                 pl.BlockSpec((d, LANES), lambda i: (0, 0)),
                 pl.BlockSpec((d, LANES), lambda i: (0, 0))]
    args += [g.reshape(1, d), rh, rl]
    return pl.pallas_call(
        functools.partial(_out_kernel, n_parts=len(parts), n_first=n_first, split_y=split_y),
        grid=(n_tiles,), in_specs=in_specs,
        out_specs=[pl.BlockSpec((tm, d), lambda i: (i, 0)),
                   pl.BlockSpec((tm * (d // LANES), LANES), lambda i: (i, 0)),
                   pl.BlockSpec((tm, LANES), lambda i: (i, 0)),
                   pl.BlockSpec((1, LANES), lambda i: (0, 0))],
        out_shape=[jax.ShapeDtypeStruct((n, d), f32),
                   jax.ShapeDtypeStruct((n * (d // LANES), LANES), f32),
                   jax.ShapeDtypeStruct((n, LANES), f32),
                   jax.ShapeDtypeStruct((1, LANES), f32)],
        scratch_shapes=[pltpu.VMEM((1, LANES), f32)],
        compiler_params=_cp(("arbitrary",)), name="out_proj")(*args)


def _moe_kernel(te_ref, tv_ref, pos_ref, plo_ref, phi_ref, h_ref, wg_ref, wu_ref, wd_ref, out_ref,
                slot_ref, x0, x1, y0, y1, gsem, ssem, wg_s, wu_s, wd_s, *, tm, n):
    t = pl.program_id(0)
    nt = pl.num_programs(0)
    n2 = 2 * n
    xs = (x0, x1)
    ys = (y0, y1)
    valid = tv_ref[t] != 0
    next_valid = (t + 1 < nt) & (tv_ref[jnp.minimum(t + 1, nt - 1)] != 0)
    prev_valid = (t >= 1) & (tv_ref[jnp.maximum(t - 1, 0)] != 0)

    nslab = x0.shape[0] // tm

    def slab(idx):
        return pl.ds(pl.multiple_of(idx * nslab, nslab), nslab)

    def gather_row(tile, r, p):
        s = slot_ref[tile * tm + r]
        tok = jnp.where(s < n, s, jnp.where(s < n2, s - n, s - n2))
        pltpu.make_async_copy(h_ref.at[slab(tok)], xs[p].at[slab(r)], gsem.at[p]).start()

    def scatter_row(tile, r, p):
        s = slot_ref[tile * tm + r]
        pltpu.make_async_copy(ys[p].at[slab(r)], out_ref.at[slab(s)], ssem.at[p]).start()

    def rows_loop(row_fn, tile, p):
        def body(r, carry):
            row_fn(tile, r, p)
            return carry
        lax.fori_loop(0, tm, body, 0, unroll=8)

    def gather_wait(p):
        pltpu.make_async_copy(h_ref.at[pl.ds(0, tm * nslab)], xs[p], gsem.at[p]).wait()

    def scatter_wait(p):
        pltpu.make_async_copy(ys[p], out_ref.at[pl.ds(0, tm * nslab)], ssem.at[p]).wait()

    def compute(p):
        x = _slab_load(xs[p], tm, nslab).astype(bf16)
        g = _dot(x, wg_s[...])
        u = _dot(x, wu_s[...])
        _slab_store(ys[p], _dot((_silu(g) * u).astype(bf16), wd_s[...]), nslab)

    @pl.when(t == 0)
    def _():
        def per_expert(e, nxt):
            def body(r, v):
                slot_ref[r] = v
                return v + 1
            return lax.fori_loop(plo_ref[e], phi_ref[e], body, nxt)
        lax.fori_loop(0, N_EXPERTS, per_expert, n2)

        def body(s, carry):
            slot_ref[pos_ref[s]] = s
            return carry
        lax.fori_loop(0, n2, body, 0, unroll=8)
        rows_loop(gather_row, 0, 0)

    def step(p):
        q = 1 - p

        @pl.when(t >= 2)
        def _():
            scatter_wait(p)

        @pl.when(valid)
        def _():
            gather_wait(p)

            @pl.when((t == 0) | (te_ref[t] != te_ref[jnp.maximum(t - 1, 0)]))
            def _():
                wg_s[...] = wg_ref[0].astype(bf16)
                wu_s[...] = wu_ref[0].astype(bf16)
                wd_s[...] = wd_ref[0].astype(bf16)

            steady = next_valid & (t >= 1)

            @pl.when(steady)
            def _():
                for r in range(tm):
                    gather_row(t + 1, r, q)
                    scatter_row(t - 1, r, q)
                compute(p)

            @pl.when(jnp.logical_not(steady))
            def _():
                @pl.when(next_valid)
                def _():
                    rows_loop(gather_row, t + 1, q)

                @pl.when(t >= 1)
                def _():
                    rows_loop(scatter_row, t - 1, q)
                compute(p)

        @pl.when(jnp.logical_not(valid))
        def _():
            @pl.when(prev_valid)
            def _():
                rows_loop(scatter_row, t - 1, q)
            ys[p][...] = jnp.zeros(ys[p].shape, f32)
            rows = tm * nslab
            pltpu.make_async_copy(ys[p], out_ref.at[pl.ds(pl.multiple_of(t * rows, rows), rows)],
                                  ssem.at[p]).start()

        @pl.when(t == nt - 1)
        def _():
            @pl.when(valid)
            def _():
                rows_loop(scatter_row, t, p)
            scatter_wait(p)
            scatter_wait(q)

    for p in range(2):
        pl.when(t % 2 == p)(functools.partial(step, p))


def _moe_experts(h2, pos, tile_e, tile_v, pad_lo, pad_hi, wg, wu, wd, *, layer, tm, n_tiles):
    d, ff = wg.shape[1:]
    nslab = d // LANES
    n = h2.shape[0] // nslab
    slab_rows = tm * nslab
    assert n >= N_EXPERTS * tm, "padding rows read tokens 0 .. N_EXPERTS * (tm - 1)"
    e0 = layer * N_EXPERTS
    grid_spec = pltpu.PrefetchScalarGridSpec(
        num_scalar_prefetch=5, grid=(n_tiles,),
        in_specs=[pl.BlockSpec(memory_space=pl.ANY),
                  pl.BlockSpec((1, d, ff), lambda t, te, *_: (e0 + te[t], 0, 0)),
                  pl.BlockSpec((1, d, ff), lambda t, te, *_: (e0 + te[t], 0, 0)),
                  pl.BlockSpec((1, ff, d), lambda t, te, *_: (e0 + te[t], 0, 0))],
        out_specs=pl.BlockSpec(memory_space=pl.ANY),
        scratch_shapes=[pltpu.SMEM((n_tiles * tm,), jnp.int32),
                        pltpu.VMEM((slab_rows, LANES), f32), pltpu.VMEM((slab_rows, LANES), f32),
                        pltpu.VMEM((slab_rows, LANES), f32), pltpu.VMEM((slab_rows, LANES), f32),
                        pltpu.SemaphoreType.DMA((2,)), pltpu.SemaphoreType.DMA((2,)),
                        pltpu.VMEM((d, ff), bf16), pltpu.VMEM((d, ff), bf16), pltpu.VMEM((ff, d), bf16)])
    return pl.pallas_call(
        functools.partial(_moe_kernel, tm=tm, n=n), grid_spec=grid_spec,
        out_shape=jax.ShapeDtypeStruct((n_tiles * slab_rows, LANES), f32),
        compiler_params=_cp(("arbitrary",)), name="moe")(
            tile_e, tile_v, pos, pad_lo, pad_hi, h2, wg, wu, wd)


def _moe(h2, route, counts, wg, wu, wd, *, layer, tm):
    n = route.shape[0]
    n_tiles = (2 * n) // tm + N_EXPERTS
    counts = counts[0, N_GROUPS:N_GROUPS + N_EXPERTS].astype(jnp.int32)
    padded = ((counts + tm - 1) // tm) * tm
    ends = jnp.cumsum(padded)
    starts = ends - padded
    e = route[:, 0:2].astype(jnp.int32)
    rank = route[:, 4:6].astype(jnp.int32)
    pos = (starts[e] + rank).T.reshape(-1)
    tile_start = jnp.arange(n_tiles, dtype=jnp.int32) * tm
    tile_e = jnp.sum((tile_start[:, None] >= ends[None, :]).astype(jnp.int32), axis=1)
    tile_v = (tile_start < ends[-1]).astype(jnp.int32)
    tile_e = jnp.minimum(tile_e, N_EXPERTS - 1)
    return _moe_experts(h2, pos, tile_e, tile_v, starts + counts, ends, wg, wu, wd, layer=layer, tm=tm,
                        n_tiles=n_tiles)


def _final_kernel(y_ref, m0_ref, m1_ref, rt_ref, g_ref, o_ref):
    o_ref[...] = _rms(y_ref[...] + _moe_combine(m0_ref, m1_ref, rt_ref), g_ref[...])


def _final(y, moe_out, route, g, *, tm, row0, rows):
    n, d = y.shape
    r0 = row0 // tm
    r1 = (n + row0) // tm
    return pl.pallas_call(
        _final_kernel, grid=(rows // tm,),
        in_specs=[pl.BlockSpec((tm, d), lambda i: (r0 + i, 0)),
                  pl.BlockSpec((tm * (d // LANES), LANES), lambda i: (r0 + i, 0)),
                  pl.BlockSpec((tm * (d // LANES), LANES), lambda i: (r1 + i, 0)),
                  pl.BlockSpec((tm, LANES), lambda i: (r0 + i, 0)),
                  pl.BlockSpec((1, d), lambda i: (0, 0))],
        out_specs=pl.BlockSpec((tm, d), lambda i: (i, 0)),
        out_shape=jax.ShapeDtypeStruct((rows, d), f32),
        compiler_params=_cp(("arbitrary",)), name="final_norm")(y, moe_out, moe_out, route, g.reshape(1, d))


def _rope_tables(pos):
    half = ROT_DIM // 2
    inv = jnp.power(ROPE_THETA, -jnp.arange(half, dtype=f32) * 2.0 / ROT_DIM)
    ang = pos.astype(f32)[:, None] * inv[None, :]
    cos, sin = lax.optimization_barrier((jnp.cos(ang), jnp.sin(ang)))
    n = pos.shape[0]
    one = jnp.ones((n, DH_C - ROT_DIM), f32)
    zero = jnp.zeros((n, DH_C - ROT_DIM), f32)
    zh = jnp.zeros((n, half), f32)
    c = jnp.concatenate([cos, cos, one], axis=1)
    s1 = jnp.concatenate([zh, sin, zero], axis=1)
    s2 = jnp.concatenate([-sin, zh, zero], axis=1)
    rep = LANES // DH_C
    return tuple(jnp.tile(t, (1, rep)) for t in (c, s1, s2))


def _dup_heads(a):
    sh = a.shape[:-1]
    a4 = a.reshape(sh + (KV_C, 1, DH_C))
    return jnp.broadcast_to(a4, sh + (KV_C, 2, DH_C)).reshape(sh + (KV_C * 2 * DH_C,))


def kernel(x_prompt, x_sample, cache_gla_state, cache_band_k, cache_band_v, cache_swa_k, cache_swa_v,
           norm_mix, norm_ffn, norm_final, w_in_ab, w_gla_lr, b_gla_lr, gla_out_norm, rel_bias_tab,
           w_out_ab, w_in_c, b_in_c, attn_sinks, w_out_c, w_router_grp, w_router_exp, w_exp_gate,
           w_exp_up, w_exp_down):
    bp, sp, d = x_prompt.shape
    bs, ts, _ = x_sample.shape
    n_p = bp * sp
    n_s = bs * ts
    n = n_p + n_s
    tm = 512 if n_s % 512 == 0 else n_s
    tmo = min(tm, 256)
    tme = 256 if (2 * n) % 256 == 0 else 64
    ff = w_exp_gate.shape[-1]

    xp = x_prompt.reshape(n_p, d)
    xs_ = x_sample.reshape(n_s, d)

    def router_w(l):
        wr = jnp.concatenate([w_router_grp[l], w_router_exp[l].reshape(d, N_EXPERTS)], axis=1)
        return jnp.zeros((d, LANES), f32).at[:, :N_GROUPS + N_EXPERTS].set(wr)

    experts = (w_exp_gate.reshape(-1, d, ff), w_exp_up.reshape(-1, d, ff), w_exp_down.reshape(-1, ff, d))

    w0 = w_in_ab[0]
    la0 = 2 * H_A * DK_A + 2 * H_A * DV_A
    w_cat = jnp.concatenate(
        [w0[:, :la0], w0[:, la0 + GLA_RANK:], w0[:, la0:la0 + GLA_RANK],
         jnp.zeros((d, LANES - GLA_RANK), f32)], axis=1).astype(bf16)
    z = _proj(xp, norm_mix[0], w_cat, tm=tmo, tn=896, yb=xs_)

    c_s = min(CHUNK, ts)
    oa_p, gla_p = _gla(z, w_gla_lr[0], b_gla_lr[0], gla_out_norm[0],
                       jnp.zeros((bp, H_A, DK_A, DV_A), f32), bsz=bp, t=sp, row0=0, C=CHUNK, CB=4)
    oa_s, gla_s = _gla(z, w_gla_lr[0], b_gla_lr[0], gla_out_norm[0], cache_gla_state[0],
                       bsz=bs, t=ts, row0=n_p, C=c_s, CB=ts // c_s)

    hd = H_B * DH_B
    qr = 256
    nq = sp // qr
    qcb, kcb, vcb = _QB0 // hd, _KB0 // hd, _VB0 // hd
    cur = lambda cb: pl.BlockSpec((qr, hd), lambda b, q: (b * nq + q, cb))
    prev = lambda cb, back: pl.BlockSpec((qr, hd), lambda b, q: (b * nq + jnp.maximum(q - back, 0), cb))
    ob_p = _band_call(
        [cur(qcb), prev(kcb, 2), prev(kcb, 1), cur(kcb), prev(vcb, 2), prev(vcb, 1), cur(vcb)],
        [z] * 7, _band_bias_prompt(rel_bias_tab[0], qr, qr), bsz=bp, nq=nq, qr=qr, mask_first=True)

    lb = cache_band_k.shape[2]
    ck = cache_band_k[0].reshape(bs * lb, hd)
    cv = cache_band_v[0].reshape(bs * lb, hd)
    rs0 = n_p // ts
    new = lambda cb: pl.BlockSpec((ts, hd), lambda b, q: (rs0 + b, cb))
    cpiece = lambda k: pl.BlockSpec((lb // 2, hd), lambda b, q: (2 * b + k, 0))
    ob_s = _band_call(
        [new(qcb), cpiece(0), cpiece(1), new(kcb), cpiece(0), cpiece(1), new(vcb)],
        [z, ck, ck, z, cv, cv, z], _band_bias_sample(rel_bias_tab[0], ts, lb),
        bsz=bs, nq=1, qr=ts, mask_first=False)

    dmix_a = H_A * DV_A
    wo = w_out_ab[0].astype(bf16)
    y1, h2, route, cnt = _out_proj(
        (xp, xs_), [(oa_p, oa_s, wo[:dmix_a]), (ob_p, ob_s, wo[dmix_a:])], norm_ffn[0], router_w(0),
        tm=tmo)
    moe0 = _moe(h2, route, cnt, *experts, layer=0, tm=tme)

    keep = min(B_PREV * CHUNK, sp)

    def prompt_tail(a, rows, c0, width):
        return jnp.stack([a[(b + 1) * sp - rows:(b + 1) * sp, c0:c0 + width] for b in range(bp)])

    band_k_p = prompt_tail(z, keep, _KB0, hd).astype(f32).reshape(bp, keep, H_B, DH_B)[None]
    band_v_p = prompt_tail(z, keep, _VB0, hd).astype(f32).reshape(bp, keep, H_B, DH_B)[None]
    kb_s = z[n_p:, _KB0:_KB0 + hd].astype(f32).reshape(bs, ts, H_B, DH_B)
    vb_s = z[n_p:, _VB0:_VB0 + hd].astype(f32).reshape(bs, ts, H_B, DH_B)
    band_k_s = jnp.concatenate([cache_band_k[0], kb_s], axis=1)[:, ts:][None]
    band_v_s = jnp.concatenate([cache_band_v[0], vb_s], axis=1)[:, ts:][None]

    w1 = w_in_c[0]
    nq_c = H_C * DH_C
    nkv = KV_C * DH_C
    w1_cat = jnp.concatenate([w1[:, :nq_c], _dup_heads(w1[:, nq_c:nq_c + nkv]),
                              _dup_heads(w1[:, nq_c + nkv:])], axis=1).astype(bf16)
    b1 = b_in_c[0]
    b1_cat = jnp.concatenate([b1[:nq_c], _dup_heads(b1[nq_c:nq_c + nkv]), _dup_heads(b1[nq_c + nkv:])])
    uniq = _rope_tables(jnp.concatenate([jnp.arange(sp), PAST_LEN + jnp.arange(ts)]))
    rope = tuple(jnp.concatenate([jnp.tile(u[:sp], (bp, 1)), jnp.tile(u[sp:], (bs, 1))]) for u in uniq)
    y2, z1 = _proj(y1, norm_mix[1], w1_cat, tm=tmo, tn=512, add=(moe0, route), bias=b1_cat,
                   rope=rope, n_rope=(nq_c + 2 * nkv) // 512)

    qr_c = WINDOW
    nq1 = sp // qr_c
    kw = 2 * nkv
    kcb1, vcb1 = _KC0 // kw, _VC0 // kw
    qspec = pl.BlockSpec((qr_c, nq_c), lambda b, q: (b * nq1 + q, 0))
    cur1 = lambda cb: pl.BlockSpec((qr_c, kw), lambda b, q: (b * nq1 + q, cb))
    prev1 = lambda cb: pl.BlockSpec((qr_c, kw), lambda b, q: (b * nq1 + jnp.maximum(q - 1, 0), cb))
    sinks = attn_sinks[0].astype(f32)
    oc_p = _swa_call([qspec, prev1(kcb1), cur1(kcb1), prev1(vcb1), cur1(vcb1)], [z1] * 5, sinks,
                     bsz=bp, nq=nq1, qr=qr_c, chunk_mask=True)

    lc = cache_swa_k.shape[2]
    csk = _dup_heads(cache_swa_k[0].reshape(bs * lc, nkv))
    csv = _dup_heads(cache_swa_v[0].reshape(bs * lc, nkv))
    qspec_s = pl.BlockSpec((ts, nq_c), lambda b, q: (rs0 + b, 0))
    new1 = lambda cb: pl.BlockSpec((ts, kw), lambda b, q: (rs0 + b, cb))
    cspec = pl.BlockSpec((lc, kw), lambda b, q: (b, 0))
    oc_s = _swa_call([qspec_s, cspec, new1(kcb1), cspec, new1(vcb1)], [z1, csk, z1, csv, z1], sinks,
                     bsz=bs, nq=1, qr=ts, chunk_mask=False)

    y3, h4, route1, cnt1 = _out_proj(y2, [(oc_p, oc_s, w_out_c[0].astype(bf16))], norm_ffn[1],
                                     router_w(1), tm=tmo)
    moe1 = _moe(h4, route1, cnt1, *experts, layer=1, tm=tme)

    keep1 = min(WINDOW, sp)

    def undup(a):
        return a.astype(f32).reshape(a.shape[:-1] + (KV_C, 2, DH_C))[..., 0, :]

    swa_k_p = undup(prompt_tail(z1, keep1, _KC0, kw))[None]
    swa_v_p = undup(prompt_tail(z1, keep1, _VC0, kw))[None]
    k1_s = undup(z1[n_p:, _KC0:_KC0 + kw]).reshape(bs, ts, KV_C, DH_C)
    v1_s = undup(z1[n_p:, _VC0:_VC0 + kw]).reshape(bs, ts, KV_C, DH_C)
    swa_k_s = jnp.concatenate([cache_swa_k[0], k1_s], axis=1)[:, ts:][None]
    swa_v_s = jnp.concatenate([cache_swa_v[0], v1_s], axis=1)[:, ts:][None]

    y_prompt = _final(y3, moe1, route1, norm_final, tm=tm, row0=0, rows=n_p).reshape(bp, sp, d)
    y_sample = _final(y3, moe1, route1, norm_final, tm=tm, row0=n_p, rows=n_s).reshape(bs, ts, d)

    return (y_prompt, y_sample, gla_p[None], gla_s[None].astype(cache_gla_state.dtype),
            band_k_p, band_v_p, band_k_s, band_v_s, swa_k_p, swa_v_p, swa_k_s, swa_v_s)
```

```python
import functools

import numpy as np
import jax
import jax.numpy as jnp
from jax import lax
from jax.experimental import pallas as pl
from jax.experimental.pallas import tpu as pltpu

f32 = jnp.float32
bf16 = jnp.bfloat16

CHUNK = 64
EPS = 1e-6
NEG_INF = -1e30
PAST_LEN = 1024
H_A = 8
DK_A = 64
DV_A = 128
GLA_RANK = 16
GLA_TAU = 16.0
H_B = 8
DH_B = 128
B_PREV = 8
REL_CLIP = 128
H_C = 32
KV_C = 4
G_C = H_C // KV_C
DH_C = 64
WINDOW = 128
ROT_DIM = DH_C // 4
ROPE_THETA = 500000.0
N_GROUPS = 4
EXP_PER_GROUP = 8
N_EXPERTS = N_GROUPS * EXP_PER_GROUP

LANES = 128
VMEM_LIMIT = 52 * 1024 * 1024

_QA0, _KA0, _VA0, _GA0 = 0, 512, 1024, 2048
_QB0, _KB0, _VB0, _LA0 = 3072, 4096, 5120, 6144
_NC_AB = 6272
_QC0, _KC0, _VC0 = 0, 2048, 2560
_NC_C = 3072


def _cp(sem):
    return pltpu.CompilerParams(dimension_semantics=sem, vmem_limit_bytes=VMEM_LIMIT)


def _dot(a, b):
    return jnp.dot(a, b, preferred_element_type=f32)


def _dot_nt(a, b):
    return lax.dot_general(a, b, (((1,), (1,)), ((), ())), preferred_element_type=f32)


def _dot_tn(a, b):
    return lax.dot_general(a, b, (((0,), (0,)), ((), ())), preferred_element_type=f32)


def _rms(x, g):
    ms = jnp.mean(x * x, axis=-1, keepdims=True)
    return (x * lax.rsqrt(ms + EPS)) * g


def _silu(x):
    return x / (1.0 + jnp.exp(-x))


def _moe_combine(m0_ref, m1_ref, route_ref):
    rt = route_ref[...]
    return rt[:, 2:3] * m0_ref[...] + rt[:, 3:4] * m1_ref[...]


def _proj_kernel(*refs, n_first, has_add, has_bias, n_rope, tn):
    it = iter(refs)
    ya_ref = next(it)
    yb_ref = next(it) if n_first is not None else None
    if has_add:
        m0_ref, m1_ref, rt_ref = next(it), next(it), next(it)
    g_ref = next(it)
    w_ref = next(it)
    b_ref = next(it) if has_bias else None
    if n_rope:
        c_ref, s1_ref, s2_ref = next(it), next(it), next(it)
    ysum_ref = next(it) if has_add else None
    z_ref = next(it)
    h_ref = next(it)
    i = pl.program_id(0)

    x = ya_ref[...]
    if n_first is not None:
        x = jnp.where(i < n_first, x, yb_ref[...])
    if has_add:
        x = x + _moe_combine(m0_ref, m1_ref, rt_ref)
        ysum_ref[...] = x
    h_ref[...] = _rms(x, g_ref[...]).astype(bf16)

    def rope(acc):
        rep = tn // LANES
        c = jnp.tile(c_ref[...], (1, rep))
        s1 = jnp.tile(s1_ref[...], (1, rep))
        s2 = jnp.tile(s2_ref[...], (1, rep))
        half = ROT_DIM // 2
        lo = pltpu.roll(acc, half, axis=1)
        hi = pltpu.roll(acc, tn - half, axis=1)
        return acc * c + lo * s1 + hi * s2

    def col_tile(j, carry):
        cols = pl.ds(pl.multiple_of(j * tn, tn), tn)
        acc = _dot(h_ref[...], w_ref[:, cols])
        if has_bias:
            acc = acc + b_ref[:, cols]
        if n_rope:
            @pl.when(j < n_rope)
            def _():
                z_ref[:, cols] = rope(acc).astype(z_ref.dtype)

            @pl.when(j >= n_rope)
            def _():
                z_ref[:, cols] = acc.astype(z_ref.dtype)
        else:
            z_ref[:, cols] = acc.astype(z_ref.dtype)
        return carry

    lax.fori_loop(0, z_ref.shape[1] // tn, col_tile, 0)


def _proj(ya, g, w, *, tm, tn, yb=None, add=None, bias=None, rope=None, n_rope=0):
    d = ya.shape[1]
    na = ya.shape[0] // tm
    nb = 0 if yb is None else yb.shape[0] // tm
    n = (na + nb) * tm
    nc = w.shape[1]
    once = dict(pipeline_mode=pl.Buffered(1))
    in_specs = []
    args = []
    if yb is None:
        in_specs.append(pl.BlockSpec((tm, d), lambda i: (i, 0)))
        args.append(ya)
    else:
        in_specs.append(pl.BlockSpec((tm, d), lambda i: (jnp.minimum(i, na - 1), 0)))
        in_specs.append(pl.BlockSpec((tm, d), lambda i: (jnp.maximum(i - na, 0), 0)))
        args += [ya, yb]
    if add is not None:
        moe_out, route = add
        in_specs.append(pl.BlockSpec((tm, d), lambda i: (i, 0)))
        in_specs.append(pl.BlockSpec((tm, d), lambda i: (i + na, 0)))
        in_specs.append(pl.BlockSpec((tm, LANES), lambda i: (i, 0)))
        args += [moe_out, moe_out, route]
    in_specs.append(pl.BlockSpec((1, d), lambda i: (0, 0)))
    args.append(g.reshape(1, d))
    in_specs.append(pl.BlockSpec((d, nc), lambda i: (0, 0), **once))
    args.append(w)
    if bias is not None:
        in_specs.append(pl.BlockSpec((1, nc), lambda i: (0, 0)))
        args.append(bias.reshape(1, nc))
    if n_rope:
        for t in rope:
            in_specs.append(pl.BlockSpec((tm, LANES), lambda i: (i, 0)))
            args.append(t)
    out_shape = []
    out_specs = []
    if add is not None:
        out_shape.append(jax.ShapeDtypeStruct((n, d), f32))
        out_specs.append(pl.BlockSpec((tm, d), lambda i: (i, 0)))
    out_shape.append(jax.ShapeDtypeStruct((n, nc), bf16))
    out_specs.append(pl.BlockSpec((tm, nc), lambda i: (i, 0)))
    kern = functools.partial(_proj_kernel, n_first=(na if yb is not None else None),
                             has_add=add is not None, has_bias=bias is not None, n_rope=n_rope, tn=tn)
    out = pl.pallas_call(
        kern, grid=(na + nb,), in_specs=in_specs, out_specs=out_specs, out_shape=out_shape,
        scratch_shapes=[pltpu.VMEM((tm, d), bf16)],
        compiler_params=_cp(("arbitrary",)), name="proj")(*args)
    return out if add is not None else out[0]


def _gla_kernel(q_ref, k_ref, v_ref, g_ref, la_ref, wlr_ref, blr_ref, gn_ref, s0_ref,
                o_ref, sfin_ref, st_ref, *, C, CB):
    s = pl.program_id(1)

    @pl.when(s == 0)
    def _():
        st_ref[...] = s0_ref[0]

    row = lax.broadcasted_iota(jnp.int32, (C, C), 0)
    col = lax.broadcasted_iota(jnp.int32, (C, C), 1)
    tril = row >= col
    tri = jnp.where(tril, 1.0, 0.0).astype(bf16)
    lane = lax.broadcasted_iota(jnp.int32, (C, LANES), 1)
    even = lane < DK_A
    srow = lax.broadcasted_iota(jnp.int32, (2 * DV_A, LANES), 0)
    slane = lax.broadcasted_iota(jnp.int32, (2 * DV_A, LANES), 1)
    blockdiag = (srow < DV_A) == (slane < DK_A)
    wlr = wlr_ref[...]
    blr = blr_ref[...]
    gn = gn_ref[...]

    def chunk(c, carry):
        rows = pl.ds(pl.multiple_of(c * C, C), C)
        pre = _dot(la_ref[rows, :], wlr) + blr
        loga = -(jnp.maximum(-pre, 0.0) + jnp.log1p(jnp.exp(-jnp.abs(pre)))) * (1.0 / GLA_TAU)
        hi = loga.astype(bf16)
        lo = (loga - hi.astype(f32)).astype(bf16)
        b = _dot(tri, hi) + _dot(tri, lo)
        b_last = b[C - 1:C, :]
        qf = q_ref[rows, :].astype(f32) * (DK_A ** -0.5)
        kf = k_ref[rows, :].astype(f32)
        q_dec = (qf * jnp.exp(b)).astype(bf16)
        k_inv = (kf * jnp.exp(-b)).astype(bf16)
        k_end = (kf * jnp.exp(b_last - b)).astype(bf16)
        dec = jnp.exp(b_last)
        for p in range(H_A // 2):
            sl = slice(p * LANES, (p + 1) * LANES)
            qp = q_dec[:, sl]
            kp = k_inv[:, sl]
            vp = v_ref[rows, p * 2 * DV_A:(p + 1) * 2 * DV_A]
            zero = jnp.zeros_like(qp)
            att_e = jnp.where(tril, _dot_nt(jnp.where(even, qp, zero), kp), 0.0).astype(bf16)
            att_o = jnp.where(tril, _dot_nt(jnp.where(even, zero, qp), kp), 0.0).astype(bf16)
            o_e = _dot(att_e, vp)[:, :DV_A]
            o_o = _dot(att_o, vp)[:, DV_A:]
            st = st_ref[p]
            o_inter = _dot_nt(qp, st.astype(bf16))
            ds = _dot_tn(vp, k_end[:, sl])
            st_ref[p] = dec[:, sl] * st + jnp.where(blockdiag, ds, 0.0)
            for half, o_intra in ((0, o_e), (1, o_o)):
                h = 2 * p + half
                hs = slice(h * DV_A, (h + 1) * DV_A)
                o = o_intra + o_inter[:, half * DV_A:(half + 1) * DV_A]
                gate = g_ref[rows, hs].astype(f32)
                o_ref[rows, hs] = (_rms(o, gn) * _silu(gate)).astype(o_ref.dtype)
        return carry

    lax.fori_loop(0, CB, chunk, 0, unroll=2 if CB % 2 == 0 else 1)

    @pl.when(s == pl.num_programs(1) - 1)
    def _():
        sfin_ref[0] = st_ref[...]


def _state_to_kernel_layout(s0):
    bsz = s0.shape[0]
    st = jnp.swapaxes(s0, -1, -2).reshape(bsz, H_A // 2, 2, DV_A, DK_A)
    z = jnp.zeros_like(st[:, :, 0])
    top = jnp.concatenate([st[:, :, 0], z], axis=-1)
    bot = jnp.concatenate([z, st[:, :, 1]], axis=-1)
    return jnp.concatenate([top, bot], axis=-2)


def _state_from_kernel_layout(st):
    bsz = st.shape[0]
    e = st[:, :, :DV_A, :DK_A]
    o = st[:, :, DV_A:, DK_A:]
    s = jnp.stack([e, o], axis=2).reshape(bsz, H_A, DV_A, DK_A)
    return jnp.swapaxes(s, -1, -2)


def _gla(z, w_lr, b_lr, g_norm, s0, *, bsz, t, row0, C, CB):
    rb = C * CB
    ns = t // rb
    r0 = row0 // rb

    def zspec(width, cb):
        return pl.BlockSpec((rb, width), lambda b, s: (r0 + b * ns + s, cb))

    hk = H_A * DK_A
    hv = H_A * DV_A
    st0 = _state_to_kernel_layout(s0.astype(f32))
    wl = jnp.zeros((LANES, hk), f32).at[:GLA_RANK].set(w_lr).astype(bf16)
    st_shape = (H_A // 2, 2 * DV_A, 2 * DK_A)
    o, sfin = pl.pallas_call(
        functools.partial(_gla_kernel, C=C, CB=CB),
        grid=(bsz, ns),
        in_specs=[zspec(hk, _QA0 // hk), zspec(hk, _KA0 // hk), zspec(hv, _VA0 // hv),
                  zspec(hv, _GA0 // hv), zspec(LANES, _LA0 // LANES),
                  pl.BlockSpec((LANES, hk), lambda b, s: (0, 0)),
                  pl.BlockSpec((1, hk), lambda b, s: (0, 0)),
                  pl.BlockSpec((1, DV_A), lambda b, s: (0, 0)),
                  pl.BlockSpec((1,) + st_shape, lambda b, s: (b, 0, 0, 0))],
        out_specs=[pl.BlockSpec((rb, hv), lambda b, s: (b * ns + s, 0)),
                   pl.BlockSpec((1,) + st_shape, lambda b, s: (b, 0, 0, 0))],
        out_shape=[jax.ShapeDtypeStruct((bsz * t, hv), bf16),
                   jax.ShapeDtypeStruct((bsz,) + st_shape, f32)],
        scratch_shapes=[pltpu.VMEM(st_shape, f32)],
        compiler_params=_cp(("arbitrary", "arbitrary")), name="gla")(
            z, z, z, z, z, wl, b_lr.reshape(1, hk).astype(f32), g_norm.reshape(1, DV_A).astype(f32), st0)
    return o, _state_from_kernel_layout(sfin)


def _band_kernel(q_ref, ka_ref, kb_ref, kc_ref, va_ref, vb_ref, vc_ref, ba_ref, bb_ref, bc_ref,
                 o_ref, *, mask_first):
    qi = pl.program_id(1)
    for h in range(H_B):
        sl = slice(h * DH_B, (h + 1) * DH_B)
        qh = (q_ref[:, sl].astype(f32) * (DH_B ** -0.5)).astype(bf16)
        sa = _dot_nt(qh, ka_ref[:, sl].astype(bf16)) + ba_ref[h]
        sb = _dot_nt(qh, kb_ref[:, sl].astype(bf16)) + bb_ref[h]
        sc = _dot_nt(qh, kc_ref[:, sl].astype(bf16)) + bc_ref[h]
        if mask_first:
            sa = jnp.where(qi >= 2, sa, NEG_INF)
            sb = jnp.where(qi >= 1, sb, NEG_INF)
        m = jnp.maximum(jnp.maximum(jnp.max(sa, axis=-1, keepdims=True),
                                    jnp.max(sb, axis=-1, keepdims=True)),
                        jnp.max(sc, axis=-1, keepdims=True))
        ea = jnp.exp(sa - m)
        eb = jnp.exp(sb - m)
        ec = jnp.exp(sc - m)
        inv = 1.0 / (jnp.sum(ea, axis=-1, keepdims=True) + jnp.sum(eb, axis=-1, keepdims=True)
                     + jnp.sum(ec, axis=-1, keepdims=True))
        o = (_dot((ea * inv).astype(bf16), va_ref[:, sl].astype(bf16))
             + _dot((eb * inv).astype(bf16), vb_ref[:, sl].astype(bf16))
             + _dot((ec * inv).astype(bf16), vc_ref[:, sl].astype(bf16)))
        o_ref[:, sl] = o.astype(o_ref.dtype)


def _band_call(qkv_specs, args, biases, *, bsz, nq, qr, mask_first):
    hd = H_B * DH_B
    bias_specs = [pl.BlockSpec(b.shape, lambda b_, q_: (0, 0, 0)) for b in biases]
    return pl.pallas_call(
        functools.partial(_band_kernel, mask_first=mask_first),
        grid=(bsz, nq),
        in_specs=list(qkv_specs) + bias_specs,
        out_specs=pl.BlockSpec((qr, hd), lambda b, q: (b * nq + q, 0)),
        out_shape=jax.ShapeDtypeStruct((bsz * nq * qr, hd), bf16),
        compiler_params=_cp(("arbitrary", "arbitrary")), name="band")(*args, *biases)


def _rel_bias_matrix(tab, nq, nk, off):
    m = nq + nk
    j = np.arange(m)
    diff = np.where(j < nk, -j, m - j)
    idx = np.clip(diff + off, -REL_CLIP, REL_CLIP) + REL_CLIP
    u = tab.astype(f32)[:, jnp.asarray(idx)]
    h = tab.shape[0]
    return jnp.tile(u, (1, nq))[:, :nq * (m - 1)].reshape(h, nq, m - 1)[:, :, :nk]


def _band_bias_prompt(tab, qr, kb):
    tot = 2 * kb + qr
    qc = (np.arange(qr)[:, None] + 2 * kb) // CHUNK
    kc = np.arange(tot)[None, :] // CHUNK
    valid = (kc <= qc) & (kc >= qc - B_PREV)
    bias = jnp.where(jnp.asarray(valid)[None], _rel_bias_matrix(tab, qr, tot, 2 * kb), NEG_INF)
    return bias[:, :, :kb], bias[:, :, kb:2 * kb], bias[:, :, 2 * kb:]


def _band_bias_sample(tab, t, lc):
    bias = _rel_bias_matrix(tab, t, lc + t, lc)
    return bias[:, :, :lc // 2], bias[:, :, lc // 2:lc], bias[:, :, lc:]


def _swa_kernel(sink_ref, q_ref, kp_ref, kc_ref, vp_ref, vc_ref, o_ref, qs_ref, *, chunk_mask):
    qi = pl.program_id(1)
    qr = q_ref.shape[0]
    pr = kp_ref.shape[0]
    nl = G_C * LANES
    even = lax.broadcasted_iota(jnp.int32, (qr, LANES), 1) < DH_C
    top = lax.broadcasted_iota(jnp.int32, (LANES, LANES), 0) < DH_C
    if chunk_mask:
        kch = lax.broadcasted_iota(jnp.int32, (pr, nl), 0) // CHUNK
        qch = (lax.broadcasted_iota(jnp.int32, (pr, nl), 1) % LANES) // CHUNK
        ok_prev = (kch >= qch) & (qi >= 1)
        kch = lax.broadcasted_iota(jnp.int32, (qr, nl), 0) // CHUNK
        qch = (lax.broadcasted_iota(jnp.int32, (qr, nl), 1) % LANES) // CHUNK
        ok_cur = kch <= qch
    if qr < LANES:
        qs_ref[...] = jnp.zeros_like(qs_ref)
    for c in range(KV_C):
        ks = slice(c * LANES, (c + 1) * LANES)
        for p in range(G_C // 2):
            cols = slice(c * G_C * DH_C + p * LANES, c * G_C * DH_C + (p + 1) * LANES)
            qp = (q_ref[:, cols].astype(f32) * (DH_C ** -0.5)).astype(bf16)
            zero = jnp.zeros_like(qp)
            qs_ref[2 * p * LANES:2 * p * LANES + qr, :] = jnp.where(even, qp, zero)
            qs_ref[(2 * p + 1) * LANES:(2 * p + 1) * LANES + qr, :] = jnp.where(even, zero, qp)
        q_all = qs_ref[...]
        sink = jnp.concatenate(
            [jnp.full((1, LANES), sink_ref[c * G_C + j], f32) for j in range(G_C)], axis=1)
        st_prev = _dot_nt(kp_ref[:, ks].astype(bf16), q_all)
        st_cur = _dot_nt(kc_ref[:, ks].astype(bf16), q_all)
        if chunk_mask:
            st_prev = jnp.where(ok_prev, st_prev, NEG_INF)
            st_cur = jnp.where(ok_cur, st_cur, NEG_INF)
        m = jnp.maximum(jnp.maximum(jnp.max(st_prev, axis=0, keepdims=True),
                                    jnp.max(st_cur, axis=0, keepdims=True)), sink)
        e_prev = jnp.exp(st_prev - m)
        e_cur = jnp.exp(st_cur - m)
        inv = 1.0 / (jnp.sum(e_prev, axis=0, keepdims=True) + jnp.sum(e_cur, axis=0, keepdims=True)
                     + jnp.exp(sink - m))
        ot = (_dot_tn(vp_ref[:, ks].astype(bf16), (e_prev * inv).astype(bf16))
              + _dot_tn(vc_ref[:, ks].astype(bf16), (e_cur * inv).astype(bf16)))
        for p in range(G_C // 2):
            cols = slice(c * G_C * DH_C + p * LANES, c * G_C * DH_C + (p + 1) * LANES)
            blk = jnp.where(top, ot[:, 2 * p * LANES:(2 * p + 1) * LANES],
                            ot[:, (2 * p + 1) * LANES:(2 * p + 2) * LANES])
            o_ref[:, cols] = blk.T[:qr].astype(o_ref.dtype)


def _swa_call(specs, args, sinks, *, bsz, nq, qr, chunk_mask):
    hd = H_C * DH_C
    return pl.pallas_call(
        functools.partial(_swa_kernel, chunk_mask=chunk_mask),
        grid=(bsz, nq),
        in_specs=[pl.BlockSpec(memory_space=pltpu.SMEM)] + list(specs),
        out_specs=pl.BlockSpec((qr, hd), lambda b, q: (b * nq + q, 0)),
        out_shape=jax.ShapeDtypeStruct((bsz * nq * qr, hd), bf16),
        scratch_shapes=[pltpu.VMEM((G_C * LANES, LANES), bf16)],
        compiler_params=_cp(("arbitrary", "arbitrary")), name="swa")(sinks, *args)


def _out_kernel(*refs, n_parts, n_first, split_y):
    ny = 2 if split_y else 1
    y_refs = refs[:ny]
    refs = refs[ny - 1:]
    o_refs = refs[1:1 + 2 * n_parts]
    w_refs = refs[1 + 2 * n_parts:1 + 3 * n_parts]
    g_ref, rh_ref, rl_ref = refs[1 + 3 * n_parts:4 + 3 * n_parts]
    y1_ref, h2_ref, route_ref, cnt_ref, run_ref = refs[4 + 3 * n_parts:]
    i = pl.program_id(0)

    @pl.when(i == 0)
    def _():
        run_ref[...] = jnp.zeros_like(run_ref)

    acc = y_refs[0][...]
    if split_y:
        acc = jnp.where(i < n_first, acc, y_refs[1][...])
    for k in range(n_parts):
        lhs = jnp.where(i < n_first, o_refs[2 * k][...], o_refs[2 * k + 1][...])
        acc = acc + _dot(lhs, w_refs[k][...])
    y1_ref[...] = acc
    h2 = _rms(acc, g_ref[...])
    hi = h2.astype(bf16)
    h2_ref[...] = hi.astype(f32)
    lo = (h2 - hi.astype(f32)).astype(bf16)
    lg = _dot(hi, rh_ref[...]) + _dot(hi, rl_ref[...]) + _dot(lo, rh_ref[...])

    lane_i = lax.broadcasted_iota(jnp.int32, lg.shape, 1)
    lane = lane_i.astype(f32)
    big = float(LANES)
    ninf = -jnp.inf
    is_g = lane_i < N_GROUPS
    gmax = jnp.max(jnp.where(is_g, lg, ninf), axis=-1, keepdims=True)
    gsel = jnp.min(jnp.where(is_g & (lg == gmax), lane, big), axis=-1, keepdims=True)
    pg = 1.0 / jnp.sum(jnp.where(is_g, jnp.exp(lg - gmax), 0.0), axis=-1, keepdims=True)
    e0 = N_GROUPS + EXP_PER_GROUP * gsel
    is_e = (lane >= e0) & (lane < e0 + EXP_PER_GROUP)
    v1 = jnp.max(jnp.where(is_e, lg, ninf), axis=-1, keepdims=True)
    i1 = jnp.min(jnp.where(is_e & (lg == v1), lane, big), axis=-1, keepdims=True)
    is_e2 = is_e & (lane != i1)
    v2 = jnp.max(jnp.where(is_e2, lg, ninf), axis=-1, keepdims=True)
    i2 = jnp.min(jnp.where(is_e2 & (lg == v2), lane, big), axis=-1, keepdims=True)
    t = jnp.exp(v2 - v1)
    w1 = pg / (1.0 + t)
    w2 = pg * t / (1.0 + t)

    tmr = lg.shape[0]
    oh1 = jnp.where(lane == i1, 1.0, 0.0)
    oh2 = jnp.where(lane == i2, 1.0, 0.0)
    before = jnp.where(lax.broadcasted_iota(jnp.int32, (tmr, tmr), 0)
                       > lax.broadcasted_iota(jnp.int32, (tmr, tmr), 1), 1.0, 0.0).astype(bf16)
    tot1 = jnp.sum(oh1, axis=0, keepdims=True)
    tot2 = jnp.sum(oh2, axis=0, keepdims=True)
    base = run_ref[...]
    r1 = jnp.sum(oh1 * (base + _dot(before, oh1.astype(bf16))), axis=-1, keepdims=True)
    r2 = jnp.sum(oh2 * (base + tot1 + _dot(before, oh2.astype(bf16))), axis=-1, keepdims=True)
    run_ref[...] = base + tot1 + tot2
    cnt_ref[...] = base + tot1 + tot2
    vals = (i1 - N_GROUPS, i2 - N_GROUPS, w1, w2, r1, r2)
    out = jnp.zeros_like(lg)
    for k, v in enumerate(vals):
        out = jnp.where(lane_i == k, v, out)
    route_ref[...] = out


def _out_proj(y, parts, g, w_router, *, tm):
    n_first = parts[0][0].shape[0] // tm
    split_y = isinstance(y, tuple)
    if split_y:
        d = y[0].shape[1]
        n = y[0].shape[0] + y[1].shape[0]
        in_specs = [pl.BlockSpec((tm, d), lambda i: (jnp.minimum(i, n_first - 1), 0)),
                    pl.BlockSpec((tm, d), lambda i: (jnp.maximum(i - n_first, 0), 0))]
        args = list(y)
    else:
        n, d = y.shape
        in_specs = [pl.BlockSpec((tm, d), lambda i: (i, 0))]
        args = [y]
    n_tiles = n // tm
    for op, os_, _ in parts:
        dk = op.shape[1]
        in_specs.append(pl.BlockSpec((tm, dk), lambda i: (jnp.minimum(i, n_first - 1), 0)))
        in_specs.append(pl.BlockSpec((tm, dk), lambda i: (jnp.maximum(i - n_first, 0), 0)))
        args += [op, os_]
    for _, _, w in parts:
        in_specs.append(pl.BlockSpec(w.shape, lambda i: (0, 0)))
        args.append(w)
    rh = w_router.astype(bf16)
    rl = (w_router - rh.astype(f32)).astype(bf16)
    in_specs += [pl.BlockSpec((1, d), lambda i: (0, 0)),
                 pl.BlockSpec((d, LANES), lambda i: (0, 0)),
                 pl.BlockSpec((d, LANES), lambda i: (0, 0))]
    args += [g.reshape(1, d), rh, rl]
    return pl.pallas_call(
        functools.partial(_out_kernel, n_parts=len(parts), n_first=n_first, split_y=split_y),
        grid=(n_tiles,), in_specs=in_specs,
        out_specs=[pl.BlockSpec((tm, d), lambda i: (i, 0)),
                   pl.BlockSpec((tm, d), lambda i: (i, 0)),
                   pl.BlockSpec((tm, LANES), lambda i: (i, 0)),
                   pl.BlockSpec((1, LANES), lambda i: (0, 0))],
        out_shape=[jax.ShapeDtypeStruct((n, d), f32),
                   jax.ShapeDtypeStruct((n, d), f32),
                   jax.ShapeDtypeStruct((n, LANES), f32),
                   jax.ShapeDtypeStruct((1, LANES), f32)],
        scratch_shapes=[pltpu.VMEM((1, LANES), f32)],
        compiler_params=_cp(("arbitrary",)), name="out_proj")(*args)


def _moe_kernel(te_ref, tv_ref, pos_ref, plo_ref, phi_ref, h_ref, wg_ref, wu_ref, wd_ref, out_ref,
                slot_ref, x0, x1, x2, y0, y1, y2, gsem, ssem, wg_s, wu_s, wd_s, *, tm, n):
    t = pl.program_id(0)
    nt = pl.num_programs(0)
    n2 = 2 * n
    xs = (x0, x1, x2)
    ys = (y0, y1, y2)
    depth = len(xs)

    def tile_valid(tile):
        return (tile >= 0) & (tile < nt) & (tv_ref[jnp.clip(tile, 0, nt - 1)] != 0)

    valid = tile_valid(t)

    def gather_row(tile, r, p):
        s = slot_ref[tile * tm + r]
        tok = jnp.where(s < n, s, jnp.where(s < n2, s - n, s - n2))
        pltpu.make_async_copy(h_ref.at[pl.ds(tok, 1)], xs[p].at[pl.ds(r, 1)], gsem.at[p]).start()

    def scatter_row(tile, r, p):
        s = slot_ref[tile * tm + r]
        pltpu.make_async_copy(ys[p].at[pl.ds(r, 1)], out_ref.at[pl.ds(s, 1)], ssem.at[p]).start()

    def rows_loop(row_fn, tile, p):
        def body(r, carry):
            row_fn(tile, r, p)
            return carry
        lax.fori_loop(0, tm, body, 0, unroll=8)

    def gather_wait(p):
        pltpu.make_async_copy(h_ref.at[pl.ds(0, tm)], xs[p], gsem.at[p]).wait()

    def scatter_wait(p):
        pltpu.make_async_copy(ys[p], out_ref.at[pl.ds(0, tm)], ssem.at[p]).wait()

    def compute(p):
        x = xs[p][...].astype(bf16)
        g = _dot(x, wg_s[...])
        u = _dot(x, wu_s[...])
        ys[p][...] = _dot((_silu(g) * u).astype(bf16), wd_s[...])

    @pl.when(t == 0)
    def _():
        def per_expert(e, nxt):
            def body(r, v):
                slot_ref[r] = v
                return v + 1
            return lax.fori_loop(plo_ref[e], phi_ref[e], body, nxt)
        lax.fori_loop(0, N_EXPERTS, per_expert, n2)

        def body(s, carry):
            slot_ref[pos_ref[s]] = s
            return carry
        lax.fori_loop(0, n2, body, 0, unroll=8)
        rows_loop(gather_row, 0, 0)

        @pl.when(tile_valid(1))
        def _():
            rows_loop(gather_row, 1, 1)

    def step(p):
        q = (p + 2) % depth

        @pl.when(t >= depth)
        def _():
            scatter_wait(p)

        @pl.when(valid)
        def _():
            gather_wait(p)

            @pl.when((t == 0) | (te_ref[t] != te_ref[jnp.maximum(t - 1, 0)]))
            def _():
                wg_s[...] = wg_ref[0].astype(bf16)
                wu_s[...] = wu_ref[0].astype(bf16)
                wd_s[...] = wd_ref[0].astype(bf16)

            steady = tile_valid(t + 2) & (t >= 1)

            @pl.when(steady)
            def _():
                for r in range(tm):
                    gather_row(t + 2, r, q)
                    scatter_row(t - 1, r, q)
                compute(p)

            @pl.when(jnp.logical_not(steady))
            def _():
                @pl.when(tile_valid(t + 2))
                def _():
                    rows_loop(gather_row, t + 2, q)

                @pl.when(t >= 1)
                def _():
                    rows_loop(scatter_row, t - 1, q)
                compute(p)

        @pl.when(jnp.logical_not(valid))
        def _():
            @pl.when(tile_valid(t - 1))
            def _():
                rows_loop(scatter_row, t - 1, q)
            ys[p][...] = jnp.zeros(ys[p].shape, f32)
            pltpu.make_async_copy(ys[p], out_ref.at[pl.ds(pl.multiple_of(t * tm, tm), tm)], ssem.at[p]).start()

        @pl.when(t == nt - 1)
        def _():
            @pl.when(valid)
            def _():
                rows_loop(scatter_row, t, p)
            for k in range(depth):
                scatter_wait(k)

    for p in range(depth):
        pl.when(t % depth == p)(functools.partial(step, p))


def _moe_experts(h2, pos, tile_e, tile_v, pad_lo, pad_hi, wg, wu, wd, *, layer, tm, n_tiles):
    n, d = h2.shape
    ff = wg.shape[2]
    assert n >= N_EXPERTS * tm, "padding rows read tokens 0 .. N_EXPERTS * (tm - 1)"
    assert n_tiles >= 3
    e0 = layer * N_EXPERTS
    row_buf = pltpu.VMEM((tm, d), f32)
    grid_spec = pltpu.PrefetchScalarGridSpec(
        num_scalar_prefetch=5, grid=(n_tiles,),
        in_specs=[pl.BlockSpec(memory_space=pl.ANY),
                  pl.BlockSpec((1, d, ff), lambda t, te, *_: (e0 + te[t], 0, 0)),
                  pl.BlockSpec((1, d, ff), lambda t, te, *_: (e0 + te[t], 0, 0)),
                  pl.BlockSpec((1, ff, d), lambda t, te, *_: (e0 + te[t], 0, 0))],
        out_specs=pl.BlockSpec(memory_space=pl.ANY),
        scratch_shapes=[pltpu.SMEM((n_tiles * tm,), jnp.int32),
                        row_buf, row_buf, row_buf, row_buf, row_buf, row_buf,
                        pltpu.SemaphoreType.DMA((3,)), pltpu.SemaphoreType.DMA((3,)),
                        pltpu.VMEM((d, ff), bf16), pltpu.VMEM((d, ff), bf16), pltpu.VMEM((ff, d), bf16)])
    return pl.pallas_call(
        functools.partial(_moe_kernel, tm=tm, n=n), grid_spec=grid_spec,
        out_shape=jax.ShapeDtypeStruct((n_tiles * tm, d), f32),
        compiler_params=_cp(("arbitrary",)), name="moe")(
            tile_e, tile_v, pos, pad_lo, pad_hi, h2, wg, wu, wd)


def _moe(h2, route, counts, wg, wu, wd, *, layer, tm):
    n = route.shape[0]
    n_tiles = (2 * n) // tm + N_EXPERTS
    counts = counts[0, N_GROUPS:N_GROUPS + N_EXPERTS].astype(jnp.int32)
    padded = ((counts + tm - 1) // tm) * tm
    ends = jnp.cumsum(padded)
    starts = ends - padded
    e = route[:, 0:2].astype(jnp.int32)
    rank = route[:, 4:6].astype(jnp.int32)
    pos = (starts[e] + rank).T.reshape(-1)
    tile_start = jnp.arange(n_tiles, dtype=jnp.int32) * tm
    tile_e = jnp.sum((tile_start[:, None] >= ends[None, :]).astype(jnp.int32), axis=1)
    tile_v = (tile_start < ends[-1]).astype(jnp.int32)
    tile_e = jnp.minimum(tile_e, N_EXPERTS - 1)
    return _moe_experts(h2, pos, tile_e, tile_v, starts + counts, ends, wg, wu, wd, layer=layer, tm=tm,
                        n_tiles=n_tiles)


def _final_kernel(y_ref, m0_ref, m1_ref, rt_ref, g_ref, o_ref):
    o_ref[...] = _rms(y_ref[...] + _moe_combine(m0_ref, m1_ref, rt_ref), g_ref[...])


def _final(y, moe_out, route, g, *, tm, row0, rows):
    n, d = y.shape
    r0 = row0 // tm
    r1 = (n + row0) // tm
    return pl.pallas_call(
        _final_kernel, grid=(rows // tm,),
        in_specs=[pl.BlockSpec((tm, d), lambda i: (r0 + i, 0)),
                  pl.BlockSpec((tm, d), lambda i: (r0 + i, 0)),
                  pl.BlockSpec((tm, d), lambda i: (r1 + i, 0)),
                  pl.BlockSpec((tm, LANES), lambda i: (r0 + i, 0)),
                  pl.BlockSpec((1, d), lambda i: (0, 0))],
        out_specs=pl.BlockSpec((tm, d), lambda i: (i, 0)),
        out_shape=jax.ShapeDtypeStruct((rows, d), f32),
        compiler_params=_cp(("arbitrary",)), name="final_norm")(y, moe_out, moe_out, route, g.reshape(1, d))


def _rope_tables(pos):
    half = ROT_DIM // 2
    inv = jnp.power(ROPE_THETA, -jnp.arange(half, dtype=f32) * 2.0 / ROT_DIM)
    ang = pos.astype(f32)[:, None] * inv[None, :]
    cos, sin = lax.optimization_barrier((jnp.cos(ang), jnp.sin(ang)))
    n = pos.shape[0]
    one = jnp.ones((n, DH_C - ROT_DIM), f32)
    zero = jnp.zeros((n, DH_C - ROT_DIM), f32)
    zh = jnp.zeros((n, half), f32)
    c = jnp.concatenate([cos, cos, one], axis=1)
    s1 = jnp.concatenate([zh, sin, zero], axis=1)
    s2 = jnp.concatenate([-sin, zh, zero], axis=1)
    rep = LANES // DH_C
    return tuple(jnp.tile(t, (1, rep)) for t in (c, s1, s2))


def _dup_heads(a):
    sh = a.shape[:-1]
    a4 = a.reshape(sh + (KV_C, 1, DH_C))
    return jnp.broadcast_to(a4, sh + (KV_C, 2, DH_C)).reshape(sh + (KV_C * 2 * DH_C,))


def kernel(x_prompt, x_sample, cache_gla_state, cache_band_k, cache_band_v, cache_swa_k, cache_swa_v,
           norm_mix, norm_ffn, norm_final, w_in_ab, w_gla_lr, b_gla_lr, gla_out_norm, rel_bias_tab,
           w_out_ab, w_in_c, b_in_c, attn_sinks, w_out_c, w_router_grp, w_router_exp, w_exp_gate,
           w_exp_up, w_exp_down):
    bp, sp, d = x_prompt.shape
    bs, ts, _ = x_sample.shape
    n_p = bp * sp
    n_s = bs * ts
    n = n_p + n_s
    tm = 512 if n_s % 512 == 0 else n_s
    tmo = min(tm, 256)
    tme = 256 if (2 * n) % 256 == 0 else 64
    ff = w_exp_gate.shape[-1]

    xp = x_prompt.reshape(n_p, d)
    xs_ = x_sample.reshape(n_s, d)

    def router_w(l):
        wr = jnp.concatenate([w_router_grp[l], w_router_exp[l].reshape(d, N_EXPERTS)], axis=1)
        return jnp.zeros((d, LANES), f32).at[:, :N_GROUPS + N_EXPERTS].set(wr)

    experts = (w_exp_gate.reshape(-1, d, ff), w_exp_up.reshape(-1, d, ff), w_exp_down.reshape(-1, ff, d))

    w0 = w_in_ab[0]
    la0 = 2 * H_A * DK_A + 2 * H_A * DV_A
    w_cat = jnp.concatenate(
        [w0[:, :la0], w0[:, la0 + GLA_RANK:], w0[:, la0:la0 + GLA_RANK],
         jnp.zeros((d, LANES - GLA_RANK), f32)], axis=1).astype(bf16)
    z = _proj(xp, norm_mix[0], w_cat, tm=tmo, tn=896, yb=xs_)

    c_s = min(CHUNK, ts)
    oa_p, gla_p = _gla(z, w_gla_lr[0], b_gla_lr[0], gla_out_norm[0],
                       jnp.zeros((bp, H_A, DK_A, DV_A), f32), bsz=bp, t=sp, row0=0, C=CHUNK, CB=4)
    oa_s, gla_s = _gla(z, w_gla_lr[0], b_gla_lr[0], gla_out_norm[0], cache_gla_state[0],
                       bsz=bs, t=ts, row0=n_p, C=c_s, CB=ts // c_s)

    hd = H_B * DH_B
    qr = 256
    nq = sp // qr
    qcb, kcb, vcb = _QB0 // hd, _KB0 // hd, _VB0 // hd
    cur = lambda cb: pl.BlockSpec((qr, hd), lambda b, q: (b * nq + q, cb))
    prev = lambda cb, back: pl.BlockSpec((qr, hd), lambda b, q: (b * nq + jnp.maximum(q - back, 0), cb))
    ob_p = _band_call(
        [cur(qcb), prev(kcb, 2), prev(kcb, 1), cur(kcb), prev(vcb, 2), prev(vcb, 1), cur(vcb)],
        [z] * 7, _band_bias_prompt(rel_bias_tab[0], qr, qr), bsz=bp, nq=nq, qr=qr, mask_first=True)

    lb = cache_band_k.shape[2]
    ck = cache_band_k[0].reshape(bs * lb, hd)
    cv = cache_band_v[0].reshape(bs * lb, hd)
    rs0 = n_p // ts
    new = lambda cb: pl.BlockSpec((ts, hd), lambda b, q: (rs0 + b, cb))
    cpiece = lambda k: pl.BlockSpec((lb // 2, hd), lambda b, q: (2 * b + k, 0))
    ob_s = _band_call(
        [new(qcb), cpiece(0), cpiece(1), new(kcb), cpiece(0), cpiece(1), new(vcb)],
        [z, ck, ck, z, cv, cv, z], _band_bias_sample(rel_bias_tab[0], ts, lb),
        bsz=bs, nq=1, qr=ts, mask_first=False)

    dmix_a = H_A * DV_A
    wo = w_out_ab[0].astype(bf16)
    y1, h2, route, cnt = _out_proj(
        (xp, xs_), [(oa_p, oa_s, wo[:dmix_a]), (ob_p, ob_s, wo[dmix_a:])], norm_ffn[0], router_w(0),
        tm=tmo)
    moe0 = _moe(h2, route, cnt, *experts, layer=0, tm=tme)

    keep = min(B_PREV * CHUNK, sp)

    def prompt_tail(a, rows, c0, width):
        return jnp.stack([a[(b + 1) * sp - rows:(b + 1) * sp, c0:c0 + width] for b in range(bp)])

    band_k_p = prompt_tail(z, keep, _KB0, hd).astype(f32).reshape(bp, keep, H_B, DH_B)[None]
    band_v_p = prompt_tail(z, keep, _VB0, hd).astype(f32).reshape(bp, keep, H_B, DH_B)[None]
    kb_s = z[n_p:, _KB0:_KB0 + hd].astype(f32).reshape(bs, ts, H_B, DH_B)
    vb_s = z[n_p:, _VB0:_VB0 + hd].astype(f32).reshape(bs, ts, H_B, DH_B)
    band_k_s = jnp.concatenate([cache_band_k[0], kb_s], axis=1)[:, ts:][None]
    band_v_s = jnp.concatenate([cache_band_v[0], vb_s], axis=1)[:, ts:][None]

    w1 = w_in_c[0]
    nq_c = H_C * DH_C
    nkv = KV_C * DH_C
    w1_cat = jnp.concatenate([w1[:, :nq_c], _dup_heads(w1[:, nq_c:nq_c + nkv]),
                              _dup_heads(w1[:, nq_c + nkv:])], axis=1).astype(bf16)
    b1 = b_in_c[0]
    b1_cat = jnp.concatenate([b1[:nq_c], _dup_heads(b1[nq_c:nq_c + nkv]), _dup_heads(b1[nq_c + nkv:])])
    uniq = _rope_tables(jnp.concatenate([jnp.arange(sp), PAST_LEN + jnp.arange(ts)]))
    rope = tuple(jnp.concatenate([jnp.tile(u[:sp], (bp, 1)), jnp.tile(u[sp:], (bs, 1))]) for u in uniq)
    y2, z1 = _proj(y1, norm_mix[1], w1_cat, tm=tmo, tn=512, add=(moe0, route), bias=b1_cat,
                   rope=rope, n_rope=(nq_c + 2 * nkv) // 512)

    qr_c = WINDOW
    nq1 = sp // qr_c
    kw = 2 * nkv
    kcb1, vcb1 = _KC0 // kw, _VC0 // kw
    qspec = pl.BlockSpec((qr_c, nq_c), lambda b, q: (b * nq1 + q, 0))
    cur1 = lambda cb: pl.BlockSpec((qr_c, kw), lambda b, q: (b * nq1 + q, cb))
    prev1 = lambda cb: pl.BlockSpec((qr_c, kw), lambda b, q: (b * nq1 + jnp.maximum(q - 1, 0), cb))
    sinks = attn_sinks[0].astype(f32)
    oc_p = _swa_call([qspec, prev1(kcb1), cur1(kcb1), prev1(vcb1), cur1(vcb1)], [z1] * 5, sinks,
                     bsz=bp, nq=nq1, qr=qr_c, chunk_mask=True)

    lc = cache_swa_k.shape[2]
    csk = _dup_heads(cache_swa_k[0].reshape(bs * lc, nkv))
    csv = _dup_heads(cache_swa_v[0].reshape(bs * lc, nkv))
    qspec_s = pl.BlockSpec((ts, nq_c), lambda b, q: (rs0 + b, 0))
    new1 = lambda cb: pl.BlockSpec((ts, kw), lambda b, q: (rs0 + b, cb))
    cspec = pl.BlockSpec((lc, kw), lambda b, q: (b, 0))
    oc_s = _swa_call([qspec_s, cspec, new1(kcb1), cspec, new1(vcb1)], [z1, csk, z1, csv, z1], sinks,
                     bsz=bs, nq=1, qr=ts, chunk_mask=False)

    y3, h4, route1, cnt1 = _out_proj(y2, [(oc_p, oc_s, w_out_c[0].astype(bf16))], norm_ffn[1],
                                     router_w(1), tm=tmo)
    moe1 = _moe(h4, route1, cnt1, *experts, layer=1, tm=tme)

    keep1 = min(WINDOW, sp)

    def undup(a):
        return a.astype(f32).reshape(a.shape[:-1] + (KV_C, 2, DH_C))[..., 0, :]

    swa_k_p = undup(prompt_tail(z1, keep1, _KC0, kw))[None]
    swa_v_p = undup(prompt_tail(z1, keep1, _VC0, kw))[None]
    k1_s = undup(z1[n_p:, _KC0:_KC0 + kw]).reshape(bs, ts, KV_C, DH_C)
    v1_s = undup(z1[n_p:, _VC0:_VC0 + kw]).reshape(bs, ts, KV_C, DH_C)
    swa_k_s = jnp.concatenate([cache_swa_k[0], k1_s], axis=1)[:, ts:][None]
    swa_v_s = jnp.concatenate([cache_swa_v[0], v1_s], axis=1)[:, ts:][None]

    y_prompt = _final(y3, moe1, route1, norm_final, tm=tm, row0=0, rows=n_p).reshape(bp, sp, d)
    y_sample = _final(y3, moe1, route1, norm_final, tm=tm, row0=n_p, rows=n_s).reshape(bs, ts, d)

    return (y_prompt, y_sample, gla_p[None], gla_s[None].astype(cache_gla_state.dtype),
            band_k_p, band_v_p, band_k_s, band_v_s, swa_k_p, swa_v_p, swa_k_s, swa_v_s)
```

```python
import functools

import numpy as np
import jax
import jax.numpy as jnp
from jax import lax
from jax.experimental import pallas as pl
from jax.experimental.pallas import tpu as pltpu

f32 = jnp.float32
bf16 = jnp.bfloat16

CHUNK = 64
EPS = 1e-6
NEG_INF = -1e30
PAST_LEN = 1024
H_A = 8
DK_A = 64
DV_A = 128
GLA_RANK = 16
GLA_TAU = 16.0
H_B = 8
DH_B = 128
B_PREV = 8
REL_CLIP = 128
H_C = 32
KV_C = 4
G_C = H_C // KV_C
DH_C = 64
WINDOW = 128
ROT_DIM = DH_C // 4
ROPE_THETA = 500000.0
N_GROUPS = 4
EXP_PER_GROUP = 8
N_EXPERTS = N_GROUPS * EXP_PER_GROUP

LANES = 128
VMEM_LIMIT = 52 * 1024 * 1024

_QA0, _KA0, _VA0, _GA0 = 0, 512, 1024, 2048
_QB0, _KB0, _VB0, _LA0 = 3072, 4096, 5120, 6144
_NC_AB = 6272
_QC0, _KC0, _VC0 = 0, 2048, 2560
_NC_C = 3072


def _cp(sem):
    return pltpu.CompilerParams(dimension_semantics=sem, vmem_limit_bytes=VMEM_LIMIT)


def _dot(a, b):
    return jnp.dot(a, b, preferred_element_type=f32)


def _dot_nt(a, b):
    return lax.dot_general(a, b, (((1,), (1,)), ((), ())), preferred_element_type=f32)


def _dot_tn(a, b):
    return lax.dot_general(a, b, (((0,), (0,)), ((), ())), preferred_element_type=f32)


def _rms(x, g):
    ms = jnp.mean(x * x, axis=-1, keepdims=True)
    return (x * lax.rsqrt(ms + EPS)) * g


def _silu(x):
    return x / (1.0 + jnp.exp(-x))


def _moe_combine(m0_ref, m1_ref, route_ref):
    rt = route_ref[...]
    return rt[:, 2:3] * m0_ref[...] + rt[:, 3:4] * m1_ref[...]


def _proj_kernel(*refs, n_first, has_add, has_bias, n_rope, tn):
    it = iter(refs)
    ya_ref = next(it)
    yb_ref = next(it) if n_first is not None else None
    if has_add:
        m0_ref, m1_ref, rt_ref = next(it), next(it), next(it)
    g_ref = next(it)
    w_ref = next(it)
    b_ref = next(it) if has_bias else None
    if n_rope:
        c_ref, s1_ref, s2_ref = next(it), next(it), next(it)
    ysum_ref = next(it) if has_add else None
    z_ref = next(it)
    h_ref = next(it)
    i = pl.program_id(0)

    x = ya_ref[...]
    if n_first is not None:
        x = jnp.where(i < n_first, x, yb_ref[...])
    if has_add:
        x = x + _moe_combine(m0_ref, m1_ref, rt_ref)
        ysum_ref[...] = x
    h_ref[...] = _rms(x, g_ref[...]).astype(bf16)

    def rope(acc):
        rep = tn // LANES
        c = jnp.tile(c_ref[...], (1, rep))
        s1 = jnp.tile(s1_ref[...], (1, rep))
        s2 = jnp.tile(s2_ref[...], (1, rep))
        half = ROT_DIM // 2
        lo = pltpu.roll(acc, half, axis=1)
        hi = pltpu.roll(acc, tn - half, axis=1)
        return acc * c + lo * s1 + hi * s2

    def col_tile(j, carry):
        cols = pl.ds(pl.multiple_of(j * tn, tn), tn)
        acc = _dot(h_ref[...], w_ref[:, cols])
        if has_bias:
            acc = acc + b_ref[:, cols]
        if n_rope:
            @pl.when(j < n_rope)
            def _():
                z_ref[:, cols] = rope(acc).astype(z_ref.dtype)

            @pl.when(j >= n_rope)
            def _():
                z_ref[:, cols] = acc.astype(z_ref.dtype)
        else:
            z_ref[:, cols] = acc.astype(z_ref.dtype)
        return carry

    nc = z_ref.shape[1]
    lax.fori_loop(0, nc // tn, col_tile, 0)
    if nc % tn:
        tail = slice(nc - nc % tn, nc)
        acc = _dot(h_ref[...], w_ref[:, tail])
        if has_bias:
            acc = acc + b_ref[:, tail]
        z_ref[:, tail] = acc.astype(z_ref.dtype)


def _proj(ya, g, w, *, tm, tn, yb=None, add=None, bias=None, rope=None, n_rope=0):
    d = ya.shape[1]
    na = ya.shape[0] // tm
    nb = 0 if yb is None else yb.shape[0] // tm
    n = (na + nb) * tm
    nc = w.shape[1]
    assert n_rope <= nc // tn
    once = dict(pipeline_mode=pl.Buffered(1))
    in_specs = []
    args = []
    if yb is None:
        in_specs.append(pl.BlockSpec((tm, d), lambda i: (i, 0)))
        args.append(ya)
    else:
        in_specs.append(pl.BlockSpec((tm, d), lambda i: (jnp.minimum(i, na - 1), 0)))
        in_specs.append(pl.BlockSpec((tm, d), lambda i: (jnp.maximum(i - na, 0), 0)))
        args += [ya, yb]
    if add is not None:
        moe_out, route = add
        in_specs.append(pl.BlockSpec((tm, d), lambda i: (i, 0)))
        in_specs.append(pl.BlockSpec((tm, d), lambda i: (i + na, 0)))
        in_specs.append(pl.BlockSpec((tm, LANES), lambda i: (i, 0)))
        args += [moe_out, moe_out, route]
    in_specs.append(pl.BlockSpec((1, d), lambda i: (0, 0)))
    args.append(g.reshape(1, d))
    in_specs.append(pl.BlockSpec((d, nc), lambda i: (0, 0), **once))
    args.append(w)
    if bias is not None:
        in_specs.append(pl.BlockSpec((1, nc), lambda i: (0, 0)))
        args.append(bias.reshape(1, nc))
    if n_rope:
        for t in rope:
            in_specs.append(pl.BlockSpec((tm, LANES), lambda i: (i, 0)))
            args.append(t)
    out_shape = []
    out_specs = []
    if add is not None:
        out_shape.append(jax.ShapeDtypeStruct((n, d), f32))
        out_specs.append(pl.BlockSpec((tm, d), lambda i: (i, 0)))
    out_shape.append(jax.ShapeDtypeStruct((n, nc), bf16))
    out_specs.append(pl.BlockSpec((tm, nc), lambda i: (i, 0)))
    kern = functools.partial(_proj_kernel, n_first=(na if yb is not None else None),
                             has_add=add is not None, has_bias=bias is not None, n_rope=n_rope, tn=tn)
    out = pl.pallas_call(
        kern, grid=(na + nb,), in_specs=in_specs, out_specs=out_specs, out_shape=out_shape,
        scratch_shapes=[pltpu.VMEM((tm, d), bf16)],
        compiler_params=_cp(("arbitrary",)), name="proj")(*args)
    return out if add is not None else out[0]


def _gla_kernel(*refs, C, CB, G):
    seq_refs = [refs[5 * g:5 * g + 5] for g in range(G)]
    wlr_ref, blr_ref, gn_ref, s0_ref, o_ref, sfin_ref, st_ref = refs[5 * G:]
    s = pl.program_id(1)

    @pl.when(s == 0)
    def _():
        st_ref[...] = s0_ref[...]

    row = lax.broadcasted_iota(jnp.int32, (C, C), 0)
    col = lax.broadcasted_iota(jnp.int32, (C, C), 1)
    tril = row >= col
    tri = jnp.where(tril, 1.0, 0.0).astype(bf16)
    lane = lax.broadcasted_iota(jnp.int32, (C, LANES), 1)
    even = lane < DK_A
    srow = lax.broadcasted_iota(jnp.int32, (2 * DV_A, LANES), 0)
    slane = lax.broadcasted_iota(jnp.int32, (2 * DV_A, LANES), 1)
    blockdiag = (srow < DV_A) == (slane < DK_A)
    wlr = wlr_ref[...]
    blr = blr_ref[...]
    gn = gn_ref[...]

    def chunk(c, carry):
        for g in range(G):
            chunk_of(g, c)
        return carry

    def chunk_of(g, c):
        q_ref, k_ref, v_ref, g_ref, la_ref = seq_refs[g]
        rows = pl.ds(pl.multiple_of(c * C, C), C)
        pre = _dot(la_ref[rows, :], wlr) + blr
        loga = -(jnp.maximum(-pre, 0.0) + jnp.log1p(jnp.exp(-jnp.abs(pre)))) * (1.0 / GLA_TAU)
        hi = loga.astype(bf16)
        lo = (loga - hi.astype(f32)).astype(bf16)
        b = _dot(tri, hi) + _dot(tri, lo)
        b_last = b[C - 1:C, :]
        qf = q_ref[rows, :].astype(f32) * (DK_A ** -0.5)
        kf = k_ref[rows, :].astype(f32)
        q_dec = (qf * jnp.exp(b)).astype(bf16)
        k_inv = (kf * jnp.exp(-b)).astype(bf16)
        k_end = (kf * jnp.exp(b_last - b)).astype(bf16)
        dec = jnp.exp(b_last)
        for p in range(H_A // 2):
            sl = slice(p * LANES, (p + 1) * LANES)
            qp = q_dec[:, sl]
            kp = k_inv[:, sl]
            vp = v_ref[rows, p * 2 * DV_A:(p + 1) * 2 * DV_A]
            zero = jnp.zeros_like(qp)
            att_e = jnp.where(tril, _dot_nt(jnp.where(even, qp, zero), kp), 0.0).astype(bf16)
            att_o = jnp.where(tril, _dot_nt(jnp.where(even, zero, qp), kp), 0.0).astype(bf16)
            o_e = _dot(att_e, vp)[:, :DV_A]
            o_o = _dot(att_o, vp)[:, DV_A:]
            st = st_ref[g, p]
            o_inter = _dot_nt(qp, st.astype(bf16))
            ds = _dot_tn(vp, k_end[:, sl])
            st_ref[g, p] = dec[:, sl] * st + jnp.where(blockdiag, ds, 0.0)
            for half, o_intra in ((0, o_e), (1, o_o)):
                h = 2 * p + half
                hs = slice(h * DV_A, (h + 1) * DV_A)
                o = o_intra + o_inter[:, half * DV_A:(half + 1) * DV_A]
                gate = g_ref[rows, hs].astype(f32)
                o_ref[g, rows, hs] = (_rms(o, gn) * _silu(gate)).astype(o_ref.dtype)

    lax.fori_loop(0, CB, chunk, 0)

    @pl.when(s == pl.num_programs(1) - 1)
    def _():
        sfin_ref[...] = st_ref[...]


def _state_to_kernel_layout(s0):
    bsz = s0.shape[0]
    st = jnp.swapaxes(s0, -1, -2).reshape(bsz, H_A // 2, 2, DV_A, DK_A)
    z = jnp.zeros_like(st[:, :, 0])
    top = jnp.concatenate([st[:, :, 0], z], axis=-1)
    bot = jnp.concatenate([z, st[:, :, 1]], axis=-1)
    return jnp.concatenate([top, bot], axis=-2)


def _state_from_kernel_layout(st):
    bsz = st.shape[0]
    e = st[:, :, :DV_A, :DK_A]
    o = st[:, :, DV_A:, DK_A:]
    s = jnp.stack([e, o], axis=2).reshape(bsz, H_A, DV_A, DK_A)
    return jnp.swapaxes(s, -1, -2)


def _gla(z, w_lr, b_lr, g_norm, s0, *, bsz, t, row0, C, CB, G=2):
    rb = C * CB
    ns = t // rb
    r0 = row0 // rb
    assert bsz % G == 0

    def zspec(g, width, cb):
        return pl.BlockSpec((rb, width), lambda i, s: (r0 + (i * G + g) * ns + s, cb))

    hk = H_A * DK_A
    hv = H_A * DV_A
    st0 = _state_to_kernel_layout(s0.astype(f32))
    wl = jnp.zeros((LANES, hk), f32).at[:GLA_RANK].set(w_lr).astype(bf16)
    st_shape = (G, H_A // 2, 2 * DV_A, 2 * DK_A)
    seq_specs = []
    for g in range(G):
        seq_specs += [zspec(g, hk, _QA0 // hk), zspec(g, hk, _KA0 // hk), zspec(g, hv, _VA0 // hv),
                      zspec(g, hv, _GA0 // hv), zspec(g, LANES, _LA0 // LANES)]
    o, sfin = pl.pallas_call(
        functools.partial(_gla_kernel, C=C, CB=CB, G=G),
        grid=(bsz // G, ns),
        in_specs=seq_specs + [
            pl.BlockSpec((LANES, hk), lambda i, s: (0, 0)),
            pl.BlockSpec((1, hk), lambda i, s: (0, 0)),
            pl.BlockSpec((1, DV_A), lambda i, s: (0, 0)),
            pl.BlockSpec(st_shape, lambda i, s: (i, 0, 0, 0))],
        out_specs=[pl.BlockSpec((G, rb, hv), lambda i, s: (i, s, 0)),
                   pl.BlockSpec(st_shape, lambda i, s: (i, 0, 0, 0))],
        out_shape=[jax.ShapeDtypeStruct((bsz, t, hv), bf16),
                   jax.ShapeDtypeStruct((bsz,) + st_shape[1:], f32)],
        scratch_shapes=[pltpu.VMEM(st_shape, f32)],
        compiler_params=_cp(("arbitrary", "arbitrary")), name="gla")(
            *([z] * (5 * G)), wl, b_lr.reshape(1, hk).astype(f32), g_norm.reshape(1, DV_A).astype(f32), st0)
    return o.reshape(bsz * t, hv), _state_from_kernel_layout(sfin)


def _band_kernel(q_ref, ka_ref, kb_ref, kc_ref, va_ref, vb_ref, vc_ref, ba_ref, bb_ref, bc_ref,
                 o_ref, *, mask_first):
    qi = pl.program_id(1)
    for h in range(H_B):
        sl = slice(h * DH_B, (h + 1) * DH_B)
        qh = (q_ref[:, sl].astype(f32) * (DH_B ** -0.5)).astype(bf16)
        sa = _dot_nt(qh, ka_ref[:, sl].astype(bf16)) + ba_ref[h]
        sb = _dot_nt(qh, kb_ref[:, sl].astype(bf16)) + bb_ref[h]
        sc = _dot_nt(qh, kc_ref[:, sl].astype(bf16)) + bc_ref[h]
        if mask_first:
            sa = jnp.where(qi >= 2, sa, NEG_INF)
            sb = jnp.where(qi >= 1, sb, NEG_INF)
        m = jnp.maximum(jnp.maximum(jnp.max(sa, axis=-1, keepdims=True),
                                    jnp.max(sb, axis=-1, keepdims=True)),
                        jnp.max(sc, axis=-1, keepdims=True))
        ea = jnp.exp(sa - m)
        eb = jnp.exp(sb - m)
        ec = jnp.exp(sc - m)
        inv = 1.0 / (jnp.sum(ea, axis=-1, keepdims=True) + jnp.sum(eb, axis=-1, keepdims=True)
                     + jnp.sum(ec, axis=-1, keepdims=True))
        o = (_dot((ea * inv).astype(bf16), va_ref[:, sl].astype(bf16))
             + _dot((eb * inv).astype(bf16), vb_ref[:, sl].astype(bf16))
             + _dot((ec * inv).astype(bf16), vc_ref[:, sl].astype(bf16)))
        o_ref[:, sl] = o.astype(o_ref.dtype)


def _band_call(qkv_specs, args, biases, *, bsz, nq, qr, mask_first):
    hd = H_B * DH_B
    bias_specs = [pl.BlockSpec(b.shape, lambda b_, q_: (0, 0, 0)) for b in biases]
    return pl.pallas_call(
        functools.partial(_band_kernel, mask_first=mask_first),
        grid=(bsz, nq),
        in_specs=list(qkv_specs) + bias_specs,
        out_specs=pl.BlockSpec((qr, hd), lambda b, q: (b * nq + q, 0)),
        out_shape=jax.ShapeDtypeStruct((bsz * nq * qr, hd), bf16),
        compiler_params=_cp(("arbitrary", "arbitrary")), name="band")(*args, *biases)


def _rel_bias_matrix(tab, nq, nk, off):
    m = nq + nk
    j = np.arange(m)
    diff = np.where(j < nk, -j, m - j)
    idx = np.clip(diff + off, -REL_CLIP, REL_CLIP) + REL_CLIP
    u = tab.astype(f32)[:, jnp.asarray(idx)]
    h = tab.shape[0]
    return jnp.tile(u, (1, nq))[:, :nq * (m - 1)].reshape(h, nq, m - 1)[:, :, :nk]


def _band_bias_prompt(tab, qr, kb):
    tot = 2 * kb + qr
    qc = (np.arange(qr)[:, None] + 2 * kb) // CHUNK
    kc = np.arange(tot)[None, :] // CHUNK
    valid = (kc <= qc) & (kc >= qc - B_PREV)
    bias = jnp.where(jnp.asarray(valid)[None], _rel_bias_matrix(tab, qr, tot, 2 * kb), NEG_INF)
    return bias[:, :, :kb], bias[:, :, kb:2 * kb], bias[:, :, 2 * kb:]


def _band_bias_sample(tab, t, lc):
    bias = _rel_bias_matrix(tab, t, lc + t, lc)
    return bias[:, :, :lc // 2], bias[:, :, lc // 2:lc], bias[:, :, lc:]


def _swa_kernel(sink_ref, q_ref, kp_ref, kc_ref, vp_ref, vc_ref, o_ref, qs_ref, *, chunk_mask):
    qi = pl.program_id(1)
    qr = q_ref.shape[0]
    pr = kp_ref.shape[0]
    nl = G_C * LANES
    even = lax.broadcasted_iota(jnp.int32, (qr, LANES), 1) < DH_C
    top = lax.broadcasted_iota(jnp.int32, (LANES, LANES), 0) < DH_C
    if chunk_mask:
        kch = lax.broadcasted_iota(jnp.int32, (pr, nl), 0) // CHUNK
        qch = (lax.broadcasted_iota(jnp.int32, (pr, nl), 1) % LANES) // CHUNK
        ok_prev = (kch >= qch) & (qi >= 1)
        kch = lax.broadcasted_iota(jnp.int32, (qr, nl), 0) // CHUNK
        qch = (lax.broadcasted_iota(jnp.int32, (qr, nl), 1) % LANES) // CHUNK
        ok_cur = kch <= qch
    if qr < LANES:
        qs_ref[...] = jnp.zeros_like(qs_ref)
    for c in range(KV_C):
        ks = slice(c * LANES, (c + 1) * LANES)
        for p in range(G_C // 2):
            cols = slice(c * G_C * DH_C + p * LANES, c * G_C * DH_C + (p + 1) * LANES)
            qp = (q_ref[:, cols].astype(f32) * (DH_C ** -0.5)).astype(bf16)
            zero = jnp.zeros_like(qp)
            qs_ref[2 * p * LANES:2 * p * LANES + qr, :] = jnp.where(even, qp, zero)
            qs_ref[(2 * p + 1) * LANES:(2 * p + 1) * LANES + qr, :] = jnp.where(even, zero, qp)
        q_all = qs_ref[...]
        sink = jnp.concatenate(
            [jnp.full((1, LANES), sink_ref[c * G_C + j], f32) for j in range(G_C)], axis=1)
        st_prev = _dot_nt(kp_ref[:, ks].astype(bf16), q_all)
        st_cur = _dot_nt(kc_ref[:, ks].astype(bf16), q_all)
        if chunk_mask:
            st_prev = jnp.where(ok_prev, st_prev, NEG_INF)
            st_cur = jnp.where(ok_cur, st_cur, NEG_INF)
        m = jnp.maximum(jnp.maximum(jnp.max(st_prev, axis=0, keepdims=True),
                                    jnp.max(st_cur, axis=0, keepdims=True)), sink)
        e_prev = jnp.exp(st_prev - m)
        e_cur = jnp.exp(st_cur - m)
        inv = 1.0 / (jnp.sum(e_prev, axis=0, keepdims=True) + jnp.sum(e_cur, axis=0, keepdims=True)
                     + jnp.exp(sink - m))
        ot = (_dot_tn(vp_ref[:, ks].astype(bf16), (e_prev * inv).astype(bf16))
              + _dot_tn(vc_ref[:, ks].astype(bf16), (e_cur * inv).astype(bf16)))
        for p in range(G_C // 2):
            cols = slice(c * G_C * DH_C + p * LANES, c * G_C * DH_C + (p + 1) * LANES)
            blk = jnp.where(top, ot[:, 2 * p * LANES:(2 * p + 1) * LANES],
                            ot[:, (2 * p + 1) * LANES:(2 * p + 2) * LANES])
            o_ref[:, cols] = blk.T[:qr].astype(o_ref.dtype)


def _swa_call(specs, args, sinks, *, bsz, nq, qr, chunk_mask):
    hd = H_C * DH_C
    return pl.pallas_call(
        functools.partial(_swa_kernel, chunk_mask=chunk_mask),
        grid=(bsz, nq),
        in_specs=[pl.BlockSpec(memory_space=pltpu.SMEM)] + list(specs),
        out_specs=pl.BlockSpec((qr, hd), lambda b, q: (b * nq + q, 0)),
        out_shape=jax.ShapeDtypeStruct((bsz * nq * qr, hd), bf16),
        scratch_shapes=[pltpu.VMEM((G_C * LANES, LANES), bf16)],
        compiler_params=_cp(("arbitrary", "arbitrary")), name="swa")(sinks, *args)


def _out_kernel(*refs, n_parts, n_first, split_y):
    ny = 2 if split_y else 1
    y_refs = refs[:ny]
    refs = refs[ny - 1:]
    o_refs = refs[1:1 + 2 * n_parts]
    w_refs = refs[1 + 2 * n_parts:1 + 3 * n_parts]
    g_ref, rh_ref, rl_ref = refs[1 + 3 * n_parts:4 + 3 * n_parts]
    y1_ref, h2_ref, route_ref, cnt_ref, run_ref = refs[4 + 3 * n_parts:]
    i = pl.program_id(0)

    @pl.when(i == 0)
    def _():
        run_ref[...] = jnp.zeros_like(run_ref)

    acc = y_refs[0][...]
    if split_y:
        acc = jnp.where(i < n_first, acc, y_refs[1][...])
    for k in range(n_parts):
        lhs = jnp.where(i < n_first, o_refs[2 * k][...], o_refs[2 * k + 1][...])
        acc = acc + _dot(lhs, w_refs[k][...])
    y1_ref[...] = acc
    h2 = _rms(acc, g_ref[...])
    hi = h2.astype(bf16)
    h2_ref[...] = hi.astype(f32)
    lo = (h2 - hi.astype(f32)).astype(bf16)
    lg = _dot(hi, rh_ref[...]) + _dot(hi, rl_ref[...]) + _dot(lo, rh_ref[...])

    lane_i = lax.broadcasted_iota(jnp.int32, lg.shape, 1)
    lane = lane_i.astype(f32)
    big = float(LANES)
    ninf = -jnp.inf
    is_g = lane_i < N_GROUPS
    gmax = jnp.max(jnp.where(is_g, lg, ninf), axis=-1, keepdims=True)
    gsel = jnp.min(jnp.where(is_g & (lg == gmax), lane, big), axis=-1, keepdims=True)
    pg = 1.0 / jnp.sum(jnp.where(is_g, jnp.exp(lg - gmax), 0.0), axis=-1, keepdims=True)
    e0 = N_GROUPS + EXP_PER_GROUP * gsel
    is_e = (lane >= e0) & (lane < e0 + EXP_PER_GROUP)
    v1 = jnp.max(jnp.where(is_e, lg, ninf), axis=-1, keepdims=True)
    i1 = jnp.min(jnp.where(is_e & (lg == v1), lane, big), axis=-1, keepdims=True)
    is_e2 = is_e & (lane != i1)
    v2 = jnp.max(jnp.where(is_e2, lg, ninf), axis=-1, keepdims=True)
    i2 = jnp.min(jnp.where(is_e2 & (lg == v2), lane, big), axis=-1, keepdims=True)
    t = jnp.exp(v2 - v1)
    w1 = pg / (1.0 + t)
    w2 = pg * t / (1.0 + t)

    tmr = lg.shape[0]
    oh1 = jnp.where(lane == i1, 1.0, 0.0)
    oh2 = jnp.where(lane == i2, 1.0, 0.0)
    before = jnp.where(lax.broadcasted_iota(jnp.int32, (tmr, tmr), 0)
                       > lax.broadcasted_iota(jnp.int32, (tmr, tmr), 1), 1.0, 0.0).astype(bf16)
    tot1 = jnp.sum(oh1, axis=0, keepdims=True)
    tot2 = jnp.sum(oh2, axis=0, keepdims=True)
    base = run_ref[...]
    r1 = jnp.sum(oh1 * (base + _dot(before, oh1.astype(bf16))), axis=-1, keepdims=True)
    r2 = jnp.sum(oh2 * (base + tot1 + _dot(before, oh2.astype(bf16))), axis=-1, keepdims=True)
    run_ref[...] = base + tot1 + tot2
    cnt_ref[...] = base + tot1 + tot2
    vals = (i1 - N_GROUPS, i2 - N_GROUPS, w1, w2, r1, r2)
    out = jnp.zeros_like(lg)
    for k, v in enumerate(vals):
        out = jnp.where(lane_i == k, v, out)
    route_ref[...] = out


def _out_proj(y, parts, g, w_router, *, tm):
    n_first = parts[0][0].shape[0] // tm
    split_y = isinstance(y, tuple)
    if split_y:
        d = y[0].shape[1]
        n = y[0].shape[0] + y[1].shape[0]
        in_specs = [pl.BlockSpec((tm, d), lambda i: (jnp.minimum(i, n_first - 1), 0)),
                    pl.BlockSpec((tm, d), lambda i: (jnp.maximum(i - n_first, 0), 0))]
        args = list(y)
    else:
        n, d = y.shape
        in_specs = [pl.BlockSpec((tm, d), lambda i: (i, 0))]
        args = [y]
    n_tiles = n // tm
    for op, os_, _ in parts:
        dk = op.shape[1]
        in_specs.append(pl.BlockSpec((tm, dk), lambda i: (jnp.minimum(i, n_first - 1), 0)))
        in_specs.append(pl.BlockSpec((tm, dk), lambda i: (jnp.maximum(i - n_first, 0), 0)))
        args += [op, os_]
    for _, _, w in parts:
        in_specs.append(pl.BlockSpec(w.shape, lambda i: (0, 0)))
        args.append(w)
    rh = w_router.astype(bf16)
    rl = (w_router - rh.astype(f32)).astype(bf16)
    in_specs += [pl.BlockSpec((1, d), lambda i: (0, 0)),
                 pl.BlockSpec((d, LANES), lambda i: (0, 0)),
                 pl.BlockSpec((d, LANES), lambda i: (0, 0))]
    args += [g.reshape(1, d), rh, rl]
    return pl.pallas_call(
        functools.partial(_out_kernel, n_parts=len(parts), n_first=n_first, split_y=split_y),
        grid=(n_tiles,), in_specs=in_specs,
        out_specs=[pl.BlockSpec((tm, d), lambda i: (i, 0)),
                   pl.BlockSpec((tm, d), lambda i: (i, 0)),
                   pl.BlockSpec((tm, LANES), lambda i: (i, 0)),
                   pl.BlockSpec((1, LANES), lambda i: (0, 0))],
        out_shape=[jax.ShapeDtypeStruct((n, d), f32),
                   jax.ShapeDtypeStruct((n, d), f32),
                   jax.ShapeDtypeStruct((n, LANES), f32),
                   jax.ShapeDtypeStruct((1, LANES), f32)],
        scratch_shapes=[pltpu.VMEM((1, LANES), f32)],
        compiler_params=_cp(("arbitrary",)), name="out_proj")(*args)


def _moe_kernel(te_ref, tv_ref, pos_ref, plo_ref, phi_ref, en_ref, es_ref,
                h_ref, wg_ref, wu_ref, wd_ref, out_ref,
                slot_ref, x0, x1, x2, y0, y1, y2, gsem, ssem, wg_f, wu_f, wd_f, wsem, wg_s, wu_s, wd_s,
                *, tm, n):
    t = pl.program_id(0)
    nt = pl.num_programs(0)
    n2 = 2 * n
    xs = (x0, x1, x2)
    ys = (y0, y1, y2)
    depth = len(xs)

    def weights_start(e, slot):
        for src, dst in ((wg_ref, wg_f), (wu_ref, wu_f), (wd_ref, wd_f)):
            pltpu.make_async_copy(src.at[e], dst.at[slot], wsem.at[slot]).start()

    def weights_wait(slot):
        for src, dst in ((wg_ref, wg_f), (wu_ref, wu_f), (wd_ref, wd_f)):
            pltpu.make_async_copy(src.at[0], dst.at[slot], wsem.at[slot]).wait()

    def tile_valid(tile):
        return (tile >= 0) & (tile < nt) & (tv_ref[jnp.clip(tile, 0, nt - 1)] != 0)

    valid = tile_valid(t)

    def gather_row(tile, r, p):
        s = slot_ref[tile * tm + r]
        tok = jnp.where(s < n, s, jnp.where(s < n2, s - n, s - n2))
        pltpu.make_async_copy(h_ref.at[pl.ds(tok, 1)], xs[p].at[pl.ds(r, 1)], gsem.at[p]).start()

    def scatter_row(tile, r, p):
        s = slot_ref[tile * tm + r]
        pltpu.make_async_copy(ys[p].at[pl.ds(r, 1)], out_ref.at[pl.ds(s, 1)], ssem.at[p]).start()

    def rows_loop(row_fn, tile, p):
        def body(r, carry):
            row_fn(tile, r, p)
            return carry
        lax.fori_loop(0, tm, body, 0, unroll=8)

    def gather_wait(p):
        pltpu.make_async_copy(h_ref.at[pl.ds(0, tm)], xs[p], gsem.at[p]).wait()

    def scatter_wait(p):
        pltpu.make_async_copy(ys[p], out_ref.at[pl.ds(0, tm)], ssem.at[p]).wait()

    def compute(p):
        x = xs[p][...].astype(bf16)
        g = _dot(x, wg_s[...])
        u = _dot(x, wu_s[...])
        ys[p][...] = _dot((_silu(g) * u).astype(bf16), wd_s[...])

    @pl.when(t == 0)
    def _():
        def per_expert(e, nxt):
            def body(r, v):
                slot_ref[r] = v
                return v + 1
            return lax.fori_loop(plo_ref[e], phi_ref[e], body, nxt)
        lax.fori_loop(0, N_EXPERTS, per_expert, n2)

        def body(s, carry):
            slot_ref[pos_ref[s]] = s
            return carry
        lax.fori_loop(0, n2, body, 0, unroll=8)
        weights_start(te_ref[0], 0)
        rows_loop(gather_row, 0, 0)

        @pl.when(tile_valid(1))
        def _():
            rows_loop(gather_row, 1, 1)

    def step(p):
        q = (p + 2) % depth

        @pl.when(t >= depth)
        def _():
            scatter_wait(p)

        @pl.when(valid)
        def _():
            gather_wait(p)

            @pl.when((t == 0) | (te_ref[t] != te_ref[jnp.maximum(t - 1, 0)]))
            def _():
                slot = es_ref[t]
                weights_wait(slot)
                wg_s[...] = wg_f[slot].astype(bf16)
                wu_s[...] = wu_f[slot].astype(bf16)
                wd_s[...] = wd_f[slot].astype(bf16)

                @pl.when(en_ref[t] >= 0)
                def _():
                    weights_start(en_ref[t], 1 - slot)

            steady = tile_valid(t + 2) & (t >= 1)

            @pl.when(steady)
            def _():
                for r in range(tm):
                    gather_row(t + 2, r, q)
                    scatter_row(t - 1, r, q)
                compute(p)

            @pl.when(jnp.logical_not(steady))
            def _():
                @pl.when(tile_valid(t + 2))
                def _():
                    rows_loop(gather_row, t + 2, q)

                @pl.when(t >= 1)
                def _():
                    rows_loop(scatter_row, t - 1, q)
                compute(p)

        @pl.when(jnp.logical_not(valid))
        def _():
            @pl.when(tile_valid(t - 1))
            def _():
                rows_loop(scatter_row, t - 1, q)
            ys[p][...] = jnp.zeros(ys[p].shape, f32)
            pltpu.make_async_copy(ys[p], out_ref.at[pl.ds(pl.multiple_of(t * tm, tm), tm)], ssem.at[p]).start()

        @pl.when(t == nt - 1)
        def _():
            @pl.when(valid)
            def _():
                rows_loop(scatter_row, t, p)
            for k in range(depth):
                scatter_wait(k)

    for p in range(depth):
        pl.when(t % depth == p)(functools.partial(step, p))


def _moe_experts(h2, pos, tile_e, tile_v, pad_lo, pad_hi, wg, wu, wd, *, layer, tm, n_tiles):
    n, d = h2.shape
    ff = wg.shape[2]
    assert n >= N_EXPERTS * tm, "padding rows read tokens 0 .. N_EXPERTS * (tm - 1)"
    assert n_tiles >= 3
    first = jnp.concatenate([jnp.ones((1,), jnp.int32), (tile_e[1:] != tile_e[:-1]).astype(jnp.int32)])
    seq = jnp.cumsum(first * tile_v) - 1
    seq = jnp.where(tile_v != 0, seq, n_tiles)
    nxt_pos = jnp.searchsorted(seq, seq + 1, side="left")
    has_next = (nxt_pos < n_tiles) & (tile_v[jnp.minimum(nxt_pos, n_tiles - 1)] != 0) & (tile_v != 0)
    tile_e = tile_e + layer * N_EXPERTS
    e_next = jnp.where(has_next, tile_e[jnp.minimum(nxt_pos, n_tiles - 1)], -1).astype(jnp.int32)
    e_slot = (seq % 2).astype(jnp.int32)
    row_buf = pltpu.VMEM((tm, d), f32)
    any_spec = pl.BlockSpec(memory_space=pl.ANY)
    grid_spec = pltpu.PrefetchScalarGridSpec(
        num_scalar_prefetch=7, grid=(n_tiles,),
        in_specs=[any_spec, any_spec, any_spec, any_spec],
        out_specs=any_spec,
        scratch_shapes=[pltpu.SMEM((n_tiles * tm,), jnp.int32),
                        row_buf, row_buf, row_buf, row_buf, row_buf, row_buf,
                        pltpu.SemaphoreType.DMA((3,)), pltpu.SemaphoreType.DMA((3,)),
                        pltpu.VMEM((2, d, ff), f32), pltpu.VMEM((2, d, ff), f32), pltpu.VMEM((2, ff, d), f32),
                        pltpu.SemaphoreType.DMA((2,)),
                        pltpu.VMEM((d, ff), bf16), pltpu.VMEM((d, ff), bf16), pltpu.VMEM((ff, d), bf16)])
    return pl.pallas_call(
        functools.partial(_moe_kernel, tm=tm, n=n), grid_spec=grid_spec,
        out_shape=jax.ShapeDtypeStruct((n_tiles * tm, d), f32),
        compiler_params=_cp(("arbitrary",)), name="moe")(
            tile_e, tile_v, pos, pad_lo, pad_hi, e_next, e_slot, h2, wg, wu, wd)


def _moe(h2, route, counts, wg, wu, wd, *, layer, tm):
    n = route.shape[0]
    n_tiles = (2 * n) // tm + N_EXPERTS
    counts = counts[0, N_GROUPS:N_GROUPS + N_EXPERTS].astype(jnp.int32)
    padded = ((counts + tm - 1) // tm) * tm
    ends = jnp.cumsum(padded)
    starts = ends - padded
    e = route[:, 0:2].astype(jnp.int32)
    rank = route[:, 4:6].astype(jnp.int32)
    pos = (starts[e] + rank).T.reshape(-1)
    tile_start = jnp.arange(n_tiles, dtype=jnp.int32) * tm
    tile_e = jnp.sum((tile_start[:, None] >= ends[None, :]).astype(jnp.int32), axis=1)
    tile_v = (tile_start < ends[-1]).astype(jnp.int32)
    tile_e = jnp.minimum(tile_e, N_EXPERTS - 1)
    return _moe_experts(h2, pos, tile_e, tile_v, starts + counts, ends, wg, wu, wd, layer=layer, tm=tm,
                        n_tiles=n_tiles)


def _final_kernel(y_ref, m0_ref, m1_ref, rt_ref, g_ref, o_ref):
    o_ref[...] = _rms(y_ref[...] + _moe_combine(m0_ref, m1_ref, rt_ref), g_ref[...])


def _final(y, moe_out, route, g, *, tm, row0, rows):
    n, d = y.shape
    r0 = row0 // tm
    r1 = (n + row0) // tm
    return pl.pallas_call(
        _final_kernel, grid=(rows // tm,),
        in_specs=[pl.BlockSpec((tm, d), lambda i: (r0 + i, 0)),
                  pl.BlockSpec((tm, d), lambda i: (r0 + i, 0)),
                  pl.BlockSpec((tm, d), lambda i: (r1 + i, 0)),
                  pl.BlockSpec((tm, LANES), lambda i: (r0 + i, 0)),
                  pl.BlockSpec((1, d), lambda i: (0, 0))],
        out_specs=pl.BlockSpec((tm, d), lambda i: (i, 0)),
        out_shape=jax.ShapeDtypeStruct((rows, d), f32),
        compiler_params=_cp(("arbitrary",)), name="final_norm")(y, moe_out, moe_out, route, g.reshape(1, d))


def _rope_tables(pos):
    half = ROT_DIM // 2
    inv = jnp.power(ROPE_THETA, -jnp.arange(half, dtype=f32) * 2.0 / ROT_DIM)
    ang = pos.astype(f32)[:, None] * inv[None, :]
    cos, sin = lax.optimization_barrier((jnp.cos(ang), jnp.sin(ang)))
    n = pos.shape[0]
    one = jnp.ones((n, DH_C - ROT_DIM), f32)
    zero = jnp.zeros((n, DH_C - ROT_DIM), f32)
    zh = jnp.zeros((n, half), f32)
    c = jnp.concatenate([cos, cos, one], axis=1)
    s1 = jnp.concatenate([zh, sin, zero], axis=1)
    s2 = jnp.concatenate([-sin, zh, zero], axis=1)
    rep = LANES // DH_C
    return tuple(jnp.tile(t, (1, rep)) for t in (c, s1, s2))


def _dup_heads(a):
    sh = a.shape[:-1]
    a4 = a.reshape(sh + (KV_C, 1, DH_C))
    return jnp.broadcast_to(a4, sh + (KV_C, 2, DH_C)).reshape(sh + (KV_C * 2 * DH_C,))


def kernel(x_prompt, x_sample, cache_gla_state, cache_band_k, cache_band_v, cache_swa_k, cache_swa_v,
           norm_mix, norm_ffn, norm_final, w_in_ab, w_gla_lr, b_gla_lr, gla_out_norm, rel_bias_tab,
           w_out_ab, w_in_c, b_in_c, attn_sinks, w_out_c, w_router_grp, w_router_exp, w_exp_gate,
           w_exp_up, w_exp_down):
    bp, sp, d = x_prompt.shape
    bs, ts, _ = x_sample.shape
    n_p = bp * sp
    n_s = bs * ts
    n = n_p + n_s
    tm = 512 if n_s % 512 == 0 else n_s
    tmo = min(tm, 256)
    tme = 256 if (2 * n) % 256 == 0 else 64
    ff = w_exp_gate.shape[-1]

    xp = x_prompt.reshape(n_p, d)
    xs_ = x_sample.reshape(n_s, d)

    def router_w(l):
        wr = jnp.concatenate([w_router_grp[l], w_router_exp[l].reshape(d, N_EXPERTS)], axis=1)
        return jnp.zeros((d, LANES), f32).at[:, :N_GROUPS + N_EXPERTS].set(wr)

    experts = (w_exp_gate.reshape(-1, d, ff), w_exp_up.reshape(-1, d, ff), w_exp_down.reshape(-1, ff, d))

    w0 = w_in_ab[0]
    la0 = 2 * H_A * DK_A + 2 * H_A * DV_A
    w_cat = jnp.concatenate(
        [w0[:, :la0], w0[:, la0 + GLA_RANK:], w0[:, la0:la0 + GLA_RANK],
         jnp.zeros((d, LANES - GLA_RANK), f32)], axis=1).astype(bf16)
    z = _proj(xp, norm_mix[0], w_cat, tm=tmo, tn=1024, yb=xs_)

    c_s = min(CHUNK, ts)
    oa_p, gla_p = _gla(z, w_gla_lr[0], b_gla_lr[0], gla_out_norm[0],
                       jnp.zeros((bp, H_A, DK_A, DV_A), f32), bsz=bp, t=sp, row0=0, C=CHUNK, CB=4)
    oa_s, gla_s = _gla(z, w_gla_lr[0], b_gla_lr[0], gla_out_norm[0], cache_gla_state[0],
                       bsz=bs, t=ts, row0=n_p, C=c_s, CB=ts // c_s)

    hd = H_B * DH_B
    qr = 256
    nq = sp // qr
    qcb, kcb, vcb = _QB0 // hd, _KB0 // hd, _VB0 // hd
    cur = lambda cb: pl.BlockSpec((qr, hd), lambda b, q: (b * nq + q, cb))
    prev = lambda cb, back: pl.BlockSpec((qr, hd), lambda b, q: (b * nq + jnp.maximum(q - back, 0), cb))
    ob_p = _band_call(
        [cur(qcb), prev(kcb, 2), prev(kcb, 1), cur(kcb), prev(vcb, 2), prev(vcb, 1), cur(vcb)],
        [z] * 7, _band_bias_prompt(rel_bias_tab[0], qr, qr), bsz=bp, nq=nq, qr=qr, mask_first=True)

    lb = cache_band_k.shape[2]
    ck = cache_band_k[0].reshape(bs * lb, hd)
    cv = cache_band_v[0].reshape(bs * lb, hd)
    rs0 = n_p // ts
    new = lambda cb: pl.BlockSpec((ts, hd), lambda b, q: (rs0 + b, cb))
    cpiece = lambda k: pl.BlockSpec((lb // 2, hd), lambda b, q: (2 * b + k, 0))
    ob_s = _band_call(
        [new(qcb), cpiece(0), cpiece(1), new(kcb), cpiece(0), cpiece(1), new(vcb)],
        [z, ck, ck, z, cv, cv, z], _band_bias_sample(rel_bias_tab[0], ts, lb),
        bsz=bs, nq=1, qr=ts, mask_first=False)

    dmix_a = H_A * DV_A
    wo = w_out_ab[0].astype(bf16)
    y1, h2, route, cnt = _out_proj(
        (xp, xs_), [(oa_p, oa_s, wo[:dmix_a]), (ob_p, ob_s, wo[dmix_a:])], norm_ffn[0], router_w(0),
        tm=tmo)
    moe0 = _moe(h2, route, cnt, *experts, layer=0, tm=tme)

    keep = min(B_PREV * CHUNK, sp)

    def prompt_tail(a, rows, c0, width):
        return jnp.stack([a[(b + 1) * sp - rows:(b + 1) * sp, c0:c0 + width] for b in range(bp)])

    band_k_p = prompt_tail(z, keep, _KB0, hd).astype(f32).reshape(bp, keep, H_B, DH_B)[None]
    band_v_p = prompt_tail(z, keep, _VB0, hd).astype(f32).reshape(bp, keep, H_B, DH_B)[None]
    kb_s = z[n_p:, _KB0:_KB0 + hd].astype(f32).reshape(bs, ts, H_B, DH_B)
    vb_s = z[n_p:, _VB0:_VB0 + hd].astype(f32).reshape(bs, ts, H_B, DH_B)
    band_k_s = jnp.concatenate([cache_band_k[0][:, ts:], kb_s], axis=1)[None]
    band_v_s = jnp.concatenate([cache_band_v[0][:, ts:], vb_s], axis=1)[None]

    w1 = w_in_c[0]
    nq_c = H_C * DH_C
    nkv = KV_C * DH_C
    w1_cat = jnp.concatenate([w1[:, :nq_c], _dup_heads(w1[:, nq_c:nq_c + nkv]),
                              _dup_heads(w1[:, nq_c + nkv:])], axis=1).astype(bf16)
    b1 = b_in_c[0]
    b1_cat = jnp.concatenate([b1[:nq_c], _dup_heads(b1[nq_c:nq_c + nkv]), _dup_heads(b1[nq_c + nkv:])])
    uniq = _rope_tables(jnp.concatenate([jnp.arange(sp), PAST_LEN + jnp.arange(ts)]))
    rope = tuple(jnp.concatenate([jnp.tile(u[:sp], (bp, 1)), jnp.tile(u[sp:], (bs, 1))]) for u in uniq)
    y2, z1 = _proj(y1, norm_mix[1], w1_cat, tm=tmo, tn=512, add=(moe0, route), bias=b1_cat,
                   rope=rope, n_rope=(nq_c + 2 * nkv) // 512)

    qr_c = WINDOW
    nq1 = sp // qr_c
    kw = 2 * nkv
    kcb1, vcb1 = _KC0 // kw, _VC0 // kw
    qspec = pl.BlockSpec((qr_c, nq_c), lambda b, q: (b * nq1 + q, 0))
    cur1 = lambda cb: pl.BlockSpec((qr_c, kw), lambda b, q: (b * nq1 + q, cb))
    prev1 = lambda cb: pl.BlockSpec((qr_c, kw), lambda b, q: (b * nq1 + jnp.maximum(q - 1, 0), cb))
    sinks = attn_sinks[0].astype(f32)
    oc_p = _swa_call([qspec, prev1(kcb1), cur1(kcb1), prev1(vcb1), cur1(vcb1)], [z1] * 5, sinks,
                     bsz=bp, nq=nq1, qr=qr_c, chunk_mask=True)

    lc = cache_swa_k.shape[2]
    csk = _dup_heads(cache_swa_k[0].reshape(bs * lc, nkv))
    csv = _dup_heads(cache_swa_v[0].reshape(bs * lc, nkv))
    qspec_s = pl.BlockSpec((ts, nq_c), lambda b, q: (rs0 + b, 0))
    new1 = lambda cb: pl.BlockSpec((ts, kw), lambda b, q: (rs0 + b, cb))
    cspec = pl.BlockSpec((lc, kw), lambda b, q: (b, 0))
    oc_s = _swa_call([qspec_s, cspec, new1(kcb1), cspec, new1(vcb1)], [z1, csk, z1, csv, z1], sinks,
                     bsz=bs, nq=1, qr=ts, chunk_mask=False)

    y3, h4, route1, cnt1 = _out_proj(y2, [(oc_p, oc_s, w_out_c[0].astype(bf16))], norm_ffn[1],
                                     router_w(1), tm=tmo)
    moe1 = _moe(h4, route1, cnt1, *experts, layer=1, tm=tme)

    keep1 = min(WINDOW, sp)

    def undup(a):
        return a.astype(f32).reshape(a.shape[:-1] + (KV_C, 2, DH_C))[..., 0, :]

    swa_k_p = undup(prompt_tail(z1, keep1, _KC0, kw))[None]
    swa_v_p = undup(prompt_tail(z1, keep1, _VC0, kw))[None]
    k1_s = undup(z1[n_p:, _KC0:_KC0 + kw]).reshape(bs, ts, KV_C, DH_C)
    v1_s = undup(z1[n_p:, _VC0:_VC0 + kw]).reshape(bs, ts, KV_C, DH_C)
    swa_k_s = jnp.concatenate([cache_swa_k[0][:, ts:], k1_s], axis=1)[None]
    swa_v_s = jnp.concatenate([cache_swa_v[0][:, ts:], v1_s], axis=1)[None]

    y_prompt = _final(y3, moe1, route1, norm_final, tm=tm, row0=0, rows=n_p).reshape(bp, sp, d)
    y_sample = _final(y3, moe1, route1, norm_final, tm=tm, row0=n_p, rows=n_s).reshape(bs, ts, d)

    return (y_prompt, y_sample, gla_p[None], gla_s[None].astype(cache_gla_state.dtype),
            band_k_p, band_v_p, band_k_s, band_v_s, swa_k_p, swa_v_p, swa_k_s, swa_v_s)
```

```python
import functools

import numpy as np
import jax
import jax.numpy as jnp
from jax import lax
from jax.experimental import pallas as pl
from jax.experimental.pallas import tpu as pltpu

f32 = jnp.float32
bf16 = jnp.bfloat16

CHUNK = 64
EPS = 1e-6
NEG_INF = -1e30
PAST_LEN = 1024
H_A = 8
DK_A = 64
DV_A = 128
GLA_RANK = 16
GLA_TAU = 16.0
H_B = 8
DH_B = 128
B_PREV = 8
REL_CLIP = 128
H_C = 32
KV_C = 4
G_C = H_C // KV_C
DH_C = 64
WINDOW = 128
ROT_DIM = DH_C // 4
ROPE_THETA = 500000.0
N_GROUPS = 4
EXP_PER_GROUP = 8
N_EXPERTS = N_GROUPS * EXP_PER_GROUP

LANES = 128
VMEM_LIMIT = 52 * 1024 * 1024

_QA0, _KA0, _VA0, _GA0 = 0, 512, 1024, 2048
_QB0, _KB0, _VB0, _LA0 = 3072, 4096, 5120, 6144
_NC_AB = 6272
_QC0, _KC0, _VC0 = 0, 2048, 2560
_NC_C = 3072


def _cp(sem):
    return pltpu.CompilerParams(dimension_semantics=sem, vmem_limit_bytes=VMEM_LIMIT)


def _dot(a, b):
    return jnp.dot(a, b, preferred_element_type=f32)


def _dot_nt(a, b):
    return lax.dot_general(a, b, (((1,), (1,)), ((), ())), preferred_element_type=f32)


def _dot_tn(a, b):
    return lax.dot_general(a, b, (((0,), (0,)), ((), ())), preferred_element_type=f32)


def _rms(x, g):
    ms = jnp.mean(x * x, axis=-1, keepdims=True)
    return (x * lax.rsqrt(ms + EPS)) * g


def _silu(x):
    return x / (1.0 + jnp.exp(-x))


def _moe_combine(m0_ref, m1_ref, route_ref):
    rt = route_ref[...]
    return rt[:, 2:3] * m0_ref[...] + rt[:, 3:4] * m1_ref[...]


def _proj_kernel(*refs, n_first, has_add, has_bias, n_rope, tn):
    it = iter(refs)
    ya_ref = next(it)
    yb_ref = next(it) if n_first is not None else None
    if has_add:
        m0_ref, m1_ref, rt_ref = next(it), next(it), next(it)
    g_ref = next(it)
    w_ref = next(it)
    b_ref = next(it) if has_bias else None
    if n_rope:
        c_ref, s1_ref, s2_ref = next(it), next(it), next(it)
    ysum_ref = next(it) if has_add else None
    z_ref = next(it)
    h_ref = next(it)
    i = pl.program_id(0)

    x = ya_ref[...]
    if n_first is not None:
        x = jnp.where(i < n_first, x, yb_ref[...])
    if has_add:
        x = x + _moe_combine(m0_ref, m1_ref, rt_ref)
        ysum_ref[...] = x
    h_ref[...] = _rms(x, g_ref[...]).astype(bf16)

    def rope(acc):
        rep = tn // LANES
        c = jnp.tile(c_ref[...], (1, rep))
        s1 = jnp.tile(s1_ref[...], (1, rep))
        s2 = jnp.tile(s2_ref[...], (1, rep))
        half = ROT_DIM // 2
        lo = pltpu.roll(acc, half, axis=1)
        hi = pltpu.roll(acc, tn - half, axis=1)
        return acc * c + lo * s1 + hi * s2

    def col_tile(j, carry):
        cols = pl.ds(pl.multiple_of(j * tn, tn), tn)
        acc = _dot(h_ref[...], w_ref[:, cols])
        if has_bias:
            acc = acc + b_ref[:, cols]
        if n_rope:
            @pl.when(j < n_rope)
            def _():
                z_ref[:, cols] = rope(acc).astype(z_ref.dtype)

            @pl.when(j >= n_rope)
            def _():
                z_ref[:, cols] = acc.astype(z_ref.dtype)
        else:
            z_ref[:, cols] = acc.astype(z_ref.dtype)
        return carry

    nc = z_ref.shape[1]
    lax.fori_loop(0, nc // tn, col_tile, 0)
    if nc % tn:
        tail = slice(nc - nc % tn, nc)
        acc = _dot(h_ref[...], w_ref[:, tail])
        if has_bias:
            acc = acc + b_ref[:, tail]
        z_ref[:, tail] = acc.astype(z_ref.dtype)


def _proj(ya, g, w, *, tm, tn, yb=None, add=None, bias=None, rope=None, n_rope=0):
    d = ya.shape[1]
    na = ya.shape[0] // tm
    nb = 0 if yb is None else yb.shape[0] // tm
    n = (na + nb) * tm
    nc = w.shape[1]
    assert n_rope <= nc // tn
    once = dict(pipeline_mode=pl.Buffered(1))
    in_specs = []
    args = []
    if yb is None:
        in_specs.append(pl.BlockSpec((tm, d), lambda i: (i, 0)))
        args.append(ya)
    else:
        in_specs.append(pl.BlockSpec((tm, d), lambda i: (jnp.minimum(i, na - 1), 0)))
        in_specs.append(pl.BlockSpec((tm, d), lambda i: (jnp.maximum(i - na, 0), 0)))
        args += [ya, yb]
    if add is not None:
        moe_out, route = add
        in_specs.append(pl.BlockSpec((tm, d), lambda i: (i, 0)))
        in_specs.append(pl.BlockSpec((tm, d), lambda i: (i + na, 0)))
        in_specs.append(pl.BlockSpec((tm, LANES), lambda i: (i, 0)))
        args += [moe_out, moe_out, route]
    in_specs.append(pl.BlockSpec((1, d), lambda i: (0, 0)))
    args.append(g.reshape(1, d))
    in_specs.append(pl.BlockSpec((d, nc), lambda i: (0, 0), **once))
    args.append(w)
    if bias is not None:
        in_specs.append(pl.BlockSpec((1, nc), lambda i: (0, 0)))
        args.append(bias.reshape(1, nc))
    if n_rope:
        for t in rope:
            in_specs.append(pl.BlockSpec((tm, LANES), lambda i: (i, 0)))
            args.append(t)
    out_shape = []
    out_specs = []
    if add is not None:
        out_shape.append(jax.ShapeDtypeStruct((n, d), f32))
        out_specs.append(pl.BlockSpec((tm, d), lambda i: (i, 0)))
    out_shape.append(jax.ShapeDtypeStruct((n, nc), bf16))
    out_specs.append(pl.BlockSpec((tm, nc), lambda i: (i, 0)))
    kern = functools.partial(_proj_kernel, n_first=(na if yb is not None else None),
                             has_add=add is not None, has_bias=bias is not None, n_rope=n_rope, tn=tn)
    out = pl.pallas_call(
        kern, grid=(na + nb,), in_specs=in_specs, out_specs=out_specs, out_shape=out_shape,
        scratch_shapes=[pltpu.VMEM((tm, d), bf16)],
        compiler_params=_cp(("arbitrary",)), name="proj")(*args)
    return out if add is not None else out[0]


def _gla_kernel(*refs, C, CB, G):
    seq_refs = [refs[5 * g:5 * g + 5] for g in range(G)]
    (wlr_ref, blr_ref, gn_ref, s0_ref, o_ref, sfin_ref,
     st_ref, qd_ref, oi_ref, ds_ref, dec_ref) = refs[5 * G:]
    s = pl.program_id(1)
    R = C * CB

    @pl.when(s == 0)
    def _():
        st_ref[...] = s0_ref[...]

    row = lax.broadcasted_iota(jnp.int32, (R, R), 0)
    col = lax.broadcasted_iota(jnp.int32, (R, R), 1)
    same = (row // C) == (col // C)
    tril = same & (row >= col)
    tri = jnp.where(tril, 1.0, 0.0).astype(bf16)
    ones = jnp.where(same, 1.0, 0.0).astype(bf16)
    even = lax.broadcasted_iota(jnp.int32, (R, LANES), 1) < DK_A
    srow = lax.broadcasted_iota(jnp.int32, (2 * DV_A, LANES), 0)
    slane = lax.broadcasted_iota(jnp.int32, (2 * DV_A, LANES), 1)
    blockdiag = (srow < DV_A) == (slane < DK_A)
    wlr = wlr_ref[...]
    blr = blr_ref[...]
    gn = gn_ref[...]
    pairs = H_A // 2

    for g in range(G):
        q_ref, k_ref, v_ref, _, la_ref = seq_refs[g]
        pre = _dot(la_ref[...], wlr) + blr
        loga = -(jnp.maximum(-pre, 0.0) + jnp.log1p(jnp.exp(-jnp.abs(pre)))) * (1.0 / GLA_TAU)
        hi = loga.astype(bf16)
        lo = (loga - hi.astype(f32)).astype(bf16)
        b = _dot(tri, hi) + _dot(tri, lo)
        b_end = _dot(ones, hi) + _dot(ones, lo)
        qf = q_ref[...].astype(f32) * (DK_A ** -0.5)
        kf = k_ref[...].astype(f32)
        q_dec = (qf * jnp.exp(b)).astype(bf16)
        k_inv = (kf * jnp.exp(-b)).astype(bf16)
        k_end = (kf * jnp.exp(b_end - b)).astype(bf16)
        dec = jnp.exp(b_end)
        qd_ref[g] = q_dec
        for c in range(CB):
            dec_ref[g, c] = dec[c * C:c * C + 1, :]
        for p in range(pairs):
            sl = slice(p * LANES, (p + 1) * LANES)
            vs = slice(p * 2 * DV_A, (p + 1) * 2 * DV_A)
            qp = q_dec[:, sl]
            kp = k_inv[:, sl]
            vp = v_ref[:, vs]
            zero = jnp.zeros_like(qp)
            att_e = jnp.where(tril, _dot_nt(jnp.where(even, qp, zero), kp), 0.0).astype(bf16)
            att_o = jnp.where(tril, _dot_nt(jnp.where(even, zero, qp), kp), 0.0).astype(bf16)
            oi_ref[g, :, vs] = jnp.concatenate([_dot(att_e, vp)[:, :DV_A], _dot(att_o, vp)[:, DV_A:]], axis=1)
            for c in range(CB):
                rows = slice(c * C, (c + 1) * C)
                ds = _dot_tn(vp[rows], k_end[rows, sl])
                ds_ref[g, c, p] = jnp.where(blockdiag, ds, 0.0)

    for c in range(CB):
        rows = slice(c * C, (c + 1) * C)
        for g in range(G):
            for p in range(pairs):
                sl = slice(p * LANES, (p + 1) * LANES)
                vs = slice(p * 2 * DV_A, (p + 1) * 2 * DV_A)
                st = st_ref[g, p]
                oi_ref[g, rows, vs] += _dot_nt(qd_ref[g, rows, sl], st.astype(bf16))
                st_ref[g, p] = dec_ref[g, c][:, sl] * st + ds_ref[g, c, p]

    for g in range(G):
        g_ref = seq_refs[g][3]
        for h in range(H_A):
            hs = slice(h * DV_A, (h + 1) * DV_A)
            o_ref[g, :, hs] = (_rms(oi_ref[g, :, hs], gn) * _silu(g_ref[:, hs].astype(f32))).astype(o_ref.dtype)

    @pl.when(s == pl.num_programs(1) - 1)
    def _():
        sfin_ref[...] = st_ref[...]


def _state_to_kernel_layout(s0):
    bsz = s0.shape[0]
    st = jnp.swapaxes(s0, -1, -2).reshape(bsz, H_A // 2, 2, DV_A, DK_A)
    z = jnp.zeros_like(st[:, :, 0])
    top = jnp.concatenate([st[:, :, 0], z], axis=-1)
    bot = jnp.concatenate([z, st[:, :, 1]], axis=-1)
    return jnp.concatenate([top, bot], axis=-2)


def _state_from_kernel_layout(st):
    bsz = st.shape[0]
    e = st[:, :, :DV_A, :DK_A]
    o = st[:, :, DV_A:, DK_A:]
    s = jnp.stack([e, o], axis=2).reshape(bsz, H_A, DV_A, DK_A)
    return jnp.swapaxes(s, -1, -2)


def _gla(z, w_lr, b_lr, g_norm, s0, *, bsz, t, row0, C, CB, G=2):
    rb = C * CB
    ns = t // rb
    r0 = row0 // rb
    assert bsz % G == 0

    def zspec(g, width, cb):
        return pl.BlockSpec((rb, width), lambda i, s: (r0 + (i * G + g) * ns + s, cb))

    hk = H_A * DK_A
    hv = H_A * DV_A
    st0 = _state_to_kernel_layout(s0.astype(f32))
    wl = jnp.zeros((LANES, hk), f32).at[:GLA_RANK].set(w_lr).astype(bf16)
    st_shape = (G, H_A // 2, 2 * DV_A, 2 * DK_A)
    seq_specs = []
    for g in range(G):
        seq_specs += [zspec(g, hk, _QA0 // hk), zspec(g, hk, _KA0 // hk), zspec(g, hv, _VA0 // hv),
                      zspec(g, hv, _GA0 // hv), zspec(g, LANES, _LA0 // LANES)]
    o, sfin = pl.pallas_call(
        functools.partial(_gla_kernel, C=C, CB=CB, G=G),
        grid=(bsz // G, ns),
        in_specs=seq_specs + [
            pl.BlockSpec((LANES, hk), lambda i, s: (0, 0)),
            pl.BlockSpec((1, hk), lambda i, s: (0, 0)),
            pl.BlockSpec((1, DV_A), lambda i, s: (0, 0)),
            pl.BlockSpec(st_shape, lambda i, s: (i, 0, 0, 0))],
        out_specs=[pl.BlockSpec((G, rb, hv), lambda i, s: (i, s, 0)),
                   pl.BlockSpec(st_shape, lambda i, s: (i, 0, 0, 0))],
        out_shape=[jax.ShapeDtypeStruct((bsz, t, hv), bf16),
                   jax.ShapeDtypeStruct((bsz,) + st_shape[1:], f32)],
        scratch_shapes=[pltpu.VMEM(st_shape, f32),
                        pltpu.VMEM((G, rb, hk), bf16),
                        pltpu.VMEM((G, rb, hv), f32),
                        pltpu.VMEM((G, CB) + st_shape[1:], f32),
                        pltpu.VMEM((G, CB, 1, hk), f32)],
        compiler_params=_cp(("arbitrary", "arbitrary")), name="gla")(
            *([z] * (5 * G)), wl, b_lr.reshape(1, hk).astype(f32), g_norm.reshape(1, DV_A).astype(f32), st0)
    return o.reshape(bsz * t, hv), _state_from_kernel_layout(sfin)


def _band_kernel(q_ref, ka_ref, kb_ref, kc_ref, va_ref, vb_ref, vc_ref, ba_ref, bb_ref, bc_ref,
                 o_ref, *, mask_first):
    qi = pl.program_id(1)
    for h in range(H_B):
        sl = slice(h * DH_B, (h + 1) * DH_B)
        qh = (q_ref[:, sl].astype(f32) * (DH_B ** -0.5)).astype(bf16)
        sa = _dot_nt(qh, ka_ref[:, sl].astype(bf16)) + ba_ref[h]
        sb = _dot_nt(qh, kb_ref[:, sl].astype(bf16)) + bb_ref[h]
        sc = _dot_nt(qh, kc_ref[:, sl].astype(bf16)) + bc_ref[h]
        if mask_first:
            sa = jnp.where(qi >= 2, sa, NEG_INF)
            sb = jnp.where(qi >= 1, sb, NEG_INF)
        m = jnp.maximum(jnp.maximum(jnp.max(sa, axis=-1, keepdims=True),
                                    jnp.max(sb, axis=-1, keepdims=True)),
                        jnp.max(sc, axis=-1, keepdims=True))
        ea = jnp.exp(sa - m)
        eb = jnp.exp(sb - m)
        ec = jnp.exp(sc - m)
        inv = 1.0 / (jnp.sum(ea, axis=-1, keepdims=True) + jnp.sum(eb, axis=-1, keepdims=True)
                     + jnp.sum(ec, axis=-1, keepdims=True))
        o = (_dot((ea * inv).astype(bf16), va_ref[:, sl].astype(bf16))
             + _dot((eb * inv).astype(bf16), vb_ref[:, sl].astype(bf16))
             + _dot((ec * inv).astype(bf16), vc_ref[:, sl].astype(bf16)))
        o_ref[:, sl] = o.astype(o_ref.dtype)


def _band_call(qkv_specs, args, biases, *, bsz, nq, qr, mask_first):
    hd = H_B * DH_B
    bias_specs = [pl.BlockSpec(b.shape, lambda b_, q_: (0, 0, 0)) for b in biases]
    return pl.pallas_call(
        functools.partial(_band_kernel, mask_first=mask_first),
        grid=(bsz, nq),
        in_specs=list(qkv_specs) + bias_specs,
        out_specs=pl.BlockSpec((qr, hd), lambda b, q: (b * nq + q, 0)),
        out_shape=jax.ShapeDtypeStruct((bsz * nq * qr, hd), bf16),
        compiler_params=_cp(("arbitrary", "arbitrary")), name="band")(*args, *biases)


def _rel_bias_matrix(tab, nq, nk, off):
    m = nq + nk
    j = np.arange(m)
    diff = np.where(j < nk, -j, m - j)
    idx = np.clip(diff + off, -REL_CLIP, REL_CLIP) + REL_CLIP
    u = tab.astype(f32)[:, jnp.asarray(idx)]
    h = tab.shape[0]
    return jnp.tile(u, (1, nq))[:, :nq * (m - 1)].reshape(h, nq, m - 1)[:, :, :nk]


def _band_bias_prompt(tab, qr, kb):
    tot = 2 * kb + qr
    qc = (np.arange(qr)[:, None] + 2 * kb) // CHUNK
    kc = np.arange(tot)[None, :] // CHUNK
    valid = (kc <= qc) & (kc >= qc - B_PREV)
    bias = jnp.where(jnp.asarray(valid)[None], _rel_bias_matrix(tab, qr, tot, 2 * kb), NEG_INF)
    return bias[:, :, :kb], bias[:, :, kb:2 * kb], bias[:, :, 2 * kb:]


def _band_bias_sample(tab, t, lc):
    bias = _rel_bias_matrix(tab, t, lc + t, lc)
    return bias[:, :, :lc // 2], bias[:, :, lc // 2:lc], bias[:, :, lc:]


def _swa_kernel(sink_ref, q_ref, kp_ref, kc_ref, vp_ref, vc_ref, o_ref, qs_ref, *, chunk_mask):
    qi = pl.program_id(1)
    qr = q_ref.shape[0]
    pr = kp_ref.shape[0]
    nl = G_C * LANES
    even = lax.broadcasted_iota(jnp.int32, (qr, LANES), 1) < DH_C
    top = lax.broadcasted_iota(jnp.int32, (LANES, LANES), 0) < DH_C
    if chunk_mask:
        kch = lax.broadcasted_iota(jnp.int32, (pr, nl), 0) // CHUNK
        qch = (lax.broadcasted_iota(jnp.int32, (pr, nl), 1) % LANES) // CHUNK
        ok_prev = (kch >= qch) & (qi >= 1)
        kch = lax.broadcasted_iota(jnp.int32, (qr, nl), 0) // CHUNK
        qch = (lax.broadcasted_iota(jnp.int32, (qr, nl), 1) % LANES) // CHUNK
        ok_cur = kch <= qch
    if qr < LANES:
        qs_ref[...] = jnp.zeros_like(qs_ref)
    for c in range(KV_C):
        ks = slice(c * LANES, (c + 1) * LANES)
        for p in range(G_C // 2):
            cols = slice(c * G_C * DH_C + p * LANES, c * G_C * DH_C + (p + 1) * LANES)
            qp = (q_ref[:, cols].astype(f32) * (DH_C ** -0.5)).astype(bf16)
            zero = jnp.zeros_like(qp)
            qs_ref[2 * p * LANES:2 * p * LANES + qr, :] = jnp.where(even, qp, zero)
            qs_ref[(2 * p + 1) * LANES:(2 * p + 1) * LANES + qr, :] = jnp.where(even, zero, qp)
        q_all = qs_ref[...]
        sink = jnp.concatenate(
            [jnp.full((1, LANES), sink_ref[c * G_C + j], f32) for j in range(G_C)], axis=1)
        st_prev = _dot_nt(kp_ref[:, ks].astype(bf16), q_all)
        st_cur = _dot_nt(kc_ref[:, ks].astype(bf16), q_all)
        if chunk_mask:
            st_prev = jnp.where(ok_prev, st_prev, NEG_INF)
            st_cur = jnp.where(ok_cur, st_cur, NEG_INF)
        m = jnp.maximum(jnp.maximum(jnp.max(st_prev, axis=0, keepdims=True),
                                    jnp.max(st_cur, axis=0, keepdims=True)), sink)
        e_prev = jnp.exp(st_prev - m)
        e_cur = jnp.exp(st_cur - m)
        inv = 1.0 / (jnp.sum(e_prev, axis=0, keepdims=True) + jnp.sum(e_cur, axis=0, keepdims=True)
                     + jnp.exp(sink - m))
        ot = (_dot_tn(vp_ref[:, ks].astype(bf16), (e_prev * inv).astype(bf16))
              + _dot_tn(vc_ref[:, ks].astype(bf16), (e_cur * inv).astype(bf16)))
        for p in range(G_C // 2):
            cols = slice(c * G_C * DH_C + p * LANES, c * G_C * DH_C + (p + 1) * LANES)
            blk = jnp.where(top, ot[:, 2 * p * LANES:(2 * p + 1) * LANES],
                            ot[:, (2 * p + 1) * LANES:(2 * p + 2) * LANES])
            o_ref[:, cols] = blk.T[:qr].astype(o_ref.dtype)


def _swa_call(specs, args, sinks, *, bsz, nq, qr, chunk_mask):
    hd = H_C * DH_C
    return pl.pallas_call(
        functools.partial(_swa_kernel, chunk_mask=chunk_mask),
        grid=(bsz, nq),
        in_specs=[pl.BlockSpec(memory_space=pltpu.SMEM)] + list(specs),
        out_specs=pl.BlockSpec((qr, hd), lambda b, q: (b * nq + q, 0)),
        out_shape=jax.ShapeDtypeStruct((bsz * nq * qr, hd), bf16),
        scratch_shapes=[pltpu.VMEM((G_C * LANES, LANES), bf16)],
        compiler_params=_cp(("arbitrary", "arbitrary")), name="swa")(sinks, *args)


def _out_kernel(*refs, n_parts, n_first, split_y):
    ny = 2 if split_y else 1
    y_refs = refs[:ny]
    refs = refs[ny - 1:]
    o_refs = refs[1:1 + 2 * n_parts]
    w_refs = refs[1 + 2 * n_parts:1 + 3 * n_parts]
    g_ref, rh_ref, rl_ref = refs[1 + 3 * n_parts:4 + 3 * n_parts]
    y1_ref, h2_ref, route_ref, cnt_ref, run_ref = refs[4 + 3 * n_parts:]
    i = pl.program_id(0)

    @pl.when(i == 0)
    def _():
        run_ref[...] = jnp.zeros_like(run_ref)

    acc = y_refs[0][...]
    if split_y:
        acc = jnp.where(i < n_first, acc, y_refs[1][...])
    for k in range(n_parts):
        lhs = jnp.where(i < n_first, o_refs[2 * k][...], o_refs[2 * k + 1][...])
        acc = acc + _dot(lhs, w_refs[k][...])
    y1_ref[...] = acc
    h2 = _rms(acc, g_ref[...])
    hi = h2.astype(bf16)
    h2_ref[...] = hi.astype(f32)
    lo = (h2 - hi.astype(f32)).astype(bf16)
    lg = _dot(hi, rh_ref[...]) + _dot(hi, rl_ref[...]) + _dot(lo, rh_ref[...])

    lane_i = lax.broadcasted_iota(jnp.int32, lg.shape, 1)
    lane = lane_i.astype(f32)
    big = float(LANES)
    ninf = -jnp.inf
    is_g = lane_i < N_GROUPS
    gmax = jnp.max(jnp.where(is_g, lg, ninf), axis=-1, keepdims=True)
    gsel = jnp.min(jnp.where(is_g & (lg == gmax), lane, big), axis=-1, keepdims=True)
    pg = 1.0 / jnp.sum(jnp.where(is_g, jnp.exp(lg - gmax), 0.0), axis=-1, keepdims=True)
    e0 = N_GROUPS + EXP_PER_GROUP * gsel
    is_e = (lane >= e0) & (lane < e0 + EXP_PER_GROUP)
    v1 = jnp.max(jnp.where(is_e, lg, ninf), axis=-1, keepdims=True)
    i1 = jnp.min(jnp.where(is_e & (lg == v1), lane, big), axis=-1, keepdims=True)
    is_e2 = is_e & (lane != i1)
    v2 = jnp.max(jnp.where(is_e2, lg, ninf), axis=-1, keepdims=True)
    i2 = jnp.min(jnp.where(is_e2 & (lg == v2), lane, big), axis=-1, keepdims=True)
    t = jnp.exp(v2 - v1)
    w1 = pg / (1.0 + t)
    w2 = pg * t / (1.0 + t)

    tmr = lg.shape[0]
    oh1 = jnp.where(lane == i1, 1.0, 0.0)
    oh2 = jnp.where(lane == i2, 1.0, 0.0)
    before = jnp.where(lax.broadcasted_iota(jnp.int32, (tmr, tmr), 0)
                       > lax.broadcasted_iota(jnp.int32, (tmr, tmr), 1), 1.0, 0.0).astype(bf16)
    tot1 = jnp.sum(oh1, axis=0, keepdims=True)
    tot2 = jnp.sum(oh2, axis=0, keepdims=True)
    base = run_ref[...]
    r1 = jnp.sum(oh1 * (base + _dot(before, oh1.astype(bf16))), axis=-1, keepdims=True)
    r2 = jnp.sum(oh2 * (base + tot1 + _dot(before, oh2.astype(bf16))), axis=-1, keepdims=True)
    run_ref[...] = base + tot1 + tot2
    cnt_ref[...] = base + tot1 + tot2
    vals = (i1 - N_GROUPS, i2 - N_GROUPS, w1, w2, r1, r2)
    out = jnp.zeros_like(lg)
    for k, v in enumerate(vals):
        out = jnp.where(lane_i == k, v, out)
    route_ref[...] = out


def _out_proj(y, parts, g, w_router, *, tm):
    n_first = parts[0][0].shape[0] // tm
    split_y = isinstance(y, tuple)
    if split_y:
        d = y[0].shape[1]
        n = y[0].shape[0] + y[1].shape[0]
        in_specs = [pl.BlockSpec((tm, d), lambda i: (jnp.minimum(i, n_first - 1), 0)),
                    pl.BlockSpec((tm, d), lambda i: (jnp.maximum(i - n_first, 0), 0))]
        args = list(y)
    else:
        n, d = y.shape
        in_specs = [pl.BlockSpec((tm, d), lambda i: (i, 0))]
        args = [y]
    n_tiles = n // tm
    for op, os_, _ in parts:
        dk = op.shape[1]
        in_specs.append(pl.BlockSpec((tm, dk), lambda i: (jnp.minimum(i, n_first - 1), 0)))
        in_specs.append(pl.BlockSpec((tm, dk), lambda i: (jnp.maximum(i - n_first, 0), 0)))
        args += [op, os_]
    for _, _, w in parts:
        in_specs.append(pl.BlockSpec(w.shape, lambda i: (0, 0)))
        args.append(w)
    rh = w_router.astype(bf16)
    rl = (w_router - rh.astype(f32)).astype(bf16)
    in_specs += [pl.BlockSpec((1, d), lambda i: (0, 0)),
                 pl.BlockSpec((d, LANES), lambda i: (0, 0)),
                 pl.BlockSpec((d, LANES), lambda i: (0, 0))]
    args += [g.reshape(1, d), rh, rl]
    return pl.pallas_call(
        functools.partial(_out_kernel, n_parts=len(parts), n_first=n_first, split_y=split_y),
        grid=(n_tiles,), in_specs=in_specs,
        out_specs=[pl.BlockSpec((tm, d), lambda i: (i, 0)),
                   pl.BlockSpec((tm, d), lambda i: (i, 0)),
                   pl.BlockSpec((tm, LANES), lambda i: (i, 0)),
                   pl.BlockSpec((1, LANES), lambda i: (0, 0))],
        out_shape=[jax.ShapeDtypeStruct((n, d), f32),
                   jax.ShapeDtypeStruct((n, d), f32),
                   jax.ShapeDtypeStruct((n, LANES), f32),
                   jax.ShapeDtypeStruct((1, LANES), f32)],
        scratch_shapes=[pltpu.VMEM((1, LANES), f32)],
        compiler_params=_cp(("arbitrary",)), name="out_proj")(*args)


def _moe_kernel(te_ref, tv_ref, pos_ref, plo_ref, phi_ref, en_ref, es_ref,
                h_ref, wg_ref, wu_ref, wd_ref, out_ref,
                slot_ref, x0, x1, x2, y0, y1, y2, gsem, ssem, wg_f, wu_f, wd_f, wsem, wg_s, wu_s, wd_s,
                *, tm, n):
    t = pl.program_id(0)
    nt = pl.num_programs(0)
    n2 = 2 * n
    xs = (x0, x1, x2)
    ys = (y0, y1, y2)
    depth = len(xs)

    def weights_start(e, slot):
        for src, dst in ((wg_ref, wg_f), (wu_ref, wu_f), (wd_ref, wd_f)):
            pltpu.make_async_copy(src.at[e], dst.at[slot], wsem.at[slot]).start()

    def weights_wait(slot):
        for src, dst in ((wg_ref, wg_f), (wu_ref, wu_f), (wd_ref, wd_f)):
            pltpu.make_async_copy(src.at[0], dst.at[slot], wsem.at[slot]).wait()

    def tile_valid(tile):
        return (tile >= 0) & (tile < nt) & (tv_ref[jnp.clip(tile, 0, nt - 1)] != 0)

    valid = tile_valid(t)

    def gather_row(tile, r, p):
        s = slot_ref[tile * tm + r]
        tok = jnp.where(s < n, s, jnp.where(s < n2, s - n, s - n2))
        pltpu.make_async_copy(h_ref.at[pl.ds(tok, 1)], xs[p].at[pl.ds(r, 1)], gsem.at[p]).start()

    def scatter_row(tile, r, p):
        s = slot_ref[tile * tm + r]
        pltpu.make_async_copy(ys[p].at[pl.ds(r, 1)], out_ref.at[pl.ds(s, 1)], ssem.at[p]).start()

    def rows_loop(row_fn, tile, p):
        def body(r, carry):
            row_fn(tile, r, p)
            return carry
        lax.fori_loop(0, tm, body, 0, unroll=8)

    def gather_wait(p):
        pltpu.make_async_copy(h_ref.at[pl.ds(0, tm)], xs[p], gsem.at[p]).wait()

    def scatter_wait(p):
        pltpu.make_async_copy(ys[p], out_ref.at[pl.ds(0, tm)], ssem.at[p]).wait()

    def compute(p):
        x = xs[p][...].astype(bf16)
        g = _dot(x, wg_s[...])
        u = _dot(x, wu_s[...])
        ys[p][...] = _dot((_silu(g) * u).astype(bf16), wd_s[...])

    @pl.when(t == 0)
    def _():
        def per_expert(e, nxt):
            def body(r, v):
                slot_ref[r] = v
                return v + 1
            return lax.fori_loop(plo_ref[e], phi_ref[e], body, nxt)
        lax.fori_loop(0, N_EXPERTS, per_expert, n2)

        def body(s, carry):
            slot_ref[pos_ref[s]] = s
            return carry
        lax.fori_loop(0, n2, body, 0, unroll=8)
        weights_start(te_ref[0], 0)
        rows_loop(gather_row, 0, 0)

        @pl.when(tile_valid(1))
        def _():
            rows_loop(gather_row, 1, 1)

    def step(p):
        q = (p + 2) % depth

        @pl.when(t >= depth)
        def _():
            scatter_wait(p)

        @pl.when(valid)
        def _():
            gather_wait(p)

            @pl.when((t == 0) | (te_ref[t] != te_ref[jnp.maximum(t - 1, 0)]))
            def _():
                slot = es_ref[t]
                weights_wait(slot)
                wg_s[...] = wg_f[slot].astype(bf16)
                wu_s[...] = wu_f[slot].astype(bf16)
                wd_s[...] = wd_f[slot].astype(bf16)

                @pl.when(en_ref[t] >= 0)
                def _():
                    weights_start(en_ref[t], 1 - slot)

            steady = tile_valid(t + 2) & (t >= 1)

            @pl.when(steady)
            def _():
                for r in range(tm):
                    gather_row(t + 2, r, q)
                    scatter_row(t - 1, r, q)
                compute(p)

            @pl.when(jnp.logical_not(steady))
            def _():
                @pl.when(tile_valid(t + 2))
                def _():
                    rows_loop(gather_row, t + 2, q)

                @pl.when(t >= 1)
                def _():
                    rows_loop(scatter_row, t - 1, q)
                compute(p)

        @pl.when(jnp.logical_not(valid))
        def _():
            @pl.when(tile_valid(t - 1))
            def _():
                rows_loop(scatter_row, t - 1, q)
            ys[p][...] = jnp.zeros(ys[p].shape, f32)
            pltpu.make_async_copy(ys[p], out_ref.at[pl.ds(pl.multiple_of(t * tm, tm), tm)], ssem.at[p]).start()

        @pl.when(t == nt - 1)
        def _():
            @pl.when(valid)
            def _():
                rows_loop(scatter_row, t, p)
            for k in range(depth):
                scatter_wait(k)

    for p in range(depth):
        pl.when(t % depth == p)(functools.partial(step, p))


def _moe_experts(h2, pos, tile_e, tile_v, pad_lo, pad_hi, wg, wu, wd, *, layer, tm, n_tiles):
    n, d = h2.shape
    ff = wg.shape[2]
    assert n >= N_EXPERTS * tm, "padding rows read tokens 0 .. N_EXPERTS * (tm - 1)"
    assert n_tiles >= 3
    first = jnp.concatenate([jnp.ones((1,), jnp.int32), (tile_e[1:] != tile_e[:-1]).astype(jnp.int32)])
    seq = jnp.cumsum(first * tile_v) - 1
    seq = jnp.where(tile_v != 0, seq, n_tiles)
    nxt_pos = jnp.sum((seq[None, :] <= seq[:, None]).astype(jnp.int32), axis=1)
    has_next = (nxt_pos < n_tiles) & (tile_v[jnp.minimum(nxt_pos, n_tiles - 1)] != 0) & (tile_v != 0)
    tile_e = tile_e + layer * N_EXPERTS
    e_next = jnp.where(has_next, tile_e[jnp.minimum(nxt_pos, n_tiles - 1)], -1).astype(jnp.int32)
    e_slot = (seq % 2).astype(jnp.int32)
    row_buf = pltpu.VMEM((tm, d), f32)
    any_spec = pl.BlockSpec(memory_space=pl.ANY)
    grid_spec = pltpu.PrefetchScalarGridSpec(
        num_scalar_prefetch=7, grid=(n_tiles,),
        in_specs=[any_spec, any_spec, any_spec, any_spec],
        out_specs=any_spec,
        scratch_shapes=[pltpu.SMEM((n_tiles * tm,), jnp.int32),
                        row_buf, row_buf, row_buf, row_buf, row_buf, row_buf,
                        pltpu.SemaphoreType.DMA((3,)), pltpu.SemaphoreType.DMA((3,)),
                        pltpu.VMEM((2, d, ff), f32), pltpu.VMEM((2, d, ff), f32), pltpu.VMEM((2, ff, d), f32),
                        pltpu.SemaphoreType.DMA((2,)),
                        pltpu.VMEM((d, ff), bf16), pltpu.VMEM((d, ff), bf16), pltpu.VMEM((ff, d), bf16)])
    return pl.pallas_call(
        functools.partial(_moe_kernel, tm=tm, n=n), grid_spec=grid_spec,
        out_shape=jax.ShapeDtypeStruct((n_tiles * tm, d), f32),
        compiler_params=_cp(("arbitrary",)), name="moe")(
            tile_e, tile_v, pos, pad_lo, pad_hi, e_next, e_slot, h2, wg, wu, wd)


def _moe(h2, route, counts, wg, wu, wd, *, layer, tm):
    n = route.shape[0]
    n_tiles = (2 * n) // tm + N_EXPERTS
    counts = counts[0, N_GROUPS:N_GROUPS + N_EXPERTS].astype(jnp.int32)
    padded = ((counts + tm - 1) // tm) * tm
    ends = jnp.cumsum(padded)
    starts = ends - padded
    e = route[:, 0:2].astype(jnp.int32)
    rank = route[:, 4:6].astype(jnp.int32)
    pos = (starts[e] + rank).T.reshape(-1)
    tile_start = jnp.arange(n_tiles, dtype=jnp.int32) * tm
    tile_e = jnp.sum((tile_start[:, None] >= ends[None, :]).astype(jnp.int32), axis=1)
    tile_v = (tile_start < ends[-1]).astype(jnp.int32)
    tile_e = jnp.minimum(tile_e, N_EXPERTS - 1)
    return _moe_experts(h2, pos, tile_e, tile_v, starts + counts, ends, wg, wu, wd, layer=layer, tm=tm,
                        n_tiles=n_tiles)


def _final_kernel(y_ref, m0_ref, m1_ref, rt_ref, g_ref, o_ref):
    o_ref[...] = _rms(y_ref[...] + _moe_combine(m0_ref, m1_ref, rt_ref), g_ref[...])


def _final(y, moe_out, route, g, *, tm, row0, rows):
    n, d = y.shape
    r0 = row0 // tm
    r1 = (n + row0) // tm
    return pl.pallas_call(
        _final_kernel, grid=(rows // tm,),
        in_specs=[pl.BlockSpec((tm, d), lambda i: (r0 + i, 0)),
                  pl.BlockSpec((tm, d), lambda i: (r0 + i, 0)),
                  pl.BlockSpec((tm, d), lambda i: (r1 + i, 0)),
                  pl.BlockSpec((tm, LANES), lambda i: (r0 + i, 0)),
                  pl.BlockSpec((1, d), lambda i: (0, 0))],
        out_specs=pl.BlockSpec((tm, d), lambda i: (i, 0)),
        out_shape=jax.ShapeDtypeStruct((rows, d), f32),
        compiler_params=_cp(("arbitrary",)), name="final_norm")(y, moe_out, moe_out, route, g.reshape(1, d))


def _rope_tables(pos):
    half = ROT_DIM // 2
    inv = jnp.power(ROPE_THETA, -jnp.arange(half, dtype=f32) * 2.0 / ROT_DIM)
    ang = pos.astype(f32)[:, None] * inv[None, :]
    cos, sin = lax.optimization_barrier((jnp.cos(ang), jnp.sin(ang)))
    n = pos.shape[0]
    one = jnp.ones((n, DH_C - ROT_DIM), f32)
    zero = jnp.zeros((n, DH_C - ROT_DIM), f32)
    zh = jnp.zeros((n, half), f32)
    c = jnp.concatenate([cos, cos, one], axis=1)
    s1 = jnp.concatenate([zh, sin, zero], axis=1)
    s2 = jnp.concatenate([-sin, zh, zero], axis=1)
    rep = LANES // DH_C
    return tuple(jnp.tile(t, (1, rep)) for t in (c, s1, s2))


def _dup_heads(a):
    sh = a.shape[:-1]
    a4 = a.reshape(sh + (KV_C, 1, DH_C))
    return jnp.broadcast_to(a4, sh + (KV_C, 2, DH_C)).reshape(sh + (KV_C * 2 * DH_C,))


def kernel(x_prompt, x_sample, cache_gla_state, cache_band_k, cache_band_v, cache_swa_k, cache_swa_v,
           norm_mix, norm_ffn, norm_final, w_in_ab, w_gla_lr, b_gla_lr, gla_out_norm, rel_bias_tab,
           w_out_ab, w_in_c, b_in_c, attn_sinks, w_out_c, w_router_grp, w_router_exp, w_exp_gate,
           w_exp_up, w_exp_down):
    bp, sp, d = x_prompt.shape
    bs, ts, _ = x_sample.shape
    n_p = bp * sp
    n_s = bs * ts
    n = n_p + n_s
    tm = 512 if n_s % 512 == 0 else n_s
    tmo = min(tm, 256)
    tme = 256 if (2 * n) % 256 == 0 else 64
    ff = w_exp_gate.shape[-1]

    xp = x_prompt.reshape(n_p, d)
    xs_ = x_sample.reshape(n_s, d)

    def router_w(l):
        wr = jnp.concatenate([w_router_grp[l], w_router_exp[l].reshape(d, N_EXPERTS)], axis=1)
        return jnp.zeros((d, LANES), f32).at[:, :N_GROUPS + N_EXPERTS].set(wr)

    experts = (w_exp_gate.reshape(-1, d, ff), w_exp_up.reshape(-1, d, ff), w_exp_down.reshape(-1, ff, d))

    w0 = w_in_ab[0]
    la0 = 2 * H_A * DK_A + 2 * H_A * DV_A
    w_cat = jnp.concatenate(
        [w0[:, :la0], w0[:, la0 + GLA_RANK:], w0[:, la0:la0 + GLA_RANK],
         jnp.zeros((d, LANES - GLA_RANK), f32)], axis=1).astype(bf16)
    z = _proj(xp, norm_mix[0], w_cat, tm=tmo, tn=1024, yb=xs_)

    c_s = min(CHUNK, ts)
    oa_p, gla_p = _gla(z, w_gla_lr[0], b_gla_lr[0], gla_out_norm[0],
                       jnp.zeros((bp, H_A, DK_A, DV_A), f32), bsz=bp, t=sp, row0=0, C=CHUNK, CB=4)
    oa_s, gla_s = _gla(z, w_gla_lr[0], b_gla_lr[0], gla_out_norm[0], cache_gla_state[0],
                       bsz=bs, t=ts, row0=n_p, C=c_s, CB=ts // c_s)

    hd = H_B * DH_B
    qr = 256
    nq = sp // qr
    qcb, kcb, vcb = _QB0 // hd, _KB0 // hd, _VB0 // hd
    cur = lambda cb: pl.BlockSpec((qr, hd), lambda b, q: (b * nq + q, cb))
    prev = lambda cb, back: pl.BlockSpec((qr, hd), lambda b, q: (b * nq + jnp.maximum(q - back, 0), cb))
    ob_p = _band_call(
        [cur(qcb), prev(kcb, 2), prev(kcb, 1), cur(kcb), prev(vcb, 2), prev(vcb, 1), cur(vcb)],
        [z] * 7, _band_bias_prompt(rel_bias_tab[0], qr, qr), bsz=bp, nq=nq, qr=qr, mask_first=True)

    lb = cache_band_k.shape[2]
    ck = cache_band_k[0].reshape(bs * lb, hd)
    cv = cache_band_v[0].reshape(bs * lb, hd)
    rs0 = n_p // ts
    new = lambda cb: pl.BlockSpec((ts, hd), lambda b, q: (rs0 + b, cb))
    cpiece = lambda k: pl.BlockSpec((lb // 2, hd), lambda b, q: (2 * b + k, 0))
    ob_s = _band_call(
        [new(qcb), cpiece(0), cpiece(1), new(kcb), cpiece(0), cpiece(1), new(vcb)],
        [z, ck, ck, z, cv, cv, z], _band_bias_sample(rel_bias_tab[0], ts, lb),
        bsz=bs, nq=1, qr=ts, mask_first=False)

    dmix_a = H_A * DV_A
    wo = w_out_ab[0].astype(bf16)
    y1, h2, route, cnt = _out_proj(
        (xp, xs_), [(oa_p, oa_s, wo[:dmix_a]), (ob_p, ob_s, wo[dmix_a:])], norm_ffn[0], router_w(0),
        tm=tmo)
    moe0 = _moe(h2, route, cnt, *experts, layer=0, tm=tme)

    keep = min(B_PREV * CHUNK, sp)

    def prompt_tail(a, rows, c0, width):
        return jnp.stack([a[(b + 1) * sp - rows:(b + 1) * sp, c0:c0 + width] for b in range(bp)])

    band_k_p = prompt_tail(z, keep, _KB0, hd).astype(f32).reshape(bp, keep, H_B, DH_B)[None]
    band_v_p = prompt_tail(z, keep, _VB0, hd).astype(f32).reshape(bp, keep, H_B, DH_B)[None]
    kb_s = z[n_p:, _KB0:_KB0 + hd].astype(f32).reshape(bs, ts, H_B, DH_B)
    vb_s = z[n_p:, _VB0:_VB0 + hd].astype(f32).reshape(bs, ts, H_B, DH_B)
    band_k_s = jnp.concatenate([cache_band_k[0][:, ts:], kb_s], axis=1)[None]
    band_v_s = jnp.concatenate([cache_band_v[0][:, ts:], vb_s], axis=1)[None]

    w1 = w_in_c[0]
    nq_c = H_C * DH_C
    nkv = KV_C * DH_C
    w1_cat = jnp.concatenate([w1[:, :nq_c], _dup_heads(w1[:, nq_c:nq_c + nkv]),
                              _dup_heads(w1[:, nq_c + nkv:])], axis=1).astype(bf16)
    b1 = b_in_c[0]
    b1_cat = jnp.concatenate([b1[:nq_c], _dup_heads(b1[nq_c:nq_c + nkv]), _dup_heads(b1[nq_c + nkv:])])
    uniq = _rope_tables(jnp.concatenate([jnp.arange(sp), PAST_LEN + jnp.arange(ts)]))
    rope = tuple(jnp.concatenate([jnp.tile(u[:sp], (bp, 1)), jnp.tile(u[sp:], (bs, 1))]) for u in uniq)
    y2, z1 = _proj(y1, norm_mix[1], w1_cat, tm=tmo, tn=512, add=(moe0, route), bias=b1_cat,
                   rope=rope, n_rope=(nq_c + 2 * nkv) // 512)

    qr_c = WINDOW
    nq1 = sp // qr_c
    kw = 2 * nkv
    kcb1, vcb1 = _KC0 // kw, _VC0 // kw
    qspec = pl.BlockSpec((qr_c, nq_c), lambda b, q: (b * nq1 + q, 0))
    cur1 = lambda cb: pl.BlockSpec((qr_c, kw), lambda b, q: (b * nq1 + q, cb))
    prev1 = lambda cb: pl.BlockSpec((qr_c, kw), lambda b, q: (b * nq1 + jnp.maximum(q - 1, 0), cb))
    sinks = attn_sinks[0].astype(f32)
    oc_p = _swa_call([qspec, prev1(kcb1), cur1(kcb1), prev1(vcb1), cur1(vcb1)], [z1] * 5, sinks,
                     bsz=bp, nq=nq1, qr=qr_c, chunk_mask=True)

    lc = cache_swa_k.shape[2]
    csk = _dup_heads(cache_swa_k[0].reshape(bs * lc, nkv))
    csv = _dup_heads(cache_swa_v[0].reshape(bs * lc, nkv))
    qspec_s = pl.BlockSpec((ts, nq_c), lambda b, q: (rs0 + b, 0))
    new1 = lambda cb: pl.BlockSpec((ts, kw), lambda b, q: (rs0 + b, cb))
    cspec = pl.BlockSpec((lc, kw), lambda b, q: (b, 0))
    oc_s = _swa_call([qspec_s, cspec, new1(kcb1), cspec, new1(vcb1)], [z1, csk, z1, csv, z1], sinks,
                     bsz=bs, nq=1, qr=ts, chunk_mask=False)

    y3, h4, route1, cnt1 = _out_proj(y2, [(oc_p, oc_s, w_out_c[0].astype(bf16))], norm_ffn[1],
                                     router_w(1), tm=tmo)
    moe1 = _moe(h4, route1, cnt1, *experts, layer=1, tm=tme)

    keep1 = min(WINDOW, sp)

    def undup(a):
        return a.astype(f32).reshape(a.shape[:-1] + (KV_C, 2, DH_C))[..., 0, :]

    swa_k_p = undup(prompt_tail(z1, keep1, _KC0, kw))[None]
    swa_v_p = undup(prompt_tail(z1, keep1, _VC0, kw))[None]
    k1_s = undup(z1[n_p:, _KC0:_KC0 + kw]).reshape(bs, ts, KV_C, DH_C)
    v1_s = undup(z1[n_p:, _VC0:_VC0 + kw]).reshape(bs, ts, KV_C, DH_C)
    swa_k_s = jnp.concatenate([cache_swa_k[0][:, ts:], k1_s], axis=1)[None]
    swa_v_s = jnp.concatenate([cache_swa_v[0][:, ts:], v1_s], axis=1)[None]

    y_prompt = _final(y3, moe1, route1, norm_final, tm=tm, row0=0, rows=n_p).reshape(bp, sp, d)
    y_sample = _final(y3, moe1, route1, norm_final, tm=tm, row0=n_p, rows=n_s).reshape(bs, ts, d)

    return (y_prompt, y_sample, gla_p[None], gla_s[None].astype(cache_gla_state.dtype),
            band_k_p, band_v_p, band_k_s, band_v_s, swa_k_p, swa_v_p, swa_k_s, swa_v_s)
```

```python
import functools

import numpy as np
import jax
import jax.numpy as jnp
from jax import lax
from jax.experimental import pallas as pl
from jax.experimental.pallas import tpu as pltpu

f32 = jnp.float32
bf16 = jnp.bfloat16

CHUNK = 64
EPS = 1e-6
NEG_INF = -1e30
PAST_LEN = 1024
H_A = 8
DK_A = 64
DV_A = 128
GLA_RANK = 16
GLA_TAU = 16.0
H_B = 8
DH_B = 128
B_PREV = 8
REL_CLIP = 128
H_C = 32
KV_C = 4
G_C = H_C // KV_C
DH_C = 64
WINDOW = 128
ROT_DIM = DH_C // 4
ROPE_THETA = 500000.0
N_GROUPS = 4
EXP_PER_GROUP = 8
N_EXPERTS = N_GROUPS * EXP_PER_GROUP

LANES = 128
VMEM_LIMIT = 52 * 1024 * 1024

_QA0, _KA0, _VA0, _GA0 = 0, 512, 1024, 2048
_QB0, _KB0, _VB0, _LA0 = 3072, 4096, 5120, 6144
_NC_AB = 6272
_QC0, _KC0, _VC0 = 0, 2048, 2304
_NC_C = 2560


def _cp(sem):
    return pltpu.CompilerParams(dimension_semantics=sem, vmem_limit_bytes=VMEM_LIMIT)


def _dot(a, b):
    return jnp.dot(a, b, preferred_element_type=f32)


def _dot_nt(a, b):
    return lax.dot_general(a, b, (((1,), (1,)), ((), ())), preferred_element_type=f32)


def _dot_tn(a, b):
    return lax.dot_general(a, b, (((0,), (0,)), ((), ())), preferred_element_type=f32)


def _rms(x, g):
    ms = jnp.mean(x * x, axis=-1, keepdims=True)
    return (x * lax.rsqrt(ms + EPS)) * g


def _silu(x):
    return x / (1.0 + jnp.exp(-x))


def _moe_combine(m0_ref, m1_ref, route_ref):
    rt = route_ref[...]
    return rt[:, 2:3] * m0_ref[...] + rt[:, 3:4] * m1_ref[...]


def _proj_kernel(*refs, n_first, has_add, has_bias, rope_cols, loop_cols, tn):
    it = iter(refs)
    ya_ref = next(it)
    yb_ref = next(it) if n_first is not None else None
    if has_add:
        m0_ref, m1_ref, rt_ref = next(it), next(it), next(it)
    g_ref = next(it)
    w_ref = next(it)
    b_ref = next(it) if has_bias else None
    if rope_cols:
        c_ref, s1_ref, s2_ref = next(it), next(it), next(it)
    ysum_ref = next(it) if has_add else None
    z_ref = next(it)
    h_ref = next(it)
    i = pl.program_id(0)

    x = ya_ref[...]
    if n_first is not None:
        x = jnp.where(i < n_first, x, yb_ref[...])
    if has_add:
        x = x + _moe_combine(m0_ref, m1_ref, rt_ref)
        ysum_ref[...] = x
    h_ref[...] = _rms(x, g_ref[...]).astype(bf16)

    def rope(acc):
        width = acc.shape[1]
        rep = width // LANES
        c = jnp.tile(c_ref[...], (1, rep))
        s1 = jnp.tile(s1_ref[...], (1, rep))
        s2 = jnp.tile(s2_ref[...], (1, rep))
        half = ROT_DIM // 2
        lo = pltpu.roll(acc, half, axis=1)
        hi = pltpu.roll(acc, width - half, axis=1)
        return acc * c + lo * s1 + hi * s2

    def col_tile(j, carry):
        cols = pl.ds(pl.multiple_of(j * tn, tn), tn)
        acc = _dot(h_ref[...], w_ref[:, cols])
        if has_bias:
            acc = acc + b_ref[:, cols]
        if rope_cols:
            acc = rope(acc)
        z_ref[:, cols] = acc.astype(z_ref.dtype)
        return carry

    nc = z_ref.shape[1]
    lax.fori_loop(0, loop_cols // tn, col_tile, 0)
    if nc > loop_cols:
        acc = _dot(h_ref[...], w_ref[:, loop_cols:])
        if has_bias:
            acc = acc + b_ref[:, loop_cols:]
        rt = max(rope_cols - loop_cols, 0)
        if rt:
            z_ref[:, loop_cols:loop_cols + rt] = rope(acc[:, :rt]).astype(z_ref.dtype)
        z_ref[:, loop_cols + rt:] = acc[:, rt:].astype(z_ref.dtype)


def _proj(ya, g, w, *, tm, tn, loop_cols, yb=None, add=None, bias=None, rope=None, rope_cols=0):
    d = ya.shape[1]
    na = ya.shape[0] // tm
    nb = 0 if yb is None else yb.shape[0] // tm
    n = (na + nb) * tm
    nc = w.shape[1]
    assert loop_cols % tn == 0 and loop_cols <= nc and (rope_cols == 0 or rope_cols >= loop_cols)
    once = dict(pipeline_mode=pl.Buffered(1))
    in_specs = []
    args = []
    if yb is None:
        in_specs.append(pl.BlockSpec((tm, d), lambda i: (i, 0)))
        args.append(ya)
    else:
        in_specs.append(pl.BlockSpec((tm, d), lambda i: (jnp.minimum(i, na - 1), 0)))
        in_specs.append(pl.BlockSpec((tm, d), lambda i: (jnp.maximum(i - na, 0), 0)))
        args += [ya, yb]
    if add is not None:
        moe_out, route = add
        in_specs.append(pl.BlockSpec((tm, d), lambda i: (i, 0)))
        in_specs.append(pl.BlockSpec((tm, d), lambda i: (i + na, 0)))
        in_specs.append(pl.BlockSpec((tm, LANES), lambda i: (i, 0)))
        args += [moe_out, moe_out, route]
    in_specs.append(pl.BlockSpec((1, d), lambda i: (0, 0)))
    args.append(g.reshape(1, d))
    in_specs.append(pl.BlockSpec((d, nc), lambda i: (0, 0), **once))
    args.append(w)
    if bias is not None:
        in_specs.append(pl.BlockSpec((1, nc), lambda i: (0, 0)))
        args.append(bias.reshape(1, nc))
    if rope_cols:
        for t in rope:
            in_specs.append(pl.BlockSpec((tm, LANES), lambda i: (i, 0)))
            args.append(t)
    out_shape = []
    out_specs = []
    if add is not None:
        out_shape.append(jax.ShapeDtypeStruct((n, d), f32))
        out_specs.append(pl.BlockSpec((tm, d), lambda i: (i, 0)))
    out_shape.append(jax.ShapeDtypeStruct((n, nc), bf16))
    out_specs.append(pl.BlockSpec((tm, nc), lambda i: (i, 0)))
    kern = functools.partial(_proj_kernel, n_first=(na if yb is not None else None),
                             has_add=add is not None, has_bias=bias is not None, rope_cols=rope_cols,
                             loop_cols=loop_cols, tn=tn)
    out = pl.pallas_call(
        kern, grid=(na + nb,), in_specs=in_specs, out_specs=out_specs, out_shape=out_shape,
        scratch_shapes=[pltpu.VMEM((tm, d), bf16)],
        compiler_params=_cp(("arbitrary",)), name="proj")(*args)
    return out if add is not None else out[0]


def _gla_kernel(*refs, C, CB, G):
    seq_refs = [refs[5 * g:5 * g + 5] for g in range(G)]
    (wlr_ref, blr_ref, gn_ref, s0_ref, o_ref, sfin_ref,
     st_ref, qd_ref, oi_ref, ds_ref, dec_ref) = refs[5 * G:]
    s = pl.program_id(1)
    R = C * CB

    @pl.when(s == 0)
    def _():
        st_ref[...] = s0_ref[...]

    row = lax.broadcasted_iota(jnp.int32, (R, R), 0)
    col = lax.broadcasted_iota(jnp.int32, (R, R), 1)
    same = (row // C) == (col // C)
    tril = same & (row >= col)
    tri = jnp.where(tril, 1.0, 0.0).astype(bf16)
    ones = jnp.where(same, 1.0, 0.0).astype(bf16)
    even = lax.broadcasted_iota(jnp.int32, (R, LANES), 1) < DK_A
    srow = lax.broadcasted_iota(jnp.int32, (2 * DV_A, LANES), 0)
    slane = lax.broadcasted_iota(jnp.int32, (2 * DV_A, LANES), 1)
    blockdiag = (srow < DV_A) == (slane < DK_A)
    wlr = wlr_ref[...]
    blr = blr_ref[...]
    gn = gn_ref[...]
    pairs = H_A // 2

    for g in range(G):
        q_ref, k_ref, v_ref, _, la_ref = seq_refs[g]
        pre = _dot(la_ref[...], wlr) + blr
        loga = -(jnp.maximum(-pre, 0.0) + jnp.log1p(jnp.exp(-jnp.abs(pre)))) * (1.0 / GLA_TAU)
        hi = loga.astype(bf16)
        lo = (loga - hi.astype(f32)).astype(bf16)
        b = _dot(tri, hi) + _dot(tri, lo)
        b_end = _dot(ones, hi) + _dot(ones, lo)
        qf = q_ref[...].astype(f32) * (DK_A ** -0.5)
        kf = k_ref[...].astype(f32)
        q_dec = (qf * jnp.exp(b)).astype(bf16)
        k_inv = (kf * jnp.exp(-b)).astype(bf16)
        k_end = (kf * jnp.exp(b_end - b)).astype(bf16)
        dec = jnp.exp(b_end)
        qd_ref[g] = q_dec
        for c in range(CB):
            dec_ref[g, c] = dec[c * C:c * C + 1, :]
        for p in range(pairs):
            sl = slice(p * LANES, (p + 1) * LANES)
            vs = slice(p * 2 * DV_A, (p + 1) * 2 * DV_A)
            qp = q_dec[:, sl]
            kp = k_inv[:, sl]
            vp = v_ref[:, vs]
            zero = jnp.zeros_like(qp)
            att_e = jnp.where(tril, _dot_nt(jnp.where(even, qp, zero), kp), 0.0).astype(bf16)
            att_o = jnp.where(tril, _dot_nt(jnp.where(even, zero, qp), kp), 0.0).astype(bf16)
            oi_ref[g, :, vs] = jnp.concatenate([_dot(att_e, vp)[:, :DV_A], _dot(att_o, vp)[:, DV_A:]], axis=1)
            for c in range(CB):
                rows = slice(c * C, (c + 1) * C)
                ds = _dot_tn(vp[rows], k_end[rows, sl])
                ds_ref[g, c, p] = jnp.where(blockdiag, ds, 0.0)

    for c in range(CB):
        rows = slice(c * C, (c + 1) * C)
        for g in range(G):
            for p in range(pairs):
                sl = slice(p * LANES, (p + 1) * LANES)
                vs = slice(p * 2 * DV_A, (p + 1) * 2 * DV_A)
                st = st_ref[g, p]
                oi_ref[g, rows, vs] += _dot_nt(qd_ref[g, rows, sl], st.astype(bf16))
                st_ref[g, p] = dec_ref[g, c][:, sl] * st + ds_ref[g, c, p]

    for g in range(G):
        g_ref = seq_refs[g][3]
        for h in range(H_A):
            hs = slice(h * DV_A, (h + 1) * DV_A)
            o_ref[g, :, hs] = (_rms(oi_ref[g, :, hs], gn) * _silu(g_ref[:, hs].astype(f32))).astype(o_ref.dtype)

    @pl.when(s == pl.num_programs(1) - 1)
    def _():
        sfin_ref[...] = st_ref[...]


def _state_to_kernel_layout(s0):
    bsz = s0.shape[0]
    st = jnp.swapaxes(s0, -1, -2).reshape(bsz, H_A // 2, 2, DV_A, DK_A)
    z = jnp.zeros_like(st[:, :, 0])
    top = jnp.concatenate([st[:, :, 0], z], axis=-1)
    bot = jnp.concatenate([z, st[:, :, 1]], axis=-1)
    return jnp.concatenate([top, bot], axis=-2)


def _state_from_kernel_layout(st):
    bsz = st.shape[0]
    e = st[:, :, :DV_A, :DK_A]
    o = st[:, :, DV_A:, DK_A:]
    s = jnp.stack([e, o], axis=2).reshape(bsz, H_A, DV_A, DK_A)
    return jnp.swapaxes(s, -1, -2)


def _gla(z, w_lr, b_lr, g_norm, s0, *, bsz, t, row0, C, CB, G=2):
    rb = C * CB
    ns = t // rb
    r0 = row0 // rb
    assert bsz % G == 0

    def zspec(g, width, cb):
        return pl.BlockSpec((rb, width), lambda i, s: (r0 + (i * G + g) * ns + s, cb))

    hk = H_A * DK_A
    hv = H_A * DV_A
    st0 = _state_to_kernel_layout(s0.astype(f32))
    wl = jnp.zeros((LANES, hk), f32).at[:GLA_RANK].set(w_lr).astype(bf16)
    st_shape = (G, H_A // 2, 2 * DV_A, 2 * DK_A)
    seq_specs = []
    for g in range(G):
        seq_specs += [zspec(g, hk, _QA0 // hk), zspec(g, hk, _KA0 // hk), zspec(g, hv, _VA0 // hv),
                      zspec(g, hv, _GA0 // hv), zspec(g, LANES, _LA0 // LANES)]
    o, sfin = pl.pallas_call(
        functools.partial(_gla_kernel, C=C, CB=CB, G=G),
        grid=(bsz // G, ns),
        in_specs=seq_specs + [
            pl.BlockSpec((LANES, hk), lambda i, s: (0, 0)),
            pl.BlockSpec((1, hk), lambda i, s: (0, 0)),
            pl.BlockSpec((1, DV_A), lambda i, s: (0, 0)),
            pl.BlockSpec(st_shape, lambda i, s: (i, 0, 0, 0))],
        out_specs=[pl.BlockSpec((G, rb, hv), lambda i, s: (i, s, 0)),
                   pl.BlockSpec(st_shape, lambda i, s: (i, 0, 0, 0))],
        out_shape=[jax.ShapeDtypeStruct((bsz, t, hv), bf16),
                   jax.ShapeDtypeStruct((bsz,) + st_shape[1:], f32)],
        scratch_shapes=[pltpu.VMEM(st_shape, f32),
                        pltpu.VMEM((G, rb, hk), bf16),
                        pltpu.VMEM((G, rb, hv), f32),
                        pltpu.VMEM((G, CB) + st_shape[1:], f32),
                        pltpu.VMEM((G, CB, 1, hk), f32)],
        compiler_params=_cp(("arbitrary", "arbitrary")), name="gla")(
            *([z] * (5 * G)), wl, b_lr.reshape(1, hk).astype(f32), g_norm.reshape(1, DV_A).astype(f32), st0)
    return o.reshape(bsz * t, hv), _state_from_kernel_layout(sfin)


def _band_kernel(q_ref, ka_ref, kb_ref, kc_ref, va_ref, vb_ref, vc_ref, ba_ref, bb_ref, bc_ref,
                 o_ref, *, mask_first):
    qi = pl.program_id(1)
    for h in range(H_B):
        sl = slice(h * DH_B, (h + 1) * DH_B)
        qh = (q_ref[:, sl].astype(f32) * (DH_B ** -0.5)).astype(bf16)
        sa = _dot_nt(qh, ka_ref[:, sl].astype(bf16)) + ba_ref[h]
        sb = _dot_nt(qh, kb_ref[:, sl].astype(bf16)) + bb_ref[h]
        sc = _dot_nt(qh, kc_ref[:, sl].astype(bf16)) + bc_ref[h]
        if mask_first:
            sa = jnp.where(qi >= 2, sa, NEG_INF)
            sb = jnp.where(qi >= 1, sb, NEG_INF)
        m = jnp.maximum(jnp.maximum(jnp.max(sa, axis=-1, keepdims=True),
                                    jnp.max(sb, axis=-1, keepdims=True)),
                        jnp.max(sc, axis=-1, keepdims=True))
        ea = jnp.exp(sa - m)
        eb = jnp.exp(sb - m)
        ec = jnp.exp(sc - m)
        inv = 1.0 / (jnp.sum(ea, axis=-1, keepdims=True) + jnp.sum(eb, axis=-1, keepdims=True)
                     + jnp.sum(ec, axis=-1, keepdims=True))
        o = (_dot((ea * inv).astype(bf16), va_ref[:, sl].astype(bf16))
             + _dot((eb * inv).astype(bf16), vb_ref[:, sl].astype(bf16))
             + _dot((ec * inv).astype(bf16), vc_ref[:, sl].astype(bf16)))
        o_ref[:, sl] = o.astype(o_ref.dtype)


def _band_call(qkv_specs, args, biases, *, bsz, nq, qr, mask_first):
    hd = H_B * DH_B
    bias_specs = [pl.BlockSpec(b.shape, lambda b_, q_: (0, 0, 0)) for b in biases]
    return pl.pallas_call(
        functools.partial(_band_kernel, mask_first=mask_first),
        grid=(bsz, nq),
        in_specs=list(qkv_specs) + bias_specs,
        out_specs=pl.BlockSpec((qr, hd), lambda b, q: (b * nq + q, 0)),
        out_shape=jax.ShapeDtypeStruct((bsz * nq * qr, hd), bf16),
        compiler_params=_cp(("arbitrary", "arbitrary")), name="band")(*args, *biases)


def _rel_bias_matrix(tab, nq, nk, off):
    m = nq + nk
    j = np.arange(m)
    diff = np.where(j < nk, -j, m - j)
    idx = np.clip(diff + off, -REL_CLIP, REL_CLIP) + REL_CLIP
    u = tab.astype(f32)[:, jnp.asarray(idx)]
    h = tab.shape[0]
    return jnp.tile(u, (1, nq))[:, :nq * (m - 1)].reshape(h, nq, m - 1)[:, :, :nk]


def _band_bias_prompt(tab, qr, kb):
    tot = 2 * kb + qr
    qc = (np.arange(qr)[:, None] + 2 * kb) // CHUNK
    kc = np.arange(tot)[None, :] // CHUNK
    valid = (kc <= qc) & (kc >= qc - B_PREV)
    bias = jnp.where(jnp.asarray(valid)[None], _rel_bias_matrix(tab, qr, tot, 2 * kb), NEG_INF)
    return bias[:, :, :kb], bias[:, :, kb:2 * kb], bias[:, :, 2 * kb:]


def _band_bias_sample(tab, t, lc):
    bias = _rel_bias_matrix(tab, t, lc + t, lc)
    return bias[:, :, :lc // 2], bias[:, :, lc // 2:lc], bias[:, :, lc:]


def _swa_kernel(sink_ref, q_ref, kp_ref, kc_ref, vp_ref, vc_ref, o_ref, qs_ref, *, chunk_mask):
    qi = pl.program_id(1)
    qr = q_ref.shape[0]
    pr = kp_ref.shape[0]
    nl = G_C * LANES
    even = lax.broadcasted_iota(jnp.int32, (qr, LANES), 1) < DH_C
    top = lax.broadcasted_iota(jnp.int32, (LANES, LANES), 0) < DH_C
    if chunk_mask:
        kch = lax.broadcasted_iota(jnp.int32, (pr, nl), 0) // CHUNK
        qch = (lax.broadcasted_iota(jnp.int32, (pr, nl), 1) % LANES) // CHUNK
        ok_prev = (kch >= qch) & (qi >= 1)
        kch = lax.broadcasted_iota(jnp.int32, (qr, nl), 0) // CHUNK
        qch = (lax.broadcasted_iota(jnp.int32, (qr, nl), 1) % LANES) // CHUNK
        ok_cur = kch <= qch
    if qr < LANES:
        qs_ref[...] = jnp.zeros_like(qs_ref)

    def twice(ref, c):
        a = ref[:, c * DH_C:(c + 1) * DH_C].astype(bf16)
        return jnp.concatenate([a, a], axis=1)

    for c in range(KV_C):
        for p in range(G_C // 2):
            cols = slice(c * G_C * DH_C + p * LANES, c * G_C * DH_C + (p + 1) * LANES)
            qp = (q_ref[:, cols].astype(f32) * (DH_C ** -0.5)).astype(bf16)
            zero = jnp.zeros_like(qp)
            qs_ref[2 * p * LANES:2 * p * LANES + qr, :] = jnp.where(even, qp, zero)
            qs_ref[(2 * p + 1) * LANES:(2 * p + 1) * LANES + qr, :] = jnp.where(even, zero, qp)
        q_all = qs_ref[...]
        sink = jnp.concatenate(
            [jnp.full((1, LANES), sink_ref[c * G_C + j], f32) for j in range(G_C)], axis=1)
        st_prev = _dot_nt(twice(kp_ref, c), q_all)
        st_cur = _dot_nt(twice(kc_ref, c), q_all)
        if chunk_mask:
            st_prev = jnp.where(ok_prev, st_prev, NEG_INF)
            st_cur = jnp.where(ok_cur, st_cur, NEG_INF)
        m = jnp.maximum(jnp.maximum(jnp.max(st_prev, axis=0, keepdims=True),
                                    jnp.max(st_cur, axis=0, keepdims=True)), sink)
        e_prev = jnp.exp(st_prev - m)
        e_cur = jnp.exp(st_cur - m)
        inv = 1.0 / (jnp.sum(e_prev, axis=0, keepdims=True) + jnp.sum(e_cur, axis=0, keepdims=True)
                     + jnp.exp(sink - m))
        ot = (_dot_tn(twice(vp_ref, c), (e_prev * inv).astype(bf16))
              + _dot_tn(twice(vc_ref, c), (e_cur * inv).astype(bf16)))
        for p in range(G_C // 2):
            cols = slice(c * G_C * DH_C + p * LANES, c * G_C * DH_C + (p + 1) * LANES)
            blk = jnp.where(top, ot[:, 2 * p * LANES:(2 * p + 1) * LANES],
                            ot[:, (2 * p + 1) * LANES:(2 * p + 2) * LANES])
            o_ref[:, cols] = blk.T[:qr].astype(o_ref.dtype)


def _swa_call(specs, args, sinks, *, bsz, nq, qr, chunk_mask):
    hd = H_C * DH_C
    return pl.pallas_call(
        functools.partial(_swa_kernel, chunk_mask=chunk_mask),
        grid=(bsz, nq),
        in_specs=[pl.BlockSpec(memory_space=pltpu.SMEM)] + list(specs),
        out_specs=pl.BlockSpec((qr, hd), lambda b, q: (b * nq + q, 0)),
        out_shape=jax.ShapeDtypeStruct((bsz * nq * qr, hd), bf16),
        scratch_shapes=[pltpu.VMEM((G_C * LANES, LANES), bf16)],
        compiler_params=_cp(("arbitrary", "arbitrary")), name="swa")(sinks, *args)


def _out_kernel(*refs, n_parts, n_first, split_y):
    ny = 2 if split_y else 1
    y_refs = refs[:ny]
    refs = refs[ny - 1:]
    o_refs = refs[1:1 + 2 * n_parts]
    w_refs = refs[1 + 2 * n_parts:1 + 3 * n_parts]
    g_ref, rh_ref, rl_ref = refs[1 + 3 * n_parts:4 + 3 * n_parts]
    y1_ref, h2_ref, route_ref, cnt_ref, run_ref = refs[4 + 3 * n_parts:]
    i = pl.program_id(0)

    @pl.when(i == 0)
    def _():
        run_ref[...] = jnp.zeros_like(run_ref)

    acc = y_refs[0][...]
    if split_y:
        acc = jnp.where(i < n_first, acc, y_refs[1][...])
    for k in range(n_parts):
        lhs = jnp.where(i < n_first, o_refs[2 * k][...], o_refs[2 * k + 1][...])
        acc = acc + _dot(lhs, w_refs[k][...])
    y1_ref[...] = acc
    h2 = _rms(acc, g_ref[...])
    hi = h2.astype(bf16)
    h2_ref[...] = hi.astype(f32)
    lo = (h2 - hi.astype(f32)).astype(bf16)
    lg = _dot(hi, rh_ref[...]) + _dot(hi, rl_ref[...]) + _dot(lo, rh_ref[...])

    lane_i = lax.broadcasted_iota(jnp.int32, lg.shape, 1)
    lane = lane_i.astype(f32)
    big = float(LANES)
    ninf = -jnp.inf
    is_g = lane_i < N_GROUPS
    gmax = jnp.max(jnp.where(is_g, lg, ninf), axis=-1, keepdims=True)
    gsel = jnp.min(jnp.where(is_g & (lg == gmax), lane, big), axis=-1, keepdims=True)
    pg = 1.0 / jnp.sum(jnp.where(is_g, jnp.exp(lg - gmax), 0.0), axis=-1, keepdims=True)
    e0 = N_GROUPS + EXP_PER_GROUP * gsel
    is_e = (lane >= e0) & (lane < e0 + EXP_PER_GROUP)
    v1 = jnp.max(jnp.where(is_e, lg, ninf), axis=-1, keepdims=True)
    i1 = jnp.min(jnp.where(is_e & (lg == v1), lane, big), axis=-1, keepdims=True)
    is_e2 = is_e & (lane != i1)
    v2 = jnp.max(jnp.where(is_e2, lg, ninf), axis=-1, keepdims=True)
    i2 = jnp.min(jnp.where(is_e2 & (lg == v2), lane, big), axis=-1, keepdims=True)
    t = jnp.exp(v2 - v1)
    w1 = pg / (1.0 + t)
    w2 = pg * t / (1.0 + t)

    tmr = lg.shape[0]
    oh1 = jnp.where(lane == i1, 1.0, 0.0)
    oh2 = jnp.where(lane == i2, 1.0, 0.0)
    before = jnp.where(lax.broadcasted_iota(jnp.int32, (tmr, tmr), 0)
                       > lax.broadcasted_iota(jnp.int32, (tmr, tmr), 1), 1.0, 0.0).astype(bf16)
    tot1 = jnp.sum(oh1, axis=0, keepdims=True)
    tot2 = jnp.sum(oh2, axis=0, keepdims=True)
    base = run_ref[...]
    r1 = jnp.sum(oh1 * (base + _dot(before, oh1.astype(bf16))), axis=-1, keepdims=True)
    r2 = jnp.sum(oh2 * (base + tot1 + _dot(before, oh2.astype(bf16))), axis=-1, keepdims=True)
    run_ref[...] = base + tot1 + tot2
    cnt_ref[...] = base + tot1 + tot2
    vals = (i1 - N_GROUPS, i2 - N_GROUPS, w1, w2, r1, r2)
    out = jnp.zeros_like(lg)
    for k, v in enumerate(vals):
        out = jnp.where(lane_i == k, v, out)
    route_ref[...] = out


def _out_proj(y, parts, g, w_router, *, tm):
    n_first = parts[0][0].shape[0] // tm
    split_y = isinstance(y, tuple)
    if split_y:
        d = y[0].shape[1]
        n = y[0].shape[0] + y[1].shape[0]
        in_specs = [pl.BlockSpec((tm, d), lambda i: (jnp.minimum(i, n_first - 1), 0)),
                    pl.BlockSpec((tm, d), lambda i: (jnp.maximum(i - n_first, 0), 0))]
        args = list(y)
    else:
        n, d = y.shape
        in_specs = [pl.BlockSpec((tm, d), lambda i: (i, 0))]
        args = [y]
    n_tiles = n // tm
    for op, os_, _ in parts:
        dk = op.shape[1]
        in_specs.append(pl.BlockSpec((tm, dk), lambda i: (jnp.minimum(i, n_first - 1), 0)))
        in_specs.append(pl.BlockSpec((tm, dk), lambda i: (jnp.maximum(i - n_first, 0), 0)))
        args += [op, os_]
    for _, _, w in parts:
        in_specs.append(pl.BlockSpec(w.shape, lambda i: (0, 0)))
        args.append(w)
    rh = w_router.astype(bf16)
    rl = (w_router - rh.astype(f32)).astype(bf16)
    in_specs += [pl.BlockSpec((1, d), lambda i: (0, 0)),
                 pl.BlockSpec((d, LANES), lambda i: (0, 0)),
                 pl.BlockSpec((d, LANES), lambda i: (0, 0))]
    args += [g.reshape(1, d), rh, rl]
    return pl.pallas_call(
        functools.partial(_out_kernel, n_parts=len(parts), n_first=n_first, split_y=split_y),
        grid=(n_tiles,), in_specs=in_specs,
        out_specs=[pl.BlockSpec((tm, d), lambda i: (i, 0)),
                   pl.BlockSpec((tm, d), lambda i: (i, 0)),
                   pl.BlockSpec((tm, LANES), lambda i: (i, 0)),
                   pl.BlockSpec((1, LANES), lambda i: (0, 0))],
        out_shape=[jax.ShapeDtypeStruct((n, d), f32),
                   jax.ShapeDtypeStruct((n, d), f32),
                   jax.ShapeDtypeStruct((n, LANES), f32),
                   jax.ShapeDtypeStruct((1, LANES), f32)],
        scratch_shapes=[pltpu.VMEM((1, LANES), f32)],
        compiler_params=_cp(("arbitrary",)), name="out_proj")(*args)


def _moe_kernel(te_ref, tv_ref, pos_ref, plo_ref, phi_ref, en_ref, es_ref,
                h_ref, wg_ref, wu_ref, wd_ref, out_ref,
                slot_ref, x0, x1, x2, y0, y1, y2, gsem, ssem, wg_f, wu_f, wd_f, wsem, wg_s, wu_s, wd_s,
                *, tm, n):
    t = pl.program_id(0)
    nt = pl.num_programs(0)
    n2 = 2 * n
    xs = (x0, x1, x2)
    ys = (y0, y1, y2)
    depth = len(xs)

    def weights_start(e, slot):
        for src, dst in ((wg_ref, wg_f), (wu_ref, wu_f), (wd_ref, wd_f)):
            pltpu.make_async_copy(src.at[e], dst.at[slot], wsem.at[slot]).start()

    def weights_wait(slot):
        for src, dst in ((wg_ref, wg_f), (wu_ref, wu_f), (wd_ref, wd_f)):
            pltpu.make_async_copy(src.at[0], dst.at[slot], wsem.at[slot]).wait()

    def tile_valid(tile):
        return (tile >= 0) & (tile < nt) & (tv_ref[jnp.clip(tile, 0, nt - 1)] != 0)

    valid = tile_valid(t)

    def gather_row(tile, r, p):
        s = slot_ref[tile * tm + r]
        tok = jnp.where(s < n, s, jnp.where(s < n2, s - n, s - n2))
        pltpu.make_async_copy(h_ref.at[pl.ds(tok, 1)], xs[p].at[pl.ds(r, 1)], gsem.at[p]).start()

    def scatter_row(tile, r, p):
        s = slot_ref[tile * tm + r]
        pltpu.make_async_copy(ys[p].at[pl.ds(r, 1)], out_ref.at[pl.ds(s, 1)], ssem.at[p]).start()

    def rows_loop(row_fn, tile, p):
        def body(r, carry):
            row_fn(tile, r, p)
            return carry
        lax.fori_loop(0, tm, body, 0, unroll=8)

    def gather_wait(p):
        pltpu.make_async_copy(h_ref.at[pl.ds(0, tm)], xs[p], gsem.at[p]).wait()

    def scatter_wait(p):
        pltpu.make_async_copy(ys[p], out_ref.at[pl.ds(0, tm)], ssem.at[p]).wait()

    def compute(p):
        x = xs[p][...].astype(bf16)
        g = _dot(x, wg_s[...])
        u = _dot(x, wu_s[...])
        ys[p][...] = _dot((_silu(g) * u).astype(bf16), wd_s[...])

    @pl.when(t == 0)
    def _():
        def per_expert(e, nxt):
            def body(r, v):
                slot_ref[r] = v
                return v + 1
            return lax.fori_loop(plo_ref[e], phi_ref[e], body, nxt)
        lax.fori_loop(0, N_EXPERTS, per_expert, n2)

        def body(s, carry):
            slot_ref[pos_ref[s]] = s
            return carry
        lax.fori_loop(0, n2, body, 0, unroll=8)
        weights_start(te_ref[0], 0)
        rows_loop(gather_row, 0, 0)

        @pl.when(tile_valid(1))
        def _():
            rows_loop(gather_row, 1, 1)

    def step(p):
        q = (p + 2) % depth

        @pl.when(t >= depth)
        def _():
            scatter_wait(p)

        @pl.when(valid)
        def _():
            gather_wait(p)

            @pl.when((t == 0) | (te_ref[t] != te_ref[jnp.maximum(t - 1, 0)]))
            def _():
                slot = es_ref[t]
                weights_wait(slot)
                wg_s[...] = wg_f[slot].astype(bf16)
                wu_s[...] = wu_f[slot].astype(bf16)
                wd_s[...] = wd_f[slot].astype(bf16)

                @pl.when(en_ref[t] >= 0)
                def _():
                    weights_start(en_ref[t], 1 - slot)

            steady = tile_valid(t + 2) & (t >= 1)

            @pl.when(steady)
            def _():
                for r in range(tm):
                    gather_row(t + 2, r, q)
                    scatter_row(t - 1, r, q)
                compute(p)

            @pl.when(jnp.logical_not(steady))
            def _():
                @pl.when(tile_valid(t + 2))
                def _():
                    rows_loop(gather_row, t + 2, q)

                @pl.when(t >= 1)
                def _():
                    rows_loop(scatter_row, t - 1, q)
                compute(p)

        @pl.when(jnp.logical_not(valid))
        def _():
            @pl.when(tile_valid(t - 1))
            def _():
                rows_loop(scatter_row, t - 1, q)
            ys[p][...] = jnp.zeros(ys[p].shape, f32)
            pltpu.make_async_copy(ys[p], out_ref.at[pl.ds(pl.multiple_of(t * tm, tm), tm)], ssem.at[p]).start()

        @pl.when(t == nt - 1)
        def _():
            @pl.when(valid)
            def _():
                rows_loop(scatter_row, t, p)
            for k in range(depth):
                scatter_wait(k)

    for p in range(depth):
        pl.when(t % depth == p)(functools.partial(step, p))


def _moe_experts(h2, pos, tile_e, tile_v, pad_lo, pad_hi, wg, wu, wd, *, layer, tm, n_tiles):
    n, d = h2.shape
    ff = wg.shape[2]
    assert n >= N_EXPERTS * tm, "padding rows read tokens 0 .. N_EXPERTS * (tm - 1)"
    assert n_tiles >= 3
    first = jnp.concatenate([jnp.ones((1,), jnp.int32), (tile_e[1:] != tile_e[:-1]).astype(jnp.int32)])
    seq = jnp.cumsum(first * tile_v) - 1
    seq = jnp.where(tile_v != 0, seq, n_tiles)
    nxt_pos = jnp.sum((seq[None, :] <= seq[:, None]).astype(jnp.int32), axis=1)
    has_next = (nxt_pos < n_tiles) & (tile_v[jnp.minimum(nxt_pos, n_tiles - 1)] != 0) & (tile_v != 0)
    tile_e = tile_e + layer * N_EXPERTS
    e_next = jnp.where(has_next, tile_e[jnp.minimum(nxt_pos, n_tiles - 1)], -1).astype(jnp.int32)
    e_slot = (seq % 2).astype(jnp.int32)
    row_buf = pltpu.VMEM((tm, d), f32)
    any_spec = pl.BlockSpec(memory_space=pl.ANY)
    grid_spec = pltpu.PrefetchScalarGridSpec(
        num_scalar_prefetch=7, grid=(n_tiles,),
        in_specs=[any_spec, any_spec, any_spec, any_spec],
        out_specs=any_spec,
        scratch_shapes=[pltpu.SMEM((n_tiles * tm,), jnp.int32),
                        row_buf, row_buf, row_buf, row_buf, row_buf, row_buf,
                        pltpu.SemaphoreType.DMA((3,)), pltpu.SemaphoreType.DMA((3,)),
                        pltpu.VMEM((2, d, ff), f32), pltpu.VMEM((2, d, ff), f32), pltpu.VMEM((2, ff, d), f32),
                        pltpu.SemaphoreType.DMA((2,)),
                        pltpu.VMEM((d, ff), bf16), pltpu.VMEM((d, ff), bf16), pltpu.VMEM((ff, d), bf16)])
    return pl.pallas_call(
        functools.partial(_moe_kernel, tm=tm, n=n), grid_spec=grid_spec,
        out_shape=jax.ShapeDtypeStruct((n_tiles * tm, d), f32),
        compiler_params=_cp(("arbitrary",)), name="moe")(
            tile_e, tile_v, pos, pad_lo, pad_hi, e_next, e_slot, h2, wg, wu, wd)


def _moe(h2, route, counts, wg, wu, wd, *, layer, tm):
    n = route.shape[0]
    n_tiles = (2 * n) // tm + N_EXPERTS
    counts = counts[0, N_GROUPS:N_GROUPS + N_EXPERTS].astype(jnp.int32)
    padded = ((counts + tm - 1) // tm) * tm
    ends = jnp.cumsum(padded)
    starts = ends - padded
    e = route[:, 0:2].astype(jnp.int32)
    rank = route[:, 4:6].astype(jnp.int32)
    pos = (starts[e] + rank).T.reshape(-1)
    tile_start = jnp.arange(n_tiles, dtype=jnp.int32) * tm
    tile_e = jnp.sum((tile_start[:, None] >= ends[None, :]).astype(jnp.int32), axis=1)
    tile_v = (tile_start < ends[-1]).astype(jnp.int32)
    tile_e = jnp.minimum(tile_e, N_EXPERTS - 1)
    return _moe_experts(h2, pos, tile_e, tile_v, starts + counts, ends, wg, wu, wd, layer=layer, tm=tm,
                        n_tiles=n_tiles)


def _final_kernel(y_ref, m0_ref, m1_ref, rt_ref, g_ref, o_ref):
    o_ref[...] = _rms(y_ref[...] + _moe_combine(m0_ref, m1_ref, rt_ref), g_ref[...])


def _final(y, moe_out, route, g, *, tm, row0, rows):
    n, d = y.shape
    r0 = row0 // tm
    r1 = (n + row0) // tm
    return pl.pallas_call(
        _final_kernel, grid=(rows // tm,),
        in_specs=[pl.BlockSpec((tm, d), lambda i: (r0 + i, 0)),
                  pl.BlockSpec((tm, d), lambda i: (r0 + i, 0)),
                  pl.BlockSpec((tm, d), lambda i: (r1 + i, 0)),
                  pl.BlockSpec((tm, LANES), lambda i: (r0 + i, 0)),
                  pl.BlockSpec((1, d), lambda i: (0, 0))],
        out_specs=pl.BlockSpec((tm, d), lambda i: (i, 0)),
        out_shape=jax.ShapeDtypeStruct((rows, d), f32),
        compiler_params=_cp(("arbitrary",)), name="final_norm")(y, moe_out, moe_out, route, g.reshape(1, d))


def _rope_tables(pos):
    half = ROT_DIM // 2
    inv = jnp.power(ROPE_THETA, -jnp.arange(half, dtype=f32) * 2.0 / ROT_DIM)
    ang = pos.astype(f32)[:, None] * inv[None, :]
    cos, sin = lax.optimization_barrier((jnp.cos(ang), jnp.sin(ang)))
    n = pos.shape[0]
    one = jnp.ones((n, DH_C - ROT_DIM), f32)
    zero = jnp.zeros((n, DH_C - ROT_DIM), f32)
    zh = jnp.zeros((n, half), f32)
    c = jnp.concatenate([cos, cos, one], axis=1)
    s1 = jnp.concatenate([zh, sin, zero], axis=1)
    s2 = jnp.concatenate([-sin, zh, zero], axis=1)
    rep = LANES // DH_C
    return tuple(jnp.tile(t, (1, rep)) for t in (c, s1, s2))


def kernel(x_prompt, x_sample, cache_gla_state, cache_band_k, cache_band_v, cache_swa_k, cache_swa_v,
           norm_mix, norm_ffn, norm_final, w_in_ab, w_gla_lr, b_gla_lr, gla_out_norm, rel_bias_tab,
           w_out_ab, w_in_c, b_in_c, attn_sinks, w_out_c, w_router_grp, w_router_exp, w_exp_gate,
           w_exp_up, w_exp_down):
    bp, sp, d = x_prompt.shape
    bs, ts, _ = x_sample.shape
    n_p = bp * sp
    n_s = bs * ts
    n = n_p + n_s
    tm = 512 if n_s % 512 == 0 else n_s
    tmo = min(tm, 256)
    tme = 256 if (2 * n) % 256 == 0 else 64
    ff = w_exp_gate.shape[-1]

    xp = x_prompt.reshape(n_p, d)
    xs_ = x_sample.reshape(n_s, d)

    def router_w(l):
        wr = jnp.concatenate([w_router_grp[l], w_router_exp[l].reshape(d, N_EXPERTS)], axis=1)
        return jnp.zeros((d, LANES), f32).at[:, :N_GROUPS + N_EXPERTS].set(wr)

    experts = (w_exp_gate.reshape(-1, d, ff), w_exp_up.reshape(-1, d, ff), w_exp_down.reshape(-1, ff, d))

    w0 = w_in_ab[0]
    la0 = 2 * H_A * DK_A + 2 * H_A * DV_A
    w_cat = jnp.concatenate(
        [w0[:, :la0], w0[:, la0 + GLA_RANK:], w0[:, la0:la0 + GLA_RANK],
         jnp.zeros((d, LANES - GLA_RANK), f32)], axis=1).astype(bf16)
    z = _proj(xp, norm_mix[0], w_cat, tm=tmo, tn=1024, loop_cols=_LA0, yb=xs_)

    c_s = min(CHUNK, ts)
    oa_p, gla_p = _gla(z, w_gla_lr[0], b_gla_lr[0], gla_out_norm[0],
                       jnp.zeros((bp, H_A, DK_A, DV_A), f32), bsz=bp, t=sp, row0=0, C=CHUNK, CB=4)
    oa_s, gla_s = _gla(z, w_gla_lr[0], b_gla_lr[0], gla_out_norm[0], cache_gla_state[0],
                       bsz=bs, t=ts, row0=n_p, C=c_s, CB=ts // c_s)

    hd = H_B * DH_B
    qr = 256
    nq = sp // qr
    qcb, kcb, vcb = _QB0 // hd, _KB0 // hd, _VB0 // hd
    cur = lambda cb: pl.BlockSpec((qr, hd), lambda b, q: (b * nq + q, cb))
    prev = lambda cb, back: pl.BlockSpec((qr, hd), lambda b, q: (b * nq + jnp.maximum(q - back, 0), cb))
    ob_p = _band_call(
        [cur(qcb), prev(kcb, 2), prev(kcb, 1), cur(kcb), prev(vcb, 2), prev(vcb, 1), cur(vcb)],
        [z] * 7, _band_bias_prompt(rel_bias_tab[0], qr, qr), bsz=bp, nq=nq, qr=qr, mask_first=True)

    lb = cache_band_k.shape[2]
    ck = cache_band_k[0].reshape(bs * lb, hd)
    cv = cache_band_v[0].reshape(bs * lb, hd)
    rs0 = n_p // ts
    new = lambda cb: pl.BlockSpec((ts, hd), lambda b, q: (rs0 + b, cb))
    cpiece = lambda k: pl.BlockSpec((lb // 2, hd), lambda b, q: (2 * b + k, 0))
    ob_s = _band_call(
        [new(qcb), cpiece(0), cpiece(1), new(kcb), cpiece(0), cpiece(1), new(vcb)],
        [z, ck, ck, z, cv, cv, z], _band_bias_sample(rel_bias_tab[0], ts, lb),
        bsz=bs, nq=1, qr=ts, mask_first=False)

    dmix_a = H_A * DV_A
    wo = w_out_ab[0].astype(bf16)
    y1, h2, route, cnt = _out_proj(
        (xp, xs_), [(oa_p, oa_s, wo[:dmix_a]), (ob_p, ob_s, wo[dmix_a:])], norm_ffn[0], router_w(0),
        tm=tmo)
    moe0 = _moe(h2, route, cnt, *experts, layer=0, tm=tme)

    keep = min(B_PREV * CHUNK, sp)

    def prompt_tail(a, rows, c0, width):
        return jnp.stack([a[(b + 1) * sp - rows:(b + 1) * sp, c0:c0 + width] for b in range(bp)])

    band_k_p = prompt_tail(z, keep, _KB0, hd).astype(f32).reshape(bp, keep, H_B, DH_B)[None]
    band_v_p = prompt_tail(z, keep, _VB0, hd).astype(f32).reshape(bp, keep, H_B, DH_B)[None]
    kb_s = z[n_p:, _KB0:_KB0 + hd].astype(f32).reshape(bs, ts, H_B, DH_B)
    vb_s = z[n_p:, _VB0:_VB0 + hd].astype(f32).reshape(bs, ts, H_B, DH_B)
    band_k_s = jnp.concatenate([cache_band_k[0][:, ts:], kb_s], axis=1)[None]
    band_v_s = jnp.concatenate([cache_band_v[0][:, ts:], vb_s], axis=1)[None]

    nq_c = H_C * DH_C
    nkv = KV_C * DH_C
    uniq = _rope_tables(jnp.concatenate([jnp.arange(sp), PAST_LEN + jnp.arange(ts)]))
    rope = tuple(jnp.concatenate([jnp.tile(u[:sp], (bp, 1)), jnp.tile(u[sp:], (bs, 1))]) for u in uniq)
    y2, z1 = _proj(y1, norm_mix[1], w_in_c[0].astype(bf16), tm=tmo, tn=512, loop_cols=nq_c,
                   add=(moe0, route), bias=b_in_c[0], rope=rope, rope_cols=nq_c + nkv)

    qr_c = WINDOW
    nq1 = sp // qr_c
    kw = nkv
    kcb1, vcb1 = _KC0 // kw, _VC0 // kw
    qspec = pl.BlockSpec((qr_c, nq_c), lambda b, q: (b * nq1 + q, 0))
    cur1 = lambda cb: pl.BlockSpec((qr_c, kw), lambda b, q: (b * nq1 + q, cb))
    prev1 = lambda cb: pl.BlockSpec((qr_c, kw), lambda b, q: (b * nq1 + jnp.maximum(q - 1, 0), cb))
    sinks = attn_sinks[0].astype(f32)
    oc_p = _swa_call([qspec, prev1(kcb1), cur1(kcb1), prev1(vcb1), cur1(vcb1)], [z1] * 5, sinks,
                     bsz=bp, nq=nq1, qr=qr_c, chunk_mask=True)

    lc = cache_swa_k.shape[2]
    csk = cache_swa_k[0].reshape(bs * lc, nkv)
    csv = cache_swa_v[0].reshape(bs * lc, nkv)
    qspec_s = pl.BlockSpec((ts, nq_c), lambda b, q: (rs0 + b, 0))
    new1 = lambda cb: pl.BlockSpec((ts, kw), lambda b, q: (rs0 + b, cb))
    cspec = pl.BlockSpec((lc, kw), lambda b, q: (b, 0))
    oc_s = _swa_call([qspec_s, cspec, new1(kcb1), cspec, new1(vcb1)], [z1, csk, z1, csv, z1], sinks,
                     bsz=bs, nq=1, qr=ts, chunk_mask=False)

    y3, h4, route1, cnt1 = _out_proj(y2, [(oc_p, oc_s, w_out_c[0].astype(bf16))], norm_ffn[1],
                                     router_w(1), tm=tmo)
    moe1 = _moe(h4, route1, cnt1, *experts, layer=1, tm=tme)

    keep1 = min(WINDOW, sp)

    swa_k_p = prompt_tail(z1, keep1, _KC0, kw).astype(f32).reshape(bp, keep1, KV_C, DH_C)[None]
    swa_v_p = prompt_tail(z1, keep1, _VC0, kw).astype(f32).reshape(bp, keep1, KV_C, DH_C)[None]
    k1_s = z1[n_p:, _KC0:_KC0 + kw].astype(f32).reshape(bs, ts, KV_C, DH_C)
    v1_s = z1[n_p:, _VC0:_VC0 + kw].astype(f32).reshape(bs, ts, KV_C, DH_C)
    swa_k_s = jnp.concatenate([cache_swa_k[0][:, ts:], k1_s], axis=1)[None]
    swa_v_s = jnp.concatenate([cache_swa_v[0][:, ts:], v1_s], axis=1)[None]

    y_prompt = _final(y3, moe1, route1, norm_final, tm=tm, row0=0, rows=n_p).reshape(bp, sp, d)
    y_sample = _final(y3, moe1, route1, norm_final, tm=tm, row0=n_p, rows=n_s).reshape(bs, ts, d)

    return (y_prompt, y_sample, gla_p[None], gla_s[None].astype(cache_gla_state.dtype),
            band_k_p, band_v_p, band_k_s, band_v_s, swa_k_p, swa_v_p, swa_k_s, swa_v_s)
```

```python
import functools

import numpy as np
import jax
import jax.numpy as jnp
from jax import lax
from jax.experimental import pallas as pl
from jax.experimental.pallas import tpu as pltpu

f32 = jnp.float32
bf16 = jnp.bfloat16

CHUNK = 64
EPS = 1e-6
NEG_INF = -1e30
PAST_LEN = 1024
H_A = 8
DK_A = 64
DV_A = 128
GLA_RANK = 16
GLA_TAU = 16.0
H_B = 8
DH_B = 128
B_PREV = 8
REL_CLIP = 128
H_C = 32
KV_C = 4
G_C = H_C // KV_C
DH_C = 64
WINDOW = 128
ROT_DIM = DH_C // 4
ROPE_THETA = 500000.0
N_GROUPS = 4
EXP_PER_GROUP = 8
N_EXPERTS = N_GROUPS * EXP_PER_GROUP

LANES = 128
VMEM_LIMIT = 52 * 1024 * 1024

_QA0, _KA0, _VA0, _GA0 = 0, 512, 1024, 2048
_QB0, _KB0, _VB0, _LA0 = 3072, 4096, 5120, 6144
_NC_AB = 6272
_QC0, _KC0, _VC0 = 0, 2048, 2304
_NC_C = 2560


def _cp(sem):
    return pltpu.CompilerParams(dimension_semantics=sem, vmem_limit_bytes=VMEM_LIMIT)


def _dot(a, b):
    return jnp.dot(a, b, preferred_element_type=f32)


def _dot_nt(a, b):
    return lax.dot_general(a, b, (((1,), (1,)), ((), ())), preferred_element_type=f32)


def _dot_tn(a, b):
    return lax.dot_general(a, b, (((0,), (0,)), ((), ())), preferred_element_type=f32)


def _rms(x, g):
    ms = jnp.mean(x * x, axis=-1, keepdims=True)
    return (x * lax.rsqrt(ms + EPS)) * g


def _silu(x):
    return x / (1.0 + jnp.exp(-x))


def _moe_combine(m0_ref, m1_ref, route_ref):
    rt = route_ref[...]
    return rt[:, 2:3] * m0_ref[...] + rt[:, 3:4] * m1_ref[...]


def _proj_kernel(*refs, n_first, has_add, has_bias, rope_cols, loop_cols, tn):
    it = iter(refs)
    ya_ref = next(it)
    yb_ref = next(it) if n_first is not None else None
    if has_add:
        m0_ref, m1_ref, rt_ref = next(it), next(it), next(it)
    g_ref = next(it)
    w_ref = next(it)
    b_ref = next(it) if has_bias else None
    if rope_cols:
        c_ref, s1_ref, s2_ref = next(it), next(it), next(it)
    ysum_ref = next(it) if has_add else None
    z_ref = next(it)
    h_ref = next(it)
    i = pl.program_id(0)

    x = ya_ref[...]
    if n_first is not None:
        x = jnp.where(i < n_first, x, yb_ref[...])
    if has_add:
        x = x + _moe_combine(m0_ref, m1_ref, rt_ref)
        ysum_ref[...] = x
    h_ref[...] = _rms(x, g_ref[...]).astype(bf16)

    def rope(acc):
        width = acc.shape[1]
        rep = width // LANES
        c = jnp.tile(c_ref[...], (1, rep))
        s1 = jnp.tile(s1_ref[...], (1, rep))
        s2 = jnp.tile(s2_ref[...], (1, rep))
        half = ROT_DIM // 2
        lo = pltpu.roll(acc, half, axis=1)
        hi = pltpu.roll(acc, width - half, axis=1)
        return acc * c + lo * s1 + hi * s2

    def col_tile(j, carry):
        cols = pl.ds(pl.multiple_of(j * tn, tn), tn)
        acc = _dot(h_ref[...], w_ref[:, cols])
        if has_bias:
            acc = acc + b_ref[:, cols]
        if rope_cols:
            acc = rope(acc)
        z_ref[:, cols] = acc.astype(z_ref.dtype)
        return carry

    nc = z_ref.shape[1]
    lax.fori_loop(0, loop_cols // tn, col_tile, 0)
    if nc > loop_cols:
        acc = _dot(h_ref[...], w_ref[:, loop_cols:])
        if has_bias:
            acc = acc + b_ref[:, loop_cols:]
        rt = max(rope_cols - loop_cols, 0)
        if rt:
            z_ref[:, loop_cols:loop_cols + rt] = rope(acc[:, :rt]).astype(z_ref.dtype)
        z_ref[:, loop_cols + rt:] = acc[:, rt:].astype(z_ref.dtype)


def _proj(ya, g, w, *, tm, tn, loop_cols, yb=None, add=None, bias=None, rope=None, rope_cols=0):
    d = ya.shape[1]
    na = ya.shape[0] // tm
    nb = 0 if yb is None else yb.shape[0] // tm
    n = (na + nb) * tm
    nc = w.shape[1]
    assert loop_cols % tn == 0 and loop_cols <= nc and (rope_cols == 0 or rope_cols >= loop_cols)
    once = dict(pipeline_mode=pl.Buffered(1))
    in_specs = []
    args = []
    if yb is None:
        in_specs.append(pl.BlockSpec((tm, d), lambda i: (i, 0)))
        args.append(ya)
    else:
        in_specs.append(pl.BlockSpec((tm, d), lambda i: (jnp.minimum(i, na - 1), 0)))
        in_specs.append(pl.BlockSpec((tm, d), lambda i: (jnp.maximum(i - na, 0), 0)))
        args += [ya, yb]
    if add is not None:
        moe_out, route = add
        in_specs.append(pl.BlockSpec((tm, d), lambda i: (i, 0)))
        in_specs.append(pl.BlockSpec((tm, d), lambda i: (i + na, 0)))
        in_specs.append(pl.BlockSpec((tm, LANES), lambda i: (i, 0)))
        args += [moe_out, moe_out, route]
    in_specs.append(pl.BlockSpec((1, d), lambda i: (0, 0)))
    args.append(g.reshape(1, d))
    in_specs.append(pl.BlockSpec((d, nc), lambda i: (0, 0), **once))
    args.append(w)
    if bias is not None:
        in_specs.append(pl.BlockSpec((1, nc), lambda i: (0, 0)))
        args.append(bias.reshape(1, nc))
    if rope_cols:
        for t in rope:
            in_specs.append(pl.BlockSpec((tm, LANES), lambda i: (i, 0)))
            args.append(t)
    out_shape = []
    out_specs = []
    if add is not None:
        out_shape.append(jax.ShapeDtypeStruct((n, d), f32))
        out_specs.append(pl.BlockSpec((tm, d), lambda i: (i, 0)))
    out_shape.append(jax.ShapeDtypeStruct((n, nc), bf16))
    out_specs.append(pl.BlockSpec((tm, nc), lambda i: (i, 0)))
    kern = functools.partial(_proj_kernel, n_first=(na if yb is not None else None),
                             has_add=add is not None, has_bias=bias is not None, rope_cols=rope_cols,
                             loop_cols=loop_cols, tn=tn)
    out = pl.pallas_call(
        kern, grid=(na + nb,), in_specs=in_specs, out_specs=out_specs, out_shape=out_shape,
        scratch_shapes=[pltpu.VMEM((tm, d), bf16)],
        compiler_params=_cp(("arbitrary",)), name="proj")(*args)
    return out if add is not None else out[0]


def _gla_kernel(*refs, C, CB, G):
    seq_refs = [refs[5 * g:5 * g + 5] for g in range(G)]
    (wlr_ref, blr_ref, gn_ref, s0_ref, o_ref, sfin_ref,
     st_ref, qd_ref, oi_ref, ds_ref, dec_ref) = refs[5 * G:]
    s = pl.program_id(1)
    R = C * CB

    @pl.when(s == 0)
    def _():
        st_ref[...] = s0_ref[...]

    row = lax.broadcasted_iota(jnp.int32, (R, R), 0)
    col = lax.broadcasted_iota(jnp.int32, (R, R), 1)
    same = (row // C) == (col // C)
    tril = same & (row >= col)
    tri = jnp.where(tril, 1.0, 0.0).astype(bf16)
    ones = jnp.where(same, 1.0, 0.0).astype(bf16)
    even = lax.broadcasted_iota(jnp.int32, (R, LANES), 1) < DK_A
    srow = lax.broadcasted_iota(jnp.int32, (2 * DV_A, LANES), 0)
    slane = lax.broadcasted_iota(jnp.int32, (2 * DV_A, LANES), 1)
    blockdiag = (srow < DV_A) == (slane < DK_A)
    wlr = wlr_ref[...]
    blr = blr_ref[...]
    gn = gn_ref[...]
    pairs = H_A // 2

    for g in range(G):
        q_ref, k_ref, v_ref, _, la_ref = seq_refs[g]
        pre = _dot(la_ref[...], wlr) + blr
        loga = -(jnp.maximum(-pre, 0.0) + jnp.log1p(jnp.exp(-jnp.abs(pre)))) * (1.0 / GLA_TAU)
        hi = loga.astype(bf16)
        lo = (loga - hi.astype(f32)).astype(bf16)
        b = _dot(tri, hi) + _dot(tri, lo)
        b_end = _dot(ones, hi) + _dot(ones, lo)
        qf = q_ref[...].astype(f32) * (DK_A ** -0.5)
        kf = k_ref[...].astype(f32)
        q_dec = (qf * jnp.exp(b)).astype(bf16)
        k_inv = (kf * jnp.exp(-b)).astype(bf16)
        k_end = (kf * jnp.exp(b_end - b)).astype(bf16)
        dec = jnp.exp(b_end)
        qd_ref[g] = q_dec
        for c in range(CB):
            dec_ref[g, c] = dec[c * C:c * C + 1, :]
        for p in range(pairs):
            sl = slice(p * LANES, (p + 1) * LANES)
            vs = slice(p * 2 * DV_A, (p + 1) * 2 * DV_A)
            qp = q_dec[:, sl]
            kp = k_inv[:, sl]
            vp = v_ref[:, vs]
            zero = jnp.zeros_like(qp)
            att_e = jnp.where(tril, _dot_nt(jnp.where(even, qp, zero), kp), 0.0).astype(bf16)
            att_o = jnp.where(tril, _dot_nt(jnp.where(even, zero, qp), kp), 0.0).astype(bf16)
            oi_ref[g, :, vs] = jnp.concatenate([_dot(att_e, vp)[:, :DV_A], _dot(att_o, vp)[:, DV_A:]], axis=1)
            for c in range(CB):
                rows = slice(c * C, (c + 1) * C)
                ds = _dot_tn(vp[rows], k_end[rows, sl])
                ds_ref[g, c, p] = jnp.where(blockdiag, ds, 0.0)

    for c in range(CB):
        rows = slice(c * C, (c + 1) * C)
        for g in range(G):
            for p in range(pairs):
                sl = slice(p * LANES, (p + 1) * LANES)
                vs = slice(p * 2 * DV_A, (p + 1) * 2 * DV_A)
                st = st_ref[g, p]
                oi_ref[g, rows, vs] += _dot_nt(qd_ref[g, rows, sl], st.astype(bf16))
                st_ref[g, p] = dec_ref[g, c][:, sl] * st + ds_ref[g, c, p]

    for g in range(G):
        g_ref = seq_refs[g][3]
        for h in range(H_A):
            hs = slice(h * DV_A, (h + 1) * DV_A)
            o_ref[g, :, hs] = (_rms(oi_ref[g, :, hs], gn) * _silu(g_ref[:, hs].astype(f32))).astype(o_ref.dtype)

    @pl.when(s == pl.num_programs(1) - 1)
    def _():
        sfin_ref[...] = st_ref[...]


def _state_to_kernel_layout(s0):
    bsz = s0.shape[0]
    st = jnp.swapaxes(s0, -1, -2).reshape(bsz, H_A // 2, 2, DV_A, DK_A)
    z = jnp.zeros_like(st[:, :, 0])
    top = jnp.concatenate([st[:, :, 0], z], axis=-1)
    bot = jnp.concatenate([z, st[:, :, 1]], axis=-1)
    return jnp.concatenate([top, bot], axis=-2)


def _state_from_kernel_layout(st):
    bsz = st.shape[0]
    e = st[:, :, :DV_A, :DK_A]
    o = st[:, :, DV_A:, DK_A:]
    s = jnp.stack([e, o], axis=2).reshape(bsz, H_A, DV_A, DK_A)
    return jnp.swapaxes(s, -1, -2)


def _gla(z, w_lr, b_lr, g_norm, s0, *, bsz, t, row0, C, CB, G=2):
    rb = C * CB
    ns = t // rb
    r0 = row0 // rb
    assert bsz % G == 0

    def zspec(g, width, cb):
        return pl.BlockSpec((rb, width), lambda i, s: (r0 + (i * G + g) * ns + s, cb))

    hk = H_A * DK_A
    hv = H_A * DV_A
    st0 = _state_to_kernel_layout(s0.astype(f32))
    wl = jnp.zeros((LANES, hk), f32).at[:GLA_RANK].set(w_lr).astype(bf16)
    st_shape = (G, H_A // 2, 2 * DV_A, 2 * DK_A)
    seq_specs = []
    for g in range(G):
        seq_specs += [zspec(g, hk, _QA0 // hk), zspec(g, hk, _KA0 // hk), zspec(g, hv, _VA0 // hv),
                      zspec(g, hv, _GA0 // hv), zspec(g, LANES, _LA0 // LANES)]
    o, sfin = pl.pallas_call(
        functools.partial(_gla_kernel, C=C, CB=CB, G=G),
        grid=(bsz // G, ns),
        in_specs=seq_specs + [
            pl.BlockSpec((LANES, hk), lambda i, s: (0, 0)),
            pl.BlockSpec((1, hk), lambda i, s: (0, 0)),
            pl.BlockSpec((1, DV_A), lambda i, s: (0, 0)),
            pl.BlockSpec(st_shape, lambda i, s: (i, 0, 0, 0))],
        out_specs=[pl.BlockSpec((G, rb, hv), lambda i, s: (i, s, 0)),
                   pl.BlockSpec(st_shape, lambda i, s: (i, 0, 0, 0))],
        out_shape=[jax.ShapeDtypeStruct((bsz, t, hv), bf16),
                   jax.ShapeDtypeStruct((bsz,) + st_shape[1:], f32)],
        scratch_shapes=[pltpu.VMEM(st_shape, f32),
                        pltpu.VMEM((G, rb, hk), bf16),
                        pltpu.VMEM((G, rb, hv), f32),
                        pltpu.VMEM((G, CB) + st_shape[1:], f32),
                        pltpu.VMEM((G, CB, 1, hk), f32)],
        compiler_params=_cp(("arbitrary", "arbitrary")), name="gla")(
            *([z] * (5 * G)), wl, b_lr.reshape(1, hk).astype(f32), g_norm.reshape(1, DV_A).astype(f32), st0)
    return o.reshape(bsz * t, hv), _state_from_kernel_layout(sfin)


def _band_kernel(q_ref, ka_ref, kb_ref, kc_ref, va_ref, vb_ref, vc_ref, ba_ref, bb_ref, bc_ref,
                 o_ref, *, mask_first):
    qi = pl.program_id(1)
    for h in range(H_B):
        sl = slice(h * DH_B, (h + 1) * DH_B)
        qh = (q_ref[:, sl].astype(f32) * (DH_B ** -0.5)).astype(bf16)
        sa = _dot_nt(qh, ka_ref[:, sl].astype(bf16)) + ba_ref[h]
        sb = _dot_nt(qh, kb_ref[:, sl].astype(bf16)) + bb_ref[h]
        sc = _dot_nt(qh, kc_ref[:, sl].astype(bf16)) + bc_ref[h]
        if mask_first:
            sa = jnp.where(qi >= 2, sa, NEG_INF)
            sb = jnp.where(qi >= 1, sb, NEG_INF)
        m = jnp.maximum(jnp.maximum(jnp.max(sa, axis=-1, keepdims=True),
                                    jnp.max(sb, axis=-1, keepdims=True)),
                        jnp.max(sc, axis=-1, keepdims=True))
        ea = jnp.exp(sa - m)
        eb = jnp.exp(sb - m)
        ec = jnp.exp(sc - m)
        inv = 1.0 / (jnp.sum(ea, axis=-1, keepdims=True) + jnp.sum(eb, axis=-1, keepdims=True)
                     + jnp.sum(ec, axis=-1, keepdims=True))
        o = (_dot((ea * inv).astype(bf16), va_ref[:, sl].astype(bf16))
             + _dot((eb * inv).astype(bf16), vb_ref[:, sl].astype(bf16))
             + _dot((ec * inv).astype(bf16), vc_ref[:, sl].astype(bf16)))
        o_ref[:, sl] = o.astype(o_ref.dtype)


def _band_call(qkv_specs, args, biases, *, bsz, nq, qr, mask_first):
    hd = H_B * DH_B
    bias_specs = [pl.BlockSpec(b.shape, lambda b_, q_: (0, 0, 0)) for b in biases]
    return pl.pallas_call(
        functools.partial(_band_kernel, mask_first=mask_first),
        grid=(bsz, nq),
        in_specs=list(qkv_specs) + bias_specs,
        out_specs=pl.BlockSpec((qr, hd), lambda b, q: (b * nq + q, 0)),
        out_shape=jax.ShapeDtypeStruct((bsz * nq * qr, hd), bf16),
        compiler_params=_cp(("arbitrary", "arbitrary")), name="band")(*args, *biases)


def _rel_bias_matrix(tab, nq, nk, off):
    m = nq + nk
    j = np.arange(m)
    diff = np.where(j < nk, -j, m - j)
    idx = np.clip(diff + off, -REL_CLIP, REL_CLIP) + REL_CLIP
    u = tab.astype(f32)[:, jnp.asarray(idx)]
    h = tab.shape[0]
    return jnp.tile(u, (1, nq))[:, :nq * (m - 1)].reshape(h, nq, m - 1)[:, :, :nk]


def _band_bias_prompt(tab, qr, kb):
    tot = 2 * kb + qr
    qc = (np.arange(qr)[:, None] + 2 * kb) // CHUNK
    kc = np.arange(tot)[None, :] // CHUNK
    valid = (kc <= qc) & (kc >= qc - B_PREV)
    bias = jnp.where(jnp.asarray(valid)[None], _rel_bias_matrix(tab, qr, tot, 2 * kb), NEG_INF)
    return bias[:, :, :kb], bias[:, :, kb:2 * kb], bias[:, :, 2 * kb:]


def _band_bias_sample(tab, t, lc):
    bias = _rel_bias_matrix(tab, t, lc + t, lc)
    return bias[:, :, :lc // 2], bias[:, :, lc // 2:lc], bias[:, :, lc:]


def _swa_kernel(sink_ref, q_ref, kp_ref, kc_ref, vp_ref, vc_ref, o_ref, qs_ref, *, chunk_mask):
    qi = pl.program_id(1)
    qr = q_ref.shape[0]
    pr = kp_ref.shape[0]
    nl = G_C * LANES
    even = lax.broadcasted_iota(jnp.int32, (qr, LANES), 1) < DH_C
    top = lax.broadcasted_iota(jnp.int32, (LANES, LANES), 0) < DH_C
    if chunk_mask:
        kch = lax.broadcasted_iota(jnp.int32, (pr, nl), 0) // CHUNK
        qch = (lax.broadcasted_iota(jnp.int32, (pr, nl), 1) % LANES) // CHUNK
        ok_prev = (kch >= qch) & (qi >= 1)
        kch = lax.broadcasted_iota(jnp.int32, (qr, nl), 0) // CHUNK
        qch = (lax.broadcasted_iota(jnp.int32, (qr, nl), 1) % LANES) // CHUNK
        ok_cur = kch <= qch
    if qr < LANES:
        qs_ref[...] = jnp.zeros_like(qs_ref)

    def twice(ref, c):
        a = ref[:, c * DH_C:(c + 1) * DH_C].astype(bf16)
        return jnp.concatenate([a, a], axis=1)

    for c in range(KV_C):
        for p in range(G_C // 2):
            cols = slice(c * G_C * DH_C + p * LANES, c * G_C * DH_C + (p + 1) * LANES)
            qp = (q_ref[:, cols].astype(f32) * (DH_C ** -0.5)).astype(bf16)
            zero = jnp.zeros_like(qp)
            qs_ref[2 * p * LANES:2 * p * LANES + qr, :] = jnp.where(even, qp, zero)
            qs_ref[(2 * p + 1) * LANES:(2 * p + 1) * LANES + qr, :] = jnp.where(even, zero, qp)
        q_all = qs_ref[...]
        sink = jnp.concatenate(
            [jnp.full((1, LANES), sink_ref[c * G_C + j], f32) for j in range(G_C)], axis=1)
        st_prev = _dot_nt(twice(kp_ref, c), q_all)
        st_cur = _dot_nt(twice(kc_ref, c), q_all)
        if chunk_mask:
            st_prev = jnp.where(ok_prev, st_prev, NEG_INF)
            st_cur = jnp.where(ok_cur, st_cur, NEG_INF)
        m = jnp.maximum(jnp.maximum(jnp.max(st_prev, axis=0, keepdims=True),
                                    jnp.max(st_cur, axis=0, keepdims=True)), sink)
        e_prev = jnp.exp(st_prev - m)
        e_cur = jnp.exp(st_cur - m)
        inv = 1.0 / (jnp.sum(e_prev, axis=0, keepdims=True) + jnp.sum(e_cur, axis=0, keepdims=True)
                     + jnp.exp(sink - m))
        ot = (_dot_tn(twice(vp_ref, c), (e_prev * inv).astype(bf16))
              + _dot_tn(twice(vc_ref, c), (e_cur * inv).astype(bf16)))
        for p in range(G_C // 2):
            cols = slice(c * G_C * DH_C + p * LANES, c * G_C * DH_C + (p + 1) * LANES)
            blk = jnp.where(top, ot[:, 2 * p * LANES:(2 * p + 1) * LANES],
                            ot[:, (2 * p + 1) * LANES:(2 * p + 2) * LANES])
            o_ref[:, cols] = blk.T[:qr].astype(o_ref.dtype)


def _swa_call(specs, args, sinks, *, bsz, nq, qr, chunk_mask):
    hd = H_C * DH_C
    return pl.pallas_call(
        functools.partial(_swa_kernel, chunk_mask=chunk_mask),
        grid=(bsz, nq),
        in_specs=[pl.BlockSpec(memory_space=pltpu.SMEM)] + list(specs),
        out_specs=pl.BlockSpec((qr, hd), lambda b, q: (b * nq + q, 0)),
        out_shape=jax.ShapeDtypeStruct((bsz * nq * qr, hd), bf16),
        scratch_shapes=[pltpu.VMEM((G_C * LANES, LANES), bf16)],
        compiler_params=_cp(("arbitrary", "arbitrary")), name="swa")(sinks, *args)


def _out_kernel(*refs, n_parts, n_first, split_y):
    ny = 2 if split_y else 1
    y_refs = refs[:ny]
    refs = refs[ny - 1:]
    o_refs = refs[1:1 + 2 * n_parts]
    w_refs = refs[1 + 2 * n_parts:1 + 3 * n_parts]
    g_ref, rh_ref, rl_ref = refs[1 + 3 * n_parts:4 + 3 * n_parts]
    y1_ref, h2_ref, route_ref, cnt_ref, run_ref, lg_ref = refs[4 + 3 * n_parts:]
    step = pl.program_id(0)
    i = jnp.minimum(step, pl.num_programs(0) - 2)

    @pl.when(step == 0)
    def _():
        run_ref[...] = jnp.zeros_like(run_ref)
        lg_ref[...] = jnp.zeros_like(lg_ref)

    lg = lg_ref[...]
    live = (step >= 1).astype(f32)

    acc = y_refs[0][...]
    if split_y:
        acc = jnp.where(i < n_first, acc, y_refs[1][...])
    for k in range(n_parts):
        lhs = jnp.where(i < n_first, o_refs[2 * k][...], o_refs[2 * k + 1][...])
        acc = acc + _dot(lhs, w_refs[k][...])
    y1_ref[...] = acc
    h2 = _rms(acc, g_ref[...])
    hi = h2.astype(bf16)
    h2_ref[...] = hi.astype(f32)
    lo = (h2 - hi.astype(f32)).astype(bf16)
    lg_ref[...] = _dot(hi, rh_ref[...]) + _dot(hi, rl_ref[...]) + _dot(lo, rh_ref[...])

    lane_i = lax.broadcasted_iota(jnp.int32, lg.shape, 1)
    lane = lane_i.astype(f32)
    big = float(LANES)
    ninf = -jnp.inf
    is_g = lane_i < N_GROUPS
    gmax = jnp.max(jnp.where(is_g, lg, ninf), axis=-1, keepdims=True)
    gsel = jnp.min(jnp.where(is_g & (lg == gmax), lane, big), axis=-1, keepdims=True)
    pg = 1.0 / jnp.sum(jnp.where(is_g, jnp.exp(lg - gmax), 0.0), axis=-1, keepdims=True)
    e0 = N_GROUPS + EXP_PER_GROUP * gsel
    is_e = (lane >= e0) & (lane < e0 + EXP_PER_GROUP)
    v1 = jnp.max(jnp.where(is_e, lg, ninf), axis=-1, keepdims=True)
    i1 = jnp.min(jnp.where(is_e & (lg == v1), lane, big), axis=-1, keepdims=True)
    is_e2 = is_e & (lane != i1)
    v2 = jnp.max(jnp.where(is_e2, lg, ninf), axis=-1, keepdims=True)
    i2 = jnp.min(jnp.where(is_e2 & (lg == v2), lane, big), axis=-1, keepdims=True)
    t = jnp.exp(v2 - v1)
    w1 = pg / (1.0 + t)
    w2 = pg * t / (1.0 + t)

    tmr = lg.shape[0]
    oh1 = jnp.where(lane == i1, 1.0, 0.0)
    oh2 = jnp.where(lane == i2, 1.0, 0.0)
    before = jnp.where(lax.broadcasted_iota(jnp.int32, (tmr, tmr), 0)
                       > lax.broadcasted_iota(jnp.int32, (tmr, tmr), 1), 1.0, 0.0).astype(bf16)
    tot1 = jnp.sum(oh1, axis=0, keepdims=True)
    tot2 = jnp.sum(oh2, axis=0, keepdims=True)
    base = run_ref[...]
    r1 = jnp.sum(oh1 * (base + _dot(before, oh1.astype(bf16))), axis=-1, keepdims=True)
    r2 = jnp.sum(oh2 * (base + tot1 + _dot(before, oh2.astype(bf16))), axis=-1, keepdims=True)
    run_ref[...] = base + live * (tot1 + tot2)
    cnt_ref[...] = base + live * (tot1 + tot2)
    vals = (i1 - N_GROUPS, i2 - N_GROUPS, w1, w2, r1, r2)
    out = jnp.zeros_like(lg)
    for k, v in enumerate(vals):
        out = jnp.where(lane_i == k, v, out)
    route_ref[...] = out


def _out_proj(y, parts, g, w_router, *, tm):
    n_first = parts[0][0].shape[0] // tm
    split_y = isinstance(y, tuple)
    n = y[0].shape[0] + y[1].shape[0] if split_y else y.shape[0]
    n_tiles = n // tm
    tile = lambda s: jnp.minimum(s, n_tiles - 1)
    first = lambda s: (jnp.minimum(s, n_first - 1), 0)
    second = lambda s: (jnp.clip(s - n_first, 0, n_tiles - n_first - 1), 0)
    if split_y:
        d = y[0].shape[1]
        in_specs = [pl.BlockSpec((tm, d), first), pl.BlockSpec((tm, d), second)]
        args = list(y)
    else:
        d = y.shape[1]
        in_specs = [pl.BlockSpec((tm, d), lambda s: (tile(s), 0))]
        args = [y]
    for op, os_, _ in parts:
        dk = op.shape[1]
        in_specs.append(pl.BlockSpec((tm, dk), first))
        in_specs.append(pl.BlockSpec((tm, dk), second))
        args += [op, os_]
    for _, _, w in parts:
        in_specs.append(pl.BlockSpec(w.shape, lambda i: (0, 0)))
        args.append(w)
    rh = w_router.astype(bf16)
    rl = (w_router - rh.astype(f32)).astype(bf16)
    in_specs += [pl.BlockSpec((1, d), lambda i: (0, 0)),
                 pl.BlockSpec((d, LANES), lambda i: (0, 0)),
                 pl.BlockSpec((d, LANES), lambda i: (0, 0))]
    args += [g.reshape(1, d), rh, rl]
    return pl.pallas_call(
        functools.partial(_out_kernel, n_parts=len(parts), n_first=n_first, split_y=split_y),
        grid=(n_tiles + 1,), in_specs=in_specs,
        out_specs=[pl.BlockSpec((tm, d), lambda s: (tile(s), 0)),
                   pl.BlockSpec((tm, d), lambda s: (tile(s), 0)),
                   pl.BlockSpec((tm, LANES), lambda s: (jnp.maximum(s - 1, 0), 0)),
                   pl.BlockSpec((1, LANES), lambda s: (0, 0))],
        out_shape=[jax.ShapeDtypeStruct((n, d), f32),
                   jax.ShapeDtypeStruct((n, d), f32),
                   jax.ShapeDtypeStruct((n, LANES), f32),
                   jax.ShapeDtypeStruct((1, LANES), f32)],
        scratch_shapes=[pltpu.VMEM((1, LANES), f32), pltpu.VMEM((tm, LANES), f32)],
        compiler_params=_cp(("arbitrary",)), name="out_proj")(*args)


def _moe_kernel(te_ref, tv_ref, pos_ref, plo_ref, phi_ref, en_ref, es_ref,
                h_ref, wg_ref, wu_ref, wd_ref, out_ref,
                slot_ref, x0, x1, x2, y0, y1, y2, gsem, ssem, wg_f, wu_f, wd_f, wsem, wg_s, wu_s, wd_s,
                *, tm, n):
    t = pl.program_id(0)
    nt = pl.num_programs(0)
    n2 = 2 * n
    xs = (x0, x1, x2)
    ys = (y0, y1, y2)
    depth = len(xs)

    def weights_start(e, slot):
        for src, dst in ((wg_ref, wg_f), (wu_ref, wu_f), (wd_ref, wd_f)):
            pltpu.make_async_copy(src.at[e], dst.at[slot], wsem.at[slot]).start()

    def weights_wait(slot):
        for src, dst in ((wg_ref, wg_f), (wu_ref, wu_f), (wd_ref, wd_f)):
            pltpu.make_async_copy(src.at[0], dst.at[slot], wsem.at[slot]).wait()

    def tile_valid(tile):
        return (tile >= 0) & (tile < nt) & (tv_ref[jnp.clip(tile, 0, nt - 1)] != 0)

    valid = tile_valid(t)

    def gather_row(tile, r, p):
        s = slot_ref[tile * tm + r]
        tok = jnp.where(s < n, s, jnp.where(s < n2, s - n, s - n2))
        pltpu.make_async_copy(h_ref.at[pl.ds(tok, 1)], xs[p].at[pl.ds(r, 1)], gsem.at[p]).start()

    def scatter_row(tile, r, p):
        s = slot_ref[tile * tm + r]
        pltpu.make_async_copy(ys[p].at[pl.ds(r, 1)], out_ref.at[pl.ds(s, 1)], ssem.at[p]).start()

    def rows_loop(row_fn, tile, p):
        def body(r, carry):
            row_fn(tile, r, p)
            return carry
        lax.fori_loop(0, tm, body, 0, unroll=8)

    def gather_wait(p):
        pltpu.make_async_copy(h_ref.at[pl.ds(0, tm)], xs[p], gsem.at[p]).wait()

    def scatter_wait(p):
        pltpu.make_async_copy(ys[p], out_ref.at[pl.ds(0, tm)], ssem.at[p]).wait()

    def compute(p):
        x = xs[p][...].astype(bf16)
        g = _dot(x, wg_s[...])
        u = _dot(x, wu_s[...])
        ys[p][...] = _dot((_silu(g) * u).astype(bf16), wd_s[...])

    @pl.when(t == 0)
    def _():
        def per_expert(e, nxt):
            def body(r, v):
                slot_ref[r] = v
                return v + 1
            return lax.fori_loop(plo_ref[e], phi_ref[e], body, nxt)
        lax.fori_loop(0, N_EXPERTS, per_expert, n2)

        def body(s, carry):
            slot_ref[pos_ref[s]] = s
            return carry
        lax.fori_loop(0, n2, body, 0, unroll=8)
        weights_start(te_ref[0], 0)
        rows_loop(gather_row, 0, 0)

        @pl.when(tile_valid(1))
        def _():
            rows_loop(gather_row, 1, 1)

    def step(p):
        q = (p + 2) % depth

        @pl.when(t >= depth)
        def _():
            scatter_wait(p)

        @pl.when(valid)
        def _():
            gather_wait(p)

            @pl.when((t == 0) | (te_ref[t] != te_ref[jnp.maximum(t - 1, 0)]))
            def _():
                slot = es_ref[t]
                weights_wait(slot)
                wg_s[...] = wg_f[slot].astype(bf16)
                wu_s[...] = wu_f[slot].astype(bf16)
                wd_s[...] = wd_f[slot].astype(bf16)

                @pl.when(en_ref[t] >= 0)
                def _():
                    weights_start(en_ref[t], 1 - slot)

            steady = tile_valid(t + 2) & (t >= 1)

            @pl.when(steady)
            def _():
                for r in range(tm):
                    gather_row(t + 2, r, q)
                    scatter_row(t - 1, r, q)
                compute(p)

            @pl.when(jnp.logical_not(steady))
            def _():
                @pl.when(tile_valid(t + 2))
                def _():
                    rows_loop(gather_row, t + 2, q)

                @pl.when(t >= 1)
                def _():
                    rows_loop(scatter_row, t - 1, q)
                compute(p)

        @pl.when(jnp.logical_not(valid))
        def _():
            @pl.when(tile_valid(t - 1))
            def _():
                rows_loop(scatter_row, t - 1, q)
            ys[p][...] = jnp.zeros(ys[p].shape, f32)
            pltpu.make_async_copy(ys[p], out_ref.at[pl.ds(pl.multiple_of(t * tm, tm), tm)], ssem.at[p]).start()

        @pl.when(t == nt - 1)
        def _():
            @pl.when(valid)
            def _():
                rows_loop(scatter_row, t, p)
            for k in range(depth):
                scatter_wait(k)

    for p in range(depth):
        pl.when(t % depth == p)(functools.partial(step, p))


def _moe_experts(h2, pos, tile_e, tile_v, pad_lo, pad_hi, wg, wu, wd, *, layer, tm, n_tiles):
    n, d = h2.shape
    ff = wg.shape[2]
    assert n >= N_EXPERTS * tm, "padding rows read tokens 0 .. N_EXPERTS * (tm - 1)"
    assert n_tiles >= 3
    first = jnp.concatenate([jnp.ones((1,), jnp.int32), (tile_e[1:] != tile_e[:-1]).astype(jnp.int32)])
    seq = jnp.cumsum(first * tile_v) - 1
    seq = jnp.where(tile_v != 0, seq, n_tiles)
    nxt_pos = jnp.sum((seq[None, :] <= seq[:, None]).astype(jnp.int32), axis=1)
    has_next = (nxt_pos < n_tiles) & (tile_v[jnp.minimum(nxt_pos, n_tiles - 1)] != 0) & (tile_v != 0)
    tile_e = tile_e + layer * N_EXPERTS
    e_next = jnp.where(has_next, tile_e[jnp.minimum(nxt_pos, n_tiles - 1)], -1).astype(jnp.int32)
    e_slot = (seq % 2).astype(jnp.int32)
    row_buf = pltpu.VMEM((tm, d), f32)
    any_spec = pl.BlockSpec(memory_space=pl.ANY)
    grid_spec = pltpu.PrefetchScalarGridSpec(
        num_scalar_prefetch=7, grid=(n_tiles,),
        in_specs=[any_spec, any_spec, any_spec, any_spec],
        out_specs=any_spec,
        scratch_shapes=[pltpu.SMEM((n_tiles * tm,), jnp.int32),
                        row_buf, row_buf, row_buf, row_buf, row_buf, row_buf,
                        pltpu.SemaphoreType.DMA((3,)), pltpu.SemaphoreType.DMA((3,)),
                        pltpu.VMEM((2, d, ff), f32), pltpu.VMEM((2, d, ff), f32), pltpu.VMEM((2, ff, d), f32),
                        pltpu.SemaphoreType.DMA((2,)),
                        pltpu.VMEM((d, ff), bf16), pltpu.VMEM((d, ff), bf16), pltpu.VMEM((ff, d), bf16)])
    return pl.pallas_call(
        functools.partial(_moe_kernel, tm=tm, n=n), grid_spec=grid_spec,
        out_shape=jax.ShapeDtypeStruct((n_tiles * tm, d), f32),
        compiler_params=_cp(("arbitrary",)), name="moe")(
            tile_e, tile_v, pos, pad_lo, pad_hi, e_next, e_slot, h2, wg, wu, wd)


def _moe(h2, route, counts, wg, wu, wd, *, layer, tm):
    n = route.shape[0]
    n_tiles = (2 * n) // tm + N_EXPERTS
    counts = counts[0, N_GROUPS:N_GROUPS + N_EXPERTS].astype(jnp.int32)
    padded = ((counts + tm - 1) // tm) * tm
    ends = jnp.cumsum(padded)
    starts = ends - padded
    e = route[:, 0:2].astype(jnp.int32)
    rank = route[:, 4:6].astype(jnp.int32)
    pos = (starts[e] + rank).T.reshape(-1)
    tile_start = jnp.arange(n_tiles, dtype=jnp.int32) * tm
    tile_e = jnp.sum((tile_start[:, None] >= ends[None, :]).astype(jnp.int32), axis=1)
    tile_v = (tile_start < ends[-1]).astype(jnp.int32)
    tile_e = jnp.minimum(tile_e, N_EXPERTS - 1)
    return _moe_experts(h2, pos, tile_e, tile_v, starts + counts, ends, wg, wu, wd, layer=layer, tm=tm,
                        n_tiles=n_tiles)


def _final_kernel(y_ref, m0_ref, m1_ref, rt_ref, g_ref, o_ref):
    o_ref[...] = _rms(y_ref[...] + _moe_combine(m0_ref, m1_ref, rt_ref), g_ref[...])


def _final(y, moe_out, route, g, *, tm, row0, rows):
    n, d = y.shape
    r0 = row0 // tm
    r1 = (n + row0) // tm
    return pl.pallas_call(
        _final_kernel, grid=(rows // tm,),
        in_specs=[pl.BlockSpec((tm, d), lambda i: (r0 + i, 0)),
                  pl.BlockSpec((tm, d), lambda i: (r0 + i, 0)),
                  pl.BlockSpec((tm, d), lambda i: (r1 + i, 0)),
                  pl.BlockSpec((tm, LANES), lambda i: (r0 + i, 0)),
                  pl.BlockSpec((1, d), lambda i: (0, 0))],
        out_specs=pl.BlockSpec((tm, d), lambda i: (i, 0)),
        out_shape=jax.ShapeDtypeStruct((rows, d), f32),
        compiler_params=_cp(("arbitrary",)), name="final_norm")(y, moe_out, moe_out, route, g.reshape(1, d))


def _rope_tables(pos):
    half = ROT_DIM // 2
    inv = jnp.power(ROPE_THETA, -jnp.arange(half, dtype=f32) * 2.0 / ROT_DIM)
    ang = pos.astype(f32)[:, None] * inv[None, :]
    cos, sin = lax.optimization_barrier((jnp.cos(ang), jnp.sin(ang)))
    n = pos.shape[0]
    one = jnp.ones((n, DH_C - ROT_DIM), f32)
    zero = jnp.zeros((n, DH_C - ROT_DIM), f32)
    zh = jnp.zeros((n, half), f32)
    c = jnp.concatenate([cos, cos, one], axis=1)
    s1 = jnp.concatenate([zh, sin, zero], axis=1)
    s2 = jnp.concatenate([-sin, zh, zero], axis=1)
    rep = LANES // DH_C
    return tuple(jnp.tile(t, (1, rep)) for t in (c, s1, s2))


def kernel(x_prompt, x_sample, cache_gla_state, cache_band_k, cache_band_v, cache_swa_k, cache_swa_v,
           norm_mix, norm_ffn, norm_final, w_in_ab, w_gla_lr, b_gla_lr, gla_out_norm, rel_bias_tab,
           w_out_ab, w_in_c, b_in_c, attn_sinks, w_out_c, w_router_grp, w_router_exp, w_exp_gate,
           w_exp_up, w_exp_down):
    bp, sp, d = x_prompt.shape
    bs, ts, _ = x_sample.shape
    n_p = bp * sp
    n_s = bs * ts
    n = n_p + n_s
    tm = 512 if n_s % 512 == 0 else n_s
    tmo = min(tm, 256)
    tme = 256 if (2 * n) % 256 == 0 else 64
    ff = w_exp_gate.shape[-1]

    xp = x_prompt.reshape(n_p, d)
    xs_ = x_sample.reshape(n_s, d)

    def router_w(l):
        wr = jnp.concatenate([w_router_grp[l], w_router_exp[l].reshape(d, N_EXPERTS)], axis=1)
        return jnp.zeros((d, LANES), f32).at[:, :N_GROUPS + N_EXPERTS].set(wr)

    experts = (w_exp_gate.reshape(-1, d, ff), w_exp_up.reshape(-1, d, ff), w_exp_down.reshape(-1, ff, d))

    w0 = w_in_ab[0]
    la0 = 2 * H_A * DK_A + 2 * H_A * DV_A
    w_cat = jnp.concatenate(
        [w0[:, :la0], w0[:, la0 + GLA_RANK:], w0[:, la0:la0 + GLA_RANK],
         jnp.zeros((d, LANES - GLA_RANK), f32)], axis=1).astype(bf16)
    z = _proj(xp, norm_mix[0], w_cat, tm=tmo, tn=1024, loop_cols=_LA0, yb=xs_)

    c_s = min(CHUNK, ts)
    oa_p, gla_p = _gla(z, w_gla_lr[0], b_gla_lr[0], gla_out_norm[0],
                       jnp.zeros((bp, H_A, DK_A, DV_A), f32), bsz=bp, t=sp, row0=0, C=CHUNK, CB=4)
    oa_s, gla_s = _gla(z, w_gla_lr[0], b_gla_lr[0], gla_out_norm[0], cache_gla_state[0],
                       bsz=bs, t=ts, row0=n_p, C=c_s, CB=ts // c_s)

    hd = H_B * DH_B
    qr = 256
    nq = sp // qr
    qcb, kcb, vcb = _QB0 // hd, _KB0 // hd, _VB0 // hd
    cur = lambda cb: pl.BlockSpec((qr, hd), lambda b, q: (b * nq + q, cb))
    prev = lambda cb, back: pl.BlockSpec((qr, hd), lambda b, q: (b * nq + jnp.maximum(q - back, 0), cb))
    ob_p = _band_call(
        [cur(qcb), prev(kcb, 2), prev(kcb, 1), cur(kcb), prev(vcb, 2), prev(vcb, 1), cur(vcb)],
        [z] * 7, _band_bias_prompt(rel_bias_tab[0], qr, qr), bsz=bp, nq=nq, qr=qr, mask_first=True)

    lb = cache_band_k.shape[2]
    ck = cache_band_k[0].reshape(bs * lb, hd)
    cv = cache_band_v[0].reshape(bs * lb, hd)
    rs0 = n_p // ts
    new = lambda cb: pl.BlockSpec((ts, hd), lambda b, q: (rs0 + b, cb))
    cpiece = lambda k: pl.BlockSpec((lb // 2, hd), lambda b, q: (2 * b + k, 0))
    ob_s = _band_call(
        [new(qcb), cpiece(0), cpiece(1), new(kcb), cpiece(0), cpiece(1), new(vcb)],
        [z, ck, ck, z, cv, cv, z], _band_bias_sample(rel_bias_tab[0], ts, lb),
        bsz=bs, nq=1, qr=ts, mask_first=False)

    dmix_a = H_A * DV_A
    wo = w_out_ab[0].astype(bf16)
    y1, h2, route, cnt = _out_proj(
        (xp, xs_), [(oa_p, oa_s, wo[:dmix_a]), (ob_p, ob_s, wo[dmix_a:])], norm_ffn[0], router_w(0),
        tm=tmo)
    moe0 = _moe(h2, route, cnt, *experts, layer=0, tm=tme)

    keep = min(B_PREV * CHUNK, sp)

    def prompt_tail(a, rows, c0, width):
        return jnp.stack([a[(b + 1) * sp - rows:(b + 1) * sp, c0:c0 + width] for b in range(bp)])

    band_k_p = prompt_tail(z, keep, _KB0, hd).astype(f32).reshape(bp, keep, H_B, DH_B)[None]
    band_v_p = prompt_tail(z, keep, _VB0, hd).astype(f32).reshape(bp, keep, H_B, DH_B)[None]
    kb_s = z[n_p:, _KB0:_KB0 + hd].astype(f32).reshape(bs, ts, H_B, DH_B)
    vb_s = z[n_p:, _VB0:_VB0 + hd].astype(f32).reshape(bs, ts, H_B, DH_B)
    band_k_s = jnp.concatenate([cache_band_k[0][:, ts:], kb_s], axis=1)[None]
    band_v_s = jnp.concatenate([cache_band_v[0][:, ts:], vb_s], axis=1)[None]

    nq_c = H_C * DH_C
    nkv = KV_C * DH_C
    uniq = _rope_tables(jnp.concatenate([jnp.arange(sp), PAST_LEN + jnp.arange(ts)]))
    rope = tuple(jnp.concatenate([jnp.tile(u[:sp], (bp, 1)), jnp.tile(u[sp:], (bs, 1))]) for u in uniq)
    y2, z1 = _proj(y1, norm_mix[1], w_in_c[0].astype(bf16), tm=tmo, tn=512, loop_cols=nq_c,
                   add=(moe0, route), bias=b_in_c[0], rope=rope, rope_cols=nq_c + nkv)

    qr_c = WINDOW
    nq1 = sp // qr_c
    kw = nkv
    kcb1, vcb1 = _KC0 // kw, _VC0 // kw
    qspec = pl.BlockSpec((qr_c, nq_c), lambda b, q: (b * nq1 + q, 0))
    cur1 = lambda cb: pl.BlockSpec((qr_c, kw), lambda b, q: (b * nq1 + q, cb))
    prev1 = lambda cb: pl.BlockSpec((qr_c, kw), lambda b, q: (b * nq1 + jnp.maximum(q - 1, 0), cb))
    sinks = attn_sinks[0].astype(f32)
    oc_p = _swa_call([qspec, prev1(kcb1), cur1(kcb1), prev1(vcb1), cur1(vcb1)], [z1] * 5, sinks,
                     bsz=bp, nq=nq1, qr=qr_c, chunk_mask=True)

    lc = cache_swa_k.shape[2]
    csk = cache_swa_k[0].reshape(bs * lc, nkv)
    csv = cache_swa_v[0].reshape(bs * lc, nkv)
    qspec_s = pl.BlockSpec((ts, nq_c), lambda b, q: (rs0 + b, 0))
    new1 = lambda cb: pl.BlockSpec((ts, kw), lambda b, q: (rs0 + b, cb))
    cspec = pl.BlockSpec((lc, kw), lambda b, q: (b, 0))
    oc_s = _swa_call([qspec_s, cspec, new1(kcb1), cspec, new1(vcb1)], [z1, csk, z1, csv, z1], sinks,
                     bsz=bs, nq=1, qr=ts, chunk_mask=False)

    y3, h4, route1, cnt1 = _out_proj(y2, [(oc_p, oc_s, w_out_c[0].astype(bf16))], norm_ffn[1],
                                     router_w(1), tm=tmo)
    moe1 = _moe(h4, route1, cnt1, *experts, layer=1, tm=tme)

    keep1 = min(WINDOW, sp)

    swa_k_p = prompt_tail(z1, keep1, _KC0, kw).astype(f32).reshape(bp, keep1, KV_C, DH_C)[None]
    swa_v_p = prompt_tail(z1, keep1, _VC0, kw).astype(f32).reshape(bp, keep1, KV_C, DH_C)[None]
    k1_s = z1[n_p:, _KC0:_KC0 + kw].astype(f32).reshape(bs, ts, KV_C, DH_C)
    v1_s = z1[n_p:, _VC0:_VC0 + kw].astype(f32).reshape(bs, ts, KV_C, DH_C)
    swa_k_s = jnp.concatenate([cache_swa_k[0][:, ts:], k1_s], axis=1)[None]
    swa_v_s = jnp.concatenate([cache_swa_v[0][:, ts:], v1_s], axis=1)[None]

    y_prompt = _final(y3, moe1, route1, norm_final, tm=tm, row0=0, rows=n_p).reshape(bp, sp, d)
    y_sample = _final(y3, moe1, route1, norm_final, tm=tm, row0=n_p, rows=n_s).reshape(bs, ts, d)

    return (y_prompt, y_sample, gla_p[None], gla_s[None].astype(cache_gla_state.dtype),
            band_k_p, band_v_p, band_k_s, band_v_s, swa_k_p, swa_v_p, swa_k_s, swa_v_s)
```

```python
import functools

import numpy as np
import jax
import jax.numpy as jnp
from jax import lax
from jax.experimental import pallas as pl
from jax.experimental.pallas import tpu as pltpu

f32 = jnp.float32
bf16 = jnp.bfloat16

CHUNK = 64
EPS = 1e-6
NEG_INF = -1e30
PAST_LEN = 1024
H_A = 8
DK_A = 64
DV_A = 128
GLA_RANK = 16
GLA_TAU = 16.0
H_B = 8
DH_B = 128
B_PREV = 8
REL_CLIP = 128
H_C = 32
KV_C = 4
G_C = H_C // KV_C
DH_C = 64
WINDOW = 128
ROT_DIM = DH_C // 4
ROPE_THETA = 500000.0
N_GROUPS = 4
EXP_PER_GROUP = 8
N_EXPERTS = N_GROUPS * EXP_PER_GROUP

LANES = 128
VMEM_LIMIT = 52 * 1024 * 1024

_QA0, _KA0, _VA0, _GA0 = 0, 512, 1024, 2048
_QB0, _KB0, _VB0, _LA0 = 3072, 4096, 5120, 6144
_NC_AB = 6272
_QC0, _KC0, _VC0 = 0, 2048, 2304
_NC_C = 2560


def _cp(sem):
    return pltpu.CompilerParams(dimension_semantics=sem, vmem_limit_bytes=VMEM_LIMIT)


def _dot(a, b):
    return jnp.dot(a, b, preferred_element_type=f32)


def _dot_nt(a, b):
    return lax.dot_general(a, b, (((1,), (1,)), ((), ())), preferred_element_type=f32)


def _dot_tn(a, b):
    return lax.dot_general(a, b, (((0,), (0,)), ((), ())), preferred_element_type=f32)


def _rms(x, g):
    ms = jnp.mean(x * x, axis=-1, keepdims=True)
    return (x * lax.rsqrt(ms + EPS)) * g


def _silu(x):
    return x / (1.0 + jnp.exp(-x))


def _moe_combine(m0_ref, m1_ref, route_ref):
    rt = route_ref[...]
    return rt[:, 2:3] * m0_ref[...] + rt[:, 3:4] * m1_ref[...]


def _proj_kernel(*refs, n_first, has_add, has_bias, rope_cols, loop_cols, tn):
    it = iter(refs)
    ya_ref = next(it)
    yb_ref = next(it) if n_first is not None else None
    if has_add:
        m0_ref, m1_ref, rt_ref = next(it), next(it), next(it)
    g_ref = next(it)
    w_ref = next(it)
    b_ref = next(it) if has_bias else None
    if rope_cols:
        c_ref, s1_ref, s2_ref = next(it), next(it), next(it)
    ysum_ref = next(it) if has_add else None
    z_ref = next(it)
    h_ref = next(it)
    i = pl.program_id(0)

    x = ya_ref[...]
    if n_first is not None:
        x = jnp.where(i < n_first, x, yb_ref[...])
    if has_add:
        x = x + _moe_combine(m0_ref, m1_ref, rt_ref)
        ysum_ref[...] = x
    h_ref[...] = _rms(x, g_ref[...]).astype(bf16)

    def rope(acc):
        width = acc.shape[1]
        rep = width // LANES
        c = jnp.tile(c_ref[...], (1, rep))
        s1 = jnp.tile(s1_ref[...], (1, rep))
        s2 = jnp.tile(s2_ref[...], (1, rep))
        half = ROT_DIM // 2
        lo = pltpu.roll(acc, half, axis=1)
        hi = pltpu.roll(acc, width - half, axis=1)
        return acc * c + lo * s1 + hi * s2

    def col_tile(j, carry):
        cols = pl.ds(pl.multiple_of(j * tn, tn), tn)
        acc = _dot(h_ref[...], w_ref[:, cols])
        if has_bias:
            acc = acc + b_ref[:, cols]
        if rope_cols:
            acc = rope(acc)
        z_ref[:, cols] = acc.astype(z_ref.dtype)
        return carry

    nc = z_ref.shape[1]
    lax.fori_loop(0, loop_cols // tn, col_tile, 0)
    if nc > loop_cols:
        acc = _dot(h_ref[...], w_ref[:, loop_cols:])
        if has_bias:
            acc = acc + b_ref[:, loop_cols:]
        rt = max(rope_cols - loop_cols, 0)
        if rt:
            z_ref[:, loop_cols:loop_cols + rt] = rope(acc[:, :rt]).astype(z_ref.dtype)
        z_ref[:, loop_cols + rt:] = acc[:, rt:].astype(z_ref.dtype)


def _proj(ya, g, w, *, tm, tn, loop_cols, yb=None, add=None, bias=None, rope=None, rope_cols=0):
    d = ya.shape[1]
    na = ya.shape[0] // tm
    nb = 0 if yb is None else yb.shape[0] // tm
    n = (na + nb) * tm
    nc = w.shape[1]
    assert loop_cols % tn == 0 and loop_cols <= nc and (rope_cols == 0 or rope_cols >= loop_cols)
    once = dict(pipeline_mode=pl.Buffered(1))
    in_specs = []
    args = []
    if yb is None:
        in_specs.append(pl.BlockSpec((tm, d), lambda i: (i, 0)))
        args.append(ya)
    else:
        in_specs.append(pl.BlockSpec((tm, d), lambda i: (jnp.minimum(i, na - 1), 0)))
        in_specs.append(pl.BlockSpec((tm, d), lambda i: (jnp.maximum(i - na, 0), 0)))
        args += [ya, yb]
    if add is not None:
        moe_out, route = add
        in_specs.append(pl.BlockSpec((tm, d), lambda i: (i, 0)))
        in_specs.append(pl.BlockSpec((tm, d), lambda i: (i + na, 0)))
        in_specs.append(pl.BlockSpec((tm, LANES), lambda i: (i, 0)))
        args += [moe_out, moe_out, route]
    in_specs.append(pl.BlockSpec((1, d), lambda i: (0, 0)))
    args.append(g.reshape(1, d))
    in_specs.append(pl.BlockSpec((d, nc), lambda i: (0, 0), **once))
    args.append(w)
    if bias is not None:
        in_specs.append(pl.BlockSpec((1, nc), lambda i: (0, 0)))
        args.append(bias.reshape(1, nc))
    if rope_cols:
        for t in rope:
            in_specs.append(pl.BlockSpec((tm, LANES), lambda i: (i, 0)))
            args.append(t)
    out_shape = []
    out_specs = []
    if add is not None:
        out_shape.append(jax.ShapeDtypeStruct((n, d), f32))
        out_specs.append(pl.BlockSpec((tm, d), lambda i: (i, 0)))
    out_shape.append(jax.ShapeDtypeStruct((n, nc), bf16))
    out_specs.append(pl.BlockSpec((tm, nc), lambda i: (i, 0)))
    kern = functools.partial(_proj_kernel, n_first=(na if yb is not None else None),
                             has_add=add is not None, has_bias=bias is not None, rope_cols=rope_cols,
                             loop_cols=loop_cols, tn=tn)
    out = pl.pallas_call(
        kern, grid=(na + nb,), in_specs=in_specs, out_specs=out_specs, out_shape=out_shape,
        scratch_shapes=[pltpu.VMEM((tm, d), bf16)],
        compiler_params=_cp(("arbitrary",)), name="proj")(*args)
    return out if add is not None else out[0]


def _gla_kernel(*refs, C, CB, G):
    seq_refs = [refs[5 * g:5 * g + 5] for g in range(G)]
    (wlr_ref, blr_ref, gn_ref, s0_ref, o_ref, sfin_ref,
     st_ref, qd_ref, oi_ref, ds_ref, dec_ref) = refs[5 * G:]
    s = pl.program_id(1)
    R = C * CB

    @pl.when(s == 0)
    def _():
        st_ref[...] = s0_ref[...]

    row = lax.broadcasted_iota(jnp.int32, (R, R), 0)
    col = lax.broadcasted_iota(jnp.int32, (R, R), 1)
    same = (row // C) == (col // C)
    tril = same & (row >= col)
    tri = jnp.where(tril, 1.0, 0.0).astype(bf16)
    ones = jnp.where(same, 1.0, 0.0).astype(bf16)
    even = lax.broadcasted_iota(jnp.int32, (R, LANES), 1) < DK_A
    srow = lax.broadcasted_iota(jnp.int32, (2 * DV_A, LANES), 0)
    slane = lax.broadcasted_iota(jnp.int32, (2 * DV_A, LANES), 1)
    blockdiag = (srow < DV_A) == (slane < DK_A)
    wlr = wlr_ref[...]
    blr = blr_ref[...]
    gn = gn_ref[...]
    pairs = H_A // 2

    for g in range(G):
        q_ref, k_ref, v_ref, _, la_ref = seq_refs[g]
        pre = _dot(la_ref[...], wlr) + blr
        loga = -(jnp.maximum(-pre, 0.0) + jnp.log1p(jnp.exp(-jnp.abs(pre)))) * (1.0 / GLA_TAU)
        hi = loga.astype(bf16)
        lo = (loga - hi.astype(f32)).astype(bf16)
        b = _dot(tri, hi) + _dot(tri, lo)
        b_end = _dot(ones, hi) + _dot(ones, lo)
        qf = q_ref[...].astype(f32) * (DK_A ** -0.5)
        kf = k_ref[...].astype(f32)
        q_dec = (qf * jnp.exp(b)).astype(bf16)
        k_inv = (kf * jnp.exp(-b)).astype(bf16)
        k_end = (kf * jnp.exp(b_end - b)).astype(bf16)
        dec = jnp.exp(b_end)
        qd_ref[g] = q_dec
        for c in range(CB):
            dec_ref[g, c] = dec[c * C:c * C + 1, :]
        for p in range(pairs):
            sl = slice(p * LANES, (p + 1) * LANES)
            vs = slice(p * 2 * DV_A, (p + 1) * 2 * DV_A)
            qp = q_dec[:, sl]
            kp = k_inv[:, sl]
            vp = v_ref[:, vs]
            zero = jnp.zeros_like(qp)
            att_e = jnp.where(tril, _dot_nt(jnp.where(even, qp, zero), kp), 0.0).astype(bf16)
            att_o = jnp.where(tril, _dot_nt(jnp.where(even, zero, qp), kp), 0.0).astype(bf16)
            oi_ref[g, :, vs] = jnp.concatenate([_dot(att_e, vp)[:, :DV_A], _dot(att_o, vp)[:, DV_A:]], axis=1)
            for c in range(CB):
                rows = slice(c * C, (c + 1) * C)
                ds = _dot_tn(vp[rows], k_end[rows, sl])
                ds_ref[g, c, p] = jnp.where(blockdiag, ds, 0.0)

    for c in range(CB):
        rows = slice(c * C, (c + 1) * C)
        for g in range(G):
            for p in range(pairs):
                sl = slice(p * LANES, (p + 1) * LANES)
                vs = slice(p * 2 * DV_A, (p + 1) * 2 * DV_A)
                st = st_ref[g, p]
                oi_ref[g, rows, vs] += _dot_nt(qd_ref[g, rows, sl], st.astype(bf16))
                st_ref[g, p] = dec_ref[g, c][:, sl] * st + ds_ref[g, c, p]

    for g in range(G):
        g_ref = seq_refs[g][3]
        for h in range(H_A):
            hs = slice(h * DV_A, (h + 1) * DV_A)
            o_ref[g, :, hs] = (_rms(oi_ref[g, :, hs], gn) * _silu(g_ref[:, hs].astype(f32))).astype(o_ref.dtype)

    @pl.when(s == pl.num_programs(1) - 1)
    def _():
        sfin_ref[...] = st_ref[...]


def _state_to_kernel_layout(s0):
    bsz = s0.shape[0]
    st = jnp.swapaxes(s0, -1, -2).reshape(bsz, H_A // 2, 2, DV_A, DK_A)
    z = jnp.zeros_like(st[:, :, 0])
    top = jnp.concatenate([st[:, :, 0], z], axis=-1)
    bot = jnp.concatenate([z, st[:, :, 1]], axis=-1)
    return jnp.concatenate([top, bot], axis=-2)


def _state_from_kernel_layout(st):
    bsz = st.shape[0]
    e = st[:, :, :DV_A, :DK_A]
    o = st[:, :, DV_A:, DK_A:]
    s = jnp.stack([e, o], axis=2).reshape(bsz, H_A, DV_A, DK_A)
    return jnp.swapaxes(s, -1, -2)


def _gla(z, w_lr, b_lr, g_norm, s0, *, bsz, t, row0, C, CB, G=2):
    rb = C * CB
    ns = t // rb
    r0 = row0 // rb
    assert bsz % G == 0

    def zspec(g, width, cb):
        return pl.BlockSpec((rb, width), lambda i, s: (r0 + (i * G + g) * ns + s, cb))

    hk = H_A * DK_A
    hv = H_A * DV_A
    st0 = _state_to_kernel_layout(s0.astype(f32))
    wl = jnp.zeros((LANES, hk), f32).at[:GLA_RANK].set(w_lr).astype(bf16)
    st_shape = (G, H_A // 2, 2 * DV_A, 2 * DK_A)
    seq_specs = []
    for g in range(G):
        seq_specs += [zspec(g, hk, _QA0 // hk), zspec(g, hk, _KA0 // hk), zspec(g, hv, _VA0 // hv),
                      zspec(g, hv, _GA0 // hv), zspec(g, LANES, _LA0 // LANES)]
    o, sfin = pl.pallas_call(
        functools.partial(_gla_kernel, C=C, CB=CB, G=G),
        grid=(bsz // G, ns),
        in_specs=seq_specs + [
            pl.BlockSpec((LANES, hk), lambda i, s: (0, 0)),
            pl.BlockSpec((1, hk), lambda i, s: (0, 0)),
            pl.BlockSpec((1, DV_A), lambda i, s: (0, 0)),
            pl.BlockSpec(st_shape, lambda i, s: (i, 0, 0, 0))],
        out_specs=[pl.BlockSpec((G, rb, hv), lambda i, s: (i, s, 0)),
                   pl.BlockSpec(st_shape, lambda i, s: (i, 0, 0, 0))],
        out_shape=[jax.ShapeDtypeStruct((bsz, t, hv), bf16),
                   jax.ShapeDtypeStruct((bsz,) + st_shape[1:], f32)],
        scratch_shapes=[pltpu.VMEM(st_shape, f32),
                        pltpu.VMEM((G, rb, hk), bf16),
                        pltpu.VMEM((G, rb, hv), f32),
                        pltpu.VMEM((G, CB) + st_shape[1:], f32),
                        pltpu.VMEM((G, CB, 1, hk), f32)],
        compiler_params=_cp(("arbitrary", "arbitrary")), name="gla")(
            *([z] * (5 * G)), wl, b_lr.reshape(1, hk).astype(f32), g_norm.reshape(1, DV_A).astype(f32), st0)
    return o.reshape(bsz * t, hv), _state_from_kernel_layout(sfin)


def _band_kernel(q_ref, ka_ref, kb_ref, kc_ref, va_ref, vb_ref, vc_ref, ba_ref, bb_ref, bc_ref,
                 o_ref, *, mask_first):
    qi = pl.program_id(1)
    for h in range(H_B):
        sl = slice(h * DH_B, (h + 1) * DH_B)
        qh = (q_ref[:, sl].astype(f32) * (DH_B ** -0.5)).astype(bf16)
        sa = _dot_nt(qh, ka_ref[:, sl].astype(bf16)) + ba_ref[h]
        sb = _dot_nt(qh, kb_ref[:, sl].astype(bf16)) + bb_ref[h]
        sc = _dot_nt(qh, kc_ref[:, sl].astype(bf16)) + bc_ref[h]
        if mask_first:
            sa = jnp.where(qi >= 2, sa, NEG_INF)
            sb = jnp.where(qi >= 1, sb, NEG_INF)
        m = jnp.maximum(jnp.maximum(jnp.max(sa, axis=-1, keepdims=True),
                                    jnp.max(sb, axis=-1, keepdims=True)),
                        jnp.max(sc, axis=-1, keepdims=True))
        ea = jnp.exp(sa - m)
        eb = jnp.exp(sb - m)
        ec = jnp.exp(sc - m)
        inv = 1.0 / (jnp.sum(ea, axis=-1, keepdims=True) + jnp.sum(eb, axis=-1, keepdims=True)
                     + jnp.sum(ec, axis=-1, keepdims=True))
        o = (_dot((ea * inv).astype(bf16), va_ref[:, sl].astype(bf16))
             + _dot((eb * inv).astype(bf16), vb_ref[:, sl].astype(bf16))
             + _dot((ec * inv).astype(bf16), vc_ref[:, sl].astype(bf16)))
        o_ref[:, sl] = o.astype(o_ref.dtype)


def _band_call(qkv_specs, args, biases, *, bsz, nq, qr, mask_first):
    hd = H_B * DH_B
    bias_specs = [pl.BlockSpec(b.shape, lambda b_, q_: (0, 0, 0)) for b in biases]
    return pl.pallas_call(
        functools.partial(_band_kernel, mask_first=mask_first),
        grid=(bsz, nq),
        in_specs=list(qkv_specs) + bias_specs,
        out_specs=pl.BlockSpec((qr, hd), lambda b, q: (b * nq + q, 0)),
        out_shape=jax.ShapeDtypeStruct((bsz * nq * qr, hd), bf16),
        compiler_params=_cp(("arbitrary", "arbitrary")), name="band")(*args, *biases)


def _rel_bias_matrix(tab, nq, nk, off):
    m = nq + nk
    j = np.arange(m)
    diff = np.where(j < nk, -j, m - j)
    idx = np.clip(diff + off, -REL_CLIP, REL_CLIP) + REL_CLIP
    u = tab.astype(f32)[:, jnp.asarray(idx)]
    h = tab.shape[0]
    return jnp.tile(u, (1, nq))[:, :nq * (m - 1)].reshape(h, nq, m - 1)[:, :, :nk]


def _band_bias_prompt(tab, qr, kb):
    tot = 2 * kb + qr
    qc = (np.arange(qr)[:, None] + 2 * kb) // CHUNK
    kc = np.arange(tot)[None, :] // CHUNK
    valid = (kc <= qc) & (kc >= qc - B_PREV)
    bias = jnp.where(jnp.asarray(valid)[None], _rel_bias_matrix(tab, qr, tot, 2 * kb), NEG_INF)
    return bias[:, :, :kb], bias[:, :, kb:2 * kb], bias[:, :, 2 * kb:]


def _band_bias_sample(tab, t, lc):
    bias = _rel_bias_matrix(tab, t, lc + t, lc)
    return bias[:, :, :lc // 2], bias[:, :, lc // 2:lc], bias[:, :, lc:]


def _swa_kernel(sink_ref, q_ref, kp_ref, kc_ref, vp_ref, vc_ref, o_ref, qs_ref, *, chunk_mask):
    qi = pl.program_id(1)
    qr = q_ref.shape[0]
    pr = kp_ref.shape[0]
    nl = G_C * LANES
    even = lax.broadcasted_iota(jnp.int32, (qr, LANES), 1) < DH_C
    top = lax.broadcasted_iota(jnp.int32, (LANES, LANES), 0) < DH_C
    if chunk_mask:
        kch = lax.broadcasted_iota(jnp.int32, (pr, nl), 0) // CHUNK
        qch = (lax.broadcasted_iota(jnp.int32, (pr, nl), 1) % LANES) // CHUNK
        ok_prev = (kch >= qch) & (qi >= 1)
        kch = lax.broadcasted_iota(jnp.int32, (qr, nl), 0) // CHUNK
        qch = (lax.broadcasted_iota(jnp.int32, (qr, nl), 1) % LANES) // CHUNK
        ok_cur = kch <= qch
    if qr < LANES:
        qs_ref[...] = jnp.zeros_like(qs_ref)

    def twice(ref, c):
        a = ref[:, c * DH_C:(c + 1) * DH_C].astype(bf16)
        return jnp.concatenate([a, a], axis=1)

    for c in range(KV_C):
        for p in range(G_C // 2):
            cols = slice(c * G_C * DH_C + p * LANES, c * G_C * DH_C + (p + 1) * LANES)
            qp = (q_ref[:, cols].astype(f32) * (DH_C ** -0.5)).astype(bf16)
            zero = jnp.zeros_like(qp)
            qs_ref[2 * p * LANES:2 * p * LANES + qr, :] = jnp.where(even, qp, zero)
            qs_ref[(2 * p + 1) * LANES:(2 * p + 1) * LANES + qr, :] = jnp.where(even, zero, qp)
        q_all = qs_ref[...]
        sink = jnp.concatenate(
            [jnp.full((1, LANES), sink_ref[c * G_C + j], f32) for j in range(G_C)], axis=1)
        st_prev = _dot_nt(twice(kp_ref, c), q_all)
        st_cur = _dot_nt(twice(kc_ref, c), q_all)
        if chunk_mask:
            st_prev = jnp.where(ok_prev, st_prev, NEG_INF)
            st_cur = jnp.where(ok_cur, st_cur, NEG_INF)
        m = jnp.maximum(jnp.maximum(jnp.max(st_prev, axis=0, keepdims=True),
                                    jnp.max(st_cur, axis=0, keepdims=True)), sink)
        e_prev = jnp.exp(st_prev - m)
        e_cur = jnp.exp(st_cur - m)
        inv = 1.0 / (jnp.sum(e_prev, axis=0, keepdims=True) + jnp.sum(e_cur, axis=0, keepdims=True)
                     + jnp.exp(sink - m))
        ot = (_dot_tn(twice(vp_ref, c), (e_prev * inv).astype(bf16))
              + _dot_tn(twice(vc_ref, c), (e_cur * inv).astype(bf16)))
        for p in range(G_C // 2):
            cols = slice(c * G_C * DH_C + p * LANES, c * G_C * DH_C + (p + 1) * LANES)
            blk = jnp.where(top, ot[:, 2 * p * LANES:(2 * p + 1) * LANES],
                            ot[:, (2 * p + 1) * LANES:(2 * p + 2) * LANES])
            o_ref[:, cols] = blk.T[:qr].astype(o_ref.dtype)


def _swa_call(specs, args, sinks, *, bsz, nq, qr, chunk_mask):
    hd = H_C * DH_C
    return pl.pallas_call(
        functools.partial(_swa_kernel, chunk_mask=chunk_mask),
        grid=(bsz, nq),
        in_specs=[pl.BlockSpec(memory_space=pltpu.SMEM)] + list(specs),
        out_specs=pl.BlockSpec((qr, hd), lambda b, q: (b * nq + q, 0)),
        out_shape=jax.ShapeDtypeStruct((bsz * nq * qr, hd), bf16),
        scratch_shapes=[pltpu.VMEM((G_C * LANES, LANES), bf16)],
        compiler_params=_cp(("arbitrary", "arbitrary")), name="swa")(sinks, *args)


def _out_kernel(*refs, n_parts, n_first, split_y):
    ny = 2 if split_y else 1
    y_refs = refs[:ny]
    refs = refs[ny - 1:]
    o_refs = refs[1:1 + 2 * n_parts]
    w_refs = refs[1 + 2 * n_parts:1 + 3 * n_parts]
    g_ref, rh_ref, rl_ref = refs[1 + 3 * n_parts:4 + 3 * n_parts]
    y1_ref, h2_ref, route_ref, cnt_ref, run_ref, lg_ref = refs[4 + 3 * n_parts:]
    step = pl.program_id(0)
    i = jnp.minimum(step, pl.num_programs(0) - 2)

    @pl.when(step == 0)
    def _():
        run_ref[...] = jnp.zeros_like(run_ref)
        lg_ref[...] = jnp.zeros_like(lg_ref)

    lg = lg_ref[...]
    live = (step >= 1).astype(f32)

    acc = y_refs[0][...]
    if split_y:
        acc = jnp.where(i < n_first, acc, y_refs[1][...])
    for k in range(n_parts):
        lhs = jnp.where(i < n_first, o_refs[2 * k][...], o_refs[2 * k + 1][...])
        acc = acc + _dot(lhs, w_refs[k][...])
    y1_ref[...] = acc
    h2 = _rms(acc, g_ref[...])
    hi = h2.astype(bf16)
    h2_ref[...] = hi.astype(f32)
    lo = (h2 - hi.astype(f32)).astype(bf16)
    lg_ref[...] = _dot(hi, rh_ref[...]) + _dot(hi, rl_ref[...]) + _dot(lo, rh_ref[...])

    lane_i = lax.broadcasted_iota(jnp.int32, lg.shape, 1)
    lane = lane_i.astype(f32)
    big = float(LANES)
    ninf = -jnp.inf
    is_g = lane_i < N_GROUPS
    gmax = jnp.max(jnp.where(is_g, lg, ninf), axis=-1, keepdims=True)
    gsel = jnp.min(jnp.where(is_g & (lg == gmax), lane, big), axis=-1, keepdims=True)
    pg = 1.0 / jnp.sum(jnp.where(is_g, jnp.exp(lg - gmax), 0.0), axis=-1, keepdims=True)
    e0 = N_GROUPS + EXP_PER_GROUP * gsel
    is_e = (lane >= e0) & (lane < e0 + EXP_PER_GROUP)
    v1 = jnp.max(jnp.where(is_e, lg, ninf), axis=-1, keepdims=True)
    i1 = jnp.min(jnp.where(is_e & (lg == v1), lane, big), axis=-1, keepdims=True)
    is_e2 = is_e & (lane != i1)
    v2 = jnp.max(jnp.where(is_e2, lg, ninf), axis=-1, keepdims=True)
    i2 = jnp.min(jnp.where(is_e2 & (lg == v2), lane, big), axis=-1, keepdims=True)
    t = jnp.exp(v2 - v1)
    w1 = pg / (1.0 + t)
    w2 = pg * t / (1.0 + t)

    tmr = lg.shape[0]
    oh1 = jnp.where(lane == i1, 1.0, 0.0)
    oh2 = jnp.where(lane == i2, 1.0, 0.0)
    before = jnp.where(lax.broadcasted_iota(jnp.int32, (tmr, tmr), 0)
                       > lax.broadcasted_iota(jnp.int32, (tmr, tmr), 1), 1.0, 0.0).astype(bf16)
    tot1 = jnp.sum(oh1, axis=0, keepdims=True)
    tot2 = jnp.sum(oh2, axis=0, keepdims=True)
    base = run_ref[...]
    r1 = jnp.sum(oh1 * (base + _dot(before, oh1.astype(bf16))), axis=-1, keepdims=True)
    r2 = jnp.sum(oh2 * (base + tot1 + _dot(before, oh2.astype(bf16))), axis=-1, keepdims=True)
    run_ref[...] = base + live * (tot1 + tot2)
    cnt_ref[...] = base + live * (tot1 + tot2)
    vals = (i1 - N_GROUPS, i2 - N_GROUPS, w1, w2, r1, r2)
    out = jnp.zeros_like(lg)
    for k, v in enumerate(vals):
        out = jnp.where(lane_i == k, v, out)
    route_ref[...] = out


def _out_proj(y, parts, g, w_router, *, tm):
    n_first = parts[0][0].shape[0] // tm
    split_y = isinstance(y, tuple)
    n = y[0].shape[0] + y[1].shape[0] if split_y else y.shape[0]
    n_tiles = n // tm
    tile = lambda s: jnp.minimum(s, n_tiles - 1)
    first = lambda s: (jnp.minimum(s, n_first - 1), 0)
    second = lambda s: (jnp.clip(s - n_first, 0, n_tiles - n_first - 1), 0)
    if split_y:
        d = y[0].shape[1]
        in_specs = [pl.BlockSpec((tm, d), first), pl.BlockSpec((tm, d), second)]
        args = list(y)
    else:
        d = y.shape[1]
        in_specs = [pl.BlockSpec((tm, d), lambda s: (tile(s), 0))]
        args = [y]
    for op, os_, _ in parts:
        dk = op.shape[1]
        in_specs.append(pl.BlockSpec((tm, dk), first))
        in_specs.append(pl.BlockSpec((tm, dk), second))
        args += [op, os_]
    for _, _, w in parts:
        in_specs.append(pl.BlockSpec(w.shape, lambda i: (0, 0)))
        args.append(w)
    rh = w_router.astype(bf16)
    rl = (w_router - rh.astype(f32)).astype(bf16)
    in_specs += [pl.BlockSpec((1, d), lambda i: (0, 0)),
                 pl.BlockSpec((d, LANES), lambda i: (0, 0)),
                 pl.BlockSpec((d, LANES), lambda i: (0, 0))]
    args += [g.reshape(1, d), rh, rl]
    return pl.pallas_call(
        functools.partial(_out_kernel, n_parts=len(parts), n_first=n_first, split_y=split_y),
        grid=(n_tiles + 1,), in_specs=in_specs,
        out_specs=[pl.BlockSpec((tm, d), lambda s: (tile(s), 0)),
                   pl.BlockSpec((tm, d), lambda s: (tile(s), 0)),
                   pl.BlockSpec((tm, LANES), lambda s: (jnp.maximum(s - 1, 0), 0)),
                   pl.BlockSpec((1, LANES), lambda s: (0, 0))],
        out_shape=[jax.ShapeDtypeStruct((n, d), f32),
                   jax.ShapeDtypeStruct((n, d), f32),
                   jax.ShapeDtypeStruct((n, LANES), f32),
                   jax.ShapeDtypeStruct((1, LANES), f32)],
        scratch_shapes=[pltpu.VMEM((1, LANES), f32), pltpu.VMEM((tm, LANES), f32)],
        compiler_params=_cp(("arbitrary",)), name="out_proj")(*args)


def _moe_kernel(te_ref, tv_ref, pos_ref, plo_ref, phi_ref, en_ref, es_ref,
                h_ref, wg_ref, wu_ref, wd_ref, out_ref,
                slot_ref, tok_ref, x0, x1, x2, y0, y1, y2, gsem, ssem, wg_f, wu_f, wd_f, wsem,
                wg_s, wu_s, wd_s, *, tm, n):
    t = pl.program_id(0)
    nt = pl.num_programs(0)
    n2 = 2 * n
    xs = (x0, x1, x2)
    ys = (y0, y1, y2)
    depth = len(xs)

    def weights_start(e, slot):
        for src, dst in ((wg_ref, wg_f), (wu_ref, wu_f), (wd_ref, wd_f)):
            pltpu.make_async_copy(src.at[e], dst.at[slot], wsem.at[slot]).start()

    def weights_wait(slot):
        for src, dst in ((wg_ref, wg_f), (wu_ref, wu_f), (wd_ref, wd_f)):
            pltpu.make_async_copy(src.at[0], dst.at[slot], wsem.at[slot]).wait()

    def tile_valid(tile):
        return (tile >= 0) & (tile < nt) & (tv_ref[jnp.clip(tile, 0, nt - 1)] != 0)

    valid = tile_valid(t)

    def gather_row(tile, r, p):
        tok = tok_ref[tile * tm + r]
        pltpu.make_async_copy(h_ref.at[pl.ds(tok, 1)], xs[p].at[pl.ds(r, 1)], gsem.at[p]).start()

    def scatter_row(tile, r, p):
        s = slot_ref[tile * tm + r]
        pltpu.make_async_copy(ys[p].at[pl.ds(r, 1)], out_ref.at[pl.ds(s, 1)], ssem.at[p]).start()

    def rows_loop(row_fn, tile, p):
        def body(r, carry):
            row_fn(tile, r, p)
            return carry
        lax.fori_loop(0, tm, body, 0, unroll=8)

    def gather_wait(p):
        pltpu.make_async_copy(h_ref.at[pl.ds(0, tm)], xs[p], gsem.at[p]).wait()

    def scatter_wait(p):
        pltpu.make_async_copy(ys[p], out_ref.at[pl.ds(0, tm)], ssem.at[p]).wait()

    def compute(p):
        x = xs[p][...].astype(bf16)
        g = _dot(x, wg_s[...])
        u = _dot(x, wu_s[...])
        ys[p][...] = _dot((_silu(g) * u).astype(bf16), wd_s[...])

    @pl.when(t == 0)
    def _():
        def per_expert(e, nxt):
            def body(r, v):
                slot_ref[r] = v
                tok_ref[r] = v - n2
                return v + 1
            return lax.fori_loop(plo_ref[e], phi_ref[e], body, nxt)
        lax.fori_loop(0, N_EXPERTS, per_expert, n2)

        def fill(choice):
            def body(tok, carry):
                s = choice * n + tok
                row = pos_ref[s]
                slot_ref[row] = s
                tok_ref[row] = tok
                return carry
            lax.fori_loop(0, n, body, 0, unroll=8)
        fill(0)
        fill(1)
        weights_start(te_ref[0], 0)
        rows_loop(gather_row, 0, 0)

        @pl.when(tile_valid(1))
        def _():
            rows_loop(gather_row, 1, 1)

    def step(p):
        q = (p + 2) % depth

        @pl.when(t >= depth)
        def _():
            scatter_wait(p)

        @pl.when(valid)
        def _():
            gather_wait(p)

            @pl.when((t == 0) | (te_ref[t] != te_ref[jnp.maximum(t - 1, 0)]))
            def _():
                slot = es_ref[t]
                weights_wait(slot)
                wg_s[...] = wg_f[slot].astype(bf16)
                wu_s[...] = wu_f[slot].astype(bf16)
                wd_s[...] = wd_f[slot].astype(bf16)

                @pl.when(en_ref[t] >= 0)
                def _():
                    weights_start(en_ref[t], 1 - slot)

            steady = tile_valid(t + 2) & (t >= 1)

            @pl.when(steady)
            def _():
                for r in range(tm):
                    gather_row(t + 2, r, q)
                    scatter_row(t - 1, r, q)
                compute(p)

            @pl.when(jnp.logical_not(steady))
            def _():
                @pl.when(tile_valid(t + 2))
                def _():
                    rows_loop(gather_row, t + 2, q)

                @pl.when(t >= 1)
                def _():
                    rows_loop(scatter_row, t - 1, q)
                compute(p)

        @pl.when(jnp.logical_not(valid))
        def _():
            @pl.when(tile_valid(t - 1))
            def _():
                rows_loop(scatter_row, t - 1, q)
            ys[p][...] = jnp.zeros(ys[p].shape, f32)
            pltpu.make_async_copy(ys[p], out_ref.at[pl.ds(pl.multiple_of(t * tm, tm), tm)], ssem.at[p]).start()

        @pl.when(t == nt - 1)
        def _():
            @pl.when(valid)
            def _():
                rows_loop(scatter_row, t, p)
            for k in range(depth):
                scatter_wait(k)

    for p in range(depth):
        pl.when(t % depth == p)(functools.partial(step, p))


def _moe_experts(h2, pos, tile_e, tile_v, pad_lo, pad_hi, wg, wu, wd, *, layer, tm, n_tiles):
    n, d = h2.shape
    ff = wg.shape[2]
    assert n >= N_EXPERTS * tm, "padding rows read tokens 0 .. N_EXPERTS * (tm - 1)"
    assert n_tiles >= 3
    first = jnp.concatenate([jnp.ones((1,), jnp.int32), (tile_e[1:] != tile_e[:-1]).astype(jnp.int32)])
    seq = jnp.cumsum(first * tile_v) - 1
    seq = jnp.where(tile_v != 0, seq, n_tiles)
    nxt_pos = jnp.sum((seq[None, :] <= seq[:, None]).astype(jnp.int32), axis=1)
    has_next = (nxt_pos < n_tiles) & (tile_v[jnp.minimum(nxt_pos, n_tiles - 1)] != 0) & (tile_v != 0)
    tile_e = tile_e + layer * N_EXPERTS
    e_next = jnp.where(has_next, tile_e[jnp.minimum(nxt_pos, n_tiles - 1)], -1).astype(jnp.int32)
    e_slot = (seq % 2).astype(jnp.int32)
    row_buf = pltpu.VMEM((tm, d), f32)
    any_spec = pl.BlockSpec(memory_space=pl.ANY)
    grid_spec = pltpu.PrefetchScalarGridSpec(
        num_scalar_prefetch=7, grid=(n_tiles,),
        in_specs=[any_spec, any_spec, any_spec, any_spec],
        out_specs=any_spec,
        scratch_shapes=[pltpu.SMEM((n_tiles * tm,), jnp.int32), pltpu.SMEM((n_tiles * tm,), jnp.int32),
                        row_buf, row_buf, row_buf, row_buf, row_buf, row_buf,
                        pltpu.SemaphoreType.DMA((3,)), pltpu.SemaphoreType.DMA((3,)),
                        pltpu.VMEM((2, d, ff), f32), pltpu.VMEM((2, d, ff), f32), pltpu.VMEM((2, ff, d), f32),
                        pltpu.SemaphoreType.DMA((2,)),
                        pltpu.VMEM((d, ff), bf16), pltpu.VMEM((d, ff), bf16), pltpu.VMEM((ff, d), bf16)])
    return pl.pallas_call(
        functools.partial(_moe_kernel, tm=tm, n=n), grid_spec=grid_spec,
        out_shape=jax.ShapeDtypeStruct((n_tiles * tm, d), f32),
        compiler_params=_cp(("arbitrary",)), name="moe")(
            tile_e, tile_v, pos, pad_lo, pad_hi, e_next, e_slot, h2, wg, wu, wd)


def _moe(h2, route, counts, wg, wu, wd, *, layer, tm):
    n = route.shape[0]
    n_tiles = (2 * n) // tm + N_EXPERTS
    counts = counts[0, N_GROUPS:N_GROUPS + N_EXPERTS].astype(jnp.int32)
    padded = ((counts + tm - 1) // tm) * tm
    ends = jnp.cumsum(padded)
    starts = ends - padded
    e = route[:, 0:2].astype(jnp.int32)
    rank = route[:, 4:6].astype(jnp.int32)
    pos = (starts[e] + rank).T.reshape(-1)
    tile_start = jnp.arange(n_tiles, dtype=jnp.int32) * tm
    tile_e = jnp.sum((tile_start[:, None] >= ends[None, :]).astype(jnp.int32), axis=1)
    tile_v = (tile_start < ends[-1]).astype(jnp.int32)
    tile_e = jnp.minimum(tile_e, N_EXPERTS - 1)
    return _moe_experts(h2, pos, tile_e, tile_v, starts + counts, ends, wg, wu, wd, layer=layer, tm=tm,
                        n_tiles=n_tiles)


def _final_kernel(y_ref, m0_ref, m1_ref, rt_ref, g_ref, o_ref):
    o_ref[...] = _rms(y_ref[...] + _moe_combine(m0_ref, m1_ref, rt_ref), g_ref[...])


def _final(y, moe_out, route, g, *, tm, row0, rows):
    n, d = y.shape
    r0 = row0 // tm
    r1 = (n + row0) // tm
    return pl.pallas_call(
        _final_kernel, grid=(rows // tm,),
        in_specs=[pl.BlockSpec((tm, d), lambda i: (r0 + i, 0)),
                  pl.BlockSpec((tm, d), lambda i: (r0 + i, 0)),
                  pl.BlockSpec((tm, d), lambda i: (r1 + i, 0)),
                  pl.BlockSpec((tm, LANES), lambda i: (r0 + i, 0)),
                  pl.BlockSpec((1, d), lambda i: (0, 0))],
        out_specs=pl.BlockSpec((tm, d), lambda i: (i, 0)),
        out_shape=jax.ShapeDtypeStruct((rows, d), f32),
        compiler_params=_cp(("arbitrary",)), name="final_norm")(y, moe_out, moe_out, route, g.reshape(1, d))


def _rope_tables(pos):
    half = ROT_DIM // 2
    inv = jnp.power(ROPE_THETA, -jnp.arange(half, dtype=f32) * 2.0 / ROT_DIM)
    ang = pos.astype(f32)[:, None] * inv[None, :]
    cos, sin = lax.optimization_barrier((jnp.cos(ang), jnp.sin(ang)))
    n = pos.shape[0]
    one = jnp.ones((n, DH_C - ROT_DIM), f32)
    zero = jnp.zeros((n, DH_C - ROT_DIM), f32)
    zh = jnp.zeros((n, half), f32)
    c = jnp.concatenate([cos, cos, one], axis=1)
    s1 = jnp.concatenate([zh, sin, zero], axis=1)
    s2 = jnp.concatenate([-sin, zh, zero], axis=1)
    rep = LANES // DH_C
    return tuple(jnp.tile(t, (1, rep)) for t in (c, s1, s2))


def kernel(x_prompt, x_sample, cache_gla_state, cache_band_k, cache_band_v, cache_swa_k, cache_swa_v,
           norm_mix, norm_ffn, norm_final, w_in_ab, w_gla_lr, b_gla_lr, gla_out_norm, rel_bias_tab,
           w_out_ab, w_in_c, b_in_c, attn_sinks, w_out_c, w_router_grp, w_router_exp, w_exp_gate,
           w_exp_up, w_exp_down):
    bp, sp, d = x_prompt.shape
    bs, ts, _ = x_sample.shape
    n_p = bp * sp
    n_s = bs * ts
    n = n_p + n_s
    tm = 512 if n_s % 512 == 0 else n_s
    tmo = min(tm, 256)
    tme = 256 if (2 * n) % 256 == 0 else 64
    ff = w_exp_gate.shape[-1]

    xp = x_prompt.reshape(n_p, d)
    xs_ = x_sample.reshape(n_s, d)

    def router_w(l):
        wr = jnp.concatenate([w_router_grp[l], w_router_exp[l].reshape(d, N_EXPERTS)], axis=1)
        return jnp.zeros((d, LANES), f32).at[:, :N_GROUPS + N_EXPERTS].set(wr)

    experts = (w_exp_gate.reshape(-1, d, ff), w_exp_up.reshape(-1, d, ff), w_exp_down.reshape(-1, ff, d))

    w0 = w_in_ab[0].astype(bf16)
    la0 = 2 * H_A * DK_A + 2 * H_A * DV_A
    w_cat = jnp.concatenate(
        [w0[:, :la0], w0[:, la0 + GLA_RANK:], w0[:, la0:la0 + GLA_RANK],
         jnp.zeros((d, LANES - GLA_RANK), bf16)], axis=1)
    z = _proj(xp, norm_mix[0], w_cat, tm=tmo, tn=1024, loop_cols=_LA0, yb=xs_)

    c_s = min(CHUNK, ts)
    oa_p, gla_p = _gla(z, w_gla_lr[0], b_gla_lr[0], gla_out_norm[0],
                       jnp.zeros((bp, H_A, DK_A, DV_A), f32), bsz=bp, t=sp, row0=0, C=CHUNK, CB=4)
    oa_s, gla_s = _gla(z, w_gla_lr[0], b_gla_lr[0], gla_out_norm[0], cache_gla_state[0],
                       bsz=bs, t=ts, row0=n_p, C=c_s, CB=ts // c_s)

    hd = H_B * DH_B
    qr = 256
    nq = sp // qr
    qcb, kcb, vcb = _QB0 // hd, _KB0 // hd, _VB0 // hd
    cur = lambda cb: pl.BlockSpec((qr, hd), lambda b, q: (b * nq + q, cb))
    prev = lambda cb, back: pl.BlockSpec((qr, hd), lambda b, q: (b * nq + jnp.maximum(q - back, 0), cb))
    ob_p = _band_call(
        [cur(qcb), prev(kcb, 2), prev(kcb, 1), cur(kcb), prev(vcb, 2), prev(vcb, 1), cur(vcb)],
        [z] * 7, _band_bias_prompt(rel_bias_tab[0], qr, qr), bsz=bp, nq=nq, qr=qr, mask_first=True)

    lb = cache_band_k.shape[2]
    ck = cache_band_k[0].reshape(bs * lb, hd)
    cv = cache_band_v[0].reshape(bs * lb, hd)
    rs0 = n_p // ts
    new = lambda cb: pl.BlockSpec((ts, hd), lambda b, q: (rs0 + b, cb))
    cpiece = lambda k: pl.BlockSpec((lb // 2, hd), lambda b, q: (2 * b + k, 0))
    ob_s = _band_call(
        [new(qcb), cpiece(0), cpiece(1), new(kcb), cpiece(0), cpiece(1), new(vcb)],
        [z, ck, ck, z, cv, cv, z], _band_bias_sample(rel_bias_tab[0], ts, lb),
        bsz=bs, nq=1, qr=ts, mask_first=False)

    dmix_a = H_A * DV_A
    wo = w_out_ab[0].astype(bf16)
    y1, h2, route, cnt = _out_proj(
        (xp, xs_), [(oa_p, oa_s, wo[:dmix_a]), (ob_p, ob_s, wo[dmix_a:])], norm_ffn[0], router_w(0),
        tm=tmo)
    moe0 = _moe(h2, route, cnt, *experts, layer=0, tm=tme)

    keep = min(B_PREV * CHUNK, sp)

    def prompt_tail(a, rows, c0, width):
        return jnp.stack([a[(b + 1) * sp - rows:(b + 1) * sp, c0:c0 + width] for b in range(bp)])

    band_k_p = prompt_tail(z, keep, _KB0, hd).astype(f32).reshape(bp, keep, H_B, DH_B)[None]
    band_v_p = prompt_tail(z, keep, _VB0, hd).astype(f32).reshape(bp, keep, H_B, DH_B)[None]
    kb_s = z[n_p:, _KB0:_KB0 + hd].astype(f32).reshape(bs, ts, H_B, DH_B)
    vb_s = z[n_p:, _VB0:_VB0 + hd].astype(f32).reshape(bs, ts, H_B, DH_B)
    band_k_s = jnp.concatenate([cache_band_k[0][:, ts:], kb_s], axis=1)[None]
    band_v_s = jnp.concatenate([cache_band_v[0][:, ts:], vb_s], axis=1)[None]

    nq_c = H_C * DH_C
    nkv = KV_C * DH_C
    uniq = _rope_tables(jnp.concatenate([jnp.arange(sp), PAST_LEN + jnp.arange(ts)]))
    rope = tuple(jnp.concatenate([jnp.tile(u[:sp], (bp, 1)), jnp.tile(u[sp:], (bs, 1))]) for u in uniq)
    y2, z1 = _proj(y1, norm_mix[1], w_in_c[0].astype(bf16), tm=tmo, tn=512, loop_cols=nq_c,
                   add=(moe0, route), bias=b_in_c[0], rope=rope, rope_cols=nq_c + nkv)

    qr_c = WINDOW
    nq1 = sp // qr_c
    kw = nkv
    kcb1, vcb1 = _KC0 // kw, _VC0 // kw
    qspec = pl.BlockSpec((qr_c, nq_c), lambda b, q: (b * nq1 + q, 0))
    cur1 = lambda cb: pl.BlockSpec((qr_c, kw), lambda b, q: (b * nq1 + q, cb))
    prev1 = lambda cb: pl.BlockSpec((qr_c, kw), lambda b, q: (b * nq1 + jnp.maximum(q - 1, 0), cb))
    sinks = attn_sinks[0].astype(f32)
    oc_p = _swa_call([qspec, prev1(kcb1), cur1(kcb1), prev1(vcb1), cur1(vcb1)], [z1] * 5, sinks,
                     bsz=bp, nq=nq1, qr=qr_c, chunk_mask=True)

    lc = cache_swa_k.shape[2]
    csk = cache_swa_k[0].reshape(bs * lc, nkv)
    csv = cache_swa_v[0].reshape(bs * lc, nkv)
    qspec_s = pl.BlockSpec((ts, nq_c), lambda b, q: (rs0 + b, 0))
    new1 = lambda cb: pl.BlockSpec((ts, kw), lambda b, q: (rs0 + b, cb))
    cspec = pl.BlockSpec((lc, kw), lambda b, q: (b, 0))
    oc_s = _swa_call([qspec_s, cspec, new1(kcb1), cspec, new1(vcb1)], [z1, csk, z1, csv, z1], sinks,
                     bsz=bs, nq=1, qr=ts, chunk_mask=False)

    y3, h4, route1, cnt1 = _out_proj(y2, [(oc_p, oc_s, w_out_c[0].astype(bf16))], norm_ffn[1],
                                     router_w(1), tm=tmo)
    moe1 = _moe(h4, route1, cnt1, *experts, layer=1, tm=tme)

    keep1 = min(WINDOW, sp)

    swa_k_p = prompt_tail(z1, keep1, _KC0, kw).astype(f32).reshape(bp, keep1, KV_C, DH_C)[None]
    swa_v_p = prompt_tail(z1, keep1, _VC0, kw).astype(f32).reshape(bp, keep1, KV_C, DH_C)[None]
    k1_s = z1[n_p:, _KC0:_KC0 + kw].astype(f32).reshape(bs, ts, KV_C, DH_C)
    v1_s = z1[n_p:, _VC0:_VC0 + kw].astype(f32).reshape(bs, ts, KV_C, DH_C)
    swa_k_s = jnp.concatenate([cache_swa_k[0][:, ts:], k1_s], axis=1)[None]
    swa_v_s = jnp.concatenate([cache_swa_v[0][:, ts:], v1_s], axis=1)[None]

    y_prompt = _final(y3, moe1, route1, norm_final, tm=tm, row0=0, rows=n_p).reshape(bp, sp, d)
    y_sample = _final(y3, moe1, route1, norm_final, tm=tm, row0=n_p, rows=n_s).reshape(bs, ts, d)

    return (y_prompt, y_sample, gla_p[None], gla_s[None].astype(cache_gla_state.dtype),
            band_k_p, band_v_p, band_k_s, band_v_s, swa_k_p, swa_v_p, swa_k_s, swa_v_s)
```

```python
import functools

import numpy as np
import jax
import jax.numpy as jnp
from jax import lax
from jax.experimental import pallas as pl
from jax.experimental.pallas import tpu as pltpu

f32 = jnp.float32
bf16 = jnp.bfloat16

CHUNK = 64
EPS = 1e-6
NEG_INF = -1e30
PAST_LEN = 1024
H_A = 8
DK_A = 64
DV_A = 128
GLA_RANK = 16
GLA_TAU = 16.0
H_B = 8
DH_B = 128
B_PREV = 8
REL_CLIP = 128
H_C = 32
KV_C = 4
G_C = H_C // KV_C
DH_C = 64
WINDOW = 128
ROT_DIM = DH_C // 4
ROPE_THETA = 500000.0
N_GROUPS = 4
EXP_PER_GROUP = 8
N_EXPERTS = N_GROUPS * EXP_PER_GROUP

LANES = 128
VMEM_LIMIT = 52 * 1024 * 1024

_QA0, _KA0, _VA0, _GA0 = 0, 512, 1024, 2048
_QB0, _KB0, _VB0, _LA0 = 3072, 4096, 5120, 6144
_NC_AB = 6272
_QC0, _KC0, _VC0 = 0, 2048, 2304
_NC_C = 2560


def _cp(sem):
    return pltpu.CompilerParams(dimension_semantics=sem, vmem_limit_bytes=VMEM_LIMIT)


def _dot(a, b):
    return jnp.dot(a, b, preferred_element_type=f32)


def _dot_nt(a, b):
    return lax.dot_general(a, b, (((1,), (1,)), ((), ())), preferred_element_type=f32)


def _dot_tn(a, b):
    return lax.dot_general(a, b, (((0,), (0,)), ((), ())), preferred_element_type=f32)


def _rms(x, g):
    ms = jnp.mean(x * x, axis=-1, keepdims=True)
    return (x * lax.rsqrt(ms + EPS)) * g


def _silu(x):
    return x / (1.0 + jnp.exp(-x))


def _moe_combine(m0_ref, m1_ref, route_ref):
    rt = route_ref[...]
    return rt[:, 2:3] * m0_ref[...] + rt[:, 3:4] * m1_ref[...]


def _proj_kernel(*refs, n_first, has_add, has_bias, rope_cols, loop_cols, tn):
    it = iter(refs)
    ya_ref = next(it)
    yb_ref = next(it) if n_first is not None else None
    if has_add:
        m0_ref, m1_ref, rt_ref = next(it), next(it), next(it)
    g_ref = next(it)
    w_ref = next(it)
    b_ref = next(it) if has_bias else None
    if rope_cols:
        c_ref, s1_ref, s2_ref = next(it), next(it), next(it)
    ysum_ref = next(it) if has_add else None
    z_ref = next(it)
    h_ref = next(it)
    i = pl.program_id(0)

    x = ya_ref[...]
    if n_first is not None:
        x = jnp.where(i < n_first, x, yb_ref[...])
    if has_add:
        x = x + _moe_combine(m0_ref, m1_ref, rt_ref)
        ysum_ref[...] = x
    h_ref[...] = _rms(x, g_ref[...]).astype(bf16)

    def rope(acc):
        width = acc.shape[1]
        rep = width // LANES
        c = jnp.tile(c_ref[...], (1, rep))
        s1 = jnp.tile(s1_ref[...], (1, rep))
        s2 = jnp.tile(s2_ref[...], (1, rep))
        half = ROT_DIM // 2
        lo = pltpu.roll(acc, half, axis=1)
        hi = pltpu.roll(acc, width - half, axis=1)
        return acc * c + lo * s1 + hi * s2

    def col_tile(j, carry):
        cols = pl.ds(pl.multiple_of(j * tn, tn), tn)
        acc = _dot(h_ref[...], w_ref[:, cols])
        if has_bias:
            acc = acc + b_ref[:, cols]
        if rope_cols:
            acc = rope(acc)
        z_ref[:, cols] = acc.astype(z_ref.dtype)
        return carry

    nc = z_ref.shape[1]
    lax.fori_loop(0, loop_cols // tn, col_tile, 0)
    if nc > loop_cols:
        acc = _dot(h_ref[...], w_ref[:, loop_cols:])
        if has_bias:
            acc = acc + b_ref[:, loop_cols:]
        rt = max(rope_cols - loop_cols, 0)
        if rt:
            z_ref[:, loop_cols:loop_cols + rt] = rope(acc[:, :rt]).astype(z_ref.dtype)
        z_ref[:, loop_cols + rt:] = acc[:, rt:].astype(z_ref.dtype)


def _proj(ya, g, w, *, tm, tn, loop_cols, yb=None, add=None, bias=None, rope=None, rope_cols=0):
    d = ya.shape[1]
    na = ya.shape[0] // tm
    nb = 0 if yb is None else yb.shape[0] // tm
    n = (na + nb) * tm
    nc = w.shape[1]
    assert loop_cols % tn == 0 and loop_cols <= nc and (rope_cols == 0 or rope_cols >= loop_cols)
    once = dict(pipeline_mode=pl.Buffered(1))
    in_specs = []
    args = []
    if yb is None:
        in_specs.append(pl.BlockSpec((tm, d), lambda i: (i, 0)))
        args.append(ya)
    else:
        in_specs.append(pl.BlockSpec((tm, d), lambda i: (jnp.minimum(i, na - 1), 0)))
        in_specs.append(pl.BlockSpec((tm, d), lambda i: (jnp.maximum(i - na, 0), 0)))
        args += [ya, yb]
    if add is not None:
        moe_out, route = add
        in_specs.append(pl.BlockSpec((tm, d), lambda i: (i, 0)))
        in_specs.append(pl.BlockSpec((tm, d), lambda i: (i + na, 0)))
        in_specs.append(pl.BlockSpec((tm, LANES), lambda i: (i, 0)))
        args += [moe_out, moe_out, route]
    in_specs.append(pl.BlockSpec((1, d), lambda i: (0, 0)))
    args.append(g.reshape(1, d))
    in_specs.append(pl.BlockSpec((d, nc), lambda i: (0, 0), **once))
    args.append(w)
    if bias is not None:
        in_specs.append(pl.BlockSpec((1, nc), lambda i: (0, 0)))
        args.append(bias.reshape(1, nc))
    if rope_cols:
        for t in rope:
            in_specs.append(pl.BlockSpec((tm, LANES), lambda i: (i, 0)))
            args.append(t)
    out_shape = []
    out_specs = []
    if add is not None:
        out_shape.append(jax.ShapeDtypeStruct((n, d), f32))
        out_specs.append(pl.BlockSpec((tm, d), lambda i: (i, 0)))
    out_shape.append(jax.ShapeDtypeStruct((n, nc), bf16))
    out_specs.append(pl.BlockSpec((tm, nc), lambda i: (i, 0)))
    kern = functools.partial(_proj_kernel, n_first=(na if yb is not None else None),
                             has_add=add is not None, has_bias=bias is not None, rope_cols=rope_cols,
                             loop_cols=loop_cols, tn=tn)
    out = pl.pallas_call(
        kern, grid=(na + nb,), in_specs=in_specs, out_specs=out_specs, out_shape=out_shape,
        scratch_shapes=[pltpu.VMEM((tm, d), bf16)],
        compiler_params=_cp(("arbitrary",)), name="proj")(*args)
    return out if add is not None else out[0]


def _gla_kernel(*refs, C, CB, G):
    seq_refs = [refs[5 * g:5 * g + 5] for g in range(G)]
    (wlr_ref, blr_ref, gn_ref, s0_ref, o_ref, sfin_ref,
     st_ref, qd_ref, oi_ref, ds_ref, dec_ref) = refs[5 * G:]
    s = pl.program_id(1)
    R = C * CB

    @pl.when(s == 0)
    def _():
        st_ref[...] = s0_ref[...]

    row = lax.broadcasted_iota(jnp.int32, (R, R), 0)
    col = lax.broadcasted_iota(jnp.int32, (R, R), 1)
    same = (row // C) == (col // C)
    tril = same & (row >= col)
    tri = jnp.where(tril, 1.0, 0.0).astype(bf16)
    ones = jnp.where(same, 1.0, 0.0).astype(bf16)
    even = lax.broadcasted_iota(jnp.int32, (R, LANES), 1) < DK_A
    srow = lax.broadcasted_iota(jnp.int32, (2 * DV_A, LANES), 0)
    slane = lax.broadcasted_iota(jnp.int32, (2 * DV_A, LANES), 1)
    blockdiag = (srow < DV_A) == (slane < DK_A)
    wlr = wlr_ref[...]
    blr = blr_ref[...]
    gn = gn_ref[...]
    pairs = H_A // 2

    for g in range(G):
        q_ref, k_ref, v_ref, _, la_ref = seq_refs[g]
        pre = _dot(la_ref[...], wlr) + blr
        loga = -(jnp.maximum(-pre, 0.0) + jnp.log1p(jnp.exp(-jnp.abs(pre)))) * (1.0 / GLA_TAU)
        hi = loga.astype(bf16)
        lo = (loga - hi.astype(f32)).astype(bf16)
        b = _dot(tri, hi) + _dot(tri, lo)
        b_end = _dot(ones, hi) + _dot(ones, lo)
        qf = q_ref[...].astype(f32) * (DK_A ** -0.5)
        kf = k_ref[...].astype(f32)
        q_dec = (qf * jnp.exp(b)).astype(bf16)
        k_inv = (kf * jnp.exp(-b)).astype(bf16)
        k_end = (kf * jnp.exp(b_end - b)).astype(bf16)
        dec = jnp.exp(b_end)
        qd_ref[g] = q_dec
        for c in range(CB):
            dec_ref[g, c] = dec[c * C:c * C + 1, :]
        for p in range(pairs):
            sl = slice(p * LANES, (p + 1) * LANES)
            vs = slice(p * 2 * DV_A, (p + 1) * 2 * DV_A)
            qp = q_dec[:, sl]
            kp = k_inv[:, sl]
            vp = v_ref[:, vs]
            zero = jnp.zeros_like(qp)
            att_e = jnp.where(tril, _dot_nt(jnp.where(even, qp, zero), kp), 0.0).astype(bf16)
            att_o = jnp.where(tril, _dot_nt(jnp.where(even, zero, qp), kp), 0.0).astype(bf16)
            oi_ref[g, :, vs] = jnp.concatenate([_dot(att_e, vp)[:, :DV_A], _dot(att_o, vp)[:, DV_A:]], axis=1)
            for c in range(CB):
                rows = slice(c * C, (c + 1) * C)
                ds = _dot_tn(vp[rows], k_end[rows, sl])
                ds_ref[g, c, p] = jnp.where(blockdiag, ds, 0.0)

    for c in range(CB):
        rows = slice(c * C, (c + 1) * C)
        for g in range(G):
            for p in range(pairs):
                sl = slice(p * LANES, (p + 1) * LANES)
                vs = slice(p * 2 * DV_A, (p + 1) * 2 * DV_A)
                st = st_ref[g, p]
                oi_ref[g, rows, vs] += _dot_nt(qd_ref[g, rows, sl], st.astype(bf16))
                st_ref[g, p] = dec_ref[g, c][:, sl] * st + ds_ref[g, c, p]

    for g in range(G):
        g_ref = seq_refs[g][3]
        for h in range(H_A):
            hs = slice(h * DV_A, (h + 1) * DV_A)
            o_ref[g, :, hs] = (_rms(oi_ref[g, :, hs], gn) * _silu(g_ref[:, hs].astype(f32))).astype(o_ref.dtype)

    @pl.when(s == pl.num_programs(1) - 1)
    def _():
        sfin_ref[...] = st_ref[...]


def _state_to_kernel_layout(s0):
    bsz = s0.shape[0]
    st = jnp.swapaxes(s0, -1, -2).reshape(bsz, H_A // 2, 2, DV_A, DK_A)
    z = jnp.zeros_like(st[:, :, 0])
    top = jnp.concatenate([st[:, :, 0], z], axis=-1)
    bot = jnp.concatenate([z, st[:, :, 1]], axis=-1)
    return jnp.concatenate([top, bot], axis=-2)


def _state_from_kernel_layout(st):
    bsz = st.shape[0]
    e = st[:, :, :DV_A, :DK_A]
    o = st[:, :, DV_A:, DK_A:]
    s = jnp.stack([e, o], axis=2).reshape(bsz, H_A, DV_A, DK_A)
    return jnp.swapaxes(s, -1, -2)


def _gla(z, w_lr, b_lr, g_norm, s0, *, bsz, t, row0, C, CB, G=2):
    rb = C * CB
    ns = t // rb
    r0 = row0 // rb
    assert bsz % G == 0

    def zspec(g, width, cb):
        return pl.BlockSpec((rb, width), lambda i, s: (r0 + (i * G + g) * ns + s, cb))

    hk = H_A * DK_A
    hv = H_A * DV_A
    st0 = _state_to_kernel_layout(s0.astype(f32))
    wl = jnp.zeros((LANES, hk), f32).at[:GLA_RANK].set(w_lr).astype(bf16)
    st_shape = (G, H_A // 2, 2 * DV_A, 2 * DK_A)
    seq_specs = []
    for g in range(G):
        seq_specs += [zspec(g, hk, _QA0 // hk), zspec(g, hk, _KA0 // hk), zspec(g, hv, _VA0 // hv),
                      zspec(g, hv, _GA0 // hv), zspec(g, LANES, _LA0 // LANES)]
    o, sfin = pl.pallas_call(
        functools.partial(_gla_kernel, C=C, CB=CB, G=G),
        grid=(bsz // G, ns),
        in_specs=seq_specs + [
            pl.BlockSpec((LANES, hk), lambda i, s: (0, 0)),
            pl.BlockSpec((1, hk), lambda i, s: (0, 0)),
            pl.BlockSpec((1, DV_A), lambda i, s: (0, 0)),
            pl.BlockSpec(st_shape, lambda i, s: (i, 0, 0, 0))],
        out_specs=[pl.BlockSpec((G, rb, hv), lambda i, s: (i, s, 0)),
                   pl.BlockSpec(st_shape, lambda i, s: (i, 0, 0, 0))],
        out_shape=[jax.ShapeDtypeStruct((bsz, t, hv), bf16),
                   jax.ShapeDtypeStruct((bsz,) + st_shape[1:], f32)],
        scratch_shapes=[pltpu.VMEM(st_shape, f32),
                        pltpu.VMEM((G, rb, hk), bf16),
                        pltpu.VMEM((G, rb, hv), f32),
                        pltpu.VMEM((G, CB) + st_shape[1:], f32),
                        pltpu.VMEM((G, CB, 1, hk), f32)],
        compiler_params=_cp(("arbitrary", "arbitrary")), name="gla")(
            *([z] * (5 * G)), wl, b_lr.reshape(1, hk).astype(f32), g_norm.reshape(1, DV_A).astype(f32), st0)
    return o.reshape(bsz * t, hv), _state_from_kernel_layout(sfin)


def _band_kernel(q_ref, ka_ref, kb_ref, kc_ref, va_ref, vb_ref, vc_ref, ba_ref, bb_ref, bc_ref,
                 o_ref, *, mask_first):
    qi = pl.program_id(1)
    for h in range(H_B):
        sl = slice(h * DH_B, (h + 1) * DH_B)
        qh = (q_ref[:, sl].astype(f32) * (DH_B ** -0.5)).astype(bf16)
        sa = _dot_nt(qh, ka_ref[:, sl].astype(bf16)) + ba_ref[h]
        sb = _dot_nt(qh, kb_ref[:, sl].astype(bf16)) + bb_ref[h]
        sc = _dot_nt(qh, kc_ref[:, sl].astype(bf16)) + bc_ref[h]
        if mask_first:
            sa = jnp.where(qi >= 2, sa, NEG_INF)
            sb = jnp.where(qi >= 1, sb, NEG_INF)
        m = jnp.maximum(jnp.maximum(jnp.max(sa, axis=-1, keepdims=True),
                                    jnp.max(sb, axis=-1, keepdims=True)),
                        jnp.max(sc, axis=-1, keepdims=True))
        ea = jnp.exp(sa - m)
        eb = jnp.exp(sb - m)
        ec = jnp.exp(sc - m)
        inv = 1.0 / (jnp.sum(ea, axis=-1, keepdims=True) + jnp.sum(eb, axis=-1, keepdims=True)
                     + jnp.sum(ec, axis=-1, keepdims=True))
        o = (_dot((ea * inv).astype(bf16), va_ref[:, sl].astype(bf16))
             + _dot((eb * inv).astype(bf16), vb_ref[:, sl].astype(bf16))
             + _dot((ec * inv).astype(bf16), vc_ref[:, sl].astype(bf16)))
        o_ref[:, sl] = o.astype(o_ref.dtype)


def _band_call(qkv_specs, args, biases, *, bsz, nq, qr, mask_first):
    hd = H_B * DH_B
    bias_specs = [pl.BlockSpec(b.shape, lambda b_, q_: (0, 0, 0)) for b in biases]
    return pl.pallas_call(
        functools.partial(_band_kernel, mask_first=mask_first),
        grid=(bsz, nq),
        in_specs=list(qkv_specs) + bias_specs,
        out_specs=pl.BlockSpec((qr, hd), lambda b, q: (b * nq + q, 0)),
        out_shape=jax.ShapeDtypeStruct((bsz * nq * qr, hd), bf16),
        compiler_params=_cp(("arbitrary", "arbitrary")), name="band")(*args, *biases)


def _rel_bias_matrix(tab, nq, nk, off):
    m = nq + nk
    j = np.arange(m)
    diff = np.where(j < nk, -j, m - j)
    idx = np.clip(diff + off, -REL_CLIP, REL_CLIP) + REL_CLIP
    u = tab.astype(f32)[:, jnp.asarray(idx)]
    h = tab.shape[0]
    return jnp.tile(u, (1, nq))[:, :nq * (m - 1)].reshape(h, nq, m - 1)[:, :, :nk]


def _band_bias_prompt(tab, qr, kb):
    tot = 2 * kb + qr
    qc = (np.arange(qr)[:, None] + 2 * kb) // CHUNK
    kc = np.arange(tot)[None, :] // CHUNK
    valid = (kc <= qc) & (kc >= qc - B_PREV)
    bias = jnp.where(jnp.asarray(valid)[None], _rel_bias_matrix(tab, qr, tot, 2 * kb), NEG_INF)
    return bias[:, :, :kb], bias[:, :, kb:2 * kb], bias[:, :, 2 * kb:]


def _band_bias_sample(tab, t, lc):
    bias = _rel_bias_matrix(tab, t, lc + t, lc)
    return bias[:, :, :lc // 2], bias[:, :, lc // 2:lc], bias[:, :, lc:]


def _swa_kernel(sink_ref, q_ref, kp_ref, kc_ref, vp_ref, vc_ref, o_ref, qs_ref, *, chunk_mask):
    qi = pl.program_id(1)
    qr = q_ref.shape[0]
    pr = kp_ref.shape[0]
    nl = G_C * LANES
    even = lax.broadcasted_iota(jnp.int32, (qr, LANES), 1) < DH_C
    top = lax.broadcasted_iota(jnp.int32, (LANES, LANES), 0) < DH_C
    if chunk_mask:
        kch = lax.broadcasted_iota(jnp.int32, (pr, nl), 0) // CHUNK
        qch = (lax.broadcasted_iota(jnp.int32, (pr, nl), 1) % LANES) // CHUNK
        ok_prev = (kch >= qch) & (qi >= 1)
        kch = lax.broadcasted_iota(jnp.int32, (qr, nl), 0) // CHUNK
        qch = (lax.broadcasted_iota(jnp.int32, (qr, nl), 1) % LANES) // CHUNK
        ok_cur = kch <= qch
    if qr < LANES:
        qs_ref[...] = jnp.zeros_like(qs_ref)

    def twice(ref, c):
        a = ref[:, c * DH_C:(c + 1) * DH_C].astype(bf16)
        return jnp.concatenate([a, a], axis=1)

    for c in range(KV_C):
        for p in range(G_C // 2):
            cols = slice(c * G_C * DH_C + p * LANES, c * G_C * DH_C + (p + 1) * LANES)
            qp = (q_ref[:, cols].astype(f32) * (DH_C ** -0.5)).astype(bf16)
            zero = jnp.zeros_like(qp)
            qs_ref[2 * p * LANES:2 * p * LANES + qr, :] = jnp.where(even, qp, zero)
            qs_ref[(2 * p + 1) * LANES:(2 * p + 1) * LANES + qr, :] = jnp.where(even, zero, qp)
        q_all = qs_ref[...]
        sink = jnp.concatenate(
            [jnp.full((1, LANES), sink_ref[c * G_C + j], f32) for j in range(G_C)], axis=1)
        st_prev = _dot_nt(twice(kp_ref, c), q_all)
        st_cur = _dot_nt(twice(kc_ref, c), q_all)
        if chunk_mask:
            st_prev = jnp.where(ok_prev, st_prev, NEG_INF)
            st_cur = jnp.where(ok_cur, st_cur, NEG_INF)
        m = jnp.maximum(jnp.maximum(jnp.max(st_prev, axis=0, keepdims=True),
                                    jnp.max(st_cur, axis=0, keepdims=True)), sink)
        e_prev = jnp.exp(st_prev - m)
        e_cur = jnp.exp(st_cur - m)
        inv = 1.0 / (jnp.sum(e_prev, axis=0, keepdims=True) + jnp.sum(e_cur, axis=0, keepdims=True)
                     + jnp.exp(sink - m))
        ot = (_dot_tn(twice(vp_ref, c), (e_prev * inv).astype(bf16))
              + _dot_tn(twice(vc_ref, c), (e_cur * inv).astype(bf16)))
        for p in range(G_C // 2):
            cols = slice(c * G_C * DH_C + p * LANES, c * G_C * DH_C + (p + 1) * LANES)
            blk = jnp.where(top, ot[:, 2 * p * LANES:(2 * p + 1) * LANES],
                            ot[:, (2 * p + 1) * LANES:(2 * p + 2) * LANES])
            o_ref[:, cols] = blk.T[:qr].astype(o_ref.dtype)


def _swa_call(specs, args, sinks, *, bsz, nq, qr, chunk_mask):
    hd = H_C * DH_C
    return pl.pallas_call(
        functools.partial(_swa_kernel, chunk_mask=chunk_mask),
        grid=(bsz, nq),
        in_specs=[pl.BlockSpec(memory_space=pltpu.SMEM)] + list(specs),
        out_specs=pl.BlockSpec((qr, hd), lambda b, q: (b * nq + q, 0)),
        out_shape=jax.ShapeDtypeStruct((bsz * nq * qr, hd), bf16),
        scratch_shapes=[pltpu.VMEM((G_C * LANES, LANES), bf16)],
        compiler_params=_cp(("arbitrary", "arbitrary")), name="swa")(sinks, *args)


def _out_kernel(*refs, n_parts, n_first, split_y):
    ny = 2 if split_y else 1
    y_refs = refs[:ny]
    refs = refs[ny - 1:]
    o_refs = refs[1:1 + 2 * n_parts]
    w_refs = refs[1 + 2 * n_parts:1 + 3 * n_parts]
    g_ref, rh_ref, rl_ref = refs[1 + 3 * n_parts:4 + 3 * n_parts]
    y1_ref, h2_ref, route_ref, cnt_ref, run_ref, lg_ref = refs[4 + 3 * n_parts:]
    step = pl.program_id(0)
    i = jnp.minimum(step, pl.num_programs(0) - 2)

    @pl.when(step == 0)
    def _():
        run_ref[...] = jnp.zeros_like(run_ref)
        lg_ref[...] = jnp.zeros_like(lg_ref)

    lg = lg_ref[...]
    live = (step >= 1).astype(f32)

    acc = y_refs[0][...]
    if split_y:
        acc = jnp.where(i < n_first, acc, y_refs[1][...])
    for k in range(n_parts):
        lhs = jnp.where(i < n_first, o_refs[2 * k][...], o_refs[2 * k + 1][...])
        acc = acc + _dot(lhs, w_refs[k][...])
    y1_ref[...] = acc
    h2 = _rms(acc, g_ref[...])
    hi = h2.astype(bf16)
    h2_ref[...] = hi.astype(f32)
    lo = (h2 - hi.astype(f32)).astype(bf16)
    lg_ref[...] = _dot(hi, rh_ref[...]) + _dot(hi, rl_ref[...]) + _dot(lo, rh_ref[...])

    lane_i = lax.broadcasted_iota(jnp.int32, lg.shape, 1)
    lane = lane_i.astype(f32)
    big = float(LANES)
    ninf = -jnp.inf
    is_g = lane_i < N_GROUPS
    gmax = jnp.max(jnp.where(is_g, lg, ninf), axis=-1, keepdims=True)
    gsel = jnp.min(jnp.where(is_g & (lg == gmax), lane, big), axis=-1, keepdims=True)
    pg = 1.0 / jnp.sum(jnp.where(is_g, jnp.exp(lg - gmax), 0.0), axis=-1, keepdims=True)
    e0 = N_GROUPS + EXP_PER_GROUP * gsel
    is_e = (lane >= e0) & (lane < e0 + EXP_PER_GROUP)
    v1 = jnp.max(jnp.where(is_e, lg, ninf), axis=-1, keepdims=True)
    i1 = jnp.min(jnp.where(is_e & (lg == v1), lane, big), axis=-1, keepdims=True)
    is_e2 = is_e & (lane != i1)
    v2 = jnp.max(jnp.where(is_e2, lg, ninf), axis=-1, keepdims=True)
    i2 = jnp.min(jnp.where(is_e2 & (lg == v2), lane, big), axis=-1, keepdims=True)
    t = jnp.exp(v2 - v1)
    w1 = pg / (1.0 + t)
    w2 = pg * t / (1.0 + t)

    tmr = lg.shape[0]
    oh1 = jnp.where(lane == i1, 1.0, 0.0)
    oh2 = jnp.where(lane == i2, 1.0, 0.0)
    before = jnp.where(lax.broadcasted_iota(jnp.int32, (tmr, tmr), 0)
                       > lax.broadcasted_iota(jnp.int32, (tmr, tmr), 1), 1.0, 0.0).astype(bf16)
    tot1 = jnp.sum(oh1, axis=0, keepdims=True)
    tot2 = jnp.sum(oh2, axis=0, keepdims=True)
    base = run_ref[...]
    r1 = jnp.sum(oh1 * (base + _dot(before, oh1.astype(bf16))), axis=-1, keepdims=True)
    r2 = jnp.sum(oh2 * (base + tot1 + _dot(before, oh2.astype(bf16))), axis=-1, keepdims=True)
    run_ref[...] = base + live * (tot1 + tot2)
    cnt_ref[...] = base + live * (tot1 + tot2)
    vals = (i1 - N_GROUPS, i2 - N_GROUPS, w1, w2, r1, r2)
    out = jnp.zeros_like(lg)
    for k, v in enumerate(vals):
        out = jnp.where(lane_i == k, v, out)
    route_ref[...] = out


def _out_proj(y, parts, g, w_router, *, tm):
    n_first = parts[0][0].shape[0] // tm
    split_y = isinstance(y, tuple)
    n = y[0].shape[0] + y[1].shape[0] if split_y else y.shape[0]
    n_tiles = n // tm
    tile = lambda s: jnp.minimum(s, n_tiles - 1)
    first = lambda s: (jnp.minimum(s, n_first - 1), 0)
    second = lambda s: (jnp.clip(s - n_first, 0, n_tiles - n_first - 1), 0)
    if split_y:
        d = y[0].shape[1]
        in_specs = [pl.BlockSpec((tm, d), first), pl.BlockSpec((tm, d), second)]
        args = list(y)
    else:
        d = y.shape[1]
        in_specs = [pl.BlockSpec((tm, d), lambda s: (tile(s), 0))]
        args = [y]
    for op, os_, _ in parts:
        dk = op.shape[1]
        in_specs.append(pl.BlockSpec((tm, dk), first))
        in_specs.append(pl.BlockSpec((tm, dk), second))
        args += [op, os_]
    for _, _, w in parts:
        in_specs.append(pl.BlockSpec(w.shape, lambda i: (0, 0)))
        args.append(w)
    rh = w_router.astype(bf16)
    rl = (w_router - rh.astype(f32)).astype(bf16)
    in_specs += [pl.BlockSpec((1, d), lambda i: (0, 0)),
                 pl.BlockSpec((d, LANES), lambda i: (0, 0)),
                 pl.BlockSpec((d, LANES), lambda i: (0, 0))]
    args += [g.reshape(1, d), rh, rl]
    return pl.pallas_call(
        functools.partial(_out_kernel, n_parts=len(parts), n_first=n_first, split_y=split_y),
        grid=(n_tiles + 1,), in_specs=in_specs,
        out_specs=[pl.BlockSpec((tm, d), lambda s: (tile(s), 0)),
                   pl.BlockSpec((tm, d), lambda s: (tile(s), 0)),
                   pl.BlockSpec((tm, LANES), lambda s: (jnp.maximum(s - 1, 0), 0)),
                   pl.BlockSpec((1, LANES), lambda s: (0, 0))],
        out_shape=[jax.ShapeDtypeStruct((n, d), f32),
                   jax.ShapeDtypeStruct((n, d), f32),
                   jax.ShapeDtypeStruct((n, LANES), f32),
                   jax.ShapeDtypeStruct((1, LANES), f32)],
        scratch_shapes=[pltpu.VMEM((1, LANES), f32), pltpu.VMEM((tm, LANES), f32)],
        compiler_params=_cp(("arbitrary",)), name="out_proj")(*args)


def _moe_kernel(te_ref, tv_ref, pos_ref, plo_ref, phi_ref, en_ref, es_ref,
                h_ref, wg_ref, wu_ref, wd_ref, out_ref,
                slot_ref, tok_ref, x0, x1, x2, y0, y1, y2, gsem, ssem, wg_f, wu_f, wd_f, wsem,
                wg_s, wu_s, wd_s, *, tm, n):
    t = pl.program_id(0)
    nt = pl.num_programs(0)
    n2 = 2 * n
    xs = (x0, x1, x2)
    ys = (y0, y1, y2)
    depth = len(xs)

    def weights_start(e, slot):
        for src, dst in ((wg_ref, wg_f), (wu_ref, wu_f), (wd_ref, wd_f)):
            pltpu.make_async_copy(src.at[e], dst.at[slot], wsem.at[slot]).start(priority=1)

    def weights_wait(slot):
        for src, dst in ((wg_ref, wg_f), (wu_ref, wu_f), (wd_ref, wd_f)):
            pltpu.make_async_copy(src.at[0], dst.at[slot], wsem.at[slot]).wait()

    def tile_valid(tile):
        return (tile >= 0) & (tile < nt) & (tv_ref[jnp.clip(tile, 0, nt - 1)] != 0)

    valid = tile_valid(t)

    def gather_row(tile, r, p):
        tok = tok_ref[tile * tm + r]
        pltpu.make_async_copy(h_ref.at[pl.ds(tok, 1)], xs[p].at[pl.ds(r, 1)], gsem.at[p]).start()

    def scatter_row(tile, r, p, priority=0):
        s = slot_ref[tile * tm + r]
        pltpu.make_async_copy(ys[p].at[pl.ds(r, 1)], out_ref.at[pl.ds(s, 1)], ssem.at[p]).start(
            priority=priority)

    def rows_loop(row_fn, tile, p):
        def body(r, carry):
            row_fn(tile, r, p)
            return carry
        lax.fori_loop(0, tm, body, 0, unroll=8)

    def gather_wait(p):
        pltpu.make_async_copy(h_ref.at[pl.ds(0, tm)], xs[p], gsem.at[p]).wait()

    def scatter_wait(p):
        pltpu.make_async_copy(ys[p], out_ref.at[pl.ds(0, tm)], ssem.at[p]).wait()

    def compute(p):
        x = xs[p][...].astype(bf16)
        g = _dot(x, wg_s[...])
        u = _dot(x, wu_s[...])
        ys[p][...] = _dot((_silu(g) * u).astype(bf16), wd_s[...])

    @pl.when(t == 0)
    def _():
        def per_expert(e, nxt):
            def body(r, v):
                slot_ref[r] = v
                tok_ref[r] = v - n2
                return v + 1
            return lax.fori_loop(plo_ref[e], phi_ref[e], body, nxt)
        lax.fori_loop(0, N_EXPERTS, per_expert, n2)

        def fill(choice):
            def body(tok, carry):
                s = choice * n + tok
                row = pos_ref[s]
                slot_ref[row] = s
                tok_ref[row] = tok
                return carry
            lax.fori_loop(0, n, body, 0, unroll=8)
        fill(0)
        fill(1)
        weights_start(te_ref[0], 0)
        rows_loop(gather_row, 0, 0)

        @pl.when(tile_valid(1))
        def _():
            rows_loop(gather_row, 1, 1)

    def step(p):
        q = (p + 2) % depth

        @pl.when(t >= depth)
        def _():
            scatter_wait(p)

        @pl.when(valid)
        def _():
            gather_wait(p)

            @pl.when((t == 0) | (te_ref[t] != te_ref[jnp.maximum(t - 1, 0)]))
            def _():
                slot = es_ref[t]
                weights_wait(slot)
                wg_s[...] = wg_f[slot].astype(bf16)
                wu_s[...] = wu_f[slot].astype(bf16)
                wd_s[...] = wd_f[slot].astype(bf16)

                @pl.when(en_ref[t] >= 0)
                def _():
                    weights_start(en_ref[t], 1 - slot)

            steady = tile_valid(t + 2) & (t >= 1)

            @pl.when(steady)
            def _():
                for r in range(tm):
                    gather_row(t + 2, r, q)
                    scatter_row(t - 1, r, q, priority=r % 2)
                compute(p)

            @pl.when(jnp.logical_not(steady))
            def _():
                @pl.when(tile_valid(t + 2))
                def _():
                    rows_loop(gather_row, t + 2, q)

                @pl.when(t >= 1)
                def _():
                    rows_loop(scatter_row, t - 1, q)
                compute(p)

        @pl.when(jnp.logical_not(valid))
        def _():
            @pl.when(tile_valid(t - 1))
            def _():
                rows_loop(scatter_row, t - 1, q)
            ys[p][...] = jnp.zeros(ys[p].shape, f32)
            pltpu.make_async_copy(ys[p], out_ref.at[pl.ds(pl.multiple_of(t * tm, tm), tm)], ssem.at[p]).start()

        @pl.when(t == nt - 1)
        def _():
            @pl.when(valid)
            def _():
                rows_loop(scatter_row, t, p)
            for k in range(depth):
                scatter_wait(k)

    for p in range(depth):
        pl.when(t % depth == p)(functools.partial(step, p))


def _moe_experts(h2, pos, tile_e, tile_v, pad_lo, pad_hi, wg, wu, wd, *, layer, tm, n_tiles):
    n, d = h2.shape
    ff = wg.shape[2]
    assert n >= N_EXPERTS * tm, "padding rows read tokens 0 .. N_EXPERTS * (tm - 1)"
    assert n_tiles >= 3
    first = jnp.concatenate([jnp.ones((1,), jnp.int32), (tile_e[1:] != tile_e[:-1]).astype(jnp.int32)])
    seq = jnp.cumsum(first * tile_v) - 1
    seq = jnp.where(tile_v != 0, seq, n_tiles)
    nxt_pos = jnp.sum((seq[None, :] <= seq[:, None]).astype(jnp.int32), axis=1)
    has_next = (nxt_pos < n_tiles) & (tile_v[jnp.minimum(nxt_pos, n_tiles - 1)] != 0) & (tile_v != 0)
    tile_e = tile_e + layer * N_EXPERTS
    e_next = jnp.where(has_next, tile_e[jnp.minimum(nxt_pos, n_tiles - 1)], -1).astype(jnp.int32)
    e_slot = (seq % 2).astype(jnp.int32)
    row_buf = pltpu.VMEM((tm, d), f32)
    any_spec = pl.BlockSpec(memory_space=pl.ANY)
    grid_spec = pltpu.PrefetchScalarGridSpec(
        num_scalar_prefetch=7, grid=(n_tiles,),
        in_specs=[any_spec, any_spec, any_spec, any_spec],
        out_specs=any_spec,
        scratch_shapes=[pltpu.SMEM((n_tiles * tm,), jnp.int32), pltpu.SMEM((n_tiles * tm,), jnp.int32),
                        row_buf, row_buf, row_buf, row_buf, row_buf, row_buf,
                        pltpu.SemaphoreType.DMA((3,)), pltpu.SemaphoreType.DMA((3,)),
                        pltpu.VMEM((2, d, ff), f32), pltpu.VMEM((2, d, ff), f32), pltpu.VMEM((2, ff, d), f32),
                        pltpu.SemaphoreType.DMA((2,)),
                        pltpu.VMEM((d, ff), bf16), pltpu.VMEM((d, ff), bf16), pltpu.VMEM((ff, d), bf16)])
    return pl.pallas_call(
        functools.partial(_moe_kernel, tm=tm, n=n), grid_spec=grid_spec,
        out_shape=jax.ShapeDtypeStruct((n_tiles * tm, d), f32),
        compiler_params=_cp(("arbitrary",)), name="moe")(
            tile_e, tile_v, pos, pad_lo, pad_hi, e_next, e_slot, h2, wg, wu, wd)


def _moe(h2, route, counts, wg, wu, wd, *, layer, tm):
    n = route.shape[0]
    n_tiles = (2 * n) // tm + N_EXPERTS
    counts = counts[0, N_GROUPS:N_GROUPS + N_EXPERTS].astype(jnp.int32)
    padded = ((counts + tm - 1) // tm) * tm
    ends = jnp.cumsum(padded)
    starts = ends - padded
    e = route[:, 0:2].astype(jnp.int32)
    rank = route[:, 4:6].astype(jnp.int32)
    pos = (starts[e] + rank).T.reshape(-1)
    tile_start = jnp.arange(n_tiles, dtype=jnp.int32) * tm
    tile_e = jnp.sum((tile_start[:, None] >= ends[None, :]).astype(jnp.int32), axis=1)
    tile_v = (tile_start < ends[-1]).astype(jnp.int32)
    tile_e = jnp.minimum(tile_e, N_EXPERTS - 1)
    return _moe_experts(h2, pos, tile_e, tile_v, starts + counts, ends, wg, wu, wd, layer=layer, tm=tm,
                        n_tiles=n_tiles)


def _final_kernel(y_ref, m0_ref, m1_ref, rt_ref, g_ref, o_ref):
    o_ref[...] = _rms(y_ref[...] + _moe_combine(m0_ref, m1_ref, rt_ref), g_ref[...])


def _final(y, moe_out, route, g, *, tm, row0, rows):
    n, d = y.shape
    r0 = row0 // tm
    r1 = (n + row0) // tm
    return pl.pallas_call(
        _final_kernel, grid=(rows // tm,),
        in_specs=[pl.BlockSpec((tm, d), lambda i: (r0 + i, 0)),
                  pl.BlockSpec((tm, d), lambda i: (r0 + i, 0)),
                  pl.BlockSpec((tm, d), lambda i: (r1 + i, 0)),
                  pl.BlockSpec((tm, LANES), lambda i: (r0 + i, 0)),
                  pl.BlockSpec((1, d), lambda i: (0, 0))],
        out_specs=pl.BlockSpec((tm, d), lambda i: (i, 0)),
        out_shape=jax.ShapeDtypeStruct((rows, d), f32),
        compiler_params=_cp(("arbitrary",)), name="final_norm")(y, moe_out, moe_out, route, g.reshape(1, d))


def _rope_tables(pos):
    half = ROT_DIM // 2
    inv = jnp.power(ROPE_THETA, -jnp.arange(half, dtype=f32) * 2.0 / ROT_DIM)
    ang = pos.astype(f32)[:, None] * inv[None, :]
    cos, sin = lax.optimization_barrier((jnp.cos(ang), jnp.sin(ang)))
    n = pos.shape[0]
    one = jnp.ones((n, DH_C - ROT_DIM), f32)
    zero = jnp.zeros((n, DH_C - ROT_DIM), f32)
    zh = jnp.zeros((n, half), f32)
    c = jnp.concatenate([cos, cos, one], axis=1)
    s1 = jnp.concatenate([zh, sin, zero], axis=1)
    s2 = jnp.concatenate([-sin, zh, zero], axis=1)
    rep = LANES // DH_C
    return tuple(jnp.tile(t, (1, rep)) for t in (c, s1, s2))


def kernel(x_prompt, x_sample, cache_gla_state, cache_band_k, cache_band_v, cache_swa_k, cache_swa_v,
           norm_mix, norm_ffn, norm_final, w_in_ab, w_gla_lr, b_gla_lr, gla_out_norm, rel_bias_tab,
           w_out_ab, w_in_c, b_in_c, attn_sinks, w_out_c, w_router_grp, w_router_exp, w_exp_gate,
           w_exp_up, w_exp_down):
    bp, sp, d = x_prompt.shape
    bs, ts, _ = x_sample.shape
    n_p = bp * sp
    n_s = bs * ts
    n = n_p + n_s
    tm = 512 if n_s % 512 == 0 else n_s
    tmo = min(tm, 256)
    tme = 256 if (2 * n) % 256 == 0 else 64
    ff = w_exp_gate.shape[-1]

    xp = x_prompt.reshape(n_p, d)
    xs_ = x_sample.reshape(n_s, d)

    def router_w(l):
        wr = jnp.concatenate([w_router_grp[l], w_router_exp[l].reshape(d, N_EXPERTS)], axis=1)
        return jnp.zeros((d, LANES), f32).at[:, :N_GROUPS + N_EXPERTS].set(wr)

    experts = (w_exp_gate.reshape(-1, d, ff), w_exp_up.reshape(-1, d, ff), w_exp_down.reshape(-1, ff, d))

    w0 = w_in_ab[0].astype(bf16)
    la0 = 2 * H_A * DK_A + 2 * H_A * DV_A
    w_cat = jnp.concatenate(
        [w0[:, :la0], w0[:, la0 + GLA_RANK:], w0[:, la0:la0 + GLA_RANK],
         jnp.zeros((d, LANES - GLA_RANK), bf16)], axis=1)
    z = _proj(xp, norm_mix[0], w_cat, tm=tmo, tn=1024, loop_cols=_LA0, yb=xs_)

    c_s = min(CHUNK, ts)
    oa_p, gla_p = _gla(z, w_gla_lr[0], b_gla_lr[0], gla_out_norm[0],
                       jnp.zeros((bp, H_A, DK_A, DV_A), f32), bsz=bp, t=sp, row0=0, C=CHUNK, CB=4)
    oa_s, gla_s = _gla(z, w_gla_lr[0], b_gla_lr[0], gla_out_norm[0], cache_gla_state[0],
                       bsz=bs, t=ts, row0=n_p, C=c_s, CB=ts // c_s)

    hd = H_B * DH_B
    qr = 256
    nq = sp // qr
    qcb, kcb, vcb = _QB0 // hd, _KB0 // hd, _VB0 // hd
    cur = lambda cb: pl.BlockSpec((qr, hd), lambda b, q: (b * nq + q, cb))
    prev = lambda cb, back: pl.BlockSpec((qr, hd), lambda b, q: (b * nq + jnp.maximum(q - back, 0), cb))
    ob_p = _band_call(
        [cur(qcb), prev(kcb, 2), prev(kcb, 1), cur(kcb), prev(vcb, 2), prev(vcb, 1), cur(vcb)],
        [z] * 7, _band_bias_prompt(rel_bias_tab[0], qr, qr), bsz=bp, nq=nq, qr=qr, mask_first=True)

    lb = cache_band_k.shape[2]
    ck = cache_band_k[0].reshape(bs * lb, hd)
    cv = cache_band_v[0].reshape(bs * lb, hd)
    rs0 = n_p // ts
    new = lambda cb: pl.BlockSpec((ts, hd), lambda b, q: (rs0 + b, cb))
    cpiece = lambda k: pl.BlockSpec((lb // 2, hd), lambda b, q: (2 * b + k, 0))
    ob_s = _band_call(
        [new(qcb), cpiece(0), cpiece(1), new(kcb), cpiece(0), cpiece(1), new(vcb)],
        [z, ck, ck, z, cv, cv, z], _band_bias_sample(rel_bias_tab[0], ts, lb),
        bsz=bs, nq=1, qr=ts, mask_first=False)

    dmix_a = H_A * DV_A
    wo = w_out_ab[0].astype(bf16)
    y1, h2, route, cnt = _out_proj(
        (xp, xs_), [(oa_p, oa_s, wo[:dmix_a]), (ob_p, ob_s, wo[dmix_a:])], norm_ffn[0], router_w(0),
        tm=tmo)
    moe0 = _moe(h2, route, cnt, *experts, layer=0, tm=tme)

    keep = min(B_PREV * CHUNK, sp)

    def prompt_tail(a, rows, c0, width):
        return jnp.stack([a[(b + 1) * sp - rows:(b + 1) * sp, c0:c0 + width] for b in range(bp)])

    band_k_p = prompt_tail(z, keep, _KB0, hd).astype(f32).reshape(bp, keep, H_B, DH_B)[None]
    band_v_p = prompt_tail(z, keep, _VB0, hd).astype(f32).reshape(bp, keep, H_B, DH_B)[None]
    kb_s = z[n_p:, _KB0:_KB0 + hd].astype(f32).reshape(bs, ts, H_B, DH_B)
    vb_s = z[n_p:, _VB0:_VB0 + hd].astype(f32).reshape(bs, ts, H_B, DH_B)
    band_k_s = jnp.concatenate([cache_band_k[0][:, ts:], kb_s], axis=1)[None]
    band_v_s = jnp.concatenate([cache_band_v[0][:, ts:], vb_s], axis=1)[None]

    nq_c = H_C * DH_C
    nkv = KV_C * DH_C
    uniq = _rope_tables(jnp.concatenate([jnp.arange(sp), PAST_LEN + jnp.arange(ts)]))
    rope = tuple(jnp.concatenate([jnp.tile(u[:sp], (bp, 1)), jnp.tile(u[sp:], (bs, 1))]) for u in uniq)
    y2, z1 = _proj(y1, norm_mix[1], w_in_c[0].astype(bf16), tm=tmo, tn=512, loop_cols=nq_c,
                   add=(moe0, route), bias=b_in_c[0], rope=rope, rope_cols=nq_c + nkv)

    qr_c = WINDOW
    nq1 = sp // qr_c
    kw = nkv
    kcb1, vcb1 = _KC0 // kw, _VC0 // kw
    qspec = pl.BlockSpec((qr_c, nq_c), lambda b, q: (b * nq1 + q, 0))
    cur1 = lambda cb: pl.BlockSpec((qr_c, kw), lambda b, q: (b * nq1 + q, cb))
    prev1 = lambda cb: pl.BlockSpec((qr_c, kw), lambda b, q: (b * nq1 + jnp.maximum(q - 1, 0), cb))
    sinks = attn_sinks[0].astype(f32)
    oc_p = _swa_call([qspec, prev1(kcb1), cur1(kcb1), prev1(vcb1), cur1(vcb1)], [z1] * 5, sinks,
                     bsz=bp, nq=nq1, qr=qr_c, chunk_mask=True)

    lc = cache_swa_k.shape[2]
    csk = cache_swa_k[0].reshape(bs * lc, nkv)
    csv = cache_swa_v[0].reshape(bs * lc, nkv)
    qspec_s = pl.BlockSpec((ts, nq_c), lambda b, q: (rs0 + b, 0))
    new1 = lambda cb: pl.BlockSpec((ts, kw), lambda b, q: (rs0 + b, cb))
    cspec = pl.BlockSpec((lc, kw), lambda b, q: (b, 0))
    oc_s = _swa_call([qspec_s, cspec, new1(kcb1), cspec, new1(vcb1)], [z1, csk, z1, csv, z1], sinks,
                     bsz=bs, nq=1, qr=ts, chunk_mask=False)

    y3, h4, route1, cnt1 = _out_proj(y2, [(oc_p, oc_s, w_out_c[0].astype(bf16))], norm_ffn[1],
                                     router_w(1), tm=tmo)
    moe1 = _moe(h4, route1, cnt1, *experts, layer=1, tm=tme)

    keep1 = min(WINDOW, sp)

    swa_k_p = prompt_tail(z1, keep1, _KC0, kw).astype(f32).reshape(bp, keep1, KV_C, DH_C)[None]
    swa_v_p = prompt_tail(z1, keep1, _VC0, kw).astype(f32).reshape(bp, keep1, KV_C, DH_C)[None]
    k1_s = z1[n_p:, _KC0:_KC0 + kw].astype(f32).reshape(bs, ts, KV_C, DH_C)
    v1_s = z1[n_p:, _VC0:_VC0 + kw].astype(f32).reshape(bs, ts, KV_C, DH_C)
    swa_k_s = jnp.concatenate([cache_swa_k[0][:, ts:], k1_s], axis=1)[None]
    swa_v_s = jnp.concatenate([cache_swa_v[0][:, ts:], v1_s], axis=1)[None]

    y_prompt = _final(y3, moe1, route1, norm_final, tm=tm, row0=0, rows=n_p).reshape(bp, sp, d)
    y_sample = _final(y3, moe1, route1, norm_final, tm=tm, row0=n_p, rows=n_s).reshape(bs, ts, d)

    return (y_prompt, y_sample, gla_p[None], gla_s[None].astype(cache_gla_state.dtype),
            band_k_p, band_v_p, band_k_s, band_v_s, swa_k_p, swa_v_p, swa_k_s, swa_v_s)
```

```python
import functools

import numpy as np
import jax
import jax.numpy as jnp
from jax import lax
from jax.experimental import pallas as pl
from jax.experimental.pallas import tpu as pltpu

f32 = jnp.float32
bf16 = jnp.bfloat16

CHUNK = 64
EPS = 1e-6
NEG_INF = -1e30
PAST_LEN = 1024
H_A = 8
DK_A = 64
DV_A = 128
GLA_RANK = 16
GLA_TAU = 16.0
H_B = 8
DH_B = 128
B_PREV = 8
REL_CLIP = 128
H_C = 32
KV_C = 4
G_C = H_C // KV_C
DH_C = 64
WINDOW = 128
ROT_DIM = DH_C // 4
ROPE_THETA = 500000.0
N_GROUPS = 4
EXP_PER_GROUP = 8
N_EXPERTS = N_GROUPS * EXP_PER_GROUP

LANES = 128
VMEM_LIMIT = 52 * 1024 * 1024

_QA0, _KA0, _VA0, _GA0 = 0, 512, 1024, 2048
_QB0, _KB0, _VB0, _LA0 = 3072, 4096, 5120, 6144
_NC_AB = 6272
_QC0, _KC0, _VC0 = 0, 2048, 2304
_NC_C = 2560


def _cp(sem):
    return pltpu.CompilerParams(dimension_semantics=sem, vmem_limit_bytes=VMEM_LIMIT)


def _dot(a, b):
    return jnp.dot(a, b, preferred_element_type=f32)


def _dot_nt(a, b):
    return lax.dot_general(a, b, (((1,), (1,)), ((), ())), preferred_element_type=f32)


def _dot_tn(a, b):
    return lax.dot_general(a, b, (((0,), (0,)), ((), ())), preferred_element_type=f32)


def _rms(x, g):
    ms = jnp.mean(x * x, axis=-1, keepdims=True)
    return (x * lax.rsqrt(ms + EPS)) * g


def _silu(x):
    return x / (1.0 + jnp.exp(-x))


def _moe_combine(m0_ref, m1_ref, route_ref):
    rt = route_ref[...]
    return rt[:, 2:3] * m0_ref[...] + rt[:, 3:4] * m1_ref[...]


def _proj_kernel(*refs, n_first, has_add, has_bias, rope_cols, loop_cols, tn):
    it = iter(refs)
    ya_ref = next(it)
    yb_ref = next(it) if n_first is not None else None
    if has_add:
        m0_ref, m1_ref, rt_ref = next(it), next(it), next(it)
    g_ref = next(it)
    w_ref = next(it)
    b_ref = next(it) if has_bias else None
    if rope_cols:
        c_ref, s1_ref, s2_ref = next(it), next(it), next(it)
    ysum_ref = next(it) if has_add else None
    z_ref = next(it)
    h_ref = next(it)
    i = pl.program_id(0)

    x = ya_ref[...]
    if n_first is not None:
        x = jnp.where(i < n_first, x, yb_ref[...])
    if has_add:
        x = x + _moe_combine(m0_ref, m1_ref, rt_ref)
        ysum_ref[...] = x
    h_ref[...] = _rms(x, g_ref[...]).astype(bf16)

    def rope(acc):
        width = acc.shape[1]
        rep = width // LANES
        c = jnp.tile(c_ref[...], (1, rep))
        s1 = jnp.tile(s1_ref[...], (1, rep))
        s2 = jnp.tile(s2_ref[...], (1, rep))
        half = ROT_DIM // 2
        lo = pltpu.roll(acc, half, axis=1)
        hi = pltpu.roll(acc, width - half, axis=1)
        return acc * c + lo * s1 + hi * s2

    def col_tile(j, carry):
        cols = pl.ds(pl.multiple_of(j * tn, tn), tn)
        acc = _dot(h_ref[...], w_ref[:, cols])
        if has_bias:
            acc = acc + b_ref[:, cols]
        if rope_cols:
            acc = rope(acc)
        z_ref[:, cols] = acc.astype(z_ref.dtype)
        return carry

    nc = z_ref.shape[1]
    lax.fori_loop(0, loop_cols // tn, col_tile, 0)
    if nc > loop_cols:
        acc = _dot(h_ref[...], w_ref[:, loop_cols:])
        if has_bias:
            acc = acc + b_ref[:, loop_cols:]
        rt = max(rope_cols - loop_cols, 0)
        if rt:
            z_ref[:, loop_cols:loop_cols + rt] = rope(acc[:, :rt]).astype(z_ref.dtype)
        z_ref[:, loop_cols + rt:] = acc[:, rt:].astype(z_ref.dtype)


def _proj(ya, g, w, *, tm, tn, loop_cols, yb=None, add=None, bias=None, rope=None, rope_cols=0):
    d = ya.shape[1]
    na = ya.shape[0] // tm
    nb = 0 if yb is None else yb.shape[0] // tm
    n = (na + nb) * tm
    nc = w.shape[1]
    assert loop_cols % tn == 0 and loop_cols <= nc and (rope_cols == 0 or rope_cols >= loop_cols)
    once = dict(pipeline_mode=pl.Buffered(1))
    in_specs = []
    args = []
    if yb is None:
        in_specs.append(pl.BlockSpec((tm, d), lambda i: (i, 0)))
        args.append(ya)
    else:
        in_specs.append(pl.BlockSpec((tm, d), lambda i: (jnp.minimum(i, na - 1), 0)))
        in_specs.append(pl.BlockSpec((tm, d), lambda i: (jnp.maximum(i - na, 0), 0)))
        args += [ya, yb]
    if add is not None:
        moe_out, route = add
        in_specs.append(pl.BlockSpec((tm, d), lambda i: (i, 0)))
        in_specs.append(pl.BlockSpec((tm, d), lambda i: (i + na, 0)))
        in_specs.append(pl.BlockSpec((tm, LANES), lambda i: (i, 0)))
        args += [moe_out, moe_out, route]
    in_specs.append(pl.BlockSpec((1, d), lambda i: (0, 0)))
    args.append(g.reshape(1, d))
    in_specs.append(pl.BlockSpec((d, nc), lambda i: (0, 0), **once))
    args.append(w)
    if bias is not None:
        in_specs.append(pl.BlockSpec((1, nc), lambda i: (0, 0)))
        args.append(bias.reshape(1, nc))
    if rope_cols:
        for t in rope:
            in_specs.append(pl.BlockSpec((tm, LANES), lambda i: (i, 0)))
            args.append(t)
    out_shape = []
    out_specs = []
    if add is not None:
        out_shape.append(jax.ShapeDtypeStruct((n, d), f32))
        out_specs.append(pl.BlockSpec((tm, d), lambda i: (i, 0)))
    out_shape.append(jax.ShapeDtypeStruct((n, nc), bf16))
    out_specs.append(pl.BlockSpec((tm, nc), lambda i: (i, 0)))
    kern = functools.partial(_proj_kernel, n_first=(na if yb is not None else None),
                             has_add=add is not None, has_bias=bias is not None, rope_cols=rope_cols,
                             loop_cols=loop_cols, tn=tn)
    out = pl.pallas_call(
        kern, grid=(na + nb,), in_specs=in_specs, out_specs=out_specs, out_shape=out_shape,
        scratch_shapes=[pltpu.VMEM((tm, d), bf16)],
        compiler_params=_cp(("arbitrary",)), name="proj")(*args)
    return out if add is not None else out[0]


def _gla_kernel(*refs, C, CB, G):
    seq_refs = [refs[5 * g:5 * g + 5] for g in range(G)]
    (wlr_ref, blr_ref, gn_ref, s0_ref, o_ref, sfin_ref,
     st_ref, qd_ref, oi_ref, ds_ref, dec_ref) = refs[5 * G:]
    s = pl.program_id(1)
    R = C * CB

    @pl.when(s == 0)
    def _():
        st_ref[...] = s0_ref[...]

    row = lax.broadcasted_iota(jnp.int32, (R, R), 0)
    col = lax.broadcasted_iota(jnp.int32, (R, R), 1)
    same = (row // C) == (col // C)
    tril = same & (row >= col)
    tri = jnp.where(tril, 1.0, 0.0).astype(bf16)
    ones = jnp.where(same, 1.0, 0.0).astype(bf16)
    even = lax.broadcasted_iota(jnp.int32, (R, LANES), 1) < DK_A
    srow = lax.broadcasted_iota(jnp.int32, (2 * DV_A, LANES), 0)
    slane = lax.broadcasted_iota(jnp.int32, (2 * DV_A, LANES), 1)
    blockdiag = (srow < DV_A) == (slane < DK_A)
    wlr = wlr_ref[...]
    blr = blr_ref[...]
    gn = gn_ref[...]
    pairs = H_A // 2

    for g in range(G):
        q_ref, k_ref, v_ref, _, la_ref = seq_refs[g]
        pre = _dot(la_ref[...], wlr) + blr
        loga = -(jnp.maximum(-pre, 0.0) + jnp.log1p(jnp.exp(-jnp.abs(pre)))) * (1.0 / GLA_TAU)
        hi = loga.astype(bf16)
        lo = (loga - hi.astype(f32)).astype(bf16)
        b = _dot(tri, hi) + _dot(tri, lo)
        b_end = _dot(ones, hi) + _dot(ones, lo)
        qf = q_ref[...].astype(f32) * (DK_A ** -0.5)
        kf = k_ref[...].astype(f32)
        q_dec = (qf * jnp.exp(b)).astype(bf16)
        k_inv = (kf * jnp.exp(-b)).astype(bf16)
        k_end = (kf * jnp.exp(b_end - b)).astype(bf16)
        dec = jnp.exp(b_end)
        qd_ref[g] = q_dec
        for c in range(CB):
            dec_ref[g, c] = dec[c * C:c * C + 1, :]
        for p in range(pairs):
            sl = slice(p * LANES, (p + 1) * LANES)
            vs = slice(p * 2 * DV_A, (p + 1) * 2 * DV_A)
            qp = q_dec[:, sl]
            kp = k_inv[:, sl]
            vp = v_ref[:, vs]
            zero = jnp.zeros_like(qp)
            att_e = jnp.where(tril, _dot_nt(jnp.where(even, qp, zero), kp), 0.0).astype(bf16)
            att_o = jnp.where(tril, _dot_nt(jnp.where(even, zero, qp), kp), 0.0).astype(bf16)
            oi_ref[g, :, vs] = jnp.concatenate([_dot(att_e, vp)[:, :DV_A], _dot(att_o, vp)[:, DV_A:]], axis=1)
            for c in range(CB):
                rows = slice(c * C, (c + 1) * C)
                ds = _dot_tn(vp[rows], k_end[rows, sl])
                ds_ref[g, c, p] = jnp.where(blockdiag, ds, 0.0)

    for c in range(CB):
        rows = slice(c * C, (c + 1) * C)
        for g in range(G):
            for p in range(pairs):
                sl = slice(p * LANES, (p + 1) * LANES)
                vs = slice(p * 2 * DV_A, (p + 1) * 2 * DV_A)
                st = st_ref[g, p]
                oi_ref[g, rows, vs] += _dot_nt(qd_ref[g, rows, sl], st.astype(bf16))
                st_ref[g, p] = dec_ref[g, c][:, sl] * st + ds_ref[g, c, p]

    for g in range(G):
        g_ref = seq_refs[g][3]
        for h in range(H_A):
            hs = slice(h * DV_A, (h + 1) * DV_A)
            o_ref[g, :, hs] = (_rms(oi_ref[g, :, hs], gn) * _silu(g_ref[:, hs].astype(f32))).astype(o_ref.dtype)

    @pl.when(s == pl.num_programs(1) - 1)
    def _():
        sfin_ref[...] = st_ref[...]


def _state_to_kernel_layout(s0):
    bsz = s0.shape[0]
    st = jnp.swapaxes(s0, -1, -2).reshape(bsz, H_A // 2, 2, DV_A, DK_A)
    z = jnp.zeros_like(st[:, :, 0])
    top = jnp.concatenate([st[:, :, 0], z], axis=-1)
    bot = jnp.concatenate([z, st[:, :, 1]], axis=-1)
    return jnp.concatenate([top, bot], axis=-2)


def _state_from_kernel_layout(st):
    bsz = st.shape[0]
    e = st[:, :, :DV_A, :DK_A]
    o = st[:, :, DV_A:, DK_A:]
    s = jnp.stack([e, o], axis=2).reshape(bsz, H_A, DV_A, DK_A)
    return jnp.swapaxes(s, -1, -2)


def _gla(z, w_lr, b_lr, g_norm, s0, *, bsz, t, row0, C, CB, G=2):
    rb = C * CB
    ns = t // rb
    r0 = row0 // rb
    assert bsz % G == 0

    def zspec(g, width, cb):
        return pl.BlockSpec((rb, width), lambda i, s: (r0 + (i * G + g) * ns + s, cb))

    hk = H_A * DK_A
    hv = H_A * DV_A
    st0 = _state_to_kernel_layout(s0.astype(f32))
    wl = jnp.zeros((LANES, hk), f32).at[:GLA_RANK].set(w_lr).astype(bf16)
    st_shape = (G, H_A // 2, 2 * DV_A, 2 * DK_A)
    seq_specs = []
    for g in range(G):
        seq_specs += [zspec(g, hk, _QA0 // hk), zspec(g, hk, _KA0 // hk), zspec(g, hv, _VA0 // hv),
                      zspec(g, hv, _GA0 // hv), zspec(g, LANES, _LA0 // LANES)]
    o, sfin = pl.pallas_call(
        functools.partial(_gla_kernel, C=C, CB=CB, G=G),
        grid=(bsz // G, ns),
        in_specs=seq_specs + [
            pl.BlockSpec((LANES, hk), lambda i, s: (0, 0)),
            pl.BlockSpec((1, hk), lambda i, s: (0, 0)),
            pl.BlockSpec((1, DV_A), lambda i, s: (0, 0)),
            pl.BlockSpec(st_shape, lambda i, s: (i, 0, 0, 0))],
        out_specs=[pl.BlockSpec((G, rb, hv), lambda i, s: (i, s, 0)),
                   pl.BlockSpec(st_shape, lambda i, s: (i, 0, 0, 0))],
        out_shape=[jax.ShapeDtypeStruct((bsz, t, hv), bf16),
                   jax.ShapeDtypeStruct((bsz,) + st_shape[1:], f32)],
        scratch_shapes=[pltpu.VMEM(st_shape, f32),
                        pltpu.VMEM((G, rb, hk), bf16),
                        pltpu.VMEM((G, rb, hv), f32),
                        pltpu.VMEM((G, CB) + st_shape[1:], f32),
                        pltpu.VMEM((G, CB, 1, hk), f32)],
        compiler_params=_cp(("arbitrary", "arbitrary")), name="gla")(
            *([z] * (5 * G)), wl, b_lr.reshape(1, hk).astype(f32), g_norm.reshape(1, DV_A).astype(f32), st0)
    return o.reshape(bsz * t, hv), _state_from_kernel_layout(sfin)


def _band_kernel(q_ref, ka_ref, kb_ref, kc_ref, va_ref, vb_ref, vc_ref, ba_ref, bb_ref, bc_ref,
                 o_ref, *, mask_first):
    qi = pl.program_id(1)
    for h in range(H_B):
        sl = slice(h * DH_B, (h + 1) * DH_B)
        qh = (q_ref[:, sl].astype(f32) * (DH_B ** -0.5)).astype(bf16)
        sa = _dot_nt(qh, ka_ref[:, sl].astype(bf16)) + ba_ref[h]
        sb = _dot_nt(qh, kb_ref[:, sl].astype(bf16)) + bb_ref[h]
        sc = _dot_nt(qh, kc_ref[:, sl].astype(bf16)) + bc_ref[h]
        if mask_first:
            sa = jnp.where(qi >= 2, sa, NEG_INF)
            sb = jnp.where(qi >= 1, sb, NEG_INF)
        m = jnp.maximum(jnp.maximum(jnp.max(sa, axis=-1, keepdims=True),
                                    jnp.max(sb, axis=-1, keepdims=True)),
                        jnp.max(sc, axis=-1, keepdims=True))
        ea = jnp.exp(sa - m)
        eb = jnp.exp(sb - m)
        ec = jnp.exp(sc - m)
        inv = 1.0 / (jnp.sum(ea, axis=-1, keepdims=True) + jnp.sum(eb, axis=-1, keepdims=True)
                     + jnp.sum(ec, axis=-1, keepdims=True))
        o = (_dot((ea * inv).astype(bf16), va_ref[:, sl].astype(bf16))
             + _dot((eb * inv).astype(bf16), vb_ref[:, sl].astype(bf16))
             + _dot((ec * inv).astype(bf16), vc_ref[:, sl].astype(bf16)))
        o_ref[:, sl] = o.astype(o_ref.dtype)


def _band_call(qkv_specs, args, biases, *, bsz, nq, qr, mask_first):
    hd = H_B * DH_B
    bias_specs = [pl.BlockSpec(b.shape, lambda b_, q_: (0, 0, 0)) for b in biases]
    return pl.pallas_call(
        functools.partial(_band_kernel, mask_first=mask_first),
        grid=(bsz, nq),
        in_specs=list(qkv_specs) + bias_specs,
        out_specs=pl.BlockSpec((qr, hd), lambda b, q: (b * nq + q, 0)),
        out_shape=jax.ShapeDtypeStruct((bsz * nq * qr, hd), bf16),
        compiler_params=_cp(("arbitrary", "arbitrary")), name="band")(*args, *biases)


def _rel_bias_matrix(tab, nq, nk, off):
    m = nq + nk
    j = np.arange(m)
    diff = np.where(j < nk, -j, m - j)
    idx = np.clip(diff + off, -REL_CLIP, REL_CLIP) + REL_CLIP
    u = tab.astype(f32)[:, jnp.asarray(idx)]
    h = tab.shape[0]
    return jnp.tile(u, (1, nq))[:, :nq * (m - 1)].reshape(h, nq, m - 1)[:, :, :nk]


def _band_bias_prompt(tab, qr, kb):
    tot = 2 * kb + qr
    qc = (np.arange(qr)[:, None] + 2 * kb) // CHUNK
    kc = np.arange(tot)[None, :] // CHUNK
    valid = (kc <= qc) & (kc >= qc - B_PREV)
    bias = jnp.where(jnp.asarray(valid)[None], _rel_bias_matrix(tab, qr, tot, 2 * kb), NEG_INF)
    return bias[:, :, :kb], bias[:, :, kb:2 * kb], bias[:, :, 2 * kb:]


def _band_bias_sample(tab, t, lc):
    bias = _rel_bias_matrix(tab, t, lc + t, lc)
    return bias[:, :, :lc // 2], bias[:, :, lc // 2:lc], bias[:, :, lc:]


def _swa_kernel(sink_ref, q_ref, kp_ref, kc_ref, vp_ref, vc_ref, o_ref, qs_ref, *, chunk_mask):
    qi = pl.program_id(1)
    qr = q_ref.shape[0]
    pr = kp_ref.shape[0]
    nl = G_C * LANES
    even = lax.broadcasted_iota(jnp.int32, (qr, LANES), 1) < DH_C
    top = lax.broadcasted_iota(jnp.int32, (LANES, LANES), 0) < DH_C
    if chunk_mask:
        kch = lax.broadcasted_iota(jnp.int32, (pr, nl), 0) // CHUNK
        qch = (lax.broadcasted_iota(jnp.int32, (pr, nl), 1) % LANES) // CHUNK
        ok_prev = (kch >= qch) & (qi >= 1)
        kch = lax.broadcasted_iota(jnp.int32, (qr, nl), 0) // CHUNK
        qch = (lax.broadcasted_iota(jnp.int32, (qr, nl), 1) % LANES) // CHUNK
        ok_cur = kch <= qch
    if qr < LANES:
        qs_ref[...] = jnp.zeros_like(qs_ref)

    def twice(ref, c):
        a = ref[:, c * DH_C:(c + 1) * DH_C].astype(bf16)
        return jnp.concatenate([a, a], axis=1)

    for c in range(KV_C):
        for p in range(G_C // 2):
            cols = slice(c * G_C * DH_C + p * LANES, c * G_C * DH_C + (p + 1) * LANES)
            qp = (q_ref[:, cols].astype(f32) * (DH_C ** -0.5)).astype(bf16)
            zero = jnp.zeros_like(qp)
            qs_ref[2 * p * LANES:2 * p * LANES + qr, :] = jnp.where(even, qp, zero)
            qs_ref[(2 * p + 1) * LANES:(2 * p + 1) * LANES + qr, :] = jnp.where(even, zero, qp)
        q_all = qs_ref[...]
        sink = jnp.concatenate(
            [jnp.full((1, LANES), sink_ref[c * G_C + j], f32) for j in range(G_C)], axis=1)
        st_prev = _dot_nt(twice(kp_ref, c), q_all)
        st_cur = _dot_nt(twice(kc_ref, c), q_all)
        if chunk_mask:
            st_prev = jnp.where(ok_prev, st_prev, NEG_INF)
            st_cur = jnp.where(ok_cur, st_cur, NEG_INF)
        m = jnp.maximum(jnp.maximum(jnp.max(st_prev, axis=0, keepdims=True),
                                    jnp.max(st_cur, axis=0, keepdims=True)), sink)
        e_prev = jnp.exp(st_prev - m)
        e_cur = jnp.exp(st_cur - m)
        inv = 1.0 / (jnp.sum(e_prev, axis=0, keepdims=True) + jnp.sum(e_cur, axis=0, keepdims=True)
                     + jnp.exp(sink - m))
        ot = (_dot_tn(twice(vp_ref, c), (e_prev * inv).astype(bf16))
              + _dot_tn(twice(vc_ref, c), (e_cur * inv).astype(bf16)))
        for p in range(G_C // 2):
            cols = slice(c * G_C * DH_C + p * LANES, c * G_C * DH_C + (p + 1) * LANES)
            blk = jnp.where(top, ot[:, 2 * p * LANES:(2 * p + 1) * LANES],
                            ot[:, (2 * p + 1) * LANES:(2 * p + 2) * LANES])
            o_ref[:, cols] = blk.T[:qr].astype(o_ref.dtype)


def _swa_call(specs, args, sinks, *, bsz, nq, qr, chunk_mask):
    hd = H_C * DH_C
    return pl.pallas_call(
        functools.partial(_swa_kernel, chunk_mask=chunk_mask),
        grid=(bsz, nq),
        in_specs=[pl.BlockSpec(memory_space=pltpu.SMEM)] + list(specs),
        out_specs=pl.BlockSpec((qr, hd), lambda b, q: (b * nq + q, 0)),
        out_shape=jax.ShapeDtypeStruct((bsz * nq * qr, hd), bf16),
        scratch_shapes=[pltpu.VMEM((G_C * LANES, LANES), bf16)],
        compiler_params=_cp(("arbitrary", "arbitrary")), name="swa")(sinks, *args)


def _out_kernel(*refs, n_parts, n_first, split_y):
    ny = 2 if split_y else 1
    y_refs = refs[:ny]
    refs = refs[ny - 1:]
    o_refs = refs[1:1 + 2 * n_parts]
    w_refs = refs[1 + 2 * n_parts:1 + 3 * n_parts]
    g_ref, rh_ref, rl_ref = refs[1 + 3 * n_parts:4 + 3 * n_parts]
    y1_ref, h2_ref, route_ref, cnt_ref, run_ref, lg_ref = refs[4 + 3 * n_parts:]
    step = pl.program_id(0)
    i = jnp.minimum(step, pl.num_programs(0) - 2)

    @pl.when(step == 0)
    def _():
        run_ref[...] = jnp.zeros_like(run_ref)
        lg_ref[...] = jnp.zeros_like(lg_ref)

    lg = lg_ref[...]
    live = (step >= 1).astype(f32)

    acc = y_refs[0][...]
    if split_y:
        acc = jnp.where(i < n_first, acc, y_refs[1][...])
    for k in range(n_parts):
        lhs = jnp.where(i < n_first, o_refs[2 * k][...], o_refs[2 * k + 1][...])
        acc = acc + _dot(lhs, w_refs[k][...])
    y1_ref[...] = acc
    h2 = _rms(acc, g_ref[...])
    hi = h2.astype(bf16)
    h2_ref[...] = hi.astype(f32)
    lo = (h2 - hi.astype(f32)).astype(bf16)
    lg_ref[...] = _dot(hi, rh_ref[...]) + _dot(hi, rl_ref[...]) + _dot(lo, rh_ref[...])

    lane_i = lax.broadcasted_iota(jnp.int32, lg.shape, 1)
    lane = lane_i.astype(f32)
    big = float(LANES)
    ninf = -jnp.inf
    is_g = lane_i < N_GROUPS
    gmax = jnp.max(jnp.where(is_g, lg, ninf), axis=-1, keepdims=True)
    gsel = jnp.min(jnp.where(is_g & (lg == gmax), lane, big), axis=-1, keepdims=True)
    pg = 1.0 / jnp.sum(jnp.where(is_g, jnp.exp(lg - gmax), 0.0), axis=-1, keepdims=True)
    e0 = N_GROUPS + EXP_PER_GROUP * gsel
    is_e = (lane >= e0) & (lane < e0 + EXP_PER_GROUP)
    v1 = jnp.max(jnp.where(is_e, lg, ninf), axis=-1, keepdims=True)
    i1 = jnp.min(jnp.where(is_e & (lg == v1), lane, big), axis=-1, keepdims=True)
    is_e2 = is_e & (lane != i1)
    v2 = jnp.max(jnp.where(is_e2, lg, ninf), axis=-1, keepdims=True)
    i2 = jnp.min(jnp.where(is_e2 & (lg == v2), lane, big), axis=-1, keepdims=True)
    t = jnp.exp(v2 - v1)
    w1 = pg / (1.0 + t)
    w2 = pg * t / (1.0 + t)

    tmr = lg.shape[0]
    oh1 = jnp.where(lane == i1, 1.0, 0.0)
    oh2 = jnp.where(lane == i2, 1.0, 0.0)
    before = jnp.where(lax.broadcasted_iota(jnp.int32, (tmr, tmr), 0)
                       > lax.broadcasted_iota(jnp.int32, (tmr, tmr), 1), 1.0, 0.0).astype(bf16)
    tot1 = jnp.sum(oh1, axis=0, keepdims=True)
    tot2 = jnp.sum(oh2, axis=0, keepdims=True)
    base = run_ref[...]
    r1 = jnp.sum(oh1 * (base + _dot(before, oh1.astype(bf16))), axis=-1, keepdims=True)
    r2 = jnp.sum(oh2 * (base + tot1 + _dot(before, oh2.astype(bf16))), axis=-1, keepdims=True)
    run_ref[...] = base + live * (tot1 + tot2)
    cnt_ref[...] = base + live * (tot1 + tot2)
    vals = (i1 - N_GROUPS, i2 - N_GROUPS, w1, w2, r1, r2)
    out = jnp.zeros_like(lg)
    for k, v in enumerate(vals):
        out = jnp.where(lane_i == k, v, out)
    route_ref[...] = out


def _out_proj(y, parts, g, w_router, *, tm):
    n_first = parts[0][0].shape[0] // tm
    split_y = isinstance(y, tuple)
    n = y[0].shape[0] + y[1].shape[0] if split_y else y.shape[0]
    n_tiles = n // tm
    tile = lambda s: jnp.minimum(s, n_tiles - 1)
    first = lambda s: (jnp.minimum(s, n_first - 1), 0)
    second = lambda s: (jnp.clip(s - n_first, 0, n_tiles - n_first - 1), 0)
    if split_y:
        d = y[0].shape[1]
        in_specs = [pl.BlockSpec((tm, d), first), pl.BlockSpec((tm, d), second)]
        args = list(y)
    else:
        d = y.shape[1]
        in_specs = [pl.BlockSpec((tm, d), lambda s: (tile(s), 0))]
        args = [y]
    for op, os_, _ in parts:
        dk = op.shape[1]
        in_specs.append(pl.BlockSpec((tm, dk), first))
        in_specs.append(pl.BlockSpec((tm, dk), second))
        args += [op, os_]
    for _, _, w in parts:
        in_specs.append(pl.BlockSpec(w.shape, lambda i: (0, 0)))
        args.append(w)
    rh = w_router.astype(bf16)
    rl = (w_router - rh.astype(f32)).astype(bf16)
    in_specs += [pl.BlockSpec((1, d), lambda i: (0, 0)),
                 pl.BlockSpec((d, LANES), lambda i: (0, 0)),
                 pl.BlockSpec((d, LANES), lambda i: (0, 0))]
    args += [g.reshape(1, d), rh, rl]
    return pl.pallas_call(
        functools.partial(_out_kernel, n_parts=len(parts), n_first=n_first, split_y=split_y),
        grid=(n_tiles + 1,), in_specs=in_specs,
        out_specs=[pl.BlockSpec((tm, d), lambda s: (tile(s), 0)),
                   pl.BlockSpec((tm, d), lambda s: (tile(s), 0)),
                   pl.BlockSpec((tm, LANES), lambda s: (jnp.maximum(s - 1, 0), 0)),
                   pl.BlockSpec((1, LANES), lambda s: (0, 0))],
        out_shape=[jax.ShapeDtypeStruct((n, d), f32),
                   jax.ShapeDtypeStruct((n, d), f32),
                   jax.ShapeDtypeStruct((n, LANES), f32),
                   jax.ShapeDtypeStruct((1, LANES), f32)],
        scratch_shapes=[pltpu.VMEM((1, LANES), f32), pltpu.VMEM((tm, LANES), f32)],
        compiler_params=_cp(("arbitrary",)), name="out_proj")(*args)


def _moe_kernel(te_ref, tv_ref, pos_ref, plo_ref, phi_ref, en_ref, es_ref,
                h_ref, wg_ref, wu_ref, wd_ref, out_ref,
                slot_ref, tok_ref, x0, x1, x2, y0, y1, y2, gsem, ssem, wg_f, wu_f, wd_f, wsem,
                wg_s, wu_s, wd_s, *, tm, n):
    t = pl.program_id(0)
    nt = pl.num_programs(0)
    n2 = 2 * n
    xs = (x0, x1, x2)
    ys = (y0, y1, y2)
    depth = len(xs)

    def weights_start(e, slot):
        for src, dst in ((wg_ref, wg_f), (wu_ref, wu_f), (wd_ref, wd_f)):
            pltpu.make_async_copy(src.at[e], dst.at[slot], wsem.at[slot]).start(priority=1)

    def weights_wait(slot):
        for src, dst in ((wg_ref, wg_f), (wu_ref, wu_f), (wd_ref, wd_f)):
            pltpu.make_async_copy(src.at[0], dst.at[slot], wsem.at[slot]).wait()

    def tile_valid(tile):
        return (tile >= 0) & (tile < nt) & (tv_ref[jnp.clip(tile, 0, nt - 1)] != 0)

    valid = tile_valid(t)

    def gather_row(tile, r, p, priority=0):
        tok = tok_ref[tile * tm + r]
        pltpu.make_async_copy(h_ref.at[pl.ds(tok, 1)], xs[p].at[pl.ds(r, 1)], gsem.at[p]).start(
            priority=priority)

    def scatter_row(tile, r, p, priority=0):
        s = slot_ref[tile * tm + r]
        pltpu.make_async_copy(ys[p].at[pl.ds(r, 1)], out_ref.at[pl.ds(s, 1)], ssem.at[p]).start(
            priority=priority)

    def rows_loop(row_fn, tile, p):
        def body(r, carry):
            row_fn(tile, r, p)
            return carry
        lax.fori_loop(0, tm, body, 0, unroll=8)

    def gather_wait(p):
        pltpu.make_async_copy(h_ref.at[pl.ds(0, tm)], xs[p], gsem.at[p]).wait()

    def scatter_wait(p):
        pltpu.make_async_copy(ys[p], out_ref.at[pl.ds(0, tm)], ssem.at[p]).wait()

    def compute(p):
        x = xs[p][...].astype(bf16)
        g = _dot(x, wg_s[...])
        u = _dot(x, wu_s[...])
        ys[p][...] = _dot((_silu(g) * u).astype(bf16), wd_s[...])

    @pl.when(t == 0)
    def _():
        def per_expert(e, nxt):
            def body(r, v):
                slot_ref[r] = v
                tok_ref[r] = v - n2
                return v + 1
            return lax.fori_loop(plo_ref[e], phi_ref[e], body, nxt)
        lax.fori_loop(0, N_EXPERTS, per_expert, n2)

        def fill(choice):
            def body(tok, carry):
                s = choice * n + tok
                row = pos_ref[s]
                slot_ref[row] = s
                tok_ref[row] = tok
                return carry
            lax.fori_loop(0, n, body, 0, unroll=8)
        fill(0)
        fill(1)
        weights_start(te_ref[0], 0)
        rows_loop(gather_row, 0, 0)

        @pl.when(tile_valid(1))
        def _():
            rows_loop(gather_row, 1, 1)

    def step(p):
        q = (p + 2) % depth

        @pl.when(t >= depth)
        def _():
            scatter_wait(p)

        @pl.when(valid)
        def _():
            gather_wait(p)

            @pl.when((t == 0) | (te_ref[t] != te_ref[jnp.maximum(t - 1, 0)]))
            def _():
                slot = es_ref[t]
                weights_wait(slot)
                wg_s[...] = wg_f[slot].astype(bf16)
                wu_s[...] = wu_f[slot].astype(bf16)
                wd_s[...] = wd_f[slot].astype(bf16)

                @pl.when(en_ref[t] >= 0)
                def _():
                    weights_start(en_ref[t], 1 - slot)

            steady = tile_valid(t + 2) & (t >= 1)

            @pl.when(steady)
            def _():
                for r in range(tm):
                    gather_row(t + 2, r, q, priority=(r + 1) % 2)
                    scatter_row(t - 1, r, q, priority=r % 2)
                compute(p)

            @pl.when(jnp.logical_not(steady))
            def _():
                @pl.when(tile_valid(t + 2))
                def _():
                    rows_loop(gather_row, t + 2, q)

                @pl.when(t >= 1)
                def _():
                    rows_loop(scatter_row, t - 1, q)
                compute(p)

        @pl.when(jnp.logical_not(valid))
        def _():
            @pl.when(tile_valid(t - 1))
            def _():
                rows_loop(scatter_row, t - 1, q)
            ys[p][...] = jnp.zeros(ys[p].shape, f32)
            pltpu.make_async_copy(ys[p], out_ref.at[pl.ds(pl.multiple_of(t * tm, tm), tm)], ssem.at[p]).start()

        @pl.when(t == nt - 1)
        def _():
            @pl.when(valid)
            def _():
                rows_loop(scatter_row, t, p)
            for k in range(depth):
                scatter_wait(k)

    for p in range(depth):
        pl.when(t % depth == p)(functools.partial(step, p))


def _moe_experts(h2, pos, tile_e, tile_v, pad_lo, pad_hi, wg, wu, wd, *, layer, tm, n_tiles):
    n, d = h2.shape
    ff = wg.shape[2]
    assert n >= N_EXPERTS * tm, "padding rows read tokens 0 .. N_EXPERTS * (tm - 1)"
    assert n_tiles >= 3
    first = jnp.concatenate([jnp.ones((1,), jnp.int32), (tile_e[1:] != tile_e[:-1]).astype(jnp.int32)])
    seq = jnp.cumsum(first * tile_v) - 1
    seq = jnp.where(tile_v != 0, seq, n_tiles)
    nxt_pos = jnp.sum((seq[None, :] <= seq[:, None]).astype(jnp.int32), axis=1)
    has_next = (nxt_pos < n_tiles) & (tile_v[jnp.minimum(nxt_pos, n_tiles - 1)] != 0) & (tile_v != 0)
    tile_e = tile_e + layer * N_EXPERTS
    e_next = jnp.where(has_next, tile_e[jnp.minimum(nxt_pos, n_tiles - 1)], -1).astype(jnp.int32)
    e_slot = (seq % 2).astype(jnp.int32)
    row_buf = pltpu.VMEM((tm, d), f32)
    any_spec = pl.BlockSpec(memory_space=pl.ANY)
    grid_spec = pltpu.PrefetchScalarGridSpec(
        num_scalar_prefetch=7, grid=(n_tiles,),
        in_specs=[any_spec, any_spec, any_spec, any_spec],
        out_specs=any_spec,
        scratch_shapes=[pltpu.SMEM((n_tiles * tm,), jnp.int32), pltpu.SMEM((n_tiles * tm,), jnp.int32),
                        row_buf, row_buf, row_buf, row_buf, row_buf, row_buf,
                        pltpu.SemaphoreType.DMA((3,)), pltpu.SemaphoreType.DMA((3,)),
                        pltpu.VMEM((2, d, ff), f32), pltpu.VMEM((2, d, ff), f32), pltpu.VMEM((2, ff, d), f32),
                        pltpu.SemaphoreType.DMA((2,)),
                        pltpu.VMEM((d, ff), bf16), pltpu.VMEM((d, ff), bf16), pltpu.VMEM((ff, d), bf16)])
    return pl.pallas_call(
        functools.partial(_moe_kernel, tm=tm, n=n), grid_spec=grid_spec,
        out_shape=jax.ShapeDtypeStruct((n_tiles * tm, d), f32),
        compiler_params=_cp(("arbitrary",)), name="moe")(
            tile_e, tile_v, pos, pad_lo, pad_hi, e_next, e_slot, h2, wg, wu, wd)


def _moe(h2, route, counts, wg, wu, wd, *, layer, tm):
    n = route.shape[0]
    n_tiles = (2 * n) // tm + N_EXPERTS
    counts = counts[0, N_GROUPS:N_GROUPS + N_EXPERTS].astype(jnp.int32)
    padded = ((counts + tm - 1) // tm) * tm
    ends = jnp.cumsum(padded)
    starts = ends - padded
    e = route[:, 0:2].astype(jnp.int32)
    rank = route[:, 4:6].astype(jnp.int32)
    pos = (starts[e] + rank).T.reshape(-1)
    tile_start = jnp.arange(n_tiles, dtype=jnp.int32) * tm
    tile_e = jnp.sum((tile_start[:, None] >= ends[None, :]).astype(jnp.int32), axis=1)
    tile_v = (tile_start < ends[-1]).astype(jnp.int32)
    tile_e = jnp.minimum(tile_e, N_EXPERTS - 1)
    return _moe_experts(h2, pos, tile_e, tile_v, starts + counts, ends, wg, wu, wd, layer=layer, tm=tm,
                        n_tiles=n_tiles)


def _final_kernel(y_ref, m0_ref, m1_ref, rt_ref, g_ref, o_ref):
    o_ref[...] = _rms(y_ref[...] + _moe_combine(m0_ref, m1_ref, rt_ref), g_ref[...])


def _final(y, moe_out, route, g, *, tm, row0, rows):
    n, d = y.shape
    r0 = row0 // tm
    r1 = (n + row0) // tm
    return pl.pallas_call(
        _final_kernel, grid=(rows // tm,),
        in_specs=[pl.BlockSpec((tm, d), lambda i: (r0 + i, 0)),
                  pl.BlockSpec((tm, d), lambda i: (r0 + i, 0)),
                  pl.BlockSpec((tm, d), lambda i: (r1 + i, 0)),
                  pl.BlockSpec((tm, LANES), lambda i: (r0 + i, 0)),
                  pl.BlockSpec((1, d), lambda i: (0, 0))],
        out_specs=pl.BlockSpec((tm, d), lambda i: (i, 0)),
        out_shape=jax.ShapeDtypeStruct((rows, d), f32),
        compiler_params=_cp(("arbitrary",)), name="final_norm")(y, moe_out, moe_out, route, g.reshape(1, d))


def _rope_tables(pos):
    half = ROT_DIM // 2
    inv = jnp.power(ROPE_THETA, -jnp.arange(half, dtype=f32) * 2.0 / ROT_DIM)
    ang = pos.astype(f32)[:, None] * inv[None, :]
    cos, sin = lax.optimization_barrier((jnp.cos(ang), jnp.sin(ang)))
    n = pos.shape[0]
    one = jnp.ones((n, DH_C - ROT_DIM), f32)
    zero = jnp.zeros((n, DH_C - ROT_DIM), f32)
    zh = jnp.zeros((n, half), f32)
    c = jnp.concatenate([cos, cos, one], axis=1)
    s1 = jnp.concatenate([zh, sin, zero], axis=1)
    s2 = jnp.concatenate([-sin, zh, zero], axis=1)
    rep = LANES // DH_C
    return tuple(jnp.tile(t, (1, rep)) for t in (c, s1, s2))


def kernel(x_prompt, x_sample, cache_gla_state, cache_band_k, cache_band_v, cache_swa_k, cache_swa_v,
           norm_mix, norm_ffn, norm_final, w_in_ab, w_gla_lr, b_gla_lr, gla_out_norm, rel_bias_tab,
           w_out_ab, w_in_c, b_in_c, attn_sinks, w_out_c, w_router_grp, w_router_exp, w_exp_gate,
           w_exp_up, w_exp_down):
    bp, sp, d = x_prompt.shape
    bs, ts, _ = x_sample.shape
    n_p = bp * sp
    n_s = bs * ts
    n = n_p + n_s
    tm = 512 if n_s % 512 == 0 else n_s
    tmo = min(tm, 256)
    tme = 256 if (2 * n) % 256 == 0 else 64
    ff = w_exp_gate.shape[-1]

    xp = x_prompt.reshape(n_p, d)
    xs_ = x_sample.reshape(n_s, d)

    def router_w(l):
        wr = jnp.concatenate([w_router_grp[l], w_router_exp[l].reshape(d, N_EXPERTS)], axis=1)
        return jnp.zeros((d, LANES), f32).at[:, :N_GROUPS + N_EXPERTS].set(wr)

    experts = (w_exp_gate.reshape(-1, d, ff), w_exp_up.reshape(-1, d, ff), w_exp_down.reshape(-1, ff, d))

    w0 = w_in_ab[0].astype(bf16)
    la0 = 2 * H_A * DK_A + 2 * H_A * DV_A
    w_cat = jnp.concatenate(
        [w0[:, :la0], w0[:, la0 + GLA_RANK:], w0[:, la0:la0 + GLA_RANK],
         jnp.zeros((d, LANES - GLA_RANK), bf16)], axis=1)
    z = _proj(xp, norm_mix[0], w_cat, tm=tmo, tn=1024, loop_cols=_LA0, yb=xs_)

    c_s = min(CHUNK, ts)
    oa_p, gla_p = _gla(z, w_gla_lr[0], b_gla_lr[0], gla_out_norm[0],
                       jnp.zeros((bp, H_A, DK_A, DV_A), f32), bsz=bp, t=sp, row0=0, C=CHUNK, CB=4)
    oa_s, gla_s = _gla(z, w_gla_lr[0], b_gla_lr[0], gla_out_norm[0], cache_gla_state[0],
                       bsz=bs, t=ts, row0=n_p, C=c_s, CB=ts // c_s)

    hd = H_B * DH_B
    qr = 256
    nq = sp // qr
    qcb, kcb, vcb = _QB0 // hd, _KB0 // hd, _VB0 // hd
    cur = lambda cb: pl.BlockSpec((qr, hd), lambda b, q: (b * nq + q, cb))
    prev = lambda cb, back: pl.BlockSpec((qr, hd), lambda b, q: (b * nq + jnp.maximum(q - back, 0), cb))
    ob_p = _band_call(
        [cur(qcb), prev(kcb, 2), prev(kcb, 1), cur(kcb), prev(vcb, 2), prev(vcb, 1), cur(vcb)],
        [z] * 7, _band_bias_prompt(rel_bias_tab[0], qr, qr), bsz=bp, nq=nq, qr=qr, mask_first=True)

    lb = cache_band_k.shape[2]
    ck = cache_band_k[0].reshape(bs * lb, hd)
    cv = cache_band_v[0].reshape(bs * lb, hd)
    rs0 = n_p // ts
    new = lambda cb: pl.BlockSpec((ts, hd), lambda b, q: (rs0 + b, cb))
    cpiece = lambda k: pl.BlockSpec((lb // 2, hd), lambda b, q: (2 * b + k, 0))
    ob_s = _band_call(
        [new(qcb), cpiece(0), cpiece(1), new(kcb), cpiece(0), cpiece(1), new(vcb)],
        [z, ck, ck, z, cv, cv, z], _band_bias_sample(rel_bias_tab[0], ts, lb),
        bsz=bs, nq=1, qr=ts, mask_first=False)

    dmix_a = H_A * DV_A
    wo = w_out_ab[0].astype(bf16)
    y1, h2, route, cnt = _out_proj(
        (xp, xs_), [(oa_p, oa_s, wo[:dmix_a]), (ob_p, ob_s, wo[dmix_a:])], norm_ffn[0], router_w(0),
        tm=tmo)
    moe0 = _moe(h2, route, cnt, *experts, layer=0, tm=tme)

    keep = min(B_PREV * CHUNK, sp)

    def prompt_tail(a, rows, c0, width):
        return jnp.stack([a[(b + 1) * sp - rows:(b + 1) * sp, c0:c0 + width] for b in range(bp)])

    band_k_p = prompt_tail(z, keep, _KB0, hd).astype(f32).reshape(bp, keep, H_B, DH_B)[None]
    band_v_p = prompt_tail(z, keep, _VB0, hd).astype(f32).reshape(bp, keep, H_B, DH_B)[None]
    kb_s = z[n_p:, _KB0:_KB0 + hd].astype(f32).reshape(bs, ts, H_B, DH_B)
    vb_s = z[n_p:, _VB0:_VB0 + hd].astype(f32).reshape(bs, ts, H_B, DH_B)
    band_k_s = jnp.concatenate([cache_band_k[0][:, ts:], kb_s], axis=1)[None]
    band_v_s = jnp.concatenate([cache_band_v[0][:, ts:], vb_s], axis=1)[None]

    nq_c = H_C * DH_C
    nkv = KV_C * DH_C
    uniq = _rope_tables(jnp.concatenate([jnp.arange(sp), PAST_LEN + jnp.arange(ts)]))
    rope = tuple(jnp.concatenate([jnp.tile(u[:sp], (bp, 1)), jnp.tile(u[sp:], (bs, 1))]) for u in uniq)
    y2, z1 = _proj(y1, norm_mix[1], w_in_c[0].astype(bf16), tm=tmo, tn=512, loop_cols=nq_c,
                   add=(moe0, route), bias=b_in_c[0], rope=rope, rope_cols=nq_c + nkv)

    qr_c = WINDOW
    nq1 = sp // qr_c
    kw = nkv
    kcb1, vcb1 = _KC0 // kw, _VC0 // kw
    qspec = pl.BlockSpec((qr_c, nq_c), lambda b, q: (b * nq1 + q, 0))
    cur1 = lambda cb: pl.BlockSpec((qr_c, kw), lambda b, q: (b * nq1 + q, cb))
    prev1 = lambda cb: pl.BlockSpec((qr_c, kw), lambda b, q: (b * nq1 + jnp.maximum(q - 1, 0), cb))
    sinks = attn_sinks[0].astype(f32)
    oc_p = _swa_call([qspec, prev1(kcb1), cur1(kcb1), prev1(vcb1), cur1(vcb1)], [z1] * 5, sinks,
                     bsz=bp, nq=nq1, qr=qr_c, chunk_mask=True)

    lc = cache_swa_k.shape[2]
    csk = cache_swa_k[0].reshape(bs * lc, nkv)
    csv = cache_swa_v[0].reshape(bs * lc, nkv)
    qspec_s = pl.BlockSpec((ts, nq_c), lambda b, q: (rs0 + b, 0))
    new1 = lambda cb: pl.BlockSpec((ts, kw), lambda b, q: (rs0 + b, cb))
    cspec = pl.BlockSpec((lc, kw), lambda b, q: (b, 0))
    oc_s = _swa_call([qspec_s, cspec, new1(kcb1), cspec, new1(vcb1)], [z1, csk, z1, csv, z1], sinks,
                     bsz=bs, nq=1, qr=ts, chunk_mask=False)

    y3, h4, route1, cnt1 = _out_proj(y2, [(oc_p, oc_s, w_out_c[0].astype(bf16))], norm_ffn[1],
                                     router_w(1), tm=tmo)
    moe1 = _moe(h4, route1, cnt1, *experts, layer=1, tm=tme)

    keep1 = min(WINDOW, sp)

    swa_k_p = prompt_tail(z1, keep1, _KC0, kw).astype(f32).reshape(bp, keep1, KV_C, DH_C)[None]
    swa_v_p = prompt_tail(z1, keep1, _VC0, kw).astype(f32).reshape(bp, keep1, KV_C, DH_C)[None]
    k1_s = z1[n_p:, _KC0:_KC0 + kw].astype(f32).reshape(bs, ts, KV_C, DH_C)
    v1_s = z1[n_p:, _VC0:_VC0 + kw].astype(f32).reshape(bs, ts, KV_C, DH_C)
    swa_k_s = jnp.concatenate([cache_swa_k[0][:, ts:], k1_s], axis=1)[None]
    swa_v_s = jnp.concatenate([cache_swa_v[0][:, ts:], v1_s], axis=1)[None]

    y_prompt = _final(y3, moe1, route1, norm_final, tm=tm, row0=0, rows=n_p).reshape(bp, sp, d)
    y_sample = _final(y3, moe1, route1, norm_final, tm=tm, row0=n_p, rows=n_s).reshape(bs, ts, d)

    return (y_prompt, y_sample, gla_p[None], gla_s[None].astype(cache_gla_state.dtype),
            band_k_p, band_v_p, band_k_s, band_v_s, swa_k_p, swa_v_p, swa_k_s, swa_v_s)
```

```python
import functools

import numpy as np
import jax
import jax.numpy as jnp
from jax import lax
from jax.experimental import pallas as pl
from jax.experimental.pallas import tpu as pltpu

f32 = jnp.float32
bf16 = jnp.bfloat16

CHUNK = 64
EPS = 1e-6
NEG_INF = -1e30
PAST_LEN = 1024
H_A = 8
DK_A = 64
DV_A = 128
GLA_RANK = 16
GLA_TAU = 16.0
H_B = 8
DH_B = 128
B_PREV = 8
REL_CLIP = 128
H_C = 32
KV_C = 4
G_C = H_C // KV_C
DH_C = 64
WINDOW = 128
ROT_DIM = DH_C // 4
ROPE_THETA = 500000.0
N_GROUPS = 4
EXP_PER_GROUP = 8
N_EXPERTS = N_GROUPS * EXP_PER_GROUP

LANES = 128
VMEM_LIMIT = 52 * 1024 * 1024

_QA0, _KA0, _VA0, _GA0 = 0, 512, 1024, 2048
_QB0, _KB0, _VB0, _LA0 = 3072, 4096, 5120, 6144
_NC_AB = 6272
_QC0, _KC0, _VC0 = 0, 2048, 2304
_NC_C = 2560


def _cp(sem):
    return pltpu.CompilerParams(dimension_semantics=sem, vmem_limit_bytes=VMEM_LIMIT)


def _dot(a, b):
    return jnp.dot(a, b, preferred_element_type=f32)


def _dot_nt(a, b):
    return lax.dot_general(a, b, (((1,), (1,)), ((), ())), preferred_element_type=f32)


def _dot_tn(a, b):
    return lax.dot_general(a, b, (((0,), (0,)), ((), ())), preferred_element_type=f32)


def _rms(x, g):
    ms = jnp.mean(x * x, axis=-1, keepdims=True)
    return (x * lax.rsqrt(ms + EPS)) * g


def _silu(x):
    return x / (1.0 + jnp.exp(-x))


def _moe_combine(m0_ref, m1_ref, route_ref):
    rt = route_ref[...]
    return rt[:, 2:3] * m0_ref[...] + rt[:, 3:4] * m1_ref[...]


def _proj_kernel(*refs, n_first, has_add, has_bias, rope_cols, loop_cols, tn):
    it = iter(refs)
    ya_ref = next(it)
    yb_ref = next(it) if n_first is not None else None
    if has_add:
        m0_ref, m1_ref, rt_ref = next(it), next(it), next(it)
    g_ref = next(it)
    w_ref = next(it)
    b_ref = next(it) if has_bias else None
    if rope_cols:
        c_ref, s1_ref, s2_ref = next(it), next(it), next(it)
    ysum_ref = next(it) if has_add else None
    z_ref = next(it)
    h_ref = next(it)
    i = pl.program_id(0)

    x = ya_ref[...]
    if n_first is not None:
        x = jnp.where(i < n_first, x, yb_ref[...])
    if has_add:
        x = x + _moe_combine(m0_ref, m1_ref, rt_ref)
        ysum_ref[...] = x
    h_ref[...] = _rms(x, g_ref[...]).astype(bf16)

    def rope(acc):
        width = acc.shape[1]
        rep = width // LANES
        c = jnp.tile(c_ref[...], (1, rep))
        s1 = jnp.tile(s1_ref[...], (1, rep))
        s2 = jnp.tile(s2_ref[...], (1, rep))
        half = ROT_DIM // 2
        lo = pltpu.roll(acc, half, axis=1)
        hi = pltpu.roll(acc, width - half, axis=1)
        return acc * c + lo * s1 + hi * s2

    def col_tile(j, carry):
        cols = pl.ds(pl.multiple_of(j * tn, tn), tn)
        acc = _dot(h_ref[...], w_ref[:, cols])
        if has_bias:
            acc = acc + b_ref[:, cols]
        if rope_cols:
            acc = rope(acc)
        z_ref[:, cols] = acc.astype(z_ref.dtype)
        return carry

    nc = z_ref.shape[1]
    lax.fori_loop(0, loop_cols // tn, col_tile, 0)
    if nc > loop_cols:
        acc = _dot(h_ref[...], w_ref[:, loop_cols:])
        if has_bias:
            acc = acc + b_ref[:, loop_cols:]
        rt = max(rope_cols - loop_cols, 0)
        if rt:
            z_ref[:, loop_cols:loop_cols + rt] = rope(acc[:, :rt]).astype(z_ref.dtype)
        z_ref[:, loop_cols + rt:] = acc[:, rt:].astype(z_ref.dtype)


def _proj(ya, g, w, *, tm, tn, loop_cols, yb=None, add=None, bias=None, rope=None, rope_cols=0):
    d = ya.shape[1]
    na = ya.shape[0] // tm
    nb = 0 if yb is None else yb.shape[0] // tm
    n = (na + nb) * tm
    nc = w.shape[1]
    assert loop_cols % tn == 0 and loop_cols <= nc and (rope_cols == 0 or rope_cols >= loop_cols)
    once = dict(pipeline_mode=pl.Buffered(1))
    in_specs = []
    args = []
    if yb is None:
        in_specs.append(pl.BlockSpec((tm, d), lambda i: (i, 0)))
        args.append(ya)
    else:
        in_specs.append(pl.BlockSpec((tm, d), lambda i: (jnp.minimum(i, na - 1), 0)))
        in_specs.append(pl.BlockSpec((tm, d), lambda i: (jnp.maximum(i - na, 0), 0)))
        args += [ya, yb]
    if add is not None:
        moe_out, route = add
        in_specs.append(pl.BlockSpec((tm, d), lambda i: (i, 0)))
        in_specs.append(pl.BlockSpec((tm, d), lambda i: (i + na, 0)))
        in_specs.append(pl.BlockSpec((tm, LANES), lambda i: (i, 0)))
        args += [moe_out, moe_out, route]
    in_specs.append(pl.BlockSpec((1, d), lambda i: (0, 0)))
    args.append(g.reshape(1, d))
    in_specs.append(pl.BlockSpec((d, nc), lambda i: (0, 0), **once))
    args.append(w)
    if bias is not None:
        in_specs.append(pl.BlockSpec((1, nc), lambda i: (0, 0)))
        args.append(bias.reshape(1, nc))
    if rope_cols:
        for t in rope:
            in_specs.append(pl.BlockSpec((tm, LANES), lambda i: (i, 0)))
            args.append(t)
    out_shape = []
    out_specs = []
    if add is not None:
        out_shape.append(jax.ShapeDtypeStruct((n, d), f32))
        out_specs.append(pl.BlockSpec((tm, d), lambda i: (i, 0)))
    out_shape.append(jax.ShapeDtypeStruct((n, nc), bf16))
    out_specs.append(pl.BlockSpec((tm, nc), lambda i: (i, 0)))
    kern = functools.partial(_proj_kernel, n_first=(na if yb is not None else None),
                             has_add=add is not None, has_bias=bias is not None, rope_cols=rope_cols,
                             loop_cols=loop_cols, tn=tn)
    out = pl.pallas_call(
        kern, grid=(na + nb,), in_specs=in_specs, out_specs=out_specs, out_shape=out_shape,
        scratch_shapes=[pltpu.VMEM((tm, d), bf16)],
        compiler_params=_cp(("arbitrary",)), name="proj")(*args)
    return out if add is not None else out[0]


def _gla_kernel(*refs, C, CB, G):
    seq_refs = [refs[5 * g:5 * g + 5] for g in range(G)]
    (wlr_ref, blr_ref, gn_ref, s0_ref, o_ref, sfin_ref,
     st_ref, qd_ref, oi_ref, ds_ref, dec_ref) = refs[5 * G:]
    s = pl.program_id(1)
    R = C * CB

    @pl.when(s == 0)
    def _():
        st_ref[...] = s0_ref[...]

    row = lax.broadcasted_iota(jnp.int32, (R, R), 0)
    col = lax.broadcasted_iota(jnp.int32, (R, R), 1)
    same = (row // C) == (col // C)
    tril = same & (row >= col)
    tri = jnp.where(tril, 1.0, 0.0).astype(bf16)
    ones = jnp.where(same, 1.0, 0.0).astype(bf16)
    even = lax.broadcasted_iota(jnp.int32, (R, LANES), 1) < DK_A
    srow = lax.broadcasted_iota(jnp.int32, (2 * DV_A, LANES), 0)
    slane = lax.broadcasted_iota(jnp.int32, (2 * DV_A, LANES), 1)
    blockdiag = (srow < DV_A) == (slane < DK_A)
    wlr = wlr_ref[...]
    blr = blr_ref[...]
    gn = gn_ref[...]
    pairs = H_A // 2

    for g in range(G):
        q_ref, k_ref, v_ref, _, la_ref = seq_refs[g]
        pre = _dot(la_ref[...], wlr) + blr
        loga = -(jnp.maximum(-pre, 0.0) + jnp.log1p(jnp.exp(-jnp.abs(pre)))) * (1.0 / GLA_TAU)
        hi = loga.astype(bf16)
        lo = (loga - hi.astype(f32)).astype(bf16)
        b = _dot(tri, hi) + _dot(tri, lo)
        b_end = _dot(ones, hi) + _dot(ones, lo)
        qf = q_ref[...].astype(f32) * (DK_A ** -0.5)
        kf = k_ref[...].astype(f32)
        q_dec = (qf * jnp.exp(b)).astype(bf16)
        k_inv = (kf * jnp.exp(-b)).astype(bf16)
        k_end = (kf * jnp.exp(b_end - b)).astype(bf16)
        dec = jnp.exp(b_end)
        qd_ref[g] = q_dec
        for c in range(CB):
            dec_ref[g, c] = dec[c * C:c * C + 1, :]
        for p in range(pairs):
            sl = slice(p * LANES, (p + 1) * LANES)
            vs = slice(p * 2 * DV_A, (p + 1) * 2 * DV_A)
            qp = q_dec[:, sl]
            kp = k_inv[:, sl]
            vp = v_ref[:, vs]
            zero = jnp.zeros_like(qp)
            att_e = jnp.where(tril, _dot_nt(jnp.where(even, qp, zero), kp), 0.0).astype(bf16)
            att_o = jnp.where(tril, _dot_nt(jnp.where(even, zero, qp), kp), 0.0).astype(bf16)
            oi_ref[g, :, vs] = jnp.concatenate([_dot(att_e, vp)[:, :DV_A], _dot(att_o, vp)[:, DV_A:]], axis=1)
            for c in range(CB):
                rows = slice(c * C, (c + 1) * C)
                ds = _dot_tn(vp[rows], k_end[rows, sl])
                ds_ref[g, c, p] = jnp.where(blockdiag, ds, 0.0)

    for c in range(CB):
        rows = slice(c * C, (c + 1) * C)
        for g in range(G):
            for p in range(pairs):
                sl = slice(p * LANES, (p + 1) * LANES)
                vs = slice(p * 2 * DV_A, (p + 1) * 2 * DV_A)
                st = st_ref[g, p]
                oi_ref[g, rows, vs] += _dot_nt(qd_ref[g, rows, sl], st.astype(bf16))
                st_ref[g, p] = dec_ref[g, c][:, sl] * st + ds_ref[g, c, p]

    for g in range(G):
        g_ref = seq_refs[g][3]
        for h in range(H_A):
            hs = slice(h * DV_A, (h + 1) * DV_A)
            o_ref[g, :, hs] = (_rms(oi_ref[g, :, hs], gn) * _silu(g_ref[:, hs].astype(f32))).astype(o_ref.dtype)

    @pl.when(s == pl.num_programs(1) - 1)
    def _():
        sfin_ref[...] = st_ref[...]


def _state_to_kernel_layout(s0):
    bsz = s0.shape[0]
    st = jnp.swapaxes(s0, -1, -2).reshape(bsz, H_A // 2, 2, DV_A, DK_A)
    z = jnp.zeros_like(st[:, :, 0])
    top = jnp.concatenate([st[:, :, 0], z], axis=-1)
    bot = jnp.concatenate([z, st[:, :, 1]], axis=-1)
    return jnp.concatenate([top, bot], axis=-2)


def _state_from_kernel_layout(st):
    bsz = st.shape[0]
    e = st[:, :, :DV_A, :DK_A]
    o = st[:, :, DV_A:, DK_A:]
    s = jnp.stack([e, o], axis=2).reshape(bsz, H_A, DV_A, DK_A)
    return jnp.swapaxes(s, -1, -2)


def _gla(z, w_lr, b_lr, g_norm, s0, *, bsz, t, row0, C, CB, G=2):
    rb = C * CB
    ns = t // rb
    r0 = row0 // rb
    assert bsz % G == 0

    def zspec(g, width, cb):
        return pl.BlockSpec((rb, width), lambda i, s: (r0 + (i * G + g) * ns + s, cb))

    hk = H_A * DK_A
    hv = H_A * DV_A
    st0 = _state_to_kernel_layout(s0.astype(f32))
    wl = jnp.zeros((LANES, hk), f32).at[:GLA_RANK].set(w_lr).astype(bf16)
    st_shape = (G, H_A // 2, 2 * DV_A, 2 * DK_A)
    seq_specs = []
    for g in range(G):
        seq_specs += [zspec(g, hk, _QA0 // hk), zspec(g, hk, _KA0 // hk), zspec(g, hv, _VA0 // hv),
                      zspec(g, hv, _GA0 // hv), zspec(g, LANES, _LA0 // LANES)]
    o, sfin = pl.pallas_call(
        functools.partial(_gla_kernel, C=C, CB=CB, G=G),
        grid=(bsz // G, ns),
        in_specs=seq_specs + [
            pl.BlockSpec((LANES, hk), lambda i, s: (0, 0)),
            pl.BlockSpec((1, hk), lambda i, s: (0, 0)),
            pl.BlockSpec((1, DV_A), lambda i, s: (0, 0)),
            pl.BlockSpec(st_shape, lambda i, s: (i, 0, 0, 0))],
        out_specs=[pl.BlockSpec((G, rb, hv), lambda i, s: (i, s, 0)),
                   pl.BlockSpec(st_shape, lambda i, s: (i, 0, 0, 0))],
        out_shape=[jax.ShapeDtypeStruct((bsz, t, hv), bf16),
                   jax.ShapeDtypeStruct((bsz,) + st_shape[1:], f32)],
        scratch_shapes=[pltpu.VMEM(st_shape, f32),
                        pltpu.VMEM((G, rb, hk), bf16),
                        pltpu.VMEM((G, rb, hv), f32),
                        pltpu.VMEM((G, CB) + st_shape[1:], f32),
                        pltpu.VMEM((G, CB, 1, hk), f32)],
        compiler_params=_cp(("arbitrary", "arbitrary")), name="gla")(
            *([z] * (5 * G)), wl, b_lr.reshape(1, hk).astype(f32), g_norm.reshape(1, DV_A).astype(f32), st0)
    return o.reshape(bsz * t, hv), _state_from_kernel_layout(sfin)


def _band_kernel(q_ref, ka_ref, kb_ref, kc_ref, va_ref, vb_ref, vc_ref, ba_ref, bb_ref, bc_ref,
                 o_ref, *, mask_first):
    qi = pl.program_id(1)
    for h in range(H_B):
        sl = slice(h * DH_B, (h + 1) * DH_B)
        qh = (q_ref[:, sl].astype(f32) * (DH_B ** -0.5)).astype(bf16)
        sa = _dot_nt(qh, ka_ref[:, sl].astype(bf16)) + ba_ref[h]
        sb = _dot_nt(qh, kb_ref[:, sl].astype(bf16)) + bb_ref[h]
        sc = _dot_nt(qh, kc_ref[:, sl].astype(bf16)) + bc_ref[h]
        if mask_first:
            sa = jnp.where(qi >= 2, sa, NEG_INF)
            sb = jnp.where(qi >= 1, sb, NEG_INF)
        m = jnp.maximum(jnp.maximum(jnp.max(sa, axis=-1, keepdims=True),
                                    jnp.max(sb, axis=-1, keepdims=True)),
                        jnp.max(sc, axis=-1, keepdims=True))
        ea = jnp.exp(sa - m)
        eb = jnp.exp(sb - m)
        ec = jnp.exp(sc - m)
        inv = 1.0 / (jnp.sum(ea, axis=-1, keepdims=True) + jnp.sum(eb, axis=-1, keepdims=True)
                     + jnp.sum(ec, axis=-1, keepdims=True))
        o = (_dot(ea.astype(bf16), va_ref[:, sl].astype(bf16))
             + _dot(eb.astype(bf16), vb_ref[:, sl].astype(bf16))
             + _dot(ec.astype(bf16), vc_ref[:, sl].astype(bf16))) * inv
        o_ref[:, sl] = o.astype(o_ref.dtype)


def _band_call(qkv_specs, args, biases, *, bsz, nq, qr, mask_first):
    hd = H_B * DH_B
    bias_specs = [pl.BlockSpec(b.shape, lambda b_, q_: (0, 0, 0)) for b in biases]
    return pl.pallas_call(
        functools.partial(_band_kernel, mask_first=mask_first),
        grid=(bsz, nq),
        in_specs=list(qkv_specs) + bias_specs,
        out_specs=pl.BlockSpec((qr, hd), lambda b, q: (b * nq + q, 0)),
        out_shape=jax.ShapeDtypeStruct((bsz * nq * qr, hd), bf16),
        compiler_params=_cp(("arbitrary", "arbitrary")), name="band")(*args, *biases)


def _rel_bias_matrix(tab, nq, nk, off):
    m = nq + nk
    j = np.arange(m)
    diff = np.where(j < nk, -j, m - j)
    idx = np.clip(diff + off, -REL_CLIP, REL_CLIP) + REL_CLIP
    u = tab.astype(f32)[:, jnp.asarray(idx)]
    h = tab.shape[0]
    return jnp.tile(u, (1, nq))[:, :nq * (m - 1)].reshape(h, nq, m - 1)[:, :, :nk]


def _band_bias_prompt(tab, qr, kb):
    tot = 2 * kb + qr
    qc = (np.arange(qr)[:, None] + 2 * kb) // CHUNK
    kc = np.arange(tot)[None, :] // CHUNK
    valid = (kc <= qc) & (kc >= qc - B_PREV)
    bias = jnp.where(jnp.asarray(valid)[None], _rel_bias_matrix(tab, qr, tot, 2 * kb), NEG_INF)
    return bias[:, :, :kb], bias[:, :, kb:2 * kb], bias[:, :, 2 * kb:]


def _band_bias_sample(tab, t, lc):
    bias = _rel_bias_matrix(tab, t, lc + t, lc)
    return bias[:, :, :lc // 2], bias[:, :, lc // 2:lc], bias[:, :, lc:]


def _swa_kernel(sink_ref, q_ref, kp_ref, kc_ref, vp_ref, vc_ref, o_ref, qs_ref, *, chunk_mask):
    qi = pl.program_id(1)
    qr = q_ref.shape[0]
    pr = kp_ref.shape[0]
    nl = G_C * LANES
    even = lax.broadcasted_iota(jnp.int32, (qr, LANES), 1) < DH_C
    top = lax.broadcasted_iota(jnp.int32, (LANES, LANES), 0) < DH_C
    if chunk_mask:
        kch = lax.broadcasted_iota(jnp.int32, (pr, nl), 0) // CHUNK
        qch = (lax.broadcasted_iota(jnp.int32, (pr, nl), 1) % LANES) // CHUNK
        ok_prev = (kch >= qch) & (qi >= 1)
        kch = lax.broadcasted_iota(jnp.int32, (qr, nl), 0) // CHUNK
        qch = (lax.broadcasted_iota(jnp.int32, (qr, nl), 1) % LANES) // CHUNK
        ok_cur = kch <= qch
    if qr < LANES:
        qs_ref[...] = jnp.zeros_like(qs_ref)

    def twice(ref, c):
        a = ref[:, c * DH_C:(c + 1) * DH_C].astype(bf16)
        return jnp.concatenate([a, a], axis=1)

    for c in range(KV_C):
        for p in range(G_C // 2):
            cols = slice(c * G_C * DH_C + p * LANES, c * G_C * DH_C + (p + 1) * LANES)
            qp = (q_ref[:, cols].astype(f32) * (DH_C ** -0.5)).astype(bf16)
            zero = jnp.zeros_like(qp)
            qs_ref[2 * p * LANES:2 * p * LANES + qr, :] = jnp.where(even, qp, zero)
            qs_ref[(2 * p + 1) * LANES:(2 * p + 1) * LANES + qr, :] = jnp.where(even, zero, qp)
        q_all = qs_ref[...]
        sink = jnp.concatenate(
            [jnp.full((1, LANES), sink_ref[c * G_C + j], f32) for j in range(G_C)], axis=1)
        st_prev = _dot_nt(twice(kp_ref, c), q_all)
        st_cur = _dot_nt(twice(kc_ref, c), q_all)
        if chunk_mask:
            st_prev = jnp.where(ok_prev, st_prev, NEG_INF)
            st_cur = jnp.where(ok_cur, st_cur, NEG_INF)
        m = jnp.maximum(jnp.maximum(jnp.max(st_prev, axis=0, keepdims=True),
                                    jnp.max(st_cur, axis=0, keepdims=True)), sink)
        e_prev = jnp.exp(st_prev - m)
        e_cur = jnp.exp(st_cur - m)
        inv = 1.0 / (jnp.sum(e_prev, axis=0, keepdims=True) + jnp.sum(e_cur, axis=0, keepdims=True)
                     + jnp.exp(sink - m))
        ot = (_dot_tn(twice(vp_ref, c), e_prev.astype(bf16))
              + _dot_tn(twice(vc_ref, c), e_cur.astype(bf16))) * inv
        for p in range(G_C // 2):
            cols = slice(c * G_C * DH_C + p * LANES, c * G_C * DH_C + (p + 1) * LANES)
            blk = jnp.where(top, ot[:, 2 * p * LANES:(2 * p + 1) * LANES],
                            ot[:, (2 * p + 1) * LANES:(2 * p + 2) * LANES])
            o_ref[:, cols] = blk.T[:qr].astype(o_ref.dtype)


def _swa_call(specs, args, sinks, *, bsz, nq, qr, chunk_mask):
    hd = H_C * DH_C
    return pl.pallas_call(
        functools.partial(_swa_kernel, chunk_mask=chunk_mask),
        grid=(bsz, nq),
        in_specs=[pl.BlockSpec(memory_space=pltpu.SMEM)] + list(specs),
        out_specs=pl.BlockSpec((qr, hd), lambda b, q: (b * nq + q, 0)),
        out_shape=jax.ShapeDtypeStruct((bsz * nq * qr, hd), bf16),
        scratch_shapes=[pltpu.VMEM((G_C * LANES, LANES), bf16)],
        compiler_params=_cp(("arbitrary", "arbitrary")), name="swa")(sinks, *args)


def _out_kernel(*refs, n_parts, n_first, split_y):
    ny = 2 if split_y else 1
    y_refs = refs[:ny]
    refs = refs[ny - 1:]
    o_refs = refs[1:1 + 2 * n_parts]
    w_refs = refs[1 + 2 * n_parts:1 + 3 * n_parts]
    g_ref, rh_ref, rl_ref = refs[1 + 3 * n_parts:4 + 3 * n_parts]
    y1_ref, h2_ref, route_ref, cnt_ref, run_ref, lg_ref = refs[4 + 3 * n_parts:]
    step = pl.program_id(0)
    i = jnp.minimum(step, pl.num_programs(0) - 2)

    @pl.when(step == 0)
    def _():
        run_ref[...] = jnp.zeros_like(run_ref)
        lg_ref[...] = jnp.zeros_like(lg_ref)

    lg = lg_ref[...]
    live = (step >= 1).astype(f32)

    acc = y_refs[0][...]
    if split_y:
        acc = jnp.where(i < n_first, acc, y_refs[1][...])
    for k in range(n_parts):
        lhs = jnp.where(i < n_first, o_refs[2 * k][...], o_refs[2 * k + 1][...])
        acc = acc + _dot(lhs, w_refs[k][...])
    y1_ref[...] = acc
    h2 = _rms(acc, g_ref[...])
    hi = h2.astype(bf16)
    h2_ref[...] = hi.astype(f32)
    lo = (h2 - hi.astype(f32)).astype(bf16)
    lg_ref[...] = _dot(hi, rh_ref[...]) + _dot(hi, rl_ref[...]) + _dot(lo, rh_ref[...])

    lane_i = lax.broadcasted_iota(jnp.int32, lg.shape, 1)
    lane = lane_i.astype(f32)
    big = float(LANES)
    ninf = -jnp.inf
    is_g = lane_i < N_GROUPS
    gmax = jnp.max(jnp.where(is_g, lg, ninf), axis=-1, keepdims=True)
    gsel = jnp.min(jnp.where(is_g & (lg == gmax), lane, big), axis=-1, keepdims=True)
    pg = 1.0 / jnp.sum(jnp.where(is_g, jnp.exp(lg - gmax), 0.0), axis=-1, keepdims=True)
    e0 = N_GROUPS + EXP_PER_GROUP * gsel
    is_e = (lane >= e0) & (lane < e0 + EXP_PER_GROUP)
    v1 = jnp.max(jnp.where(is_e, lg, ninf), axis=-1, keepdims=True)
    i1 = jnp.min(jnp.where(is_e & (lg == v1), lane, big), axis=-1, keepdims=True)
    is_e2 = is_e & (lane != i1)
    v2 = jnp.max(jnp.where(is_e2, lg, ninf), axis=-1, keepdims=True)
    i2 = jnp.min(jnp.where(is_e2 & (lg == v2), lane, big), axis=-1, keepdims=True)
    t = jnp.exp(v2 - v1)
    w1 = pg / (1.0 + t)
    w2 = pg * t / (1.0 + t)

    tmr = lg.shape[0]
    oh1 = jnp.where(lane == i1, 1.0, 0.0)
    oh2 = jnp.where(lane == i2, 1.0, 0.0)
    before = jnp.where(lax.broadcasted_iota(jnp.int32, (tmr, tmr), 0)
                       > lax.broadcasted_iota(jnp.int32, (tmr, tmr), 1), 1.0, 0.0).astype(bf16)
    tot1 = jnp.sum(oh1, axis=0, keepdims=True)
    tot2 = jnp.sum(oh2, axis=0, keepdims=True)
    base = run_ref[...]
    r1 = jnp.sum(oh1 * (base + _dot(before, oh1.astype(bf16))), axis=-1, keepdims=True)
    r2 = jnp.sum(oh2 * (base + tot1 + _dot(before, oh2.astype(bf16))), axis=-1, keepdims=True)
    run_ref[...] = base + live * (tot1 + tot2)
    cnt_ref[...] = base + live * (tot1 + tot2)
    vals = (i1 - N_GROUPS, i2 - N_GROUPS, w1, w2, r1, r2)
    out = jnp.zeros_like(lg)
    for k, v in enumerate(vals):
        out = jnp.where(lane_i == k, v, out)
    route_ref[...] = out


def _out_proj(y, parts, g, w_router, *, tm):
    n_first = parts[0][0].shape[0] // tm
    split_y = isinstance(y, tuple)
    n = y[0].shape[0] + y[1].shape[0] if split_y else y.shape[0]
    n_tiles = n // tm
    tile = lambda s: jnp.minimum(s, n_tiles - 1)
    first = lambda s: (jnp.minimum(s, n_first - 1), 0)
    second = lambda s: (jnp.clip(s - n_first, 0, n_tiles - n_first - 1), 0)
    if split_y:
        d = y[0].shape[1]
        in_specs = [pl.BlockSpec((tm, d), first), pl.BlockSpec((tm, d), second)]
        args = list(y)
    else:
        d = y.shape[1]
        in_specs = [pl.BlockSpec((tm, d), lambda s: (tile(s), 0))]
        args = [y]
    for op, os_, _ in parts:
        dk = op.shape[1]
        in_specs.append(pl.BlockSpec((tm, dk), first))
        in_specs.append(pl.BlockSpec((tm, dk), second))
        args += [op, os_]
    for _, _, w in parts:
        in_specs.append(pl.BlockSpec(w.shape, lambda i: (0, 0)))
        args.append(w)
    rh = w_router.astype(bf16)
    rl = (w_router - rh.astype(f32)).astype(bf16)
    in_specs += [pl.BlockSpec((1, d), lambda i: (0, 0)),
                 pl.BlockSpec((d, LANES), lambda i: (0, 0)),
                 pl.BlockSpec((d, LANES), lambda i: (0, 0))]
    args += [g.reshape(1, d), rh, rl]
    return pl.pallas_call(
        functools.partial(_out_kernel, n_parts=len(parts), n_first=n_first, split_y=split_y),
        grid=(n_tiles + 1,), in_specs=in_specs,
        out_specs=[pl.BlockSpec((tm, d), lambda s: (tile(s), 0)),
                   pl.BlockSpec((tm, d), lambda s: (tile(s), 0)),
                   pl.BlockSpec((tm, LANES), lambda s: (jnp.maximum(s - 1, 0), 0)),
                   pl.BlockSpec((1, LANES), lambda s: (0, 0))],
        out_shape=[jax.ShapeDtypeStruct((n, d), f32),
                   jax.ShapeDtypeStruct((n, d), f32),
                   jax.ShapeDtypeStruct((n, LANES), f32),
                   jax.ShapeDtypeStruct((1, LANES), f32)],
        scratch_shapes=[pltpu.VMEM((1, LANES), f32), pltpu.VMEM((tm, LANES), f32)],
        compiler_params=_cp(("arbitrary",)), name="out_proj")(*args)


def _moe_kernel(te_ref, tv_ref, pos_ref, plo_ref, phi_ref, en_ref, es_ref,
                h_ref, wg_ref, wu_ref, wd_ref, out_ref,
                slot_ref, tok_ref, x0, x1, x2, y0, y1, y2, gsem, ssem, wg_f, wu_f, wd_f, wsem,
                wg_s, wu_s, wd_s, *, tm, n):
    t = pl.program_id(0)
    nt = pl.num_programs(0)
    n2 = 2 * n
    xs = (x0, x1, x2)
    ys = (y0, y1, y2)
    depth = len(xs)

    def weights_start(e, slot):
        for src, dst in ((wg_ref, wg_f), (wu_ref, wu_f), (wd_ref, wd_f)):
            pltpu.make_async_copy(src.at[e], dst.at[slot], wsem.at[slot]).start(priority=1)

    def weights_wait(slot):
        for src, dst in ((wg_ref, wg_f), (wu_ref, wu_f), (wd_ref, wd_f)):
            pltpu.make_async_copy(src.at[0], dst.at[slot], wsem.at[slot]).wait()

    def tile_valid(tile):
        return (tile >= 0) & (tile < nt) & (tv_ref[jnp.clip(tile, 0, nt - 1)] != 0)

    valid = tile_valid(t)

    def gather_row(tile, r, p):
        tok = tok_ref[tile * tm + r]
        pltpu.make_async_copy(h_ref.at[pl.ds(tok, 1)], xs[p].at[pl.ds(r, 1)], gsem.at[p]).start()

    def scatter_row(tile, r, p, priority=0):
        s = slot_ref[tile * tm + r]
        pltpu.make_async_copy(ys[p].at[pl.ds(r, 1)], out_ref.at[pl.ds(s, 1)], ssem.at[p]).start(
            priority=priority)

    def rows_loop(row_fn, tile, p):
        def body(r, carry):
            row_fn(tile, r, p)
            return carry
        lax.fori_loop(0, tm, body, 0, unroll=8)

    def gather_wait(p):
        pltpu.make_async_copy(h_ref.at[pl.ds(0, tm)], xs[p], gsem.at[p]).wait()

    def scatter_wait(p):
        pltpu.make_async_copy(ys[p], out_ref.at[pl.ds(0, tm)], ssem.at[p]).wait()

    def compute(p):
        x = xs[p][...].astype(bf16)
        g = _dot(x, wg_s[...])
        u = _dot(x, wu_s[...])
        ys[p][...] = _dot((_silu(g) * u).astype(bf16), wd_s[...])

    @pl.when(t == 0)
    def _():
        def per_expert(e, nxt):
            def body(r, v):
                slot_ref[r] = v
                tok_ref[r] = v - n2
                return v + 1
            return lax.fori_loop(plo_ref[e], phi_ref[e], body, nxt)
        lax.fori_loop(0, N_EXPERTS, per_expert, n2)

        def fill(choice):
            def body(tok, carry):
                s = choice * n + tok
                row = pos_ref[s]
                slot_ref[row] = s
                tok_ref[row] = tok
                return carry
            lax.fori_loop(0, n, body, 0, unroll=8)
        fill(0)
        fill(1)
        weights_start(te_ref[0], 0)
        rows_loop(gather_row, 0, 0)

        @pl.when(tile_valid(1))
        def _():
            rows_loop(gather_row, 1, 1)

    def step(p):
        q = (p + 2) % depth

        @pl.when(t >= depth)
        def _():
            scatter_wait(p)

        @pl.when(valid)
        def _():
            gather_wait(p)

            @pl.when((t == 0) | (te_ref[t] != te_ref[jnp.maximum(t - 1, 0)]))
            def _():
                slot = es_ref[t]
                weights_wait(slot)
                wg_s[...] = wg_f[slot].astype(bf16)
                wu_s[...] = wu_f[slot].astype(bf16)
                wd_s[...] = wd_f[slot].astype(bf16)

                @pl.when(en_ref[t] >= 0)
                def _():
                    weights_start(en_ref[t], 1 - slot)

            steady = tile_valid(t + 2) & (t >= 1)

            @pl.when(steady)
            def _():
                for r in range(tm):
                    gather_row(t + 2, r, q)
                    scatter_row(t - 1, r, q, priority=r % 2)
                compute(p)

            @pl.when(jnp.logical_not(steady))
            def _():
                @pl.when(tile_valid(t + 2))
                def _():
                    rows_loop(gather_row, t + 2, q)

                @pl.when(t >= 1)
                def _():
                    rows_loop(scatter_row, t - 1, q)
                compute(p)

        @pl.when(jnp.logical_not(valid))
        def _():
            @pl.when(tile_valid(t - 1))
            def _():
                rows_loop(scatter_row, t - 1, q)
            ys[p][...] = jnp.zeros(ys[p].shape, f32)
            pltpu.make_async_copy(ys[p], out_ref.at[pl.ds(pl.multiple_of(t * tm, tm), tm)], ssem.at[p]).start()

        @pl.when(t == nt - 1)
        def _():
            @pl.when(valid)
            def _():
                rows_loop(scatter_row, t, p)
            for k in range(depth):
                scatter_wait(k)

    for p in range(depth):
        pl.when(t % depth == p)(functools.partial(step, p))


def _moe_experts(h2, pos, tile_e, tile_v, pad_lo, pad_hi, wg, wu, wd, *, layer, tm, n_tiles):
    n, d = h2.shape
    ff = wg.shape[2]
    assert n >= N_EXPERTS * tm, "padding rows read tokens 0 .. N_EXPERTS * (tm - 1)"
    assert n_tiles >= 3
    first = jnp.concatenate([jnp.ones((1,), jnp.int32), (tile_e[1:] != tile_e[:-1]).astype(jnp.int32)])
    seq = jnp.cumsum(first * tile_v) - 1
    seq = jnp.where(tile_v != 0, seq, n_tiles)
    nxt_pos = jnp.sum((seq[None, :] <= seq[:, None]).astype(jnp.int32), axis=1)
    has_next = (nxt_pos < n_tiles) & (tile_v[jnp.minimum(nxt_pos, n_tiles - 1)] != 0) & (tile_v != 0)
    tile_e = tile_e + layer * N_EXPERTS
    e_next = jnp.where(has_next, tile_e[jnp.minimum(nxt_pos, n_tiles - 1)], -1).astype(jnp.int32)
    e_slot = (seq % 2).astype(jnp.int32)
    row_buf = pltpu.VMEM((tm, d), f32)
    any_spec = pl.BlockSpec(memory_space=pl.ANY)
    grid_spec = pltpu.PrefetchScalarGridSpec(
        num_scalar_prefetch=7, grid=(n_tiles,),
        in_specs=[any_spec, any_spec, any_spec, any_spec],
        out_specs=any_spec,
        scratch_shapes=[pltpu.SMEM((n_tiles * tm,), jnp.int32), pltpu.SMEM((n_tiles * tm,), jnp.int32),
                        row_buf, row_buf, row_buf, row_buf, row_buf, row_buf,
                        pltpu.SemaphoreType.DMA((3,)), pltpu.SemaphoreType.DMA((3,)),
                        pltpu.VMEM((2, d, ff), f32), pltpu.VMEM((2, d, ff), f32), pltpu.VMEM((2, ff, d), f32),
                        pltpu.SemaphoreType.DMA((2,)),
                        pltpu.VMEM((d, ff), bf16), pltpu.VMEM((d, ff), bf16), pltpu.VMEM((ff, d), bf16)])
    return pl.pallas_call(
        functools.partial(_moe_kernel, tm=tm, n=n), grid_spec=grid_spec,
        out_shape=jax.ShapeDtypeStruct((n_tiles * tm, d), f32),
        compiler_params=_cp(("arbitrary",)), name="moe")(
            tile_e, tile_v, pos, pad_lo, pad_hi, e_next, e_slot, h2, wg, wu, wd)


def _moe(h2, route, counts, wg, wu, wd, *, layer, tm):
    n = route.shape[0]
    n_tiles = (2 * n) // tm + N_EXPERTS
    counts = counts[0, N_GROUPS:N_GROUPS + N_EXPERTS].astype(jnp.int32)
    padded = ((counts + tm - 1) // tm) * tm
    ends = jnp.cumsum(padded)
    starts = ends - padded
    e = route[:, 0:2].astype(jnp.int32)
    rank = route[:, 4:6].astype(jnp.int32)
    pos = (starts[e] + rank).T.reshape(-1)
    tile_start = jnp.arange(n_tiles, dtype=jnp.int32) * tm
    tile_e = jnp.sum((tile_start[:, None] >= ends[None, :]).astype(jnp.int32), axis=1)
    tile_v = (tile_start < ends[-1]).astype(jnp.int32)
    tile_e = jnp.minimum(tile_e, N_EXPERTS - 1)
    return _moe_experts(h2, pos, tile_e, tile_v, starts + counts, ends, wg, wu, wd, layer=layer, tm=tm,
                        n_tiles=n_tiles)


def _final_kernel(y_ref, m0_ref, m1_ref, rt_ref, g_ref, o_ref):
    o_ref[...] = _rms(y_ref[...] + _moe_combine(m0_ref, m1_ref, rt_ref), g_ref[...])


def _final(y, moe_out, route, g, *, tm, row0, rows):
    n, d = y.shape
    r0 = row0 // tm
    r1 = (n + row0) // tm
    return pl.pallas_call(
        _final_kernel, grid=(rows // tm,),
        in_specs=[pl.BlockSpec((tm, d), lambda i: (r0 + i, 0)),
                  pl.BlockSpec((tm, d), lambda i: (r0 + i, 0)),
                  pl.BlockSpec((tm, d), lambda i: (r1 + i, 0)),
                  pl.BlockSpec((tm, LANES), lambda i: (r0 + i, 0)),
                  pl.BlockSpec((1, d), lambda i: (0, 0))],
        out_specs=pl.BlockSpec((tm, d), lambda i: (i, 0)),
        out_shape=jax.ShapeDtypeStruct((rows, d), f32),
        compiler_params=_cp(("arbitrary",)), name="final_norm")(y, moe_out, moe_out, route, g.reshape(1, d))


def _rope_tables(pos):
    half = ROT_DIM // 2
    inv = jnp.power(ROPE_THETA, -jnp.arange(half, dtype=f32) * 2.0 / ROT_DIM)
    ang = pos.astype(f32)[:, None] * inv[None, :]
    cos, sin = lax.optimization_barrier((jnp.cos(ang), jnp.sin(ang)))
    n = pos.shape[0]
    one = jnp.ones((n, DH_C - ROT_DIM), f32)
    zero = jnp.zeros((n, DH_C - ROT_DIM), f32)
    zh = jnp.zeros((n, half), f32)
    c = jnp.concatenate([cos, cos, one], axis=1)
    s1 = jnp.concatenate([zh, sin, zero], axis=1)
    s2 = jnp.concatenate([-sin, zh, zero], axis=1)
    rep = LANES // DH_C
    return tuple(jnp.tile(t, (1, rep)) for t in (c, s1, s2))


def kernel(x_prompt, x_sample, cache_gla_state, cache_band_k, cache_band_v, cache_swa_k, cache_swa_v,
           norm_mix, norm_ffn, norm_final, w_in_ab, w_gla_lr, b_gla_lr, gla_out_norm, rel_bias_tab,
           w_out_ab, w_in_c, b_in_c, attn_sinks, w_out_c, w_router_grp, w_router_exp, w_exp_gate,
           w_exp_up, w_exp_down):
    bp, sp, d = x_prompt.shape
    bs, ts, _ = x_sample.shape
    n_p = bp * sp
    n_s = bs * ts
    n = n_p + n_s
    tm = 512 if n_s % 512 == 0 else n_s
    tmo = min(tm, 256)
    tme = 256 if (2 * n) % 256 == 0 else 64
    ff = w_exp_gate.shape[-1]

    xp = x_prompt.reshape(n_p, d)
    xs_ = x_sample.reshape(n_s, d)

    def router_w(l):
        wr = jnp.concatenate([w_router_grp[l], w_router_exp[l].reshape(d, N_EXPERTS)], axis=1)
        return jnp.zeros((d, LANES), f32).at[:, :N_GROUPS + N_EXPERTS].set(wr)

    experts = (w_exp_gate.reshape(-1, d, ff), w_exp_up.reshape(-1, d, ff), w_exp_down.reshape(-1, ff, d))

    w0 = w_in_ab[0].astype(bf16)
    la0 = 2 * H_A * DK_A + 2 * H_A * DV_A
    w_cat = jnp.concatenate(
        [w0[:, :la0], w0[:, la0 + GLA_RANK:], w0[:, la0:la0 + GLA_RANK],
         jnp.zeros((d, LANES - GLA_RANK), bf16)], axis=1)
    z = _proj(xp, norm_mix[0], w_cat, tm=tmo, tn=1024, loop_cols=_LA0, yb=xs_)

    c_s = min(CHUNK, ts)
    oa_p, gla_p = _gla(z, w_gla_lr[0], b_gla_lr[0], gla_out_norm[0],
                       jnp.zeros((bp, H_A, DK_A, DV_A), f32), bsz=bp, t=sp, row0=0, C=CHUNK, CB=4)
    oa_s, gla_s = _gla(z, w_gla_lr[0], b_gla_lr[0], gla_out_norm[0], cache_gla_state[0],
                       bsz=bs, t=ts, row0=n_p, C=c_s, CB=ts // c_s)

    hd = H_B * DH_B
    qr = 256
    nq = sp // qr
    qcb, kcb, vcb = _QB0 // hd, _KB0 // hd, _VB0 // hd
    cur = lambda cb: pl.BlockSpec((qr, hd), lambda b, q: (b * nq + q, cb))
    prev = lambda cb, back: pl.BlockSpec((qr, hd), lambda b, q: (b * nq + jnp.maximum(q - back, 0), cb))
    ob_p = _band_call(
        [cur(qcb), prev(kcb, 2), prev(kcb, 1), cur(kcb), prev(vcb, 2), prev(vcb, 1), cur(vcb)],
        [z] * 7, _band_bias_prompt(rel_bias_tab[0], qr, qr), bsz=bp, nq=nq, qr=qr, mask_first=True)

    lb = cache_band_k.shape[2]
    ck = cache_band_k[0].reshape(bs * lb, hd)
    cv = cache_band_v[0].reshape(bs * lb, hd)
    rs0 = n_p // ts
    new = lambda cb: pl.BlockSpec((ts, hd), lambda b, q: (rs0 + b, cb))
    cpiece = lambda k: pl.BlockSpec((lb // 2, hd), lambda b, q: (2 * b + k, 0))
    ob_s = _band_call(
        [new(qcb), cpiece(0), cpiece(1), new(kcb), cpiece(0), cpiece(1), new(vcb)],
        [z, ck, ck, z, cv, cv, z], _band_bias_sample(rel_bias_tab[0], ts, lb),
        bsz=bs, nq=1, qr=ts, mask_first=False)

    dmix_a = H_A * DV_A
    wo = w_out_ab[0].astype(bf16)
    y1, h2, route, cnt = _out_proj(
        (xp, xs_), [(oa_p, oa_s, wo[:dmix_a]), (ob_p, ob_s, wo[dmix_a:])], norm_ffn[0], router_w(0),
        tm=tmo)
    moe0 = _moe(h2, route, cnt, *experts, layer=0, tm=tme)

    keep = min(B_PREV * CHUNK, sp)

    def prompt_tail(a, rows, c0, width):
        return jnp.stack([a[(b + 1) * sp - rows:(b + 1) * sp, c0:c0 + width] for b in range(bp)])

    band_k_p = prompt_tail(z, keep, _KB0, hd).astype(f32).reshape(bp, keep, H_B, DH_B)[None]
    band_v_p = prompt_tail(z, keep, _VB0, hd).astype(f32).reshape(bp, keep, H_B, DH_B)[None]
    kb_s = z[n_p:, _KB0:_KB0 + hd].astype(f32).reshape(bs, ts, H_B, DH_B)
    vb_s = z[n_p:, _VB0:_VB0 + hd].astype(f32).reshape(bs, ts, H_B, DH_B)
    band_k_s = jnp.concatenate([cache_band_k[0][:, ts:], kb_s], axis=1)[None]
    band_v_s = jnp.concatenate([cache_band_v[0][:, ts:], vb_s], axis=1)[None]

    nq_c = H_C * DH_C
    nkv = KV_C * DH_C
    uniq = _rope_tables(jnp.concatenate([jnp.arange(sp), PAST_LEN + jnp.arange(ts)]))
    rope = tuple(jnp.concatenate([jnp.tile(u[:sp], (bp, 1)), jnp.tile(u[sp:], (bs, 1))]) for u in uniq)
    y2, z1 = _proj(y1, norm_mix[1], w_in_c[0].astype(bf16), tm=tmo, tn=512, loop_cols=nq_c,
                   add=(moe0, route), bias=b_in_c[0], rope=rope, rope_cols=nq_c + nkv)

    qr_c = WINDOW
    nq1 = sp // qr_c
    kw = nkv
    kcb1, vcb1 = _KC0 // kw, _VC0 // kw
    qspec = pl.BlockSpec((qr_c, nq_c), lambda b, q: (b * nq1 + q, 0))
    cur1 = lambda cb: pl.BlockSpec((qr_c, kw), lambda b, q: (b * nq1 + q, cb))
    prev1 = lambda cb: pl.BlockSpec((qr_c, kw), lambda b, q: (b * nq1 + jnp.maximum(q - 1, 0), cb))
    sinks = attn_sinks[0].astype(f32)
    oc_p = _swa_call([qspec, prev1(kcb1), cur1(kcb1), prev1(vcb1), cur1(vcb1)], [z1] * 5, sinks,
                     bsz=bp, nq=nq1, qr=qr_c, chunk_mask=True)

    lc = cache_swa_k.shape[2]
    csk = cache_swa_k[0].reshape(bs * lc, nkv)
    csv = cache_swa_v[0].reshape(bs * lc, nkv)
    qspec_s = pl.BlockSpec((ts, nq_c), lambda b, q: (rs0 + b, 0))
    new1 = lambda cb: pl.BlockSpec((ts, kw), lambda b, q: (rs0 + b, cb))
    cspec = pl.BlockSpec((lc, kw), lambda b, q: (b, 0))
    oc_s = _swa_call([qspec_s, cspec, new1(kcb1), cspec, new1(vcb1)], [z1, csk, z1, csv, z1], sinks,
                     bsz=bs, nq=1, qr=ts, chunk_mask=False)

    y3, h4, route1, cnt1 = _out_proj(y2, [(oc_p, oc_s, w_out_c[0].astype(bf16))], norm_ffn[1],
                                     router_w(1), tm=tmo)
    moe1 = _moe(h4, route1, cnt1, *experts, layer=1, tm=tme)

    keep1 = min(WINDOW, sp)

    swa_k_p = prompt_tail(z1, keep1, _KC0, kw).astype(f32).reshape(bp, keep1, KV_C, DH_C)[None]
    swa_v_p = prompt_tail(z1, keep1, _VC0, kw).astype(f32).reshape(bp, keep1, KV_C, DH_C)[None]
    k1_s = z1[n_p:, _KC0:_KC0 + kw].astype(f32).reshape(bs, ts, KV_C, DH_C)
    v1_s = z1[n_p:, _VC0:_VC0 + kw].astype(f32).reshape(bs, ts, KV_C, DH_C)
    swa_k_s = jnp.concatenate([cache_swa_k[0][:, ts:], k1_s], axis=1)[None]
    swa_v_s = jnp.concatenate([cache_swa_v[0][:, ts:], v1_s], axis=1)[None]

    y_prompt = _final(y3, moe1, route1, norm_final, tm=tm, row0=0, rows=n_p).reshape(bp, sp, d)
    y_sample = _final(y3, moe1, route1, norm_final, tm=tm, row0=n_p, rows=n_s).reshape(bs, ts, d)

    return (y_prompt, y_sample, gla_p[None], gla_s[None].astype(cache_gla_state.dtype),
            band_k_p, band_v_p, band_k_s, band_v_s, swa_k_p, swa_v_p, swa_k_s, swa_v_s)
```
